```python
import math
import jax, jax.numpy as jnp
from jax import lax
import numpy as np

D_MODEL = 1024
BATCH = 32
SEQ = 256
DEPTH = 4
DEC_BATCH = 4
DEC_SEQ = 1024
PAST_LEN = 512

GRID_W = 64
N_HEADS_A = 4
QK_NOPE = 128
QK_ROPE = 64
V_HEAD = 128
Q_LORA = 384
KV_LORA = 256
WIDTH_A = N_HEADS_A * V_HEAD
ROPE_THETA = 10000.0
WIDTH_B = 256
N_HEADS_B = 4
HEAD_B = WIDTH_B // N_HEADS_B
CHUNK = 128
WIDTH_C = 256
CONV_W = 3
D_FF = 4 * D_MODEL
MIX_WIDTH = WIDTH_A + WIDTH_B + WIDTH_C
IN_SPLITS = (Q_LORA, KV_LORA, QK_ROPE, WIDTH_B, WIDTH_B, WIDTH_C, WIDTH_C, WIDTH_C)
IN_COLS = Q_LORA + KV_LORA + QK_ROPE + 2 * WIDTH_B + 3 * WIDTH_C
N_MOD = 6
EPS = 1e-6

kernel_name = "hybrid_diffusion_prefix_trunk_step"


def rmsnorm(x, g):
    xf = x.astype(jnp.float32)
    y = xf * lax.rsqrt(jnp.mean(xf * xf, axis=-1, keepdims=True) + EPS)
    return (y * g.astype(jnp.float32)).astype(x.dtype)


def axial_rope_tables(n_tokens):
    rows = n_tokens // GRID_W
    row = jnp.repeat(jnp.arange(rows, dtype=jnp.float32), GRID_W)
    col = jnp.tile(jnp.arange(GRID_W, dtype=jnp.float32), rows)
    nf = QK_ROPE // 4
    inv = ROPE_THETA ** (-jnp.arange(nf, dtype=jnp.float32) / nf)
    ang = jnp.stack([row[:, None] * inv, col[:, None] * inv], axis=1)
    return jnp.cos(ang), jnp.sin(ang)


def apply_axial_rope(x, cos, sin):
    xs = x.astype(jnp.float32).reshape(x.shape[:-1] + (2, 2, QK_ROPE // 4))
    x1, x2 = xs[..., 0, :], xs[..., 1, :]
    out = jnp.stack([x1 * cos - x2 * sin, x1 * sin + x2 * cos], axis=-2)
    return out.reshape(x.shape).astype(x.dtype)


def split_projection(z):
    idx, acc = [], 0
    for s in IN_SPLITS[:-1]:
        acc += s
        idx.append(acc)
    return jnp.split(z, idx, axis=-1)


def modulation(cond, w_ada, b_ada):
    m = jax.nn.silu(cond) @ w_ada + b_ada
    return jnp.split(m[:, None, :], N_MOD, axis=-1)


def mla_queries(q_c, g_q, w_uq):
    q = jnp.einsum('bnr,rhd->bhnd', rmsnorm(q_c, g_q), w_uq)
    return q[..., :QK_NOPE], q[..., QK_NOPE:]


def mla_keys_values(ckv, w_ukv):
    kv = jnp.einsum('blr,rhd->bhld', ckv, w_ukv)
    return kv[..., :QK_NOPE], kv[..., QK_NOPE:]


def mla_attend(q_nope, q_rope, k_nope, k_rope, v):
    b, _, n, _ = q_nope.shape
    scale = 1.0 / math.sqrt(QK_NOPE + QK_ROPE)
    s = (jnp.einsum('bhnd,bhld->bhnl', q_nope, k_nope)
         + jnp.einsum('bhnd,bld->bhnl', q_rope, k_rope)) * scale
    p = jax.nn.softmax(s.astype(jnp.float32), axis=-1).astype(v.dtype)
    o = jnp.einsum('bhnl,bhld->bnhd', p, v)
    return o.reshape(b, n, WIDTH_A)


def chunk_gmlp(u, v, g_v, w_s, b_s):
    b, n, _ = v.shape
    vn = rmsnorm(v, g_v).reshape(b, n // CHUNK, CHUNK, N_HEADS_B, HEAD_B)
    mixed = jnp.einsum('hpq,bcqhd->bcphd', w_s, vn) + b_s.T[None, None, :, :, None]
    return u * mixed.reshape(b, n, WIDTH_B)


def short_conv(bg, cg, hh, w_conv):
    n = hh.shape[1]
    z = cg * hh
    pad = CONV_W // 2
    zp = jnp.pad(z, ((0, 0), (pad, pad), (0, 0)))
    y = zp[:, 0:n] * w_conv[0]
    for k in range(1, CONV_W):
        y = y + zp[:, k:k + n] * w_conv[k]
    return bg * y


def trunk_layer(x, cond, ctx_ckv, ctx_krope, rope,
                w_ada, b_ada, g_pre_mix, w_in, g_q, w_uq, g_kv, w_ukv,
                g_v, w_s, b_s, w_conv, w_out, g_post_mix,
                g_pre_ffn, w_ff1, w_ff2, g_post_ffn):
    sh1, sc1, ga1, sh2, sc2, ga2 = modulation(cond, w_ada, b_ada)
    h = rmsnorm(x, g_pre_mix) * (1.0 + sc1) + sh1
    q_c, ckv_raw, kr, u, v, bg, cg, hh = split_projection(h @ w_in)
    u = jax.nn.gelu(u)
    v = jax.nn.gelu(v)
    ckv = rmsnorm(ckv_raw, g_kv)
    q_nope, q_rope = mla_queries(q_c, g_q, w_uq)
    k_nope, v_a = mla_keys_values(ckv, w_ukv)
    if ctx_ckv is None:
        o_a = mla_attend(q_nope, q_rope, k_nope, kr, v_a)
    else:
        cos, sin = rope
        q_rope = apply_axial_rope(q_rope, cos, sin)
        kr_lat = apply_axial_rope(kr, cos, sin)
        kc_nope, vc = mla_keys_values(ctx_ckv, w_ukv)
        o_a = mla_attend(q_nope, q_rope,
                         jnp.concatenate([kc_nope, k_nope], axis=2),
                         jnp.concatenate([ctx_krope, kr_lat], axis=1),
                         jnp.concatenate([vc, v_a], axis=2))
    o_b = chunk_gmlp(u, v, g_v, w_s, b_s)
    o_c = short_conv(bg, cg, hh, w_conv)
    mix = jnp.concatenate([o_a, o_b, o_c], axis=-1) @ w_out
    x = x + ga1 * rmsnorm(mix, g_post_mix)
    h2 = rmsnorm(x, g_pre_ffn) * (1.0 + sc2) + sh2
    f = jnp.square(jax.nn.relu(h2 @ w_ff1)) @ w_ff2
    x = x + ga2 * rmsnorm(f, g_post_ffn)
    return x, ckv, kr


def setup_inputs(seed: int = 0) -> dict:
    key = jax.random.key(seed)
    ks = jax.random.split(key, 24)
    f32 = jnp.float32

    def nrm(k, shape, scale):
        return jax.random.normal(k, shape, f32) * scale

    def gain(k, shape):
        return 1.0 + 0.02 * jax.random.normal(k, shape, f32)

    return {
        "x_prompt": nrm(ks[0], (BATCH, SEQ, D_MODEL), 1.0),
        "x_sample": nrm(ks[1], (DEC_BATCH, DEC_SEQ, D_MODEL), 1.0),
        "cache_ckv": nrm(ks[2], (DEC_BATCH, DEPTH, PAST_LEN, KV_LORA), 1.0),
        "cache_krope": nrm(ks[3], (DEC_BATCH, DEPTH, PAST_LEN, QK_ROPE), 1.0),
        "c": nrm(ks[4], (DEC_BATCH, D_MODEL), 1.0),
        "c_ctx": nrm(ks[5], (D_MODEL,), 1.0),
        "w_ada": nrm(ks[6], (DEPTH, D_MODEL, N_MOD * D_MODEL), 0.5 * D_MODEL ** -0.5),
        "b_ada": nrm(ks[7], (DEPTH, N_MOD * D_MODEL), 0.02),
        "g_pre_mix": gain(ks[8], (DEPTH, D_MODEL)),
        "w_in": nrm(ks[9], (DEPTH, D_MODEL, IN_COLS), D_MODEL ** -0.5),
        "g_q": gain(ks[10], (DEPTH, Q_LORA)),
        "w_uq": nrm(ks[11], (DEPTH, Q_LORA, N_HEADS_A, QK_NOPE + QK_ROPE), Q_LORA ** -0.5),
        "g_kv": gain(ks[12], (DEPTH, KV_LORA)),
        "w_ukv": nrm(ks[13], (DEPTH, KV_LORA, N_HEADS_A, QK_NOPE + V_HEAD), KV_LORA ** -0.5),
        "g_v": gain(ks[14], (DEPTH, WIDTH_B)),
        "w_s": nrm(ks[15], (DEPTH, N_HEADS_B, CHUNK, CHUNK), CHUNK ** -0.5),
        "b_s": 1.0 + nrm(ks[16], (DEPTH, N_HEADS_B, CHUNK), 0.02),
        "w_conv": nrm(ks[17], (DEPTH, CONV_W, WIDTH_C), CONV_W ** -0.5),
        "w_out": nrm(ks[18], (DEPTH, MIX_WIDTH, D_MODEL), MIX_WIDTH ** -0.5),
        "g_post_mix": gain(ks[19], (DEPTH, D_MODEL)),
        "g_pre_ffn": gain(ks[20], (DEPTH, D_MODEL)),
        "w_ff1": nrm(ks[21], (DEPTH, D_MODEL, D_FF), D_MODEL ** -0.5),
        "w_ff2": nrm(ks[22], (DEPTH, D_FF, D_MODEL), D_FF ** -0.5),
        "g_post_ffn": gain(ks[23], (DEPTH, D_MODEL)),
    }


def reference(x_prompt, x_sample, cache_ckv, cache_krope, c, c_ctx,
              w_ada, b_ada, g_pre_mix, w_in, g_q, w_uq, g_kv, w_ukv,
              g_v, w_s, b_s, w_conv, w_out, g_post_mix,
              g_pre_ffn, w_ff1, w_ff2, g_post_ffn):
    rope = axial_rope_tables(x_sample.shape[1])
    cond_ctx = c_ctx[None, :]
    xp, xs = x_prompt, x_sample
    ckv_list, kr_list = [], []
    for l in range(DEPTH):
        layer_w = (w_ada[l], b_ada[l], g_pre_mix[l], w_in[l], g_q[l], w_uq[l], g_kv[l], w_ukv[l],
                   g_v[l], w_s[l], b_s[l], w_conv[l], w_out[l], g_post_mix[l],
                   g_pre_ffn[l], w_ff1[l], w_ff2[l], g_post_ffn[l])
        xp, ckv_l, kr_l = trunk_layer(xp, cond_ctx, None, None, None, *layer_w)
        ckv_list.append(ckv_l)
        kr_list.append(kr_l)
        xs, _, _ = trunk_layer(xs, c, cache_ckv[:, l], cache_krope[:, l], rope, *layer_w)
    new_ckv = jnp.stack(ckv_list, axis=1)
    new_krope = jnp.stack(kr_list, axis=1)
    return (xp, xs, new_ckv, new_krope)
```

```python
import functools
import math

import jax
import jax.numpy as jnp
from jax import lax
from jax.experimental import pallas as pl
from jax.experimental.pallas import tpu as pltpu

F32 = jnp.float32
BF16 = jnp.bfloat16

D_MODEL = 1024
DEPTH = 4
GRID_W = 64
N_HEADS = 4
QK_NOPE = 128
QK_ROPE = 64
V_HEAD = 128
Q_LORA = 384
KV_LORA = 256
WIDTH_A = N_HEADS * V_HEAD
ROPE_THETA = 10000.0
WIDTH_B = 256
N_HEADS_B = 4
HEAD_B = WIDTH_B // N_HEADS_B
CHUNK = 128
WIDTH_C = 256
D_FF = 4 * D_MODEL
N_MOD = 6
EPS = 1e-6

LANES = 128
QK_PAD = 2 * LANES
QK_WIDTH = N_HEADS * QK_PAD
KV_COLS = N_HEADS * (QK_NOPE + V_HEAD)
BC_WIDTH = WIDTH_B + WIDTH_C
OFF_Q = 0
OFF_CKV = OFF_Q + Q_LORA
OFF_U = OFF_CKV + KV_LORA
OFF_V = OFF_U + WIDTH_B
OFF_BG = OFF_V + WIDTH_B
OFF_CG = OFF_BG + WIDTH_C
OFF_HH = OFF_CG + WIDTH_C
OFF_KR = OFF_HH + WIDTH_C
IN_COLS_PAD = OFF_KR + LANES
COND_ROWS = 8
VMEM_LIMIT = 52 * 1024 * 1024
SM_SCALE = 1.0 / math.sqrt(QK_NOPE + QK_ROPE)


def _rms(x, g):
    return x * lax.rsqrt(jnp.mean(x * x, axis=-1, keepdims=True) + EPS) * g


def _dot(a, b):
    return jnp.dot(a, b, preferred_element_type=F32)


def _dot_nt(a, b):
    return lax.dot_general(a, b, (((1,), (1,)), ((), ())), preferred_element_type=F32)


def _params(n_axes):
    return pltpu.CompilerParams(
        dimension_semantics=("parallel",) * n_axes, vmem_limit_bytes=VMEM_LIMIT)


def _mod_kernel(cond_ref, w_ref, b_ref, o_ref):
    c = cond_ref[...]
    s = c / (1.0 + jnp.exp(-c))
    o_ref[...] = _dot(s.astype(BF16), w_ref[...].astype(BF16)) + b_ref[...]


def _modulations(cond, w_ada, b_ada):
    tn = 1536
    n_cols = N_MOD * D_MODEL
    return pl.pallas_call(
        _mod_kernel,
        grid=(DEPTH, n_cols // tn),
        in_specs=[
            pl.BlockSpec((COND_ROWS, D_MODEL), lambda l, j: (0, 0)),
            pl.BlockSpec((None, D_MODEL, tn), lambda l, j: (l, 0, j)),
            pl.BlockSpec((None, 1, tn), lambda l, j: (l, 0, j)),
        ],
        out_specs=pl.BlockSpec((None, COND_ROWS, tn), lambda l, j: (l, 0, j)),
        out_shape=jax.ShapeDtypeStruct((DEPTH, COND_ROWS, n_cols), F32),
        compiler_params=_params(2),
        name="modulation",
    )(cond, w_ada, b_ada.reshape(DEPTH, 1, n_cols))


def _cache_kv_kernel(ckv_ref, kr_ref, wkv_ref, k_ref, v_ref):
    kv = _dot(ckv_ref[...].astype(BF16), wkv_ref[...])
    kr = kr_ref[...].astype(BF16)
    for h in range(N_HEADS):
        k_ref[:, h * QK_PAD:h * QK_PAD + QK_NOPE] = kv[:, h * QK_NOPE:(h + 1) * QK_NOPE].astype(BF16)
        k_ref[:, h * QK_PAD + QK_NOPE:(h + 1) * QK_PAD] = kr
    v_ref[...] = kv[:, N_HEADS * QK_NOPE:].astype(BF16)


def _cache_kv(cache_ckv, cache_krope_pad, w_kv):
    nb, _, past, _ = cache_ckv.shape
    return pl.pallas_call(
        _cache_kv_kernel,
        grid=(DEPTH, nb),
        in_specs=[
            pl.BlockSpec((None, None, past, KV_LORA), lambda l, b: (b, l, 0, 0)),
            pl.BlockSpec((None, None, past, LANES), lambda l, b: (b, l, 0, 0)),
            pl.BlockSpec((None, KV_LORA, KV_COLS), lambda l, b: (l, 0, 0)),
        ],
        out_specs=[
            pl.BlockSpec((None, None, past, QK_WIDTH), lambda l, b: (l, b, 0, 0)),
            pl.BlockSpec((None, None, past, WIDTH_A), lambda l, b: (l, b, 0, 0)),
        ],
        out_shape=[
            jax.ShapeDtypeStruct((DEPTH, nb, past, QK_WIDTH), BF16),
            jax.ShapeDtypeStruct((DEPTH, nb, past, WIDTH_A), BF16),
        ],
        compiler_params=_params(2),
        name="cache_kv",
    )(cache_ckv, cache_krope_pad, w_kv)


def _swap_halves(x):
    lane = lax.broadcasted_iota(jnp.int32, x.shape, 1)
    first_half = (lane & (QK_ROPE // 2 - 1)) < QK_ROPE // 4
    quarter = QK_ROPE // 4
    return jnp.where(first_half, pltpu.roll(x, LANES - quarter, 1), pltpu.roll(x, quarter, 1))


def _pre_kernel(*refs, seq_len, rope, emit_cache):
    (x_ref, mod_ref, gpre_ref, win_ref, gq_ref, wq_ref, gkv_ref, wkv_ref,
     gv_ref, ws_ref, bs_ref, wconv_ref) = refs[:12]
    refs = refs[12:]
    if rope:
        cos_ref, sin_ref = refs[:2]
        refs = refs[2:]
    q_ref, k_ref, v_ref, obc_ref = refs[:4]
    refs = refs[4:]
    if emit_cache:
        ckv_ref, kr_ref = refs

    tm = x_ref.shape[0]
    x = x_ref[...]
    m = mod_ref[...]
    sh1 = m[:, 0:D_MODEL]
    sc1 = m[:, D_MODEL:2 * D_MODEL]
    h = _rms(x, gpre_ref[...]) * (1.0 + sc1) + sh1
    z = _dot(h.astype(BF16), win_ref[...])

    ckv = _rms(z[:, OFF_CKV:OFF_CKV + KV_LORA], gkv_ref[...])
    krz = z[:, OFF_KR:OFF_KR + LANES]
    if emit_cache:
        ckv_ref[...] = ckv
        kr_ref[...] = krz[:, 0:QK_ROPE]
    qn = _rms(z[:, OFF_Q:OFF_Q + Q_LORA], gq_ref[...])
    q = _dot(qn.astype(BF16), wq_ref[...]) * SM_SCALE
    kv = _dot(ckv.astype(BF16), wkv_ref[...])
    if rope:
        cos = cos_ref[...]
        sin = sin_ref[...]
        krz = krz * cos + _swap_halves(krz) * sin
    krz = krz.astype(BF16)
    for hd in range(N_HEADS):
        lo = hd * QK_PAD
        q_ref[:, lo:lo + QK_NOPE] = q[:, lo:lo + QK_NOPE].astype(BF16)
        qr = q[:, lo + QK_NOPE:lo + QK_PAD]
        if rope:
            qr = qr * cos + _swap_halves(qr) * sin
        q_ref[:, lo + QK_NOPE:lo + QK_PAD] = qr.astype(BF16)
        k_ref[:, lo:lo + QK_NOPE] = kv[:, hd * QK_NOPE:(hd + 1) * QK_NOPE].astype(BF16)
        k_ref[:, lo + QK_NOPE:lo + QK_PAD] = krz
    v_ref[...] = kv[:, N_HEADS * QK_NOPE:].astype(BF16)

    u = jax.nn.gelu(z[:, OFF_U:OFF_U + WIDTH_B])
    vn = _rms(jax.nn.gelu(z[:, OFF_V:OFF_V + WIDTH_B]), gv_ref[...]).astype(BF16)
    lane = lax.broadcasted_iota(jnp.int32, (CHUNK, WIDTH_B), 1)
    ws = ws_ref[...]
    bs = bs_ref[...]
    for c in range(tm // CHUNK):
        rows = slice(c * CHUNK, (c + 1) * CHUNK)
        r = _dot(ws, vn[rows, :])
        mixed = r[(N_HEADS_B - 1) * CHUNK:, :]
        for hb in range(N_HEADS_B - 2, -1, -1):
            mixed = jnp.where(lane < (hb + 1) * HEAD_B, r[hb * CHUNK:(hb + 1) * CHUNK, :], mixed)
        obc_ref[rows, 0:WIDTH_B] = (u[rows, :] * (mixed + bs)).astype(BF16)

    zc = z[:, OFF_CG:OFF_CG + WIDTH_C] * z[:, OFF_HH:OFF_HH + WIDTH_C]
    pos = lax.broadcasted_iota(jnp.int32, (tm, WIDTH_C), 0) & (seq_len - 1)
    z_prev = jnp.where(pos == 0, 0.0, pltpu.roll(zc, 1, 0))
    z_next = jnp.where(pos == seq_len - 1, 0.0, pltpu.roll(zc, tm - 1, 0))
    wc = wconv_ref[...]
    y = z_prev * wc[0:1, :] + zc * wc[1:2, :] + z_next * wc[2:3, :]
    obc_ref[:, WIDTH_B:] = (z[:, OFF_BG:OFF_BG + WIDTH_C] * y).astype(BF16)


def _layer_spec(shape):
    nd = len(shape)
    return lambda l: pl.BlockSpec((None,) + tuple(shape[1:]), lambda *_: (l,) + (0,) * (nd - 1))


def _pre(x, mods, wts, l, *, seq_len, tm, rope_tabs, emit_cache):
    t = x.shape[0]
    tiles_per_mod = (t // mods.shape[0]) // tm
    rope = rope_tabs is not None
    row_spec = lambda w: pl.BlockSpec((tm, w), lambda i: (i, 0))
    in_specs = [
        row_spec(D_MODEL),
        pl.BlockSpec((None, 1, N_MOD * D_MODEL), lambda i: (i // tiles_per_mod, 0, 0)),
    ]
    names = ("g_pre_mix", "w_in", "g_q", "w_q", "g_kv", "w_kv", "g_v", "w_s", "b_s", "w_conv")
    args = [x, mods]
    for n in names:
        in_specs.append(_layer_spec(wts[n].shape)(l))
        args.append(wts[n])
    if rope:
        in_specs += [pl.BlockSpec((tm, LANES), lambda i: (0, 0))] * 2
        args += list(rope_tabs)
    out_specs = [row_spec(QK_WIDTH), row_spec(QK_WIDTH), row_spec(WIDTH_A), row_spec(BC_WIDTH)]
    out_shape = [
        jax.ShapeDtypeStruct((t, QK_WIDTH), BF16),
        jax.ShapeDtypeStruct((t, QK_WIDTH), BF16),
        jax.ShapeDtypeStruct((t, WIDTH_A), BF16),
        jax.ShapeDtypeStruct((t, BC_WIDTH), BF16),
    ]
    if emit_cache:
        out_specs += [row_spec(KV_LORA), row_spec(QK_ROPE)]
        out_shape += [jax.ShapeDtypeStruct((t, KV_LORA), F32), jax.ShapeDtypeStruct((t, QK_ROPE), F32)]
    return pl.pallas_call(
        functools.partial(_pre_kernel, seq_len=seq_len, rope=rope, emit_cache=emit_cache),
        grid=(t // tm,),
        in_specs=in_specs,
        out_specs=out_specs,
        out_shape=out_shape,
        compiler_params=_params(1),
        name="pre_latent" if rope else "pre_context",
    )(*args)


def _attn_kernel(*refs, n_seq, has_cache):
    q_ref, k_ref, v_ref = refs[:3]
    refs = refs[3:]
    if has_cache:
        kc_ref, vc_ref = refs[:2]
        refs = refs[2:]
    obc_ref, x_ref, mod_ref, gpost_ref, wout_ref, o_ref, mix_ref = refs

    ql = q_ref.shape[0] // n_seq
    kl = k_ref.shape[0] // n_seq
    for s in range(n_seq):
        qrows = slice(s * ql, (s + 1) * ql)
        krows = slice(s * kl, (s + 1) * kl)
        for hd in range(N_HEADS):
            qk_cols = slice(hd * QK_PAD, (hd + 1) * QK_PAD)
            v_cols = slice(hd * V_HEAD, (hd + 1) * V_HEAD)
            qh = q_ref[qrows, qk_cols]
            s_lat = _dot_nt(qh, k_ref[krows, qk_cols])
            mx = jnp.max(s_lat, axis=-1, keepdims=True)
            if has_cache:
                s_ctx = _dot_nt(qh, kc_ref[:, qk_cols])
                mx = jnp.maximum(mx, jnp.max(s_ctx, axis=-1, keepdims=True))
            p_lat = jnp.exp(s_lat - mx)
            den = jnp.sum(p_lat, axis=-1, keepdims=True)
            o = _dot(p_lat.astype(BF16), v_ref[krows, v_cols])
            if has_cache:
                p_ctx = jnp.exp(s_ctx - mx)
                den = den + jnp.sum(p_ctx, axis=-1, keepdims=True)
                o = o + _dot(p_ctx.astype(BF16), vc_ref[:, v_cols])
            mix_ref[qrows, v_cols] = (o * (1.0 / den)).astype(BF16)
    mix_ref[:, WIDTH_A:] = obc_ref[...]
    mo = _dot(mix_ref[...], wout_ref[...])
    ga1 = mod_ref[...][:, 2 * D_MODEL:3 * D_MODEL]
    o_ref[...] = x_ref[...] + ga1 * _rms(mo, gpost_ref[...])


def _attn(q, k, v, obc, x, mods, wts, l, *, tq, keys_per_seq, cache):
    t = x.shape[0]
    has_cache = cache is not None
    if keys_per_seq <= tq:
        n_seq, k_rows, k_of = tq // keys_per_seq, tq, (lambda i: i)
    else:
        n_seq, k_rows = 1, keys_per_seq
        k_of = lambda i: i // (keys_per_seq // tq)
    tiles_per_mod = (t // mods.shape[0]) // tq
    mod_of = lambda i: i // tiles_per_mod
    row_spec = lambda w: pl.BlockSpec((tq, w), lambda i: (i, 0))
    in_specs = [
        row_spec(QK_WIDTH),
        pl.BlockSpec((k_rows, QK_WIDTH), lambda i: (k_of(i), 0)),
        pl.BlockSpec((k_rows, WIDTH_A), lambda i: (k_of(i), 0)),
    ]
    args = [q, k, v]
    if has_cache:
        kc, vc = cache
        past = kc.shape[2]
        in_specs += [
            pl.BlockSpec((None, None, past, QK_WIDTH), lambda i: (l, mod_of(i), 0, 0)),
            pl.BlockSpec((None, None, past, WIDTH_A), lambda i: (l, mod_of(i), 0, 0)),
        ]
        args += [kc, vc]
    in_specs += [
        row_spec(BC_WIDTH),
        row_spec(D_MODEL),
        pl.BlockSpec((None, 1, N_MOD * D_MODEL), lambda i: (mod_of(i), 0, 0)),
        _layer_spec(wts["g_post_mix"].shape)(l),
        _layer_spec(wts["w_out"].shape)(l),
    ]
    args += [obc, x, mods, wts["g_post_mix"], wts["w_out"]]
    return pl.pallas_call(
        functools.partial(_attn_kernel, n_seq=n_seq, has_cache=has_cache),
        grid=(t // tq,),
        in_specs=in_specs,
        out_specs=row_spec(D_MODEL),
        out_shape=jax.ShapeDtypeStruct((t, D_MODEL), F32),
        scratch_shapes=[pltpu.VMEM((tq, D_MODEL), BF16)],
        compiler_params=_params(1),
        name="attn_latent" if has_cache else "attn_context",
    )(*args)


FF_CHUNK = 1024


def _ffn_kernel(x_ref, mod_ref, gpre_ref, w1_ref, w2_ref, gpost_ref, o_ref):
    x = x_ref[...]
    m = mod_ref[...]
    sh2 = m[:, 3 * D_MODEL:4 * D_MODEL]
    sc2 = m[:, 4 * D_MODEL:5 * D_MODEL]
    ga2 = m[:, 5 * D_MODEL:6 * D_MODEL]
    h2 = (_rms(x, gpre_ref[...]) * (1.0 + sc2) + sh2).astype(BF16)
    f = None
    for j in range(D_FF // FF_CHUNK):
        cols = slice(j * FF_CHUNK, (j + 1) * FF_CHUNK)
        a = jnp.square(jnp.maximum(_dot(h2, w1_ref[:, cols]), 0.0)).astype(BF16)
        part = _dot(a, w2_ref[cols, :])
        f = part if f is None else f + part
    o_ref[...] = x + ga2 * _rms(f, gpost_ref[...])


def _ffn(x, mods, wts, l, *, tm):
    t = x.shape[0]
    tiles_per_mod = (t // mods.shape[0]) // tm
    row_spec = pl.BlockSpec((tm, D_MODEL), lambda i: (i, 0))
    return pl.pallas_call(
        _ffn_kernel,
        grid=(t // tm,),
        in_specs=[
            row_spec,
            pl.BlockSpec((None, 1, N_MOD * D_MODEL), lambda i: (i // tiles_per_mod, 0, 0)),
            _layer_spec(wts["g_pre_ffn"].shape)(l),
            _layer_spec(wts["w_ff1"].shape)(l),
            _layer_spec(wts["w_ff2"].shape)(l),
            _layer_spec(wts["g_post_ffn"].shape)(l),
        ],
        out_specs=row_spec,
        out_shape=jax.ShapeDtypeStruct((t, D_MODEL), F32),
        compiler_params=_params(1),
        name="ffn",
    )(x, mods, wts["g_pre_ffn"], wts["w_ff1"], wts["w_ff2"], wts["g_post_ffn"])


def _rope_tables(n_tokens):
    rows = n_tokens // GRID_W
    row = jnp.repeat(jnp.arange(rows, dtype=F32), GRID_W)
    col = jnp.tile(jnp.arange(GRID_W, dtype=F32), rows)
    nf = QK_ROPE // 4
    inv = ROPE_THETA ** (-jnp.arange(nf, dtype=F32) / nf)
    ang_r = row[:, None] * inv
    ang_c = col[:, None] * inv
    zeros = jnp.zeros((n_tokens, LANES - QK_ROPE), F32)
    cos = jnp.concatenate([jnp.cos(ang_r), jnp.cos(ang_r), jnp.cos(ang_c), jnp.cos(ang_c), zeros], axis=1)
    sin = jnp.concatenate([-jnp.sin(ang_r), jnp.sin(ang_r), -jnp.sin(ang_c), jnp.sin(ang_c), zeros], axis=1)
    return cos, sin


def _prepare_weights(w_in, w_uq, w_ukv, w_s, b_s, w_out, w_ff1, w_ff2, gains):
    o_kr = Q_LORA + KV_LORA
    w_in_r = jnp.concatenate(
        [w_in[..., :o_kr], w_in[..., o_kr + QK_ROPE:], w_in[..., o_kr:o_kr + QK_ROPE],
         jnp.zeros(w_in.shape[:-1] + (LANES - QK_ROPE,), w_in.dtype)], axis=-1).astype(BF16)
    w_q = jnp.pad(w_uq, ((0, 0), (0, 0), (0, 0), (0, QK_PAD - QK_NOPE - QK_ROPE)))
    w_q = w_q.reshape(DEPTH, Q_LORA, QK_WIDTH).astype(BF16)
    w_kv = jnp.concatenate(
        [w_ukv[..., :QK_NOPE].reshape(DEPTH, KV_LORA, N_HEADS * QK_NOPE),
         w_ukv[..., QK_NOPE:].reshape(DEPTH, KV_LORA, N_HEADS * V_HEAD)], axis=-1).astype(BF16)
    wts = {
        "w_in": w_in_r, "w_q": w_q, "w_kv": w_kv,
        "w_s": w_s.reshape(DEPTH, N_HEADS_B * CHUNK, CHUNK).astype(BF16),
        "b_s": jnp.repeat(jnp.swapaxes(b_s, 1, 2), HEAD_B, axis=-1),
        "w_out": w_out.astype(BF16), "w_ff1": w_ff1.astype(BF16), "w_ff2": w_ff2.astype(BF16),
    }
    for name, g in gains.items():
        wts[name] = g.reshape(DEPTH, 1, g.shape[-1])
    return wts


def kernel(x_prompt, x_sample, cache_ckv, cache_krope, c, c_ctx, w_ada, b_ada, g_pre_mix, w_in, g_q, w_uq, g_kv, w_ukv, g_v, w_s, b_s, w_conv, w_out, g_post_mix, g_pre_ffn, w_ff1, w_ff2, g_post_ffn):
    batch, seq, _ = x_prompt.shape
    dec_batch, dec_seq, _ = x_sample.shape

    wts = _prepare_weights(
        w_in, w_uq, w_ukv, w_s, b_s, w_out, w_ff1, w_ff2,
        {"g_pre_mix": g_pre_mix, "g_q": g_q, "g_kv": g_kv, "g_v": g_v,
         "g_post_mix": g_post_mix, "g_pre_ffn": g_pre_ffn, "g_post_ffn": g_post_ffn})
    wts["w_conv"] = w_conv

    cond = jnp.concatenate(
        [c_ctx[None, :], c, jnp.zeros((COND_ROWS - 1 - dec_batch, D_MODEL), F32)], axis=0)
    mods = _modulations(cond, w_ada, b_ada)
    mods_p = mods[:, 0:1, :].reshape(DEPTH, 1, 1, N_MOD * D_MODEL)
    mods_s = mods[:, 1:1 + dec_batch, :].reshape(DEPTH, dec_batch, 1, N_MOD * D_MODEL)

    krope_pad = jnp.pad(cache_krope, ((0, 0), (0, 0), (0, 0), (0, LANES - QK_ROPE)))
    cache = _cache_kv(cache_ckv, krope_pad, wts["w_kv"])
    rope_tabs = _rope_tables(dec_seq)

    xp = x_prompt.reshape(batch * seq, D_MODEL)
    xs = x_sample.reshape(dec_batch * dec_seq, D_MODEL)
    ckv_list, kr_list = [], []
    for l in range(DEPTH):
        q, k, v, obc, ckv_l, kr_l = _pre(
            xp, mods_p[l], wts, l, seq_len=seq, tm=2 * seq, rope_tabs=None, emit_cache=True)
        ckv_list.append(ckv_l.reshape(batch, seq, KV_LORA))
        kr_list.append(kr_l.reshape(batch, seq, QK_ROPE))
        xp = _attn(q, k, v, obc, xp, mods_p[l], wts, l, tq=2 * seq, keys_per_seq=seq, cache=None)
        xp = _ffn(xp, mods_p[l], wts, l, tm=512)

        q, k, v, obc = _pre(
            xs, mods_s[l], wts, l, seq_len=dec_seq, tm=dec_seq, rope_tabs=rope_tabs, emit_cache=False)
        xs = _attn(q, k, v, obc, xs, mods_s[l], wts, l, tq=512, keys_per_seq=dec_seq, cache=cache)
        xs = _ffn(xs, mods_s[l], wts, l, tm=512)

    return (xp.reshape(batch, seq, D_MODEL), xs.reshape(dec_batch, dec_seq, D_MODEL),
            jnp.stack(ckv_list, axis=1), jnp.stack(kr_list, axis=1))
```

```python
import functools
import math

import jax
import jax.numpy as jnp
from jax import lax
from jax.experimental import pallas as pl
from jax.experimental.pallas import tpu as pltpu

F32 = jnp.float32
BF16 = jnp.bfloat16

D_MODEL = 1024
DEPTH = 4
GRID_W = 64
N_HEADS = 4
QK_NOPE = 128
QK_ROPE = 64
V_HEAD = 128
Q_LORA = 384
KV_LORA = 256
WIDTH_A = N_HEADS * V_HEAD
ROPE_THETA = 10000.0
WIDTH_B = 256
N_HEADS_B = 4
HEAD_B = WIDTH_B // N_HEADS_B
CHUNK = 128
WIDTH_C = 256
D_FF = 4 * D_MODEL
N_MOD = 6
EPS = 1e-6

LANES = 128
QK_PAD = 2 * LANES
QK_WIDTH = N_HEADS * QK_PAD
KV_COLS = N_HEADS * (QK_NOPE + V_HEAD)
BC_WIDTH = WIDTH_B + WIDTH_C
OFF_Q = 0
OFF_CKV = OFF_Q + Q_LORA
OFF_U = OFF_CKV + KV_LORA
OFF_V = OFF_U + WIDTH_B
OFF_BG = OFF_V + WIDTH_B
OFF_CG = OFF_BG + WIDTH_C
OFF_HH = OFF_CG + WIDTH_C
OFF_KR = OFF_HH + WIDTH_C
IN_COLS_PAD = OFF_KR + LANES
COND_ROWS = 8
VMEM_LIMIT = 56 * 1024 * 1024
SM_SCALE = 1.0 / math.sqrt(QK_NOPE + QK_ROPE)
FF_CHUNK = 1024

PRE_WEIGHTS = ("g_pre_mix", "w_in", "g_q", "w_q", "g_kv", "w_kv", "g_v", "w_s", "b_s", "w_conv")
POST_WEIGHTS = ("g_post_mix", "w_out", "g_pre_ffn", "w_ff1", "w_ff2", "g_post_ffn")


def _rms(x, g):
    return x * lax.rsqrt(jnp.mean(x * x, axis=-1, keepdims=True) + EPS) * g


def _dot(a, b):
    return jnp.dot(a, b, preferred_element_type=F32)


def _dot_nt(a, b):
    return lax.dot_general(a, b, (((1,), (1,)), ((), ())), preferred_element_type=F32)


def _params(n_axes):
    return pltpu.CompilerParams(
        dimension_semantics=("parallel",) * n_axes, vmem_limit_bytes=VMEM_LIMIT)


def _layer_spec(shape, l):
    nd = len(shape)
    return pl.BlockSpec((None,) + tuple(shape[1:]), lambda *_: (l,) + (0,) * (nd - 1),
                        pipeline_mode=pl.Buffered(1))


def _mod_spec(tiles_per_mod):
    return pl.BlockSpec((None, 1, N_MOD * D_MODEL), lambda i: (i // tiles_per_mod, 0, 0))


def _mod_kernel(cond_ref, w_ref, b_ref, o_ref):
    c = cond_ref[...]
    s = c / (1.0 + jnp.exp(-c))
    o_ref[...] = _dot(s.astype(BF16), w_ref[...].astype(BF16)) + b_ref[...]


def _modulations(cond, w_ada, b_ada):
    tn = 1536
    n_cols = N_MOD * D_MODEL
    return pl.pallas_call(
        _mod_kernel,
        grid=(DEPTH, n_cols // tn),
        in_specs=[
            pl.BlockSpec((COND_ROWS, D_MODEL), lambda l, j: (0, 0)),
            pl.BlockSpec((None, D_MODEL, tn), lambda l, j: (l, 0, j)),
            pl.BlockSpec((None, 1, tn), lambda l, j: (l, 0, j)),
        ],
        out_specs=pl.BlockSpec((None, COND_ROWS, tn), lambda l, j: (l, 0, j)),
        out_shape=jax.ShapeDtypeStruct((DEPTH, COND_ROWS, n_cols), F32),
        compiler_params=_params(2),
        name="modulation",
    )(cond, w_ada, b_ada.reshape(DEPTH, 1, n_cols))


def _cache_kv_kernel(ckv_ref, kr_ref, wkv_ref, k_ref, v_ref):
    kv = _dot(ckv_ref[...].astype(BF16), wkv_ref[...])
    kr = kr_ref[...].astype(BF16)
    for h in range(N_HEADS):
        k_ref[:, h * QK_PAD:h * QK_PAD + QK_NOPE] = kv[:, h * QK_NOPE:(h + 1) * QK_NOPE].astype(BF16)
        k_ref[:, h * QK_PAD + QK_NOPE:(h + 1) * QK_PAD] = kr
    v_ref[...] = kv[:, N_HEADS * QK_NOPE:].astype(BF16)


def _cache_kv(cache_ckv, cache_krope_pad, w_kv):
    nb, _, past, _ = cache_ckv.shape
    return pl.pallas_call(
        _cache_kv_kernel,
        grid=(DEPTH, nb),
        in_specs=[
            pl.BlockSpec((None, None, past, KV_LORA), lambda l, b: (b, l, 0, 0)),
            pl.BlockSpec((None, None, past, LANES), lambda l, b: (b, l, 0, 0)),
            pl.BlockSpec((None, KV_LORA, KV_COLS), lambda l, b: (l, 0, 0)),
        ],
        out_specs=[
            pl.BlockSpec((None, None, past, QK_WIDTH), lambda l, b: (l, b, 0, 0)),
            pl.BlockSpec((None, None, past, WIDTH_A), lambda l, b: (l, b, 0, 0)),
        ],
        out_shape=[
            jax.ShapeDtypeStruct((DEPTH, nb, past, QK_WIDTH), BF16),
            jax.ShapeDtypeStruct((DEPTH, nb, past, WIDTH_A), BF16),
        ],
        compiler_params=_params(2),
        name="cache_kv",
    )(cache_ckv, cache_krope_pad, w_kv)


def _swap_halves(x):
    lane = lax.broadcasted_iota(jnp.int32, x.shape, 1)
    quarter = QK_ROPE // 4
    first_half = (lane & (2 * quarter - 1)) < quarter
    return jnp.where(first_half, pltpu.roll(x, LANES - quarter, 1), pltpu.roll(x, quarter, 1))


def _pre_stage(x, mod, w, rope, seq_len, q_ref, k_ref, v_ref, ob_ref, oc_ref, cache_refs):
    tm = x.shape[0]
    sh1 = mod[:, 0:D_MODEL]
    sc1 = mod[:, D_MODEL:2 * D_MODEL]
    h = _rms(x, w["g_pre_mix"][...]) * (1.0 + sc1) + sh1
    z = _dot(h.astype(BF16), w["w_in"][...])

    ckv = _rms(z[:, OFF_CKV:OFF_CKV + KV_LORA], w["g_kv"][...])
    krz = z[:, OFF_KR:OFF_KR + LANES]
    if cache_refs is not None:
        ckv_ref, kr_ref = cache_refs
        ckv_ref[...] = ckv.reshape(ckv_ref.shape)
        kr_ref[...] = krz[:, 0:QK_ROPE].reshape(kr_ref.shape)
    qn = _rms(z[:, OFF_Q:OFF_Q + Q_LORA], w["g_q"][...])
    q = _dot(qn.astype(BF16), w["w_q"][...]) * SM_SCALE
    kv = _dot(ckv.astype(BF16), w["w_kv"][...])
    if rope is not None:
        cos, sin = rope
        krz = krz * cos + _swap_halves(krz) * sin
    krz = krz.astype(BF16)
    for hd in range(N_HEADS):
        lo = hd * QK_PAD
        q_ref[:, lo:lo + QK_NOPE] = q[:, lo:lo + QK_NOPE].astype(BF16)
        qr = q[:, lo + QK_NOPE:lo + QK_PAD]
        if rope is not None:
            qr = qr * cos + _swap_halves(qr) * sin
        q_ref[:, lo + QK_NOPE:lo + QK_PAD] = qr.astype(BF16)
        k_ref[:, lo:lo + QK_NOPE] = kv[:, hd * QK_NOPE:(hd + 1) * QK_NOPE].astype(BF16)
        k_ref[:, lo + QK_NOPE:lo + QK_PAD] = krz
    v_ref[...] = kv[:, N_HEADS * QK_NOPE:].astype(BF16)

    u = jax.nn.gelu(z[:, OFF_U:OFF_U + WIDTH_B])
    vn = _rms(jax.nn.gelu(z[:, OFF_V:OFF_V + WIDTH_B]), w["g_v"][...]).astype(BF16)
    lane = lax.broadcasted_iota(jnp.int32, (CHUNK, WIDTH_B), 1)
    ws = w["w_s"][...]
    bs = w["b_s"][...]
    for c in range(tm // CHUNK):
        rows = slice(c * CHUNK, (c + 1) * CHUNK)
        r = _dot(ws, vn[rows, :])
        mixed = r[(N_HEADS_B - 1) * CHUNK:, :]
        for hb in range(N_HEADS_B - 2, -1, -1):
            mixed = jnp.where(lane < (hb + 1) * HEAD_B, r[hb * CHUNK:(hb + 1) * CHUNK, :], mixed)
        ob_ref[rows, :] = (u[rows, :] * (mixed + bs)).astype(BF16)

    zc = z[:, OFF_CG:OFF_CG + WIDTH_C] * z[:, OFF_HH:OFF_HH + WIDTH_C]
    pos = lax.broadcasted_iota(jnp.int32, (tm, WIDTH_C), 0) & (seq_len - 1)
    z_prev = jnp.where(pos == 0, 0.0, pltpu.roll(zc, 1, 0))
    z_next = jnp.where(pos == seq_len - 1, 0.0, pltpu.roll(zc, tm - 1, 0))
    wc = w["w_conv"][...]
    y = z_prev * wc[0:1, :] + zc * wc[1:2, :] + z_next * wc[2:3, :]
    oc_ref[...] = (z[:, OFF_BG:OFF_BG + WIDTH_C] * y).astype(BF16)


def _attn_stage(q_ref, k_ref, v_ref, cache, n_seq, oa_ref):
    ql = q_ref.shape[0] // n_seq
    kl = k_ref.shape[0] // n_seq
    for s in range(n_seq):
        qrows = slice(s * ql, (s + 1) * ql)
        krows = slice(s * kl, (s + 1) * kl)
        for hd in range(N_HEADS):
            qk_cols = slice(hd * QK_PAD, (hd + 1) * QK_PAD)
            v_cols = slice(hd * V_HEAD, (hd + 1) * V_HEAD)
            qh = q_ref[qrows, qk_cols]
            s_lat = _dot_nt(qh, k_ref[krows, qk_cols])
            mx = jnp.max(s_lat, axis=-1, keepdims=True)
            if cache is not None:
                kc_ref, vc_ref = cache
                s_ctx = _dot_nt(qh, kc_ref[:, qk_cols])
                mx = jnp.maximum(mx, jnp.max(s_ctx, axis=-1, keepdims=True))
            p_lat = jnp.exp(s_lat - mx)
            den = jnp.sum(p_lat, axis=-1, keepdims=True)
            o = _dot(p_lat.astype(BF16), v_ref[krows, v_cols])
            if cache is not None:
                p_ctx = jnp.exp(s_ctx - mx)
                den = den + jnp.sum(p_ctx, axis=-1, keepdims=True)
                o = o + _dot(p_ctx.astype(BF16), vc_ref[:, v_cols])
            oa_ref[qrows, v_cols] = (o * (1.0 / den)).astype(BF16)


def _post_stage(x, mod, mix_ref, w):
    ga1 = mod[:, 2 * D_MODEL:3 * D_MODEL]
    sh2 = mod[:, 3 * D_MODEL:4 * D_MODEL]
    sc2 = mod[:, 4 * D_MODEL:5 * D_MODEL]
    ga2 = mod[:, 5 * D_MODEL:6 * D_MODEL]
    x = x + ga1 * _rms(_dot(mix_ref[...], w["w_out"][...]), w["g_post_mix"][...])
    h2 = (_rms(x, w["g_pre_ffn"][...]) * (1.0 + sc2) + sh2).astype(BF16)
    f = None
    for j in range(D_FF // FF_CHUNK):
        cols = slice(j * FF_CHUNK, (j + 1) * FF_CHUNK)
        a = jnp.square(jnp.maximum(_dot(h2, w["w_ff1"][:, cols]), 0.0)).astype(BF16)
        part = _dot(a, w["w_ff2"][cols, :])
        f = part if f is None else f + part
    return x + ga2 * _rms(f, w["g_post_ffn"][...])


def _context_kernel(*refs, seq_len, aliased):
    n_in = 2 + len(PRE_WEIGHTS) + len(POST_WEIGHTS) + (2 if aliased else 0)
    x_ref, mod_ref = refs[:2]
    wrefs = refs[2:2 + len(PRE_WEIGHTS) + len(POST_WEIGHTS)]
    w = dict(zip(PRE_WEIGHTS + POST_WEIGHTS, wrefs))
    o_ref, ckv_ref, kr_ref = refs[n_in:n_in + 3]
    q_ref, k_ref, v_ref, mix_ref = refs[n_in + 3:]
    x = x_ref[...]
    mod = mod_ref[...]
    _pre_stage(x, mod, w, None, seq_len, q_ref, k_ref, v_ref,
               mix_ref.at[:, WIDTH_A:WIDTH_A + WIDTH_B], mix_ref.at[:, WIDTH_A + WIDTH_B:],
               (ckv_ref, kr_ref))
    _attn_stage(q_ref, k_ref, v_ref, None, x.shape[0] // seq_len, mix_ref.at[:, 0:WIDTH_A])
    o_ref[...] = _post_stage(x, mod, mix_ref, w)


def _context_layer(x, mods, wts, l, new_ckv, new_kr, *, seq_len, seqs_per_tile):
    t = x.shape[0]
    tm = seq_len * seqs_per_tile
    n_seq_total = t // seq_len
    aliased = new_ckv is not None
    row_spec = pl.BlockSpec((tm, D_MODEL), lambda i: (i, 0))
    ckv_spec = pl.BlockSpec((seqs_per_tile, None, seq_len, KV_LORA), lambda i: (i, l, 0, 0))
    kr_spec = pl.BlockSpec((seqs_per_tile, None, seq_len, QK_ROPE), lambda i: (i, l, 0, 0))
    in_specs = [row_spec, _mod_spec(t // tm)]
    args = [x, mods]
    for n in PRE_WEIGHTS + POST_WEIGHTS:
        in_specs.append(_layer_spec(wts[n].shape, l))
        args.append(wts[n])
    aliases = {}
    if aliased:
        in_specs += [pl.BlockSpec(memory_space=pl.ANY)] * 2
        aliases = {len(args): 1, len(args) + 1: 2}
        args += [new_ckv, new_kr]
    return pl.pallas_call(
        functools.partial(_context_kernel, seq_len=seq_len, aliased=aliased),
        grid=(t // tm,),
        in_specs=in_specs,
        out_specs=[row_spec, ckv_spec, kr_spec],
        out_shape=[
            jax.ShapeDtypeStruct((t, D_MODEL), F32),
            jax.ShapeDtypeStruct((n_seq_total, DEPTH, seq_len, KV_LORA), F32),
            jax.ShapeDtypeStruct((n_seq_total, DEPTH, seq_len, QK_ROPE), F32),
        ],
        scratch_shapes=[
            pltpu.VMEM((tm, QK_WIDTH), BF16), pltpu.VMEM((tm, QK_WIDTH), BF16),
            pltpu.VMEM((tm, WIDTH_A), BF16), pltpu.VMEM((tm, D_MODEL), BF16),
        ],
        input_output_aliases=aliases,
        compiler_params=_params(1),
        name="context_layer",
    )(*args)


def _latent_pre_kernel(*refs, seq_len):
    x_ref, mod_ref = refs[:2]
    w = dict(zip(PRE_WEIGHTS, refs[2:2 + len(PRE_WEIGHTS)]))
    cos_ref, sin_ref, q_ref, k_ref, v_ref, obc_ref = refs[2 + len(PRE_WEIGHTS):]
    _pre_stage(x_ref[...], mod_ref[...], w, (cos_ref[...], sin_ref[...]), seq_len,
               q_ref, k_ref, v_ref, obc_ref.at[:, 0:WIDTH_B], obc_ref.at[:, WIDTH_B:], None)


def _latent_pre(x, mods, wts, l, rope_tabs, *, seq_len):
    t = x.shape[0]
    tm = seq_len
    row_spec = lambda w: pl.BlockSpec((tm, w), lambda i: (i, 0))
    in_specs = [row_spec(D_MODEL), _mod_spec(1)]
    args = [x, mods]
    for n in PRE_WEIGHTS:
        in_specs.append(_layer_spec(wts[n].shape, l))
        args.append(wts[n])
    in_specs += [pl.BlockSpec((tm, LANES), lambda i: (0, 0), pipeline_mode=pl.Buffered(1))] * 2
    args += list(rope_tabs)
    return pl.pallas_call(
        functools.partial(_latent_pre_kernel, seq_len=seq_len),
        grid=(t // tm,),
        in_specs=in_specs,
        out_specs=[row_spec(QK_WIDTH), row_spec(QK_WIDTH), row_spec(WIDTH_A), row_spec(BC_WIDTH)],
        out_shape=[
            jax.ShapeDtypeStruct((t, QK_WIDTH), BF16),
            jax.ShapeDtypeStruct((t, QK_WIDTH), BF16),
            jax.ShapeDtypeStruct((t, WIDTH_A), BF16),
            jax.ShapeDtypeStruct((t, BC_WIDTH), BF16),
        ],
        compiler_params=_params(1),
        name="latent_pre",
    )(*args)


def _latent_post_kernel(*refs):
    q_ref, k_ref, v_ref, kc_ref, vc_ref, obc_ref, x_ref, mod_ref = refs[:8]
    w = dict(zip(POST_WEIGHTS, refs[8:8 + len(POST_WEIGHTS)]))
    o_ref, mix_ref = refs[8 + len(POST_WEIGHTS):]
    _attn_stage(q_ref, k_ref, v_ref, (kc_ref, vc_ref), 1, mix_ref.at[:, 0:WIDTH_A])
    mix_ref[:, WIDTH_A:] = obc_ref[...]
    o_ref[...] = _post_stage(x_ref[...], mod_ref[...], mix_ref, w)


def _latent_post(q, k, v, obc, x, mods, cache, wts, l, *, seq_len, tq):
    t = x.shape[0]
    tiles_per_seq = seq_len // tq
    kc, vc = cache
    past = kc.shape[2]
    seq_of = lambda i: i // tiles_per_seq
    row_spec = lambda w: pl.BlockSpec((tq, w), lambda i: (i, 0))
    in_specs = [
        row_spec(QK_WIDTH),
        pl.BlockSpec((seq_len, QK_WIDTH), lambda i: (seq_of(i), 0)),
        pl.BlockSpec((seq_len, WIDTH_A), lambda i: (seq_of(i), 0)),
        pl.BlockSpec((None, None, past, QK_WIDTH), lambda i: (l, seq_of(i), 0, 0)),
        pl.BlockSpec((None, None, past, WIDTH_A), lambda i: (l, seq_of(i), 0, 0)),
        row_spec(BC_WIDTH),
        row_spec(D_MODEL),
        _mod_spec(tiles_per_seq),
    ]
    args = [q, k, v, kc, vc, obc, x, mods]
    for n in POST_WEIGHTS:
        in_specs.append(_layer_spec(wts[n].shape, l))
        args.append(wts[n])
    return pl.pallas_call(
        _latent_post_kernel,
        grid=(t // tq,),
        in_specs=in_specs,
        out_specs=row_spec(D_MODEL),
        out_shape=jax.ShapeDtypeStruct((t, D_MODEL), F32),
        scratch_shapes=[pltpu.VMEM((tq, D_MODEL), BF16)],
        compiler_params=_params(1),
        name="latent_post",
    )(*args)


def _rope_tables(n_tokens):
    rows = n_tokens // GRID_W
    row = jnp.repeat(jnp.arange(rows, dtype=F32), GRID_W)
    col = jnp.tile(jnp.arange(GRID_W, dtype=F32), rows)
    nf = QK_ROPE // 4
    inv = ROPE_THETA ** (-jnp.arange(nf, dtype=F32) / nf)
    ang_r = row[:, None] * inv
    ang_c = col[:, None] * inv
    zeros = jnp.zeros((n_tokens, LANES - QK_ROPE), F32)
    cos = jnp.concatenate([jnp.cos(ang_r), jnp.cos(ang_r), jnp.cos(ang_c), jnp.cos(ang_c), zeros], axis=1)
    sin = jnp.concatenate([-jnp.sin(ang_r), jnp.sin(ang_r), -jnp.sin(ang_c), jnp.sin(ang_c), zeros], axis=1)
    return cos, sin


def _prepare_weights(w_in, w_uq, w_ukv, w_s, b_s, w_conv, w_out, w_ff1, w_ff2, gains):
    o_kr = Q_LORA + KV_LORA
    w_in_r = jnp.concatenate(
        [w_in[..., :o_kr], w_in[..., o_kr + QK_ROPE:], w_in[..., o_kr:o_kr + QK_ROPE],
         jnp.zeros(w_in.shape[:-1] + (LANES - QK_ROPE,), w_in.dtype)], axis=-1).astype(BF16)
    w_q = jnp.pad(w_uq, ((0, 0), (0, 0), (0, 0), (0, QK_PAD - QK_NOPE - QK_ROPE)))
    w_q = w_q.reshape(DEPTH, Q_LORA, QK_WIDTH).astype(BF16)
    w_kv = jnp.concatenate(
        [w_ukv[..., :QK_NOPE].reshape(DEPTH, KV_LORA, N_HEADS * QK_NOPE),
         w_ukv[..., QK_NOPE:].reshape(DEPTH, KV_LORA, N_HEADS * V_HEAD)], axis=-1).astype(BF16)
    wts = {
        "w_in": w_in_r, "w_q": w_q, "w_kv": w_kv,
        "w_s": w_s.reshape(DEPTH, N_HEADS_B * CHUNK, CHUNK).astype(BF16),
        "b_s": jnp.repeat(jnp.swapaxes(b_s, 1, 2), HEAD_B, axis=-1),
        "w_conv": w_conv,
        "w_out": w_out.astype(BF16), "w_ff1": w_ff1.astype(BF16), "w_ff2": w_ff2.astype(BF16),
    }
    for name, g in gains.items():
        wts[name] = g.reshape(DEPTH, 1, g.shape[-1])
    return wts


def kernel(x_prompt, x_sample, cache_ckv, cache_krope, c, c_ctx, w_ada, b_ada, g_pre_mix, w_in, g_q, w_uq, g_kv, w_ukv, g_v, w_s, b_s, w_conv, w_out, g_post_mix, g_pre_ffn, w_ff1, w_ff2, g_post_ffn):
    batch, seq, _ = x_prompt.shape
    dec_batch, dec_seq, _ = x_sample.shape

    wts = _prepare_weights(
        w_in, w_uq, w_ukv, w_s, b_s, w_conv, w_out, w_ff1, w_ff2,
        {"g_pre_mix": g_pre_mix, "g_q": g_q, "g_kv": g_kv, "g_v": g_v,
         "g_post_mix": g_post_mix, "g_pre_ffn": g_pre_ffn, "g_post_ffn": g_post_ffn})

    cond = jnp.concatenate(
        [c_ctx[None, :], c, jnp.zeros((COND_ROWS - 1 - dec_batch, D_MODEL), F32)], axis=0)
    mods = _modulations(cond, w_ada, b_ada)
    mods_p = mods[:, 0:1, :].reshape(DEPTH, 1, 1, N_MOD * D_MODEL)
    mods_s = mods[:, 1:1 + dec_batch, :].reshape(DEPTH, dec_batch, 1, N_MOD * D_MODEL)

    krope_pad = jnp.pad(cache_krope, ((0, 0), (0, 0), (0, 0), (0, LANES - QK_ROPE)))
    cache = _cache_kv(cache_ckv, krope_pad, wts["w_kv"])
    rope_tabs = _rope_tables(dec_seq)

    xp = x_prompt.reshape(batch * seq, D_MODEL)
    xs = x_sample.reshape(dec_batch * dec_seq, D_MODEL)
    new_ckv = new_kr = None
    for l in range(DEPTH):
        xp, new_ckv, new_kr = _context_layer(
            xp, mods_p[l], wts, l, new_ckv, new_kr, seq_len=seq, seqs_per_tile=2)
        q, k, v, obc = _latent_pre(xs, mods_s[l], wts, l, rope_tabs, seq_len=dec_seq)
        xs = _latent_post(q, k, v, obc, xs, mods_s[l], cache, wts, l, seq_len=dec_seq, tq=512)

    return (xp.reshape(batch, seq, D_MODEL), xs.reshape(dec_batch, dec_seq, D_MODEL), new_ckv, new_kr)
```

```python
import functools
import math

import jax
import jax.numpy as jnp
import numpy as np
from jax import lax
from jax.experimental import pallas as pl
from jax.experimental.pallas import tpu as pltpu

F32 = jnp.float32
BF16 = jnp.bfloat16

D_MODEL = 1024
DEPTH = 4
GRID_W = 64
N_HEADS = 4
QK_NOPE = 128
QK_ROPE = 64
V_HEAD = 128
Q_LORA = 384
KV_LORA = 256
WIDTH_A = N_HEADS * V_HEAD
ROPE_THETA = 10000.0
WIDTH_B = 256
N_HEADS_B = 4
HEAD_B = WIDTH_B // N_HEADS_B
CHUNK = 128
WIDTH_C = 256
D_FF = 4 * D_MODEL
N_MOD = 6
EPS = 1e-6

LANES = 128
QK_PAD = 2 * LANES
QK_WIDTH = N_HEADS * QK_PAD
KV_COLS = N_HEADS * (QK_NOPE + V_HEAD)
BC_WIDTH = WIDTH_B + WIDTH_C
OFF_Q = 0
OFF_CKV = OFF_Q + Q_LORA
OFF_U = OFF_CKV + KV_LORA
OFF_V = OFF_U + WIDTH_B
OFF_BG = OFF_V + WIDTH_B
OFF_CG = OFF_BG + WIDTH_C
OFF_HH = OFF_CG + WIDTH_C
OFF_KR = OFF_HH + WIDTH_C
IN_COLS_PAD = OFF_KR + LANES
COND_ROWS = 8
VMEM_LIMIT = 56 * 1024 * 1024
SM_SCALE = 1.0 / math.sqrt(QK_NOPE + QK_ROPE)
FF_CHUNK = 1024

PRE_WEIGHTS = ("g_pre_mix", "w_in", "g_q", "w_q", "g_kv", "w_kv", "g_v", "w_s", "b_s", "w_conv")
POST_WEIGHTS = ("g_post_mix", "w_out", "g_pre_ffn", "w_ff1", "w_ff2", "g_post_ffn")
BIG_WEIGHTS = ("w_in", "w_out", "w_ff1", "w_ff2")
BIG_SHAPES = ((D_MODEL, IN_COLS_PAD), (D_MODEL, D_MODEL), (D_MODEL, D_FF), (D_FF, D_MODEL))
CAST_STEPS = 16


def _rms(x, g):
    return x * lax.rsqrt(jnp.mean(x * x, axis=-1, keepdims=True) + EPS) * g


def _dot(a, b):
    return jnp.dot(a, b, preferred_element_type=F32)


def _dot_nt(a, b):
    return lax.dot_general(a, b, (((1,), (1,)), ((), ())), preferred_element_type=F32)


def _params(n_axes):
    return pltpu.CompilerParams(
        dimension_semantics=("parallel",) * n_axes, vmem_limit_bytes=VMEM_LIMIT)


def _layer_spec(shape, l):
    if len(shape) == 2:
        return pl.BlockSpec(tuple(shape), lambda *_: (0, 0), pipeline_mode=pl.Buffered(1))
    return pl.BlockSpec((None,) + tuple(shape[1:]), lambda *_: (l, 0, 0),
                        pipeline_mode=pl.Buffered(1))


def _mod_spec(tiles_per_mod):
    return pl.BlockSpec((None, 1, N_MOD * D_MODEL), lambda i: (i // tiles_per_mod, 0, 0))


def _mod_kernel(cond_ref, w_ref, b_ref, o_ref):
    c = cond_ref[...]
    s = c / (1.0 + jnp.exp(-c))
    o_ref[...] = _dot(s.astype(BF16), w_ref[...].astype(BF16)) + b_ref[...]


def _modulations(cond, w_ada, b_ada):
    tn = 1536
    n_cols = N_MOD * D_MODEL
    return pl.pallas_call(
        _mod_kernel,
        grid=(DEPTH, n_cols // tn),
        in_specs=[
            pl.BlockSpec((COND_ROWS, D_MODEL), lambda l, j: (0, 0)),
            pl.BlockSpec((None, D_MODEL, tn), lambda l, j: (l, 0, j)),
            pl.BlockSpec((None, 1, tn), lambda l, j: (l, 0, j)),
        ],
        out_specs=pl.BlockSpec((None, COND_ROWS, tn), lambda l, j: (l, 0, j)),
        out_shape=jax.ShapeDtypeStruct((DEPTH, COND_ROWS, n_cols), F32),
        compiler_params=_params(2),
        name="modulation",
    )(cond, w_ada, b_ada.reshape(DEPTH, 1, n_cols))


def _cache_kv_kernel(ckv_ref, kr_ref, wkv_ref, k_ref, v_ref):
    kv = _dot(ckv_ref[...].astype(BF16), wkv_ref[...])
    kr = kr_ref[...].astype(BF16)
    zeros = jnp.zeros((kr.shape[0], QK_PAD - QK_NOPE - QK_ROPE), BF16)
    for h in range(N_HEADS):
        lo = h * QK_PAD
        k_ref[:, lo:lo + QK_NOPE] = kv[:, h * QK_NOPE:(h + 1) * QK_NOPE].astype(BF16)
        k_ref[:, lo + QK_NOPE:lo + QK_NOPE + QK_ROPE] = kr
        k_ref[:, lo + QK_NOPE + QK_ROPE:lo + QK_PAD] = zeros
    v_ref[...] = kv[:, N_HEADS * QK_NOPE:].astype(BF16)


def _cache_kv(cache_ckv, cache_krope, w_kv):
    nb, _, past, _ = cache_ckv.shape
    return pl.pallas_call(
        _cache_kv_kernel,
        grid=(DEPTH, nb),
        in_specs=[
            pl.BlockSpec((None, None, past, KV_LORA), lambda l, b: (b, l, 0, 0)),
            pl.BlockSpec((None, None, past, QK_ROPE), lambda l, b: (b, l, 0, 0)),
            pl.BlockSpec((None, KV_LORA, KV_COLS), lambda l, b: (l, 0, 0)),
        ],
        out_specs=[
            pl.BlockSpec((None, None, past, QK_WIDTH), lambda l, b: (l, b, 0, 0)),
            pl.BlockSpec((None, None, past, WIDTH_A), lambda l, b: (l, b, 0, 0)),
        ],
        out_shape=[
            jax.ShapeDtypeStruct((DEPTH, nb, past, QK_WIDTH), BF16),
            jax.ShapeDtypeStruct((DEPTH, nb, past, WIDTH_A), BF16),
        ],
        compiler_params=_params(2),
        name="cache_kv",
    )(cache_ckv, cache_krope, w_kv)


def _cast_stage(in_refs, out_refs):
    win, wout, w1, w2 = in_refs
    win_o, wout_o, w1_o, w2_o = out_refs
    o_kr = Q_LORA + KV_LORA
    a = win[...]
    win_o[:, 0:o_kr] = a[:, 0:o_kr].astype(BF16)
    win_o[:, o_kr:OFF_KR] = a[:, o_kr + QK_ROPE:].astype(BF16)
    win_o[:, OFF_KR:OFF_KR + QK_ROPE] = a[:, o_kr:o_kr + QK_ROPE].astype(BF16)
    win_o[:, OFF_KR + QK_ROPE:] = jnp.zeros((a.shape[0], LANES - QK_ROPE), BF16)
    wout_o[...] = wout[...].astype(BF16)
    w1_o[...] = w1[...].astype(BF16)
    w2_o[...] = w2[...].astype(BF16)


def _cast_specs(raw, l):
    in_specs, out_specs, out_shape = [], [], []
    for a, (rows, cols) in zip(raw, BIG_SHAPES):
        chunk = rows // CAST_STEPS
        in_specs.append(pl.BlockSpec((None, chunk, a.shape[2]), lambda i: (l, i, 0)))
        out_specs.append(pl.BlockSpec((chunk, cols), lambda i: (i, 0)))
        out_shape.append(jax.ShapeDtypeStruct((rows, cols), BF16))
    return in_specs, out_specs, out_shape


def _cast_kernel(*refs):
    _cast_stage(refs[:4], refs[4:])


def _cast_layer(raw, l):
    in_specs, out_specs, out_shape = _cast_specs(raw, l)
    return pl.pallas_call(
        _cast_kernel,
        grid=(CAST_STEPS,),
        in_specs=in_specs,
        out_specs=out_specs,
        out_shape=out_shape,
        compiler_params=_params(1),
        name="cast_weights",
    )(*raw)


def _swap_halves(x):
    lane = lax.broadcasted_iota(jnp.int32, x.shape, 1)
    quarter = QK_ROPE // 4
    first_half = (lane & (2 * quarter - 1)) < quarter
    return jnp.where(first_half, pltpu.roll(x, LANES - quarter, 1), pltpu.roll(x, quarter, 1))


def _pre_stage(x, mod, w, rope, seq_len, q_ref, k_ref, v_ref, ob_ref, oc_ref, cache_refs):
    tm = x.shape[0]
    sh1 = mod[:, 0:D_MODEL]
    sc1 = mod[:, D_MODEL:2 * D_MODEL]
    h = _rms(x, w["g_pre_mix"][...]) * (1.0 + sc1) + sh1
    z = _dot(h.astype(BF16), w["w_in"][...])

    ckv = _rms(z[:, OFF_CKV:OFF_CKV + KV_LORA], w["g_kv"][...])
    krz = z[:, OFF_KR:OFF_KR + LANES]
    if cache_refs is not None:
        ckv_ref, kr_ref = cache_refs
        ckv_ref[...] = ckv.reshape(ckv_ref.shape)
        kr_ref[...] = krz[:, 0:QK_ROPE].reshape(kr_ref.shape)
    qn = _rms(z[:, OFF_Q:OFF_Q + Q_LORA], w["g_q"][...])
    q = _dot(qn.astype(BF16), w["w_q"][...]) * SM_SCALE
    kv = _dot(ckv.astype(BF16), w["w_kv"][...])
    if rope is not None:
        cos, sin = rope
        krz = krz * cos + _swap_halves(krz) * sin
    krz = krz.astype(BF16)
    for hd in range(N_HEADS):
        lo = hd * QK_PAD
        q_ref[:, lo:lo + QK_NOPE] = q[:, lo:lo + QK_NOPE].astype(BF16)
        qr = q[:, lo + QK_NOPE:lo + QK_PAD]
        if rope is not None:
            qr = qr * cos + _swap_halves(qr) * sin
        q_ref[:, lo + QK_NOPE:lo + QK_PAD] = qr.astype(BF16)
        k_ref[:, lo:lo + QK_NOPE] = kv[:, hd * QK_NOPE:(hd + 1) * QK_NOPE].astype(BF16)
        k_ref[:, lo + QK_NOPE:lo + QK_PAD] = krz
    v_ref[...] = kv[:, N_HEADS * QK_NOPE:].astype(BF16)

    u = jax.nn.gelu(z[:, OFF_U:OFF_U + WIDTH_B])
    vn = _rms(jax.nn.gelu(z[:, OFF_V:OFF_V + WIDTH_B]), w["g_v"][...]).astype(BF16)
    lane = lax.broadcasted_iota(jnp.int32, (CHUNK, WIDTH_B), 1)
    ws = w["w_s"][...]
    bs = w["b_s"][...]
    for c in range(tm // CHUNK):
        rows = slice(c * CHUNK, (c + 1) * CHUNK)
        r = _dot(ws, vn[rows, :])
        mixed = r[(N_HEADS_B - 1) * CHUNK:, :]
        for hb in range(N_HEADS_B - 2, -1, -1):
            mixed = jnp.where(lane < (hb + 1) * HEAD_B, r[hb * CHUNK:(hb + 1) * CHUNK, :], mixed)
        ob_ref[rows, :] = (u[rows, :] * (mixed + bs)).astype(BF16)

    zc = z[:, OFF_CG:OFF_CG + WIDTH_C] * z[:, OFF_HH:OFF_HH + WIDTH_C]
    pos = lax.broadcasted_iota(jnp.int32, (tm, WIDTH_C), 0) & (seq_len - 1)
    z_prev = jnp.where(pos == 0, 0.0, pltpu.roll(zc, 1, 0))
    z_next = jnp.where(pos == seq_len - 1, 0.0, pltpu.roll(zc, tm - 1, 0))
    wc = w["w_conv"][...]
    y = z_prev * wc[0:1, :] + zc * wc[1:2, :] + z_next * wc[2:3, :]
    oc_ref[...] = (z[:, OFF_BG:OFF_BG + WIDTH_C] * y).astype(BF16)


def _attn_stage(q_ref, k_ref, v_ref, cache, n_seq, oa_ref):
    ql = q_ref.shape[0] // n_seq
    kl = k_ref.shape[0] // n_seq
    for s in range(n_seq):
        qrows = slice(s * ql, (s + 1) * ql)
        krows = slice(s * kl, (s + 1) * kl)
        for hd in range(N_HEADS):
            qk_cols = slice(hd * QK_PAD, (hd + 1) * QK_PAD)
            v_cols = slice(hd * V_HEAD, (hd + 1) * V_HEAD)
            qh = q_ref[qrows, qk_cols]
            s_lat = _dot_nt(qh, k_ref[krows, qk_cols])
            mx = jnp.max(s_lat, axis=-1, keepdims=True)
            if cache is not None:
                kc_ref, vc_ref = cache
                s_ctx = _dot_nt(qh, kc_ref[:, qk_cols])
                mx = jnp.maximum(mx, jnp.max(s_ctx, axis=-1, keepdims=True))
            p_lat = jnp.exp(s_lat - mx)
            den = jnp.sum(p_lat, axis=-1, keepdims=True)
            o = _dot(p_lat.astype(BF16), v_ref[krows, v_cols])
            if cache is not None:
                p_ctx = jnp.exp(s_ctx - mx)
                den = den + jnp.sum(p_ctx, axis=-1, keepdims=True)
                o = o + _dot(p_ctx.astype(BF16), vc_ref[:, v_cols])
            oa_ref[qrows, v_cols] = (o * (1.0 / den)).astype(BF16)


def _post_stage(x, mod, mix_ref, w):
    ga1 = mod[:, 2 * D_MODEL:3 * D_MODEL]
    sh2 = mod[:, 3 * D_MODEL:4 * D_MODEL]
    sc2 = mod[:, 4 * D_MODEL:5 * D_MODEL]
    ga2 = mod[:, 5 * D_MODEL:6 * D_MODEL]
    x = x + ga1 * _rms(_dot(mix_ref[...], w["w_out"][...]), w["g_post_mix"][...])
    h2 = (_rms(x, w["g_pre_ffn"][...]) * (1.0 + sc2) + sh2).astype(BF16)
    f = None
    for j in range(D_FF // FF_CHUNK):
        cols = slice(j * FF_CHUNK, (j + 1) * FF_CHUNK)
        a = jnp.square(jnp.maximum(_dot(h2, w["w_ff1"][:, cols]), 0.0)).astype(BF16)
        part = _dot(a, w["w_ff2"][cols, :])
        f = part if f is None else f + part
    return x + ga2 * _rms(f, w["g_post_ffn"][...])


def _context_kernel(*refs, seq_len, aliased, cast_next):
    n_w = len(PRE_WEIGHTS) + len(POST_WEIGHTS)
    n_cast = len(BIG_WEIGHTS) if cast_next else 0
    x_ref, mod_ref = refs[:2]
    w = dict(zip(PRE_WEIGHTS + POST_WEIGHTS, refs[2:2 + n_w]))
    cast_in = refs[2 + n_w:2 + n_w + n_cast]
    n_in = 2 + n_w + n_cast + (2 if aliased else 0)
    o_ref, ckv_ref, kr_ref = refs[n_in:n_in + 3]
    cast_out = refs[n_in + 3:n_in + 3 + n_cast]
    q_ref, k_ref, v_ref, mix_ref = refs[n_in + 3 + n_cast:]
    if cast_next:
        _cast_stage(cast_in, cast_out)
    x = x_ref[...]
    mod = mod_ref[...]
    _pre_stage(x, mod, w, None, seq_len, q_ref, k_ref, v_ref,
               mix_ref.at[:, WIDTH_A:WIDTH_A + WIDTH_B], mix_ref.at[:, WIDTH_A + WIDTH_B:],
               (ckv_ref, kr_ref))
    _attn_stage(q_ref, k_ref, v_ref, None, x.shape[0] // seq_len, mix_ref.at[:, 0:WIDTH_A])
    o_ref[...] = _post_stage(x, mod, mix_ref, w)


def _context_layer(x, mods, wts, l, new_ckv, new_kr, raw_big, *, seq_len, seqs_per_tile):
    t = x.shape[0]
    tm = seq_len * seqs_per_tile
    n_seq_total = t // seq_len
    aliased = new_ckv is not None
    cast_next = raw_big is not None
    row_spec = pl.BlockSpec((tm, D_MODEL), lambda i: (i, 0))
    ckv_spec = pl.BlockSpec((seqs_per_tile, None, seq_len, KV_LORA), lambda i: (i, l, 0, 0))
    kr_spec = pl.BlockSpec((seqs_per_tile, None, seq_len, QK_ROPE), lambda i: (i, l, 0, 0))
    in_specs = [row_spec, _mod_spec(t // tm)]
    args = [x, mods]
    for n in PRE_WEIGHTS + POST_WEIGHTS:
        in_specs.append(_layer_spec(wts[n].shape, l))
        args.append(wts[n])
    out_specs = [row_spec, ckv_spec, kr_spec]
    out_shape = [
        jax.ShapeDtypeStruct((t, D_MODEL), F32),
        jax.ShapeDtypeStruct((n_seq_total, DEPTH, seq_len, KV_LORA), F32),
        jax.ShapeDtypeStruct((n_seq_total, DEPTH, seq_len, QK_ROPE), F32),
    ]
    if cast_next:
        assert t // tm == CAST_STEPS
        c_in, c_out, c_shape = _cast_specs(raw_big, l + 1)
        in_specs += c_in
        args += list(raw_big)
        out_specs += c_out
        out_shape += c_shape
    aliases = {}
    if aliased:
        in_specs += [pl.BlockSpec(memory_space=pl.ANY)] * 2
        aliases = {len(args): 1, len(args) + 1: 2}
        args += [new_ckv, new_kr]
    return pl.pallas_call(
        functools.partial(_context_kernel, seq_len=seq_len, aliased=aliased, cast_next=cast_next),
        grid=(t // tm,),
        in_specs=in_specs,
        out_specs=out_specs,
        out_shape=out_shape,
        scratch_shapes=[
            pltpu.VMEM((tm, QK_WIDTH), BF16), pltpu.VMEM((tm, QK_WIDTH), BF16),
            pltpu.VMEM((tm, WIDTH_A), BF16), pltpu.VMEM((tm, D_MODEL), BF16),
        ],
        input_output_aliases=aliases,
        compiler_params=_params(1),
        name="context_layer",
    )(*args)


def _latent_pre_kernel(*refs, seq_len):
    x_ref, mod_ref = refs[:2]
    w = dict(zip(PRE_WEIGHTS, refs[2:2 + len(PRE_WEIGHTS)]))
    cos_ref, sin_ref, q_ref, k_ref, v_ref, obc_ref = refs[2 + len(PRE_WEIGHTS):]
    _pre_stage(x_ref[...], mod_ref[...], w, (cos_ref[...], sin_ref[...]), seq_len,
               q_ref, k_ref, v_ref, obc_ref.at[:, 0:WIDTH_B], obc_ref.at[:, WIDTH_B:], None)


def _latent_pre(x, mods, wts, l, rope_tabs, *, seq_len):
    t = x.shape[0]
    tm = seq_len
    row_spec = lambda w: pl.BlockSpec((tm, w), lambda i: (i, 0))
    in_specs = [row_spec(D_MODEL), _mod_spec(1)]
    args = [x, mods]
    for n in PRE_WEIGHTS:
        in_specs.append(_layer_spec(wts[n].shape, l))
        args.append(wts[n])
    in_specs += [pl.BlockSpec((tm, LANES), lambda i: (0, 0), pipeline_mode=pl.Buffered(1))] * 2
    args += list(rope_tabs)
    return pl.pallas_call(
        functools.partial(_latent_pre_kernel, seq_len=seq_len),
        grid=(t // tm,),
        in_specs=in_specs,
        out_specs=[row_spec(QK_WIDTH), row_spec(QK_WIDTH), row_spec(WIDTH_A), row_spec(BC_WIDTH)],
        out_shape=[
            jax.ShapeDtypeStruct((t, QK_WIDTH), BF16),
            jax.ShapeDtypeStruct((t, QK_WIDTH), BF16),
            jax.ShapeDtypeStruct((t, WIDTH_A), BF16),
            jax.ShapeDtypeStruct((t, BC_WIDTH), BF16),
        ],
        compiler_params=_params(1),
        name="latent_pre",
    )(*args)


def _latent_post_kernel(*refs):
    q_ref, k_ref, v_ref, kc_ref, vc_ref, obc_ref, x_ref, mod_ref = refs[:8]
    w = dict(zip(POST_WEIGHTS, refs[8:8 + len(POST_WEIGHTS)]))
    o_ref, mix_ref = refs[8 + len(POST_WEIGHTS):]
    _attn_stage(q_ref, k_ref, v_ref, (kc_ref, vc_ref), 1, mix_ref.at[:, 0:WIDTH_A])
    mix_ref[:, WIDTH_A:] = obc_ref[...]
    o_ref[...] = _post_stage(x_ref[...], mod_ref[...], mix_ref, w)


def _latent_post(q, k, v, obc, x, mods, cache, wts, l, *, seq_len, tq):
    t = x.shape[0]
    tiles_per_seq = seq_len // tq
    kc, vc = cache
    past = kc.shape[2]
    seq_of = lambda i: i // tiles_per_seq
    row_spec = lambda w: pl.BlockSpec((tq, w), lambda i: (i, 0))
    in_specs = [
        row_spec(QK_WIDTH),
        pl.BlockSpec((seq_len, QK_WIDTH), lambda i: (seq_of(i), 0)),
        pl.BlockSpec((seq_len, WIDTH_A), lambda i: (seq_of(i), 0)),
        pl.BlockSpec((None, None, past, QK_WIDTH), lambda i: (l, seq_of(i), 0, 0)),
        pl.BlockSpec((None, None, past, WIDTH_A), lambda i: (l, seq_of(i), 0, 0)),
        row_spec(BC_WIDTH),
        row_spec(D_MODEL),
        _mod_spec(tiles_per_seq),
    ]
    args = [q, k, v, kc, vc, obc, x, mods]
    for n in POST_WEIGHTS:
        in_specs.append(_layer_spec(wts[n].shape, l))
        args.append(wts[n])
    return pl.pallas_call(
        _latent_post_kernel,
        grid=(t // tq,),
        in_specs=in_specs,
        out_specs=row_spec(D_MODEL),
        out_shape=jax.ShapeDtypeStruct((t, D_MODEL), F32),
        scratch_shapes=[pltpu.VMEM((tq, D_MODEL), BF16)],
        compiler_params=_params(1),
        name="latent_post",
    )(*args)


def _rope_tables(n_tokens):
    rows = n_tokens // GRID_W
    row = np.repeat(np.arange(rows, dtype=np.float64), GRID_W)
    col = np.tile(np.arange(GRID_W, dtype=np.float64), rows)
    nf = QK_ROPE // 4
    inv = ROPE_THETA ** (-np.arange(nf, dtype=np.float64) / nf)
    ang_r = row[:, None] * inv
    ang_c = col[:, None] * inv
    zeros = np.zeros((n_tokens, LANES - QK_ROPE))
    cos = np.concatenate([np.cos(ang_r), np.cos(ang_r), np.cos(ang_c), np.cos(ang_c), zeros], axis=1)
    sin = np.concatenate([-np.sin(ang_r), np.sin(ang_r), -np.sin(ang_c), np.sin(ang_c), zeros], axis=1)
    return jnp.asarray(cos, F32), jnp.asarray(sin, F32)


def _prepare_weights(w_uq, w_ukv, w_s, b_s, w_conv, gains):
    w_q = jnp.pad(w_uq, ((0, 0), (0, 0), (0, 0), (0, QK_PAD - QK_NOPE - QK_ROPE)))
    w_q = w_q.reshape(DEPTH, Q_LORA, QK_WIDTH).astype(BF16)
    w_kv = jnp.concatenate(
        [w_ukv[..., :QK_NOPE].reshape(DEPTH, KV_LORA, N_HEADS * QK_NOPE),
         w_ukv[..., QK_NOPE:].reshape(DEPTH, KV_LORA, N_HEADS * V_HEAD)], axis=-1).astype(BF16)
    wts = {
        "w_q": w_q, "w_kv": w_kv,
        "w_s": w_s.reshape(DEPTH, N_HEADS_B * CHUNK, CHUNK).astype(BF16),
        "b_s": jnp.repeat(jnp.swapaxes(b_s, 1, 2), HEAD_B, axis=-1),
        "w_conv": w_conv,
    }
    for name, g in gains.items():
        wts[name] = g.reshape(DEPTH, 1, g.shape[-1])
    return wts


def kernel(x_prompt, x_sample, cache_ckv, cache_krope, c, c_ctx, w_ada, b_ada, g_pre_mix, w_in, g_q, w_uq, g_kv, w_ukv, g_v, w_s, b_s, w_conv, w_out, g_post_mix, g_pre_ffn, w_ff1, w_ff2, g_post_ffn):
    batch, seq, _ = x_prompt.shape
    dec_batch, dec_seq, _ = x_sample.shape

    wts = _prepare_weights(
        w_uq, w_ukv, w_s, b_s, w_conv,
        {"g_pre_mix": g_pre_mix, "g_q": g_q, "g_kv": g_kv, "g_v": g_v,
         "g_post_mix": g_post_mix, "g_pre_ffn": g_pre_ffn, "g_post_ffn": g_post_ffn})
    raw_big = (w_in, w_out, w_ff1, w_ff2)
    big = _cast_layer(raw_big, 0)

    cond = jnp.concatenate(
        [c_ctx[None, :], c, jnp.zeros((COND_ROWS - 1 - dec_batch, D_MODEL), F32)], axis=0)
    mods = _modulations(cond, w_ada, b_ada)
    mods_p = mods[:, 0:1, :].reshape(DEPTH, 1, 1, N_MOD * D_MODEL)
    mods_s = mods[:, 1:1 + dec_batch, :].reshape(DEPTH, dec_batch, 1, N_MOD * D_MODEL)

    cache = _cache_kv(cache_ckv, cache_krope, wts["w_kv"])
    rope_tabs = _rope_tables(dec_seq)

    xp = x_prompt.reshape(batch * seq, D_MODEL)
    xs = x_sample.reshape(dec_batch * dec_seq, D_MODEL)
    new_ckv = new_kr = None
    for l in range(DEPTH):
        wl = dict(wts, **dict(zip(BIG_WEIGHTS, big)))
        xp, new_ckv, new_kr, *big = _context_layer(
            xp, mods_p[l], wl, l, new_ckv, new_kr, raw_big if l + 1 < DEPTH else None,
            seq_len=seq, seqs_per_tile=2)
        q, k, v, obc = _latent_pre(xs, mods_s[l], wl, l, rope_tabs, seq_len=dec_seq)
        xs = _latent_post(q, k, v, obc, xs, mods_s[l], cache, wl, l, seq_len=dec_seq, tq=512)

    return (xp.reshape(batch, seq, D_MODEL), xs.reshape(dec_batch, dec_seq, D_MODEL), new_ckv, new_kr)
```

```python
import functools
import math

import jax
import jax.numpy as jnp
import numpy as np
from jax import lax
from jax.experimental import pallas as pl
from jax.experimental.pallas import tpu as pltpu

F32 = jnp.float32
BF16 = jnp.bfloat16

D_MODEL = 1024
DEPTH = 4
GRID_W = 64
N_HEADS = 4
QK_NOPE = 128
QK_ROPE = 64
V_HEAD = 128
Q_LORA = 384
KV_LORA = 256
WIDTH_A = N_HEADS * V_HEAD
ROPE_THETA = 10000.0
WIDTH_B = 256
N_HEADS_B = 4
HEAD_B = WIDTH_B // N_HEADS_B
CHUNK = 128
WIDTH_C = 256
D_FF = 4 * D_MODEL
N_MOD = 6
EPS = 1e-6

LANES = 128
QK_PAD = 2 * LANES
QK_WIDTH = N_HEADS * QK_PAD
KV_COLS = N_HEADS * (QK_NOPE + V_HEAD)
BC_WIDTH = WIDTH_B + WIDTH_C
OFF_Q = 0
OFF_CKV = OFF_Q + Q_LORA
OFF_U = OFF_CKV + KV_LORA
OFF_V = OFF_U + WIDTH_B
OFF_BG = OFF_V + WIDTH_B
OFF_CG = OFF_BG + WIDTH_C
OFF_HH = OFF_CG + WIDTH_C
OFF_KR = OFF_HH + WIDTH_C
IN_COLS_PAD = OFF_KR + LANES
COND_ROWS = 8
VMEM_LIMIT = 56 * 1024 * 1024
SM_SCALE = 1.0 / math.sqrt(QK_NOPE + QK_ROPE)
FF_CHUNK = 1024

PRE_WEIGHTS = ("g_pre_mix", "w_in", "g_q", "w_q", "g_kv", "w_kv", "g_v", "w_s", "b_s", "w_conv")
POST_WEIGHTS = ("g_post_mix", "w_out", "g_pre_ffn", "w_ff1", "w_ff2", "g_post_ffn")
BIG_WEIGHTS = ("w_in", "w_out", "w_ff1", "w_ff2")
BIG_SHAPES = ((D_MODEL, IN_COLS_PAD), (D_MODEL, D_MODEL), (D_MODEL, D_FF), (D_FF, D_MODEL))
CAST_STEPS = 16


def _rms(x, g):
    return x * lax.rsqrt(jnp.mean(x * x, axis=-1, keepdims=True) + EPS) * g


def _dot(a, b):
    return jnp.dot(a, b, preferred_element_type=F32)


def _dot_nt(a, b):
    return lax.dot_general(a, b, (((1,), (1,)), ((), ())), preferred_element_type=F32)


def _params(n_axes):
    return pltpu.CompilerParams(
        dimension_semantics=("parallel",) * n_axes, vmem_limit_bytes=VMEM_LIMIT)


def _layer_spec(shape, l):
    if len(shape) == 2:
        return pl.BlockSpec(tuple(shape), lambda *_: (0, 0), pipeline_mode=pl.Buffered(1))
    return pl.BlockSpec((None,) + tuple(shape[1:]), lambda *_: (l, 0, 0),
                        pipeline_mode=pl.Buffered(1))


def _mod_spec(tiles_per_mod):
    return pl.BlockSpec((None, 1, N_MOD * D_MODEL), lambda i: (i // tiles_per_mod, 0, 0))


def _mod_kernel(cond_ref, w_ref, b_ref, o_ref):
    c = cond_ref[...]
    s = c / (1.0 + jnp.exp(-c))
    o_ref[...] = _dot(s.astype(BF16), w_ref[...].astype(BF16)) + b_ref[...]


def _modulations(cond, w_ada, b_ada):
    tn = 1536
    n_cols = N_MOD * D_MODEL
    return pl.pallas_call(
        _mod_kernel,
        grid=(DEPTH, n_cols // tn),
        in_specs=[
            pl.BlockSpec((COND_ROWS, D_MODEL), lambda l, j: (0, 0)),
            pl.BlockSpec((None, D_MODEL, tn), lambda l, j: (l, 0, j)),
            pl.BlockSpec((None, 1, tn), lambda l, j: (l, 0, j)),
        ],
        out_specs=pl.BlockSpec((None, COND_ROWS, tn), lambda l, j: (l, 0, j)),
        out_shape=jax.ShapeDtypeStruct((DEPTH, COND_ROWS, n_cols), F32),
        compiler_params=_params(2),
        name="modulation",
    )(cond, w_ada, b_ada.reshape(DEPTH, 1, n_cols))


def _cache_kv_kernel(ckv_ref, kr_ref, wkv_ref, k_ref, v_ref):
    kv = _dot(ckv_ref[...].astype(BF16), wkv_ref[...])
    kr_t = kr_ref[...]
    pad = jnp.zeros((QK_PAD - QK_NOPE - QK_ROPE, kr_t.shape[1]), F32)
    krz = jnp.concatenate([kr_t, pad], axis=0).T.astype(BF16)
    for h in range(N_HEADS):
        lo = h * QK_PAD
        k_ref[:, lo:lo + QK_NOPE] = kv[:, h * QK_NOPE:(h + 1) * QK_NOPE].astype(BF16)
        k_ref[:, lo + QK_NOPE:lo + QK_PAD] = krz
    v_ref[...] = kv[:, N_HEADS * QK_NOPE:].astype(BF16)


def _cache_kv(cache_ckv, cache_krope_t, w_kv):
    nb, _, past, _ = cache_ckv.shape
    return pl.pallas_call(
        _cache_kv_kernel,
        grid=(DEPTH, nb),
        in_specs=[
            pl.BlockSpec((None, None, past, KV_LORA), lambda l, b: (b, l, 0, 0)),
            pl.BlockSpec((None, None, QK_ROPE, past), lambda l, b: (b, l, 0, 0)),
            pl.BlockSpec((None, KV_LORA, KV_COLS), lambda l, b: (l, 0, 0)),
        ],
        out_specs=[
            pl.BlockSpec((None, None, past, QK_WIDTH), lambda l, b: (l, b, 0, 0)),
            pl.BlockSpec((None, None, past, WIDTH_A), lambda l, b: (l, b, 0, 0)),
        ],
        out_shape=[
            jax.ShapeDtypeStruct((DEPTH, nb, past, QK_WIDTH), BF16),
            jax.ShapeDtypeStruct((DEPTH, nb, past, WIDTH_A), BF16),
        ],
        compiler_params=_params(2),
        name="cache_kv",
    )(cache_ckv, cache_krope_t, w_kv)


IN_HALF = IN_COLS_PAD // CAST_STEPS // 2


def _cast_stage(in_refs, out_refs):
    win_a, win_b, wout, w1, w2 = in_refs
    win_o, wout_o, w1_o, w2_o = out_refs
    last = pl.program_id(0) == CAST_STEPS - 1
    b = jnp.where(last, 0.0, win_b[...])
    win_o[...] = jnp.concatenate([win_a[...], b], axis=0).T.astype(BF16)
    wout_o[...] = wout[...].astype(BF16)
    w1_o[...] = w1[...].astype(BF16)
    w2_o[...] = w2[...].astype(BF16)


def _w_in_block(i, half):
    kr_block = (Q_LORA + KV_LORA) // IN_HALF
    n_front = kr_block // 2
    shifted = jnp.where(i < CAST_STEPS - 1, 2 * i + 1 + half, kr_block)
    return jnp.where(i < n_front, 2 * i + half, shifted)


def _cast_specs(raw, l):
    w_in_t = raw[0]
    in_specs = [pl.BlockSpec((None, IN_HALF, D_MODEL), lambda i, h=h: (l, _w_in_block(i, h), 0))
                for h in range(2)]
    out_specs = [pl.BlockSpec((D_MODEL, IN_COLS_PAD // CAST_STEPS), lambda i: (0, i))]
    out_shape = [jax.ShapeDtypeStruct(BIG_SHAPES[0], BF16)]
    for a, (rows, cols) in zip(raw[1:], BIG_SHAPES[1:]):
        chunk = rows // CAST_STEPS
        in_specs.append(pl.BlockSpec((None, chunk, cols), lambda i: (l, i, 0)))
        out_specs.append(pl.BlockSpec((chunk, cols), lambda i: (i, 0)))
        out_shape.append(jax.ShapeDtypeStruct((rows, cols), BF16))
    return in_specs, out_specs, out_shape, [w_in_t, w_in_t] + list(raw[1:])


def _cast_kernel(*refs):
    _cast_stage(refs[:5], refs[5:])


def _cast_layer(raw, l):
    in_specs, out_specs, out_shape, args = _cast_specs(raw, l)
    return pl.pallas_call(
        _cast_kernel,
        grid=(CAST_STEPS,),
        in_specs=in_specs,
        out_specs=out_specs,
        out_shape=out_shape,
        compiler_params=_params(1),
        name="cast_weights",
    )(*args)


def _swap_halves(x):
    lane = lax.broadcasted_iota(jnp.int32, x.shape, 1)
    quarter = QK_ROPE // 4
    first_half = (lane & (2 * quarter - 1)) < quarter
    return jnp.where(first_half, pltpu.roll(x, LANES - quarter, 1), pltpu.roll(x, quarter, 1))


def _pre_stage(x, mod, w, rope, seq_len, q_ref, k_ref, v_ref, ob_ref, oc_ref, cache_refs):
    tm = x.shape[0]
    sh1 = mod[:, 0:D_MODEL]
    sc1 = mod[:, D_MODEL:2 * D_MODEL]
    h = _rms(x, w["g_pre_mix"][...]) * (1.0 + sc1) + sh1
    z = _dot(h.astype(BF16), w["w_in"][...])

    ckv = _rms(z[:, OFF_CKV:OFF_CKV + KV_LORA], w["g_kv"][...])
    krz = z[:, OFF_KR:OFF_KR + LANES]
    if cache_refs is not None:
        ckv_ref, kr_ref = cache_refs
        ckv_ref[...] = ckv.reshape(ckv_ref.shape)
        for s in range(tm // seq_len):
            kr_ref[s] = krz[s * seq_len:(s + 1) * seq_len, :].T[0:QK_ROPE, :]
    qn = _rms(z[:, OFF_Q:OFF_Q + Q_LORA], w["g_q"][...])
    q = _dot(qn.astype(BF16), w["w_q"][...]) * SM_SCALE
    kv = _dot(ckv.astype(BF16), w["w_kv"][...])
    if rope is not None:
        cos, sin = rope
        krz = krz * cos + _swap_halves(krz) * sin
    krz = krz.astype(BF16)
    for hd in range(N_HEADS):
        lo = hd * QK_PAD
        q_ref[:, lo:lo + QK_NOPE] = q[:, lo:lo + QK_NOPE].astype(BF16)
        qr = q[:, lo + QK_NOPE:lo + QK_PAD]
        if rope is not None:
            qr = qr * cos + _swap_halves(qr) * sin
        q_ref[:, lo + QK_NOPE:lo + QK_PAD] = qr.astype(BF16)
        k_ref[:, lo:lo + QK_NOPE] = kv[:, hd * QK_NOPE:(hd + 1) * QK_NOPE].astype(BF16)
        k_ref[:, lo + QK_NOPE:lo + QK_PAD] = krz
    v_ref[...] = kv[:, N_HEADS * QK_NOPE:].astype(BF16)

    u = jax.nn.gelu(z[:, OFF_U:OFF_U + WIDTH_B])
    vn = _rms(jax.nn.gelu(z[:, OFF_V:OFF_V + WIDTH_B]), w["g_v"][...]).astype(BF16)
    lane = lax.broadcasted_iota(jnp.int32, (CHUNK, WIDTH_B), 1)
    ws = w["w_s"][...]
    bs = w["b_s"][...]
    for c in range(tm // CHUNK):
        rows = slice(c * CHUNK, (c + 1) * CHUNK)
        r = _dot(ws, vn[rows, :])
        mixed = r[(N_HEADS_B - 1) * CHUNK:, :]
        for hb in range(N_HEADS_B - 2, -1, -1):
            mixed = jnp.where(lane < (hb + 1) * HEAD_B, r[hb * CHUNK:(hb + 1) * CHUNK, :], mixed)
        ob_ref[rows, :] = (u[rows, :] * (mixed + bs)).astype(BF16)

    zc = z[:, OFF_CG:OFF_CG + WIDTH_C] * z[:, OFF_HH:OFF_HH + WIDTH_C]
    pos = lax.broadcasted_iota(jnp.int32, (tm, WIDTH_C), 0) & (seq_len - 1)
    z_prev = jnp.where(pos == 0, 0.0, pltpu.roll(zc, 1, 0))
    z_next = jnp.where(pos == seq_len - 1, 0.0, pltpu.roll(zc, tm - 1, 0))
    wc = w["w_conv"][...]
    y = z_prev * wc[0:1, :] + zc * wc[1:2, :] + z_next * wc[2:3, :]
    oc_ref[...] = (z[:, OFF_BG:OFF_BG + WIDTH_C] * y).astype(BF16)


def _attn_stage(q_ref, k_ref, v_ref, cache, n_seq, oa_ref):
    ql = q_ref.shape[0] // n_seq
    kl = k_ref.shape[0] // n_seq
    for s in range(n_seq):
        qrows = slice(s * ql, (s + 1) * ql)
        krows = slice(s * kl, (s + 1) * kl)
        for hd in range(N_HEADS):
            qk_cols = slice(hd * QK_PAD, (hd + 1) * QK_PAD)
            v_cols = slice(hd * V_HEAD, (hd + 1) * V_HEAD)
            qh = q_ref[qrows, qk_cols]
            s_lat = _dot_nt(qh, k_ref[krows, qk_cols])
            mx = jnp.max(s_lat, axis=-1, keepdims=True)
            if cache is not None:
                kc_ref, vc_ref = cache
                s_ctx = _dot_nt(qh, kc_ref[:, qk_cols])
                mx = jnp.maximum(mx, jnp.max(s_ctx, axis=-1, keepdims=True))
            p_lat = jnp.exp(s_lat - mx)
            den = jnp.sum(p_lat, axis=-1, keepdims=True)
            o = _dot(p_lat.astype(BF16), v_ref[krows, v_cols])
            if cache is not None:
                p_ctx = jnp.exp(s_ctx - mx)
                den = den + jnp.sum(p_ctx, axis=-1, keepdims=True)
                o = o + _dot(p_ctx.astype(BF16), vc_ref[:, v_cols])
            oa_ref[qrows, v_cols] = (o * (1.0 / den)).astype(BF16)


def _post_stage(x, mod, mix_ref, w):
    ga1 = mod[:, 2 * D_MODEL:3 * D_MODEL]
    sh2 = mod[:, 3 * D_MODEL:4 * D_MODEL]
    sc2 = mod[:, 4 * D_MODEL:5 * D_MODEL]
    ga2 = mod[:, 5 * D_MODEL:6 * D_MODEL]
    x = x + ga1 * _rms(_dot(mix_ref[...], w["w_out"][...]), w["g_post_mix"][...])
    h2 = (_rms(x, w["g_pre_ffn"][...]) * (1.0 + sc2) + sh2).astype(BF16)
    f = None
    for j in range(D_FF // FF_CHUNK):
        cols = slice(j * FF_CHUNK, (j + 1) * FF_CHUNK)
        a = jnp.square(jnp.maximum(_dot(h2, w["w_ff1"][:, cols]), 0.0)).astype(BF16)
        part = _dot(a, w["w_ff2"][cols, :])
        f = part if f is None else f + part
    return x + ga2 * _rms(f, w["g_post_ffn"][...])


def _context_kernel(*refs, seq_len, aliased, cast_next):
    n_w = len(PRE_WEIGHTS) + len(POST_WEIGHTS)
    n_cast_out = len(BIG_WEIGHTS) if cast_next else 0
    n_cast_in = n_cast_out + 1 if cast_next else 0
    x_ref, mod_ref = refs[:2]
    w = dict(zip(PRE_WEIGHTS + POST_WEIGHTS, refs[2:2 + n_w]))
    cast_in = refs[2 + n_w:2 + n_w + n_cast_in]
    n_in = 2 + n_w + n_cast_in + (2 if aliased else 0)
    o_ref, ckv_ref, kr_ref = refs[n_in:n_in + 3]
    cast_out = refs[n_in + 3:n_in + 3 + n_cast_out]
    q_ref, k_ref, v_ref, mix_ref = refs[n_in + 3 + n_cast_out:]
    if cast_next:
        _cast_stage(cast_in, cast_out)
    x = x_ref[...]
    mod = mod_ref[...]
    _pre_stage(x, mod, w, None, seq_len, q_ref, k_ref, v_ref,
               mix_ref.at[:, WIDTH_A:WIDTH_A + WIDTH_B], mix_ref.at[:, WIDTH_A + WIDTH_B:],
               (ckv_ref, kr_ref))
    _attn_stage(q_ref, k_ref, v_ref, None, x.shape[0] // seq_len, mix_ref.at[:, 0:WIDTH_A])
    o_ref[...] = _post_stage(x, mod, mix_ref, w)


def _context_layer(x, mods, wts, l, new_ckv, new_kr, raw_big, *, seq_len, seqs_per_tile):
    t = x.shape[0]
    tm = seq_len * seqs_per_tile
    n_seq_total = t // seq_len
    aliased = new_ckv is not None
    cast_next = raw_big is not None
    row_spec = pl.BlockSpec((tm, D_MODEL), lambda i: (i, 0))
    ckv_spec = pl.BlockSpec((seqs_per_tile, None, seq_len, KV_LORA), lambda i: (i, l, 0, 0))
    kr_spec = pl.BlockSpec((seqs_per_tile, None, QK_ROPE, seq_len), lambda i: (i, l, 0, 0))
    in_specs = [row_spec, _mod_spec(t // tm)]
    args = [x, mods]
    for n in PRE_WEIGHTS + POST_WEIGHTS:
        in_specs.append(_layer_spec(wts[n].shape, l))
        args.append(wts[n])
    out_specs = [row_spec, ckv_spec, kr_spec]
    out_shape = [
        jax.ShapeDtypeStruct((t, D_MODEL), F32),
        jax.ShapeDtypeStruct((n_seq_total, DEPTH, seq_len, KV_LORA), F32),
        jax.ShapeDtypeStruct((n_seq_total, DEPTH, QK_ROPE, seq_len), F32),
    ]
    if cast_next:
        assert t // tm == CAST_STEPS
        c_in, c_out, c_shape, c_args = _cast_specs(raw_big, l + 1)
        in_specs += c_in
        args += c_args
        out_specs += c_out
        out_shape += c_shape
    aliases = {}
    if aliased:
        in_specs += [pl.BlockSpec(memory_space=pl.ANY)] * 2
        aliases = {len(args): 1, len(args) + 1: 2}
        args += [new_ckv, new_kr]
    return pl.pallas_call(
        functools.partial(_context_kernel, seq_len=seq_len, aliased=aliased, cast_next=cast_next),
        grid=(t // tm,),
        in_specs=in_specs,
        out_specs=out_specs,
        out_shape=out_shape,
        scratch_shapes=[
            pltpu.VMEM((tm, QK_WIDTH), BF16), pltpu.VMEM((tm, QK_WIDTH), BF16),
            pltpu.VMEM((tm, WIDTH_A), BF16), pltpu.VMEM((tm, D_MODEL), BF16),
        ],
        input_output_aliases=aliases,
        compiler_params=_params(1),
        name="context_layer",
    )(*args)


def _latent_pre_kernel(*refs, seq_len):
    x_ref, mod_ref = refs[:2]
    w = dict(zip(PRE_WEIGHTS, refs[2:2 + len(PRE_WEIGHTS)]))
    cos_ref, sin_ref, q_ref, k_ref, v_ref, obc_ref = refs[2 + len(PRE_WEIGHTS):]
    _pre_stage(x_ref[...], mod_ref[...], w, (cos_ref[...], sin_ref[...]), seq_len,
               q_ref, k_ref, v_ref, obc_ref.at[:, 0:WIDTH_B], obc_ref.at[:, WIDTH_B:], None)


def _latent_pre(x, mods, wts, l, rope_tabs, *, seq_len):
    t = x.shape[0]
    tm = seq_len
    row_spec = lambda w: pl.BlockSpec((tm, w), lambda i: (i, 0))
    in_specs = [row_spec(D_MODEL), _mod_spec(1)]
    args = [x, mods]
    for n in PRE_WEIGHTS:
        in_specs.append(_layer_spec(wts[n].shape, l))
        args.append(wts[n])
    in_specs += [pl.BlockSpec((tm, LANES), lambda i: (0, 0), pipeline_mode=pl.Buffered(1))] * 2
    args += list(rope_tabs)
    return pl.pallas_call(
        functools.partial(_latent_pre_kernel, seq_len=seq_len),
        grid=(t // tm,),
        in_specs=in_specs,
        out_specs=[row_spec(QK_WIDTH), row_spec(QK_WIDTH), row_spec(WIDTH_A), row_spec(BC_WIDTH)],
        out_shape=[
            jax.ShapeDtypeStruct((t, QK_WIDTH), BF16),
            jax.ShapeDtypeStruct((t, QK_WIDTH), BF16),
            jax.ShapeDtypeStruct((t, WIDTH_A), BF16),
            jax.ShapeDtypeStruct((t, BC_WIDTH), BF16),
        ],
        compiler_params=_params(1),
        name="latent_pre",
    )(*args)


def _latent_post_kernel(*refs):
    q_ref, k_ref, v_ref, kc_ref, vc_ref, obc_ref, x_ref, mod_ref = refs[:8]
    w = dict(zip(POST_WEIGHTS, refs[8:8 + len(POST_WEIGHTS)]))
    o_ref, mix_ref = refs[8 + len(POST_WEIGHTS):]
    _attn_stage(q_ref, k_ref, v_ref, (kc_ref, vc_ref), 1, mix_ref.at[:, 0:WIDTH_A])
    mix_ref[:, WIDTH_A:] = obc_ref[...]
    o_ref[...] = _post_stage(x_ref[...], mod_ref[...], mix_ref, w)


def _latent_post(q, k, v, obc, x, mods, cache, wts, l, *, seq_len, tq):
    t = x.shape[0]
    tiles_per_seq = seq_len // tq
    kc, vc = cache
    past = kc.shape[2]
    seq_of = lambda i: i // tiles_per_seq
    row_spec = lambda w: pl.BlockSpec((tq, w), lambda i: (i, 0))
    in_specs = [
        row_spec(QK_WIDTH),
        pl.BlockSpec((seq_len, QK_WIDTH), lambda i: (seq_of(i), 0)),
        pl.BlockSpec((seq_len, WIDTH_A), lambda i: (seq_of(i), 0)),
        pl.BlockSpec((None, None, past, QK_WIDTH), lambda i: (l, seq_of(i), 0, 0)),
        pl.BlockSpec((None, None, past, WIDTH_A), lambda i: (l, seq_of(i), 0, 0)),
        row_spec(BC_WIDTH),
        row_spec(D_MODEL),
        _mod_spec(tiles_per_seq),
    ]
    args = [q, k, v, kc, vc, obc, x, mods]
    for n in POST_WEIGHTS:
        in_specs.append(_layer_spec(wts[n].shape, l))
        args.append(wts[n])
    return pl.pallas_call(
        _latent_post_kernel,
        grid=(t // tq,),
        in_specs=in_specs,
        out_specs=row_spec(D_MODEL),
        out_shape=jax.ShapeDtypeStruct((t, D_MODEL), F32),
        scratch_shapes=[pltpu.VMEM((tq, D_MODEL), BF16)],
        compiler_params=_params(1),
        name="latent_post",
    )(*args)


def _rope_tables(n_tokens):
    rows = n_tokens // GRID_W
    row = np.repeat(np.arange(rows, dtype=np.float64), GRID_W)
    col = np.tile(np.arange(GRID_W, dtype=np.float64), rows)
    nf = QK_ROPE // 4
    inv = ROPE_THETA ** (-np.arange(nf, dtype=np.float64) / nf)
    ang_r = row[:, None] * inv
    ang_c = col[:, None] * inv
    zeros = np.zeros((n_tokens, LANES - QK_ROPE))
    cos = np.concatenate([np.cos(ang_r), np.cos(ang_r), np.cos(ang_c), np.cos(ang_c), zeros], axis=1)
    sin = np.concatenate([-np.sin(ang_r), np.sin(ang_r), -np.sin(ang_c), np.sin(ang_c), zeros], axis=1)
    return jnp.asarray(cos, F32), jnp.asarray(sin, F32)


def _prepare_weights(w_uq, w_ukv, w_s, b_s, w_conv, gains):
    w_q = jnp.pad(w_uq, ((0, 0), (0, 0), (0, 0), (0, QK_PAD - QK_NOPE - QK_ROPE)))
    w_q = w_q.reshape(DEPTH, Q_LORA, QK_WIDTH).astype(BF16)
    w_kv = jnp.concatenate(
        [w_ukv[..., :QK_NOPE].reshape(DEPTH, KV_LORA, N_HEADS * QK_NOPE),
         w_ukv[..., QK_NOPE:].reshape(DEPTH, KV_LORA, N_HEADS * V_HEAD)], axis=-1).astype(BF16)
    wts = {
        "w_q": w_q, "w_kv": w_kv,
        "w_s": w_s.reshape(DEPTH, N_HEADS_B * CHUNK, CHUNK).astype(BF16),
        "b_s": jnp.repeat(jnp.swapaxes(b_s, 1, 2), HEAD_B, axis=-1),
        "w_conv": w_conv,
    }
    for name, g in gains.items():
        wts[name] = g.reshape(DEPTH, 1, g.shape[-1])
    return wts


def kernel(x_prompt, x_sample, cache_ckv, cache_krope, c, c_ctx, w_ada, b_ada, g_pre_mix, w_in, g_q, w_uq, g_kv, w_ukv, g_v, w_s, b_s, w_conv, w_out, g_post_mix, g_pre_ffn, w_ff1, w_ff2, g_post_ffn):
    batch, seq, _ = x_prompt.shape
    dec_batch, dec_seq, _ = x_sample.shape

    wts = _prepare_weights(
        w_uq, w_ukv, w_s, b_s, w_conv,
        {"g_pre_mix": g_pre_mix, "g_q": g_q, "g_kv": g_kv, "g_v": g_v,
         "g_post_mix": g_post_mix, "g_pre_ffn": g_pre_ffn, "g_post_ffn": g_post_ffn})
    raw_big = (jnp.swapaxes(w_in, 1, 2), w_out, w_ff1, w_ff2)
    big = _cast_layer(raw_big, 0)

    cond = jnp.concatenate(
        [c_ctx[None, :], c, jnp.zeros((COND_ROWS - 1 - dec_batch, D_MODEL), F32)], axis=0)
    mods = _modulations(cond, w_ada, b_ada)
    mods_p = mods[:, 0:1, :].reshape(DEPTH, 1, 1, N_MOD * D_MODEL)
    mods_s = mods[:, 1:1 + dec_batch, :].reshape(DEPTH, dec_batch, 1, N_MOD * D_MODEL)

    cache = _cache_kv(cache_ckv, jnp.swapaxes(cache_krope, 2, 3), wts["w_kv"])
    rope_tabs = _rope_tables(dec_seq)

    xp = x_prompt.reshape(batch * seq, D_MODEL)
    xs = x_sample.reshape(dec_batch * dec_seq, D_MODEL)
    new_ckv = new_kr = None
    for l in range(DEPTH):
        wl = dict(wts, **dict(zip(BIG_WEIGHTS, big)))
        xp, new_ckv, new_kr, *big = _context_layer(
            xp, mods_p[l], wl, l, new_ckv, new_kr, raw_big if l + 1 < DEPTH else None,
            seq_len=seq, seqs_per_tile=2)
        q, k, v, obc = _latent_pre(xs, mods_s[l], wl, l, rope_tabs, seq_len=dec_seq)
        xs = _latent_post(q, k, v, obc, xs, mods_s[l], cache, wl, l, seq_len=dec_seq, tq=512)

    return (xp.reshape(batch, seq, D_MODEL), xs.reshape(dec_batch, dec_seq, D_MODEL),
            new_ckv, jnp.swapaxes(new_kr, 2, 3))
```

```python
import functools
import math

import jax
import jax.numpy as jnp
import numpy as np
from jax import lax
from jax.experimental import pallas as pl
from jax.experimental.pallas import tpu as pltpu

F32 = jnp.float32
BF16 = jnp.bfloat16

D_MODEL = 1024
DEPTH = 4
GRID_W = 64
N_HEADS = 4
QK_NOPE = 128
QK_ROPE = 64
V_HEAD = 128
Q_LORA = 384
KV_LORA = 256
WIDTH_A = N_HEADS * V_HEAD
ROPE_THETA = 10000.0
WIDTH_B = 256
N_HEADS_B = 4
HEAD_B = WIDTH_B // N_HEADS_B
CHUNK = 128
WIDTH_C = 256
D_FF = 4 * D_MODEL
N_MOD = 6
EPS = 1e-6

LANES = 128
QK_PAD = 2 * LANES
QK_WIDTH = N_HEADS * QK_PAD
KV_COLS = N_HEADS * (QK_NOPE + V_HEAD)
BC_WIDTH = WIDTH_B + WIDTH_C
OFF_Q = 0
OFF_CKV = OFF_Q + Q_LORA
OFF_U = OFF_CKV + KV_LORA
OFF_V = OFF_U + WIDTH_B
OFF_BG = OFF_V + WIDTH_B
OFF_CG = OFF_BG + WIDTH_C
OFF_HH = OFF_CG + WIDTH_C
OFF_KR = OFF_HH + WIDTH_C
IN_COLS_PAD = OFF_KR + LANES
COND_ROWS = 8
VMEM_LIMIT = 56 * 1024 * 1024
SM_SCALE = 1.0 / math.sqrt(QK_NOPE + QK_ROPE)
FF_CHUNK = 1024

PRE_WEIGHTS = ("g_pre_mix", "w_in", "g_q", "w_q", "g_kv", "w_kv", "g_v", "w_s", "b_s", "w_conv")
POST_WEIGHTS = ("g_post_mix", "w_out", "g_pre_ffn", "w_ff1", "w_ff2", "g_post_ffn")
BIG_WEIGHTS = ("w_in", "w_out", "w_ff1", "w_ff2")
BIG_SHAPES = ((D_MODEL, IN_COLS_PAD), (D_MODEL, D_MODEL), (D_MODEL, D_FF), (D_FF, D_MODEL))
POST_BIG = ("w_out", "w_ff1", "w_ff2")
CAST_STEPS = 16


def _rms(x, g):
    return x * lax.rsqrt(jnp.mean(x * x, axis=-1, keepdims=True) + EPS) * g


def _dot(a, b):
    return jnp.dot(a, b, preferred_element_type=F32)


def _dot_nt(a, b):
    return lax.dot_general(a, b, (((1,), (1,)), ((), ())), preferred_element_type=F32)


def _params(n_axes, sequential=False):
    semantics = "arbitrary" if sequential else "parallel"
    return pltpu.CompilerParams(
        dimension_semantics=(semantics,) * n_axes, vmem_limit_bytes=VMEM_LIMIT)


def _layer_spec(shape, l):
    if len(shape) == 2:
        return pl.BlockSpec(tuple(shape), lambda *_: (0, 0), pipeline_mode=pl.Buffered(1))
    return pl.BlockSpec((None,) + tuple(shape[1:]), lambda *_: (l, 0, 0),
                        pipeline_mode=pl.Buffered(1))


def _weight_spec(name, shape, l):
    if name in BIG_WEIGHTS:
        return pl.BlockSpec(memory_space=pl.ANY)
    return _layer_spec(shape, l)


def _fetch_scratch(names, wts):
    return ([pltpu.VMEM(wts[n].shape, BF16) for n in names]
            + [pltpu.SemaphoreType.DMA((len(names),))])


def _mod_spec(tiles_per_mod):
    return pl.BlockSpec((None, 1, N_MOD * D_MODEL), lambda i: (i // tiles_per_mod, 0, 0))


def _mod_kernel(cond_ref, w_ref, b_ref, o_ref):
    c = cond_ref[...]
    s = c / (1.0 + jnp.exp(-c))
    o_ref[...] = _dot(s.astype(BF16), w_ref[...].astype(BF16)) + b_ref[...]


def _modulations(cond, w_ada, b_ada):
    tn = 1536
    n_cols = N_MOD * D_MODEL
    return pl.pallas_call(
        _mod_kernel,
        grid=(DEPTH, n_cols // tn),
        in_specs=[
            pl.BlockSpec((COND_ROWS, D_MODEL), lambda l, j: (0, 0)),
            pl.BlockSpec((None, D_MODEL, tn), lambda l, j: (l, 0, j)),
            pl.BlockSpec((None, 1, tn), lambda l, j: (l, 0, j)),
        ],
        out_specs=pl.BlockSpec((None, COND_ROWS, tn), lambda l, j: (l, 0, j)),
        out_shape=jax.ShapeDtypeStruct((DEPTH, COND_ROWS, n_cols), F32),
        compiler_params=_params(2),
        name="modulation",
    )(cond, w_ada, b_ada.reshape(DEPTH, 1, n_cols))


def _cache_kv_kernel(ckv_ref, kr_ref, wkv_ref, k_ref, v_ref):
    kv = _dot(ckv_ref[...].astype(BF16), wkv_ref[...])
    kr_t = kr_ref[...]
    pad = jnp.zeros((QK_PAD - QK_NOPE - QK_ROPE, kr_t.shape[1]), F32)
    krz = jnp.concatenate([kr_t, pad], axis=0).T.astype(BF16)
    for h in range(N_HEADS):
        lo = h * QK_PAD
        k_ref[:, lo:lo + QK_NOPE] = kv[:, h * QK_NOPE:(h + 1) * QK_NOPE].astype(BF16)
        k_ref[:, lo + QK_NOPE:lo + QK_PAD] = krz
    v_ref[...] = kv[:, N_HEADS * QK_NOPE:].astype(BF16)


def _cache_kv(cache_ckv, cache_krope_t, w_kv):
    nb, _, past, _ = cache_ckv.shape
    return pl.pallas_call(
        _cache_kv_kernel,
        grid=(DEPTH, nb),
        in_specs=[
            pl.BlockSpec((None, None, past, KV_LORA), lambda l, b: (b, l, 0, 0)),
            pl.BlockSpec((None, None, QK_ROPE, past), lambda l, b: (b, l, 0, 0)),
            pl.BlockSpec((None, KV_LORA, KV_COLS), lambda l, b: (l, 0, 0)),
        ],
        out_specs=[
            pl.BlockSpec((None, None, past, QK_WIDTH), lambda l, b: (l, b, 0, 0)),
            pl.BlockSpec((None, None, past, WIDTH_A), lambda l, b: (l, b, 0, 0)),
        ],
        out_shape=[
            jax.ShapeDtypeStruct((DEPTH, nb, past, QK_WIDTH), BF16),
            jax.ShapeDtypeStruct((DEPTH, nb, past, WIDTH_A), BF16),
        ],
        compiler_params=_params(2),
        name="cache_kv",
    )(cache_ckv, cache_krope_t, w_kv)


IN_HALF = IN_COLS_PAD // CAST_STEPS // 2


def _cast_stage(in_refs, out_refs):
    win_a, win_b, wout, w1, w2 = in_refs
    win_o, wout_o, w1_o, w2_o = out_refs
    last = pl.program_id(0) == CAST_STEPS - 1
    b = jnp.where(last, 0.0, win_b[...])
    win_o[...] = jnp.concatenate([win_a[...], b], axis=0).T.astype(BF16)
    wout_o[...] = wout[...].astype(BF16)
    w1_o[...] = w1[...].astype(BF16)
    w2_o[...] = w2[...].astype(BF16)


def _w_in_block(i, half):
    kr_block = (Q_LORA + KV_LORA) // IN_HALF
    n_front = kr_block // 2
    shifted = jnp.where(i < CAST_STEPS - 1, 2 * i + 1 + half, kr_block)
    return jnp.where(i < n_front, 2 * i + half, shifted)


def _cast_specs(raw, l):
    w_in_t = raw[0]
    in_specs = [pl.BlockSpec((None, IN_HALF, D_MODEL), lambda i, h=h: (l, _w_in_block(i, h), 0))
                for h in range(2)]
    out_specs = [pl.BlockSpec((D_MODEL, IN_COLS_PAD // CAST_STEPS), lambda i: (0, i))]
    out_shape = [jax.ShapeDtypeStruct(BIG_SHAPES[0], BF16)]
    for a, (rows, cols) in zip(raw[1:], BIG_SHAPES[1:]):
        chunk = rows // CAST_STEPS
        in_specs.append(pl.BlockSpec((None, chunk, cols), lambda i: (l, i, 0)))
        out_specs.append(pl.BlockSpec((chunk, cols), lambda i: (i, 0)))
        out_shape.append(jax.ShapeDtypeStruct((rows, cols), BF16))
    return in_specs, out_specs, out_shape, [w_in_t, w_in_t] + list(raw[1:])


def _cast_kernel(*refs):
    _cast_stage(refs[:5], refs[5:])


def _cast_layer(raw, l):
    in_specs, out_specs, out_shape, args = _cast_specs(raw, l)
    return pl.pallas_call(
        _cast_kernel,
        grid=(CAST_STEPS,),
        in_specs=in_specs,
        out_specs=out_specs,
        out_shape=out_shape,
        compiler_params=_params(1),
        name="cast_weights",
    )(*args)


def _swap_halves(x):
    lane = lax.broadcasted_iota(jnp.int32, x.shape, 1)
    quarter = QK_ROPE // 4
    first_half = (lane & (2 * quarter - 1)) < quarter
    return jnp.where(first_half, pltpu.roll(x, LANES - quarter, 1), pltpu.roll(x, quarter, 1))


def _pre_stage(x, mod, w, rope, seq_len, q_ref, k_ref, v_ref, ob_ref, oc_ref, cache_refs):
    tm = x.shape[0]
    sh1 = mod[:, 0:D_MODEL]
    sc1 = mod[:, D_MODEL:2 * D_MODEL]
    h = _rms(x, w["g_pre_mix"][...]) * (1.0 + sc1) + sh1
    z = _dot(h.astype(BF16), w["w_in"][...])

    ckv = _rms(z[:, OFF_CKV:OFF_CKV + KV_LORA], w["g_kv"][...])
    krz = z[:, OFF_KR:OFF_KR + LANES]
    if cache_refs is not None:
        ckv_ref, kr_ref = cache_refs
        if len(ckv_ref.shape) == 4:
            ckv_ref[:, 1:] = jnp.zeros((ckv_ref.shape[0], DEPTH - 1) + ckv_ref.shape[2:], F32)
            kr_ref[:, 1:] = jnp.zeros((kr_ref.shape[0], DEPTH - 1) + kr_ref.shape[2:], F32)
            ckv_ref, kr_ref = ckv_ref.at[:, 0], kr_ref.at[:, 0]
        ckv_ref[...] = ckv.reshape(ckv_ref.shape)
        for s in range(tm // seq_len):
            kr_ref[s] = krz[s * seq_len:(s + 1) * seq_len, :].T[0:QK_ROPE, :]
    qn = _rms(z[:, OFF_Q:OFF_Q + Q_LORA], w["g_q"][...])
    q = _dot(qn.astype(BF16), w["w_q"][...]) * SM_SCALE
    kv = _dot(ckv.astype(BF16), w["w_kv"][...])
    if rope is not None:
        cos, sin = rope
        krz = krz * cos + _swap_halves(krz) * sin
    krz = krz.astype(BF16)
    for hd in range(N_HEADS):
        lo = hd * QK_PAD
        q_ref[:, lo:lo + QK_NOPE] = q[:, lo:lo + QK_NOPE].astype(BF16)
        qr = q[:, lo + QK_NOPE:lo + QK_PAD]
        if rope is not None:
            qr = qr * cos + _swap_halves(qr) * sin
        q_ref[:, lo + QK_NOPE:lo + QK_PAD] = qr.astype(BF16)
        k_ref[:, lo:lo + QK_NOPE] = kv[:, hd * QK_NOPE:(hd + 1) * QK_NOPE].astype(BF16)
        k_ref[:, lo + QK_NOPE:lo + QK_PAD] = krz
    v_ref[...] = kv[:, N_HEADS * QK_NOPE:].astype(BF16)

    u = jax.nn.gelu(z[:, OFF_U:OFF_U + WIDTH_B])
    vn = _rms(jax.nn.gelu(z[:, OFF_V:OFF_V + WIDTH_B]), w["g_v"][...]).astype(BF16)
    lane = lax.broadcasted_iota(jnp.int32, (CHUNK, WIDTH_B), 1)
    ws = w["w_s"][...]
    bs = w["b_s"][...]
    for c in range(tm // CHUNK):
        rows = slice(c * CHUNK, (c + 1) * CHUNK)
        r = _dot(ws, vn[rows, :])
        mixed = r[(N_HEADS_B - 1) * CHUNK:, :]
        for hb in range(N_HEADS_B - 2, -1, -1):
            mixed = jnp.where(lane < (hb + 1) * HEAD_B, r[hb * CHUNK:(hb + 1) * CHUNK, :], mixed)
        ob_ref[rows, :] = (u[rows, :] * (mixed + bs)).astype(BF16)

    zc = z[:, OFF_CG:OFF_CG + WIDTH_C] * z[:, OFF_HH:OFF_HH + WIDTH_C]
    pos = lax.broadcasted_iota(jnp.int32, (tm, WIDTH_C), 0) & (seq_len - 1)
    z_prev = jnp.where(pos == 0, 0.0, pltpu.roll(zc, 1, 0))
    z_next = jnp.where(pos == seq_len - 1, 0.0, pltpu.roll(zc, tm - 1, 0))
    wc = w["w_conv"][...]
    y = z_prev * wc[0:1, :] + zc * wc[1:2, :] + z_next * wc[2:3, :]
    oc_ref[...] = (z[:, OFF_BG:OFF_BG + WIDTH_C] * y).astype(BF16)


def _attn_stage(q_ref, k_ref, v_ref, cache, n_seq, oa_ref):
    ql = q_ref.shape[0] // n_seq
    kl = k_ref.shape[0] // n_seq
    for s in range(n_seq):
        qrows = slice(s * ql, (s + 1) * ql)
        krows = slice(s * kl, (s + 1) * kl)
        for hd in range(N_HEADS):
            qk_cols = slice(hd * QK_PAD, (hd + 1) * QK_PAD)
            v_cols = slice(hd * V_HEAD, (hd + 1) * V_HEAD)
            qh = q_ref[qrows, qk_cols]
            s_lat = _dot_nt(qh, k_ref[krows, qk_cols])
            mx = jnp.max(s_lat, axis=-1, keepdims=True)
            if cache is not None:
                kc_ref, vc_ref = cache
                s_ctx = _dot_nt(qh, kc_ref[:, qk_cols])
                mx = jnp.maximum(mx, jnp.max(s_ctx, axis=-1, keepdims=True))
            p_lat = jnp.exp(s_lat - mx)
            den = jnp.sum(p_lat, axis=-1, keepdims=True)
            o = _dot(p_lat.astype(BF16), v_ref[krows, v_cols])
            if cache is not None:
                p_ctx = jnp.exp(s_ctx - mx)
                den = den + jnp.sum(p_ctx, axis=-1, keepdims=True)
                o = o + _dot(p_ctx.astype(BF16), vc_ref[:, v_cols])
            oa_ref[qrows, v_cols] = (o * (1.0 / den)).astype(BF16)


class _WeightFetch:
    def __init__(self, names, hbm_refs, vmem_refs, sem):
        self.copies = {n: pltpu.make_async_copy(h, v, sem.at[j])
                       for j, (n, h, v) in enumerate(zip(names, hbm_refs, vmem_refs))}
        self.vmem = dict(zip(names, vmem_refs))
        self.first = False

    def start_all(self):
        if self.first:
            for c in self.copies.values():
                c.start()

    def wait(self, *names):
        if self.first:
            for n in names:
                self.copies[n].wait()

    def run(self, body):
        def traced(first):
            self.first = first
            body()
        pl.when(pl.program_id(0) == 0)(functools.partial(traced, True))
        pl.when(pl.program_id(0) != 0)(functools.partial(traced, False))


def _post_stage(x, mod, mix_ref, w, fetch):
    ga1 = mod[:, 2 * D_MODEL:3 * D_MODEL]
    sh2 = mod[:, 3 * D_MODEL:4 * D_MODEL]
    sc2 = mod[:, 4 * D_MODEL:5 * D_MODEL]
    ga2 = mod[:, 5 * D_MODEL:6 * D_MODEL]
    fetch.wait("w_out")
    x = x + ga1 * _rms(_dot(mix_ref[...], w["w_out"][...]), w["g_post_mix"][...])
    h2 = (_rms(x, w["g_pre_ffn"][...]) * (1.0 + sc2) + sh2).astype(BF16)
    fetch.wait("w_ff1", "w_ff2")
    f = None
    for j in range(D_FF // FF_CHUNK):
        cols = slice(j * FF_CHUNK, (j + 1) * FF_CHUNK)
        a = jnp.square(jnp.maximum(_dot(h2, w["w_ff1"][:, cols]), 0.0)).astype(BF16)
        part = _dot(a, w["w_ff2"][cols, :])
        f = part if f is None else f + part
    return x + ga2 * _rms(f, w["g_post_ffn"][...])


def _context_kernel(*refs, seq_len, aliased, cast_next):
    n_w = len(PRE_WEIGHTS) + len(POST_WEIGHTS)
    n_cast_out = len(BIG_WEIGHTS) if cast_next else 0
    n_cast_in = n_cast_out + 1 if cast_next else 0
    x_ref, mod_ref = refs[:2]
    w = dict(zip(PRE_WEIGHTS + POST_WEIGHTS, refs[2:2 + n_w]))
    cast_in = refs[2 + n_w:2 + n_w + n_cast_in]
    n_in = 2 + n_w + n_cast_in + (2 if aliased else 0)
    o_ref, ckv_ref, kr_ref = refs[n_in:n_in + 3]
    cast_out = refs[n_in + 3:n_in + 3 + n_cast_out]
    scratch = refs[n_in + 3 + n_cast_out:]
    q_ref, k_ref, v_ref, mix_ref = scratch[:4]
    fetch = _WeightFetch(BIG_WEIGHTS, [w[n] for n in BIG_WEIGHTS], scratch[4:-1], scratch[-1])
    w.update(fetch.vmem)

    def body():
        fetch.start_all()
        if cast_next:
            _cast_stage(cast_in, cast_out)
        x = x_ref[...]
        mod = mod_ref[...]
        fetch.wait("w_in")
        _pre_stage(x, mod, w, None, seq_len, q_ref, k_ref, v_ref,
                   mix_ref.at[:, WIDTH_A:WIDTH_A + WIDTH_B], mix_ref.at[:, WIDTH_A + WIDTH_B:],
                   (ckv_ref, kr_ref))
        _attn_stage(q_ref, k_ref, v_ref, None, x.shape[0] // seq_len, mix_ref.at[:, 0:WIDTH_A])
        o_ref[...] = _post_stage(x, mod, mix_ref, w, fetch)

    fetch.run(body)


def _context_layer(x, mods, wts, l, new_ckv, new_kr, raw_big, *, seq_len, seqs_per_tile):
    t = x.shape[0]
    tm = seq_len * seqs_per_tile
    n_seq_total = t // seq_len
    aliased = new_ckv is not None
    assert aliased or l == 0
    cast_next = raw_big is not None
    row_spec = pl.BlockSpec((tm, D_MODEL), lambda i: (i, 0))
    layer_dim, layer_idx = (None, l) if aliased else (DEPTH, 0)
    ckv_spec = pl.BlockSpec((seqs_per_tile, layer_dim, seq_len, KV_LORA), lambda i: (i, layer_idx, 0, 0))
    kr_spec = pl.BlockSpec((seqs_per_tile, layer_dim, QK_ROPE, seq_len), lambda i: (i, layer_idx, 0, 0))
    in_specs = [row_spec, _mod_spec(t // tm)]
    args = [x, mods]
    for n in PRE_WEIGHTS + POST_WEIGHTS:
        in_specs.append(_weight_spec(n, wts[n].shape, l))
        args.append(wts[n])
    out_specs = [row_spec, ckv_spec, kr_spec]
    out_shape = [
        jax.ShapeDtypeStruct((t, D_MODEL), F32),
        jax.ShapeDtypeStruct((n_seq_total, DEPTH, seq_len, KV_LORA), F32),
        jax.ShapeDtypeStruct((n_seq_total, DEPTH, QK_ROPE, seq_len), F32),
    ]
    if cast_next:
        assert t // tm == CAST_STEPS
        c_in, c_out, c_shape, c_args = _cast_specs(raw_big, l + 1)
        in_specs += c_in
        args += c_args
        out_specs += c_out
        out_shape += c_shape
    aliases = {}
    if aliased:
        in_specs += [pl.BlockSpec(memory_space=pl.ANY)] * 2
        aliases = {len(args): 1, len(args) + 1: 2}
        args += [new_ckv, new_kr]
    return pl.pallas_call(
        functools.partial(_context_kernel, seq_len=seq_len, aliased=aliased, cast_next=cast_next),
        grid=(t // tm,),
        in_specs=in_specs,
        out_specs=out_specs,
        out_shape=out_shape,
        scratch_shapes=[
            pltpu.VMEM((tm, QK_WIDTH), BF16), pltpu.VMEM((tm, QK_WIDTH), BF16),
            pltpu.VMEM((tm, WIDTH_A), BF16), pltpu.VMEM((tm, D_MODEL), BF16),
        ] + _fetch_scratch(BIG_WEIGHTS, wts),
        input_output_aliases=aliases,
        compiler_params=_params(1, sequential=True),
        name="context_layer",
    )(*args)


def _latent_pre_kernel(*refs, seq_len):
    x_ref, mod_ref = refs[:2]
    w = dict(zip(PRE_WEIGHTS, refs[2:2 + len(PRE_WEIGHTS)]))
    cos_ref, sin_ref, q_ref, k_ref, v_ref, obc_ref = refs[2 + len(PRE_WEIGHTS):]
    _pre_stage(x_ref[...], mod_ref[...], w, (cos_ref[...], sin_ref[...]), seq_len,
               q_ref, k_ref, v_ref, obc_ref.at[:, 0:WIDTH_B], obc_ref.at[:, WIDTH_B:], None)


def _latent_pre(x, mods, wts, l, rope_tabs, *, seq_len):
    t = x.shape[0]
    tm = seq_len
    row_spec = lambda w: pl.BlockSpec((tm, w), lambda i: (i, 0))
    in_specs = [row_spec(D_MODEL), _mod_spec(1)]
    args = [x, mods]
    for n in PRE_WEIGHTS:
        in_specs.append(_layer_spec(wts[n].shape, l))
        args.append(wts[n])
    in_specs += [pl.BlockSpec((tm, LANES), lambda i: (0, 0), pipeline_mode=pl.Buffered(1))] * 2
    args += list(rope_tabs)
    return pl.pallas_call(
        functools.partial(_latent_pre_kernel, seq_len=seq_len),
        grid=(t // tm,),
        in_specs=in_specs,
        out_specs=[row_spec(QK_WIDTH), row_spec(QK_WIDTH), row_spec(WIDTH_A), row_spec(BC_WIDTH)],
        out_shape=[
            jax.ShapeDtypeStruct((t, QK_WIDTH), BF16),
            jax.ShapeDtypeStruct((t, QK_WIDTH), BF16),
            jax.ShapeDtypeStruct((t, WIDTH_A), BF16),
            jax.ShapeDtypeStruct((t, BC_WIDTH), BF16),
        ],
        compiler_params=_params(1),
        name="latent_pre",
    )(*args)


def _latent_post_kernel(*refs):
    q_ref, k_ref, v_ref, kc_ref, vc_ref, obc_ref, x_ref, mod_ref = refs[:8]
    w = dict(zip(POST_WEIGHTS, refs[8:8 + len(POST_WEIGHTS)]))
    o_ref, mix_ref = refs[8 + len(POST_WEIGHTS):10 + len(POST_WEIGHTS)]
    scratch = refs[10 + len(POST_WEIGHTS):]
    fetch = _WeightFetch(POST_BIG, [w[n] for n in POST_BIG], scratch[:-1], scratch[-1])
    w.update(fetch.vmem)

    def body():
        fetch.start_all()
        _attn_stage(q_ref, k_ref, v_ref, (kc_ref, vc_ref), 1, mix_ref.at[:, 0:WIDTH_A])
        mix_ref[:, WIDTH_A:] = obc_ref[...]
        o_ref[...] = _post_stage(x_ref[...], mod_ref[...], mix_ref, w, fetch)

    fetch.run(body)


def _latent_post(q, k, v, obc, x, mods, cache, wts, l, *, seq_len, tq):
    t = x.shape[0]
    tiles_per_seq = seq_len // tq
    kc, vc = cache
    past = kc.shape[2]
    seq_of = lambda i: i // tiles_per_seq
    row_spec = lambda w: pl.BlockSpec((tq, w), lambda i: (i, 0))
    in_specs = [
        row_spec(QK_WIDTH),
        pl.BlockSpec((seq_len, QK_WIDTH), lambda i: (seq_of(i), 0)),
        pl.BlockSpec((seq_len, WIDTH_A), lambda i: (seq_of(i), 0)),
        pl.BlockSpec((None, None, past, QK_WIDTH), lambda i: (l, seq_of(i), 0, 0)),
        pl.BlockSpec((None, None, past, WIDTH_A), lambda i: (l, seq_of(i), 0, 0)),
        row_spec(BC_WIDTH),
        row_spec(D_MODEL),
        _mod_spec(tiles_per_seq),
    ]
    args = [q, k, v, kc, vc, obc, x, mods]
    for n in POST_WEIGHTS:
        in_specs.append(_weight_spec(n, wts[n].shape, l))
        args.append(wts[n])
    return pl.pallas_call(
        _latent_post_kernel,
        grid=(t // tq,),
        in_specs=in_specs,
        out_specs=row_spec(D_MODEL),
        out_shape=jax.ShapeDtypeStruct((t, D_MODEL), F32),
        scratch_shapes=[pltpu.VMEM((tq, D_MODEL), BF16)] + _fetch_scratch(POST_BIG, wts),
        compiler_params=_params(1, sequential=True),
        name="latent_post",
    )(*args)


def _rope_tables(n_tokens):
    rows = n_tokens // GRID_W
    row = np.repeat(np.arange(rows, dtype=np.float64), GRID_W)
    col = np.tile(np.arange(GRID_W, dtype=np.float64), rows)
    nf = QK_ROPE // 4
    inv = ROPE_THETA ** (-np.arange(nf, dtype=np.float64) / nf)
    ang_r = row[:, None] * inv
    ang_c = col[:, None] * inv
    zeros = np.zeros((n_tokens, LANES - QK_ROPE))
    cos = np.concatenate([np.cos(ang_r), np.cos(ang_r), np.cos(ang_c), np.cos(ang_c), zeros], axis=1)
    sin = np.concatenate([-np.sin(ang_r), np.sin(ang_r), -np.sin(ang_c), np.sin(ang_c), zeros], axis=1)
    return jnp.asarray(cos, F32), jnp.asarray(sin, F32)


def _prepare_weights(w_uq, w_ukv, w_s, b_s, w_conv, gains):
    w_q = jnp.pad(w_uq, ((0, 0), (0, 0), (0, 0), (0, QK_PAD - QK_NOPE - QK_ROPE)))
    w_q = w_q.reshape(DEPTH, Q_LORA, QK_WIDTH).astype(BF16)
    w_kv = jnp.concatenate(
        [w_ukv[..., :QK_NOPE].reshape(DEPTH, KV_LORA, N_HEADS * QK_NOPE),
         w_ukv[..., QK_NOPE:].reshape(DEPTH, KV_LORA, N_HEADS * V_HEAD)], axis=-1).astype(BF16)
    wts = {
        "w_q": w_q, "w_kv": w_kv,
        "w_s": w_s.reshape(DEPTH, N_HEADS_B * CHUNK, CHUNK).astype(BF16),
        "b_s": jnp.repeat(jnp.swapaxes(b_s, 1, 2), HEAD_B, axis=-1),
        "w_conv": w_conv,
    }
    for name, g in gains.items():
        wts[name] = g.reshape(DEPTH, 1, g.shape[-1])
    return wts


def kernel(x_prompt, x_sample, cache_ckv, cache_krope, c, c_ctx, w_ada, b_ada, g_pre_mix, w_in, g_q, w_uq, g_kv, w_ukv, g_v, w_s, b_s, w_conv, w_out, g_post_mix, g_pre_ffn, w_ff1, w_ff2, g_post_ffn):
    batch, seq, _ = x_prompt.shape
    dec_batch, dec_seq, _ = x_sample.shape

    wts = _prepare_weights(
        w_uq, w_ukv, w_s, b_s, w_conv,
        {"g_pre_mix": g_pre_mix, "g_q": g_q, "g_kv": g_kv, "g_v": g_v,
         "g_post_mix": g_post_mix, "g_pre_ffn": g_pre_ffn, "g_post_ffn": g_post_ffn})
    raw_big = (jnp.swapaxes(w_in, 1, 2), w_out, w_ff1, w_ff2)
    big = _cast_layer(raw_big, 0)

    cond = jnp.concatenate(
        [c_ctx[None, :], c, jnp.zeros((COND_ROWS - 1 - dec_batch, D_MODEL), F32)], axis=0)
    mods = _modulations(cond, w_ada, b_ada)
    mods_p = mods[:, 0:1, :].reshape(DEPTH, 1, 1, N_MOD * D_MODEL)
    mods_s = mods[:, 1:1 + dec_batch, :].reshape(DEPTH, dec_batch, 1, N_MOD * D_MODEL)

    cache = _cache_kv(cache_ckv, jnp.swapaxes(cache_krope, 2, 3), wts["w_kv"])
    rope_tabs = _rope_tables(dec_seq)

    xp = x_prompt.reshape(batch * seq, D_MODEL)
    xs = x_sample.reshape(dec_batch * dec_seq, D_MODEL)
    new_ckv = new_kr = None
    for l in range(DEPTH):
        wl = dict(wts, **dict(zip(BIG_WEIGHTS, big)))
        xp, new_ckv, new_kr, *big = _context_layer(
            xp, mods_p[l], wl, l, new_ckv, new_kr, raw_big if l + 1 < DEPTH else None,
            seq_len=seq, seqs_per_tile=2)
        q, k, v, obc = _latent_pre(xs, mods_s[l], wl, l, rope_tabs, seq_len=dec_seq)
        xs = _latent_post(q, k, v, obc, xs, mods_s[l], cache, wl, l, seq_len=dec_seq, tq=512)

    return (xp.reshape(batch, seq, D_MODEL), xs.reshape(dec_batch, dec_seq, D_MODEL),
            new_ckv, jnp.swapaxes(new_kr, 2, 3))
```

```python
import functools
import math

import jax
import jax.numpy as jnp
import numpy as np
from jax import lax
from jax.experimental import pallas as pl
from jax.experimental.pallas import tpu as pltpu

F32 = jnp.float32
BF16 = jnp.bfloat16

D_MODEL = 1024
DEPTH = 4
GRID_W = 64
N_HEADS = 4
QK_NOPE = 128
QK_ROPE = 64
V_HEAD = 128
Q_LORA = 384
KV_LORA = 256
WIDTH_A = N_HEADS * V_HEAD
ROPE_THETA = 10000.0
WIDTH_B = 256
N_HEADS_B = 4
HEAD_B = WIDTH_B // N_HEADS_B
CHUNK = 128
WIDTH_C = 256
D_FF = 4 * D_MODEL
N_MOD = 6
EPS = 1e-6

LANES = 128
QK_PAD = 2 * LANES
QK_WIDTH = N_HEADS * QK_PAD
KV_COLS = N_HEADS * (QK_NOPE + V_HEAD)
BC_WIDTH = WIDTH_B + WIDTH_C
OFF_Q = 0
OFF_CKV = OFF_Q + Q_LORA
OFF_U = OFF_CKV + KV_LORA
OFF_V = OFF_U + WIDTH_B
OFF_BG = OFF_V + WIDTH_B
OFF_CG = OFF_BG + WIDTH_C
OFF_HH = OFF_CG + WIDTH_C
OFF_KR = OFF_HH + WIDTH_C
IN_COLS_PAD = OFF_KR + LANES
COND_ROWS = 8
VMEM_LIMIT = 60 * 1024 * 1024
SM_SCALE = 1.0 / math.sqrt(QK_NOPE + QK_ROPE)
FF_CHUNK = 1024

PRE_WEIGHTS = ("g_pre_mix", "w_in", "g_q", "w_q", "g_kv", "w_kv", "g_v", "w_s", "b_s", "w_conv")
POST_WEIGHTS = ("g_post_mix", "w_out", "g_pre_ffn", "w_ff1", "w_ff2", "g_post_ffn")
BIG_WEIGHTS = ("w_in", "w_out", "w_ff1", "w_ff2")
BIG_SHAPES = ((D_MODEL, IN_COLS_PAD), (D_MODEL, D_MODEL), (D_MODEL, D_FF), (D_FF, D_MODEL))
CAST_STEPS = 16


def _rms(x, g):
    return x * lax.rsqrt(jnp.mean(x * x, axis=-1, keepdims=True) + EPS) * g


def _dot(a, b):
    return jnp.dot(a, b, preferred_element_type=F32)


def _dot_nt(a, b):
    return lax.dot_general(a, b, (((1,), (1,)), ((), ())), preferred_element_type=F32)


def _params(n_axes, sequential=False):
    semantics = "arbitrary" if sequential else "parallel"
    return pltpu.CompilerParams(
        dimension_semantics=(semantics,) * n_axes, vmem_limit_bytes=VMEM_LIMIT)


def _layer_spec(shape, l):
    if len(shape) == 2:
        return pl.BlockSpec(tuple(shape), lambda *_: (0, 0), pipeline_mode=pl.Buffered(1))
    return pl.BlockSpec((None,) + tuple(shape[1:]), lambda *_: (l, 0, 0),
                        pipeline_mode=pl.Buffered(1))


def _mod_spec(tiles_per_mod):
    return pl.BlockSpec((None, 1, N_MOD * D_MODEL), lambda i: (i // tiles_per_mod, 0, 0))


def _mod_kernel(cond_ref, w_ref, b_ref, o_ref):
    c = cond_ref[...]
    s = c / (1.0 + jnp.exp(-c))
    o_ref[...] = _dot(s.astype(BF16), w_ref[...].astype(BF16)) + b_ref[...]


def _modulations(cond, w_ada, b_ada):
    tn = 1536
    n_cols = N_MOD * D_MODEL
    return pl.pallas_call(
        _mod_kernel,
        grid=(DEPTH, n_cols // tn),
        in_specs=[
            pl.BlockSpec((COND_ROWS, D_MODEL), lambda l, j: (0, 0)),
            pl.BlockSpec((None, D_MODEL, tn), lambda l, j: (l, 0, j)),
            pl.BlockSpec((None, 1, tn), lambda l, j: (l, 0, j)),
        ],
        out_specs=pl.BlockSpec((None, COND_ROWS, tn), lambda l, j: (l, 0, j)),
        out_shape=jax.ShapeDtypeStruct((DEPTH, COND_ROWS, n_cols), F32),
        compiler_params=_params(2),
        name="modulation",
    )(cond, w_ada, b_ada.reshape(DEPTH, 1, n_cols))


def _cache_kv_kernel(ckv_ref, kr_ref, wkv_ref, k_ref, v_ref):
    kv = _dot(ckv_ref[...].astype(BF16), wkv_ref[...])
    kr_t = kr_ref[...]
    pad = jnp.zeros((QK_PAD - QK_NOPE - QK_ROPE, kr_t.shape[1]), F32)
    krz = jnp.concatenate([kr_t, pad], axis=0).T.astype(BF16)
    for h in range(N_HEADS):
        lo = h * QK_PAD
        k_ref[:, lo:lo + QK_NOPE] = kv[:, h * QK_NOPE:(h + 1) * QK_NOPE].astype(BF16)
        k_ref[:, lo + QK_NOPE:lo + QK_PAD] = krz
    v_ref[...] = kv[:, N_HEADS * QK_NOPE:].astype(BF16)


def _cache_kv(cache_ckv, cache_krope_t, w_kv):
    nb, _, past, _ = cache_ckv.shape
    return pl.pallas_call(
        _cache_kv_kernel,
        grid=(DEPTH, nb),
        in_specs=[
            pl.BlockSpec((None, None, past, KV_LORA), lambda l, b: (b, l, 0, 0)),
            pl.BlockSpec((None, None, QK_ROPE, past), lambda l, b: (b, l, 0, 0)),
            pl.BlockSpec((None, KV_LORA, KV_COLS), lambda l, b: (l, 0, 0)),
        ],
        out_specs=[
            pl.BlockSpec((None, None, past, QK_WIDTH), lambda l, b: (l, b, 0, 0)),
            pl.BlockSpec((None, None, past, WIDTH_A), lambda l, b: (l, b, 0, 0)),
        ],
        out_shape=[
            jax.ShapeDtypeStruct((DEPTH, nb, past, QK_WIDTH), BF16),
            jax.ShapeDtypeStruct((DEPTH, nb, past, WIDTH_A), BF16),
        ],
        compiler_params=_params(2),
        name="cache_kv",
    )(cache_ckv, cache_krope_t, w_kv)


IN_HALF = IN_COLS_PAD // CAST_STEPS // 2


def _cast_stage(in_refs, out_refs):
    win_a, win_b, wout, w1, w2 = in_refs
    win_o, wout_o, w1_o, w2_o = out_refs
    last = pl.program_id(0) == CAST_STEPS - 1
    b = jnp.where(last, 0.0, win_b[...])
    win_o[...] = jnp.concatenate([win_a[...], b], axis=0).T.astype(BF16)
    wout_o[...] = wout[...].astype(BF16)
    w1_o[...] = w1[...].astype(BF16)
    w2_o[...] = w2[...].astype(BF16)


def _w_in_block(i, half):
    kr_block = (Q_LORA + KV_LORA) // IN_HALF
    n_front = kr_block // 2
    shifted = jnp.where(i < CAST_STEPS - 1, 2 * i + 1 + half, kr_block)
    return jnp.where(i < n_front, 2 * i + half, shifted)


def _cast_specs(raw, l, chunk_of=lambda i: i):
    w_in_t = raw[0]
    in_specs = [pl.BlockSpec((None, IN_HALF, D_MODEL),
                             lambda i, h=h: (l, _w_in_block(chunk_of(i), h), 0))
                for h in range(2)]
    out_specs = [pl.BlockSpec((D_MODEL, IN_COLS_PAD // CAST_STEPS), lambda i: (0, chunk_of(i)))]
    out_shape = [jax.ShapeDtypeStruct(BIG_SHAPES[0], BF16)]
    for a, (rows, cols) in zip(raw[1:], BIG_SHAPES[1:]):
        chunk = rows // CAST_STEPS
        in_specs.append(pl.BlockSpec((None, chunk, cols), lambda i: (l, chunk_of(i), 0)))
        out_specs.append(pl.BlockSpec((chunk, cols), lambda i: (chunk_of(i), 0)))
        out_shape.append(jax.ShapeDtypeStruct((rows, cols), BF16))
    return in_specs, out_specs, out_shape, [w_in_t, w_in_t] + list(raw[1:])


def _cast_kernel(*refs):
    _cast_stage(refs[:5], refs[5:])


def _cast_layer(raw, l):
    in_specs, out_specs, out_shape, args = _cast_specs(raw, l)
    return pl.pallas_call(
        _cast_kernel,
        grid=(CAST_STEPS,),
        in_specs=in_specs,
        out_specs=out_specs,
        out_shape=out_shape,
        compiler_params=_params(1),
        name="cast_weights",
    )(*args)


def _swap_halves(x):
    lane = lax.broadcasted_iota(jnp.int32, x.shape, 1)
    quarter = QK_ROPE // 4
    first_half = (lane & (2 * quarter - 1)) < quarter
    return jnp.where(first_half, pltpu.roll(x, LANES - quarter, 1), pltpu.roll(x, quarter, 1))


def _pre_stage(x, mod, w, rope, seq_len, q_ref, k_ref, v_ref, ob_ref, oc_ref, cache_refs):
    tm = x.shape[0]
    sh1 = mod[:, 0:D_MODEL]
    sc1 = mod[:, D_MODEL:2 * D_MODEL]
    h = _rms(x, w["g_pre_mix"][...]) * (1.0 + sc1) + sh1
    z = _dot(h.astype(BF16), w["w_in"][...])

    ckv = _rms(z[:, OFF_CKV:OFF_CKV + KV_LORA], w["g_kv"][...])
    krz = z[:, OFF_KR:OFF_KR + LANES]
    if cache_refs is not None:
        ckv_ref, kr_ref = cache_refs
        ckv_ref[...] = ckv.reshape(ckv_ref.shape)
        for s in range(tm // seq_len):
            kr_ref[s] = krz[s * seq_len:(s + 1) * seq_len, :].T[0:QK_ROPE, :]
    qn = _rms(z[:, OFF_Q:OFF_Q + Q_LORA], w["g_q"][...])
    q = _dot(qn.astype(BF16), w["w_q"][...]) * SM_SCALE
    kv = _dot(ckv.astype(BF16), w["w_kv"][...])
    if rope is not None:
        cos, sin = rope
        krz = krz * cos + _swap_halves(krz) * sin
    krz = krz.astype(BF16)
    for hd in range(N_HEADS):
        lo = hd * QK_PAD
        q_ref[:, lo:lo + QK_NOPE] = q[:, lo:lo + QK_NOPE].astype(BF16)
        qr = q[:, lo + QK_NOPE:lo + QK_PAD]
        if rope is not None:
            qr = qr * cos + _swap_halves(qr) * sin
        q_ref[:, lo + QK_NOPE:lo + QK_PAD] = qr.astype(BF16)
        k_ref[:, lo:lo + QK_NOPE] = kv[:, hd * QK_NOPE:(hd + 1) * QK_NOPE].astype(BF16)
        k_ref[:, lo + QK_NOPE:lo + QK_PAD] = krz
    v_ref[...] = kv[:, N_HEADS * QK_NOPE:].astype(BF16)

    u = jax.nn.gelu(z[:, OFF_U:OFF_U + WIDTH_B])
    vn = _rms(jax.nn.gelu(z[:, OFF_V:OFF_V + WIDTH_B]), w["g_v"][...]).astype(BF16)
    lane = lax.broadcasted_iota(jnp.int32, (CHUNK, WIDTH_B), 1)
    ws = w["w_s"][...]
    bs = w["b_s"][...]
    for c in range(tm // CHUNK):
        rows = slice(c * CHUNK, (c + 1) * CHUNK)
        r = _dot(ws, vn[rows, :])
        mixed = r[(N_HEADS_B - 1) * CHUNK:, :]
        for hb in range(N_HEADS_B - 2, -1, -1):
            mixed = jnp.where(lane < (hb + 1) * HEAD_B, r[hb * CHUNK:(hb + 1) * CHUNK, :], mixed)
        ob_ref[rows, :] = (u[rows, :] * (mixed + bs)).astype(BF16)

    zc = z[:, OFF_CG:OFF_CG + WIDTH_C] * z[:, OFF_HH:OFF_HH + WIDTH_C]
    pos = lax.broadcasted_iota(jnp.int32, (tm, WIDTH_C), 0) & (seq_len - 1)
    z_prev = jnp.where(pos == 0, 0.0, pltpu.roll(zc, 1, 0))
    z_next = jnp.where(pos == seq_len - 1, 0.0, pltpu.roll(zc, tm - 1, 0))
    wc = w["w_conv"][...]
    y = z_prev * wc[0:1, :] + zc * wc[1:2, :] + z_next * wc[2:3, :]
    oc_ref[...] = (z[:, OFF_BG:OFF_BG + WIDTH_C] * y).astype(BF16)


def _attn_stage(q_ref, k_ref, v_ref, cache, n_seq, oa_ref):
    ql = q_ref.shape[0] // n_seq
    kl = k_ref.shape[0] // n_seq
    for s in range(n_seq):
        qrows = slice(s * ql, (s + 1) * ql)
        krows = slice(s * kl, (s + 1) * kl)
        for hd in range(N_HEADS):
            qk_cols = slice(hd * QK_PAD, (hd + 1) * QK_PAD)
            v_cols = slice(hd * V_HEAD, (hd + 1) * V_HEAD)
            qh = q_ref[qrows, qk_cols]
            s_lat = _dot_nt(qh, k_ref[krows, qk_cols])
            mx = jnp.max(s_lat, axis=-1, keepdims=True)
            if cache is not None:
                kc_ref, vc_ref = cache
                s_ctx = _dot_nt(qh, kc_ref[:, qk_cols])
                mx = jnp.maximum(mx, jnp.max(s_ctx, axis=-1, keepdims=True))
            p_lat = jnp.exp(s_lat - mx)
            den = jnp.sum(p_lat, axis=-1, keepdims=True)
            o = _dot(p_lat.astype(BF16), v_ref[krows, v_cols])
            if cache is not None:
                p_ctx = jnp.exp(s_ctx - mx)
                den = den + jnp.sum(p_ctx, axis=-1, keepdims=True)
                o = o + _dot(p_ctx.astype(BF16), vc_ref[:, v_cols])
            oa_ref[qrows, v_cols] = (o * (1.0 / den)).astype(BF16)


def _post_stage(x, mod, mix_ref, w):
    ga1 = mod[:, 2 * D_MODEL:3 * D_MODEL]
    sh2 = mod[:, 3 * D_MODEL:4 * D_MODEL]
    sc2 = mod[:, 4 * D_MODEL:5 * D_MODEL]
    ga2 = mod[:, 5 * D_MODEL:6 * D_MODEL]
    x = x + ga1 * _rms(_dot(mix_ref[...], w["w_out"][...]), w["g_post_mix"][...])
    h2 = (_rms(x, w["g_pre_ffn"][...]) * (1.0 + sc2) + sh2).astype(BF16)
    f = None
    for j in range(D_FF // FF_CHUNK):
        cols = slice(j * FF_CHUNK, (j + 1) * FF_CHUNK)
        a = jnp.square(jnp.maximum(_dot(h2, w["w_ff1"][:, cols]), 0.0)).astype(BF16)
        part = _dot(a, w["w_ff2"][cols, :])
        f = part if f is None else f + part
    return x + ga2 * _rms(f, w["g_post_ffn"][...])


def _two_stage(first_stage, second_stage):
    j = pl.program_id(0)
    last = pl.num_programs(0) - 1

    @pl.when(j == 0)
    def _():
        first_stage()

    @pl.when(jnp.logical_and(j > 0, j < last))
    def _():
        second_stage()
        first_stage()

    @pl.when(j == last)
    def _():
        second_stage()


def _tile_specs(n_tiles, tm, width):
    cur = pl.BlockSpec((tm, width), lambda j: (jnp.minimum(j, n_tiles - 1), 0))
    prev = pl.BlockSpec((tm, width), lambda j: (jnp.maximum(j - 1, 0), 0))
    return cur, prev


def _context_kernel(*refs, seq_len, aliased, cast_next):
    n_w = len(PRE_WEIGHTS) + len(POST_WEIGHTS)
    n_cast_out = len(BIG_WEIGHTS) if cast_next else 0
    n_cast_in = n_cast_out + 1 if cast_next else 0
    x_ref, xprev_ref, mod_ref = refs[:3]
    w = dict(zip(PRE_WEIGHTS + POST_WEIGHTS, refs[3:3 + n_w]))
    cast_in = refs[3 + n_w:3 + n_w + n_cast_in]
    n_in = 3 + n_w + n_cast_in + (2 if aliased else 0)
    o_ref, ckv_ref, kr_ref = refs[n_in:n_in + 3]
    cast_out = refs[n_in + 3:n_in + 3 + n_cast_out]
    q_ref, k_ref, v_ref, mix_ref = refs[n_in + 3 + n_cast_out:]
    slot = pl.program_id(0) % 2

    def mixers():
        if cast_next:
            _cast_stage(cast_in, cast_out)
        x = x_ref[...]
        _pre_stage(x, mod_ref[...], w, None, seq_len, q_ref, k_ref, v_ref,
                   mix_ref.at[slot, :, WIDTH_A:WIDTH_A + WIDTH_B],
                   mix_ref.at[slot, :, WIDTH_A + WIDTH_B:], (ckv_ref, kr_ref))
        _attn_stage(q_ref, k_ref, v_ref, None, x.shape[0] // seq_len,
                    mix_ref.at[slot, :, 0:WIDTH_A])

    def channel_mix():
        o_ref[...] = _post_stage(xprev_ref[...], mod_ref[...], mix_ref.at[1 - slot], w)

    _two_stage(mixers, channel_mix)


def _context_layer(x, mods, wts, l, new_ckv, new_kr, raw_big, *, seq_len, seqs_per_tile):
    t = x.shape[0]
    tm = seq_len * seqs_per_tile
    n_seq_total = t // seq_len
    aliased = new_ckv is not None
    assert aliased or l == 0
    cast_next = raw_big is not None
    n_tiles = t // tm
    tile = lambda j: jnp.minimum(j, n_tiles - 1)
    cur_spec, prev_spec = _tile_specs(n_tiles, tm, D_MODEL)
    ckv_spec = pl.BlockSpec((seqs_per_tile, None, seq_len, KV_LORA), lambda j: (tile(j), l, 0, 0))
    kr_spec = pl.BlockSpec((seqs_per_tile, None, QK_ROPE, seq_len), lambda j: (tile(j), l, 0, 0))
    in_specs = [cur_spec, prev_spec, _mod_spec(n_tiles + 1)]
    args = [x, x, mods]
    for n in PRE_WEIGHTS + POST_WEIGHTS:
        in_specs.append(_layer_spec(wts[n].shape, l))
        args.append(wts[n])
    out_specs = [prev_spec, ckv_spec, kr_spec]
    out_shape = [
        jax.ShapeDtypeStruct((t, D_MODEL), F32),
        jax.ShapeDtypeStruct((n_seq_total, DEPTH, seq_len, KV_LORA), F32),
        jax.ShapeDtypeStruct((n_seq_total, DEPTH, QK_ROPE, seq_len), F32),
    ]
    if cast_next:
        assert n_tiles == CAST_STEPS
        c_in, c_out, c_shape, c_args = _cast_specs(raw_big, l + 1, tile)
        in_specs += c_in
        args += c_args
        out_specs += c_out
        out_shape += c_shape
    aliases = {}
    if aliased:
        in_specs += [pl.BlockSpec(memory_space=pl.ANY)] * 2
        aliases = {len(args): 1, len(args) + 1: 2}
        args += [new_ckv, new_kr]
    return pl.pallas_call(
        functools.partial(_context_kernel, seq_len=seq_len, aliased=aliased, cast_next=cast_next),
        grid=(n_tiles + 1,),
        in_specs=in_specs,
        out_specs=out_specs,
        out_shape=out_shape,
        scratch_shapes=[
            pltpu.VMEM((tm, QK_WIDTH), BF16), pltpu.VMEM((tm, QK_WIDTH), BF16),
            pltpu.VMEM((tm, WIDTH_A), BF16), pltpu.VMEM((2, tm, D_MODEL), BF16),
        ],
        input_output_aliases=aliases,
        compiler_params=_params(1, sequential=True),
        name="context_layer",
    )(*args)


def _latent_pre_kernel(*refs, seq_len):
    x_ref, mod_ref = refs[:2]
    w = dict(zip(PRE_WEIGHTS, refs[2:2 + len(PRE_WEIGHTS)]))
    cos_ref, sin_ref, q_ref, k_ref, v_ref, obc_ref = refs[2 + len(PRE_WEIGHTS):]
    _pre_stage(x_ref[...], mod_ref[...], w, (cos_ref[...], sin_ref[...]), seq_len,
               q_ref, k_ref, v_ref, obc_ref.at[:, 0:WIDTH_B], obc_ref.at[:, WIDTH_B:], None)


def _latent_pre(x, mods, wts, l, rope_tabs, *, seq_len):
    t = x.shape[0]
    tm = seq_len
    row_spec = lambda w: pl.BlockSpec((tm, w), lambda i: (i, 0))
    in_specs = [row_spec(D_MODEL), _mod_spec(1)]
    args = [x, mods]
    for n in PRE_WEIGHTS:
        in_specs.append(_layer_spec(wts[n].shape, l))
        args.append(wts[n])
    in_specs += [pl.BlockSpec((tm, LANES), lambda i: (0, 0), pipeline_mode=pl.Buffered(1))] * 2
    args += list(rope_tabs)
    return pl.pallas_call(
        functools.partial(_latent_pre_kernel, seq_len=seq_len),
        grid=(t // tm,),
        in_specs=in_specs,
        out_specs=[row_spec(QK_WIDTH), row_spec(QK_WIDTH), row_spec(WIDTH_A), row_spec(BC_WIDTH)],
        out_shape=[
            jax.ShapeDtypeStruct((t, QK_WIDTH), BF16),
            jax.ShapeDtypeStruct((t, QK_WIDTH), BF16),
            jax.ShapeDtypeStruct((t, WIDTH_A), BF16),
            jax.ShapeDtypeStruct((t, BC_WIDTH), BF16),
        ],
        compiler_params=_params(1),
        name="latent_pre",
    )(*args)


def _latent_post_kernel(*refs):
    q_ref, k_ref, v_ref, kc_ref, vc_ref, obc_ref, x_ref, mod_ref = refs[:8]
    w = dict(zip(POST_WEIGHTS, refs[8:8 + len(POST_WEIGHTS)]))
    o_ref, mix_ref = refs[8 + len(POST_WEIGHTS):]
    slot = pl.program_id(0) % 2

    def attention():
        _attn_stage(q_ref, k_ref, v_ref, (kc_ref, vc_ref), 1, mix_ref.at[slot, :, 0:WIDTH_A])

    def channel_mix():
        mix_ref[1 - slot, :, WIDTH_A:] = obc_ref[...]
        o_ref[...] = _post_stage(x_ref[...], mod_ref[...], mix_ref.at[1 - slot], w)

    _two_stage(attention, channel_mix)


def _latent_post(q, k, v, obc, x, mods, cache, wts, l, *, seq_len, tq):
    t = x.shape[0]
    tiles_per_seq = seq_len // tq
    kc, vc = cache
    past = kc.shape[2]
    n_tiles = t // tq
    seq_of = lambda j: jnp.minimum(j, n_tiles - 1) // tiles_per_seq
    prev_seq_of = lambda j: jnp.maximum(j - 1, 0) // tiles_per_seq
    in_specs = [
        _tile_specs(n_tiles, tq, QK_WIDTH)[0],
        pl.BlockSpec((seq_len, QK_WIDTH), lambda j: (seq_of(j), 0)),
        pl.BlockSpec((seq_len, WIDTH_A), lambda j: (seq_of(j), 0)),
        pl.BlockSpec((None, None, past, QK_WIDTH), lambda j: (l, seq_of(j), 0, 0)),
        pl.BlockSpec((None, None, past, WIDTH_A), lambda j: (l, seq_of(j), 0, 0)),
        _tile_specs(n_tiles, tq, BC_WIDTH)[1],
        _tile_specs(n_tiles, tq, D_MODEL)[1],
        pl.BlockSpec((None, 1, N_MOD * D_MODEL), lambda j: (prev_seq_of(j), 0, 0)),
    ]
    args = [q, k, v, kc, vc, obc, x, mods]
    for n in POST_WEIGHTS:
        in_specs.append(_layer_spec(wts[n].shape, l))
        args.append(wts[n])
    return pl.pallas_call(
        _latent_post_kernel,
        grid=(n_tiles + 1,),
        in_specs=in_specs,
        out_specs=_tile_specs(n_tiles, tq, D_MODEL)[1],
        out_shape=jax.ShapeDtypeStruct((t, D_MODEL), F32),
        scratch_shapes=[pltpu.VMEM((2, tq, D_MODEL), BF16)],
        compiler_params=_params(1, sequential=True),
        name="latent_post",
    )(*args)


def _rope_tables(n_tokens):
    rows = n_tokens // GRID_W
    row = np.repeat(np.arange(rows, dtype=np.float64), GRID_W)
    col = np.tile(np.arange(GRID_W, dtype=np.float64), rows)
    nf = QK_ROPE // 4
    inv = ROPE_THETA ** (-np.arange(nf, dtype=np.float64) / nf)
    ang_r = row[:, None] * inv
    ang_c = col[:, None] * inv
    zeros = np.zeros((n_tokens, LANES - QK_ROPE))
    cos = np.concatenate([np.cos(ang_r), np.cos(ang_r), np.cos(ang_c), np.cos(ang_c), zeros], axis=1)
    sin = np.concatenate([-np.sin(ang_r), np.sin(ang_r), -np.sin(ang_c), np.sin(ang_c), zeros], axis=1)
    return jnp.asarray(cos, F32), jnp.asarray(sin, F32)


def _prepare_weights(w_uq, w_ukv, w_s, b_s, w_conv, gains):
    w_q = jnp.pad(w_uq, ((0, 0), (0, 0), (0, 0), (0, QK_PAD - QK_NOPE - QK_ROPE)))
    w_q = w_q.reshape(DEPTH, Q_LORA, QK_WIDTH).astype(BF16)
    w_kv = jnp.concatenate(
        [w_ukv[..., :QK_NOPE].reshape(DEPTH, KV_LORA, N_HEADS * QK_NOPE),
         w_ukv[..., QK_NOPE:].reshape(DEPTH, KV_LORA, N_HEADS * V_HEAD)], axis=-1).astype(BF16)
    wts = {
        "w_q": w_q, "w_kv": w_kv,
        "w_s": w_s.reshape(DEPTH, N_HEADS_B * CHUNK, CHUNK).astype(BF16),
        "b_s": jnp.repeat(jnp.swapaxes(b_s, 1, 2), HEAD_B, axis=-1),
        "w_conv": w_conv,
    }
    for name, g in gains.items():
        wts[name] = g.reshape(DEPTH, 1, g.shape[-1])
    return wts


def kernel(x_prompt, x_sample, cache_ckv, cache_krope, c, c_ctx, w_ada, b_ada, g_pre_mix, w_in, g_q, w_uq, g_kv, w_ukv, g_v, w_s, b_s, w_conv, w_out, g_post_mix, g_pre_ffn, w_ff1, w_ff2, g_post_ffn):
    batch, seq, _ = x_prompt.shape
    dec_batch, dec_seq, _ = x_sample.shape

    wts = _prepare_weights(
        w_uq, w_ukv, w_s, b_s, w_conv,
        {"g_pre_mix": g_pre_mix, "g_q": g_q, "g_kv": g_kv, "g_v": g_v,
         "g_post_mix": g_post_mix, "g_pre_ffn": g_pre_ffn, "g_post_ffn": g_post_ffn})
    raw_big = (jnp.swapaxes(w_in, 1, 2), w_out, w_ff1, w_ff2)
    big = _cast_layer(raw_big, 0)

    cond = jnp.concatenate(
        [c_ctx[None, :], c, jnp.zeros((COND_ROWS - 1 - dec_batch, D_MODEL), F32)], axis=0)
    mods = _modulations(cond, w_ada, b_ada)
    mods_p = mods[:, 0:1, :].reshape(DEPTH, 1, 1, N_MOD * D_MODEL)
    mods_s = mods[:, 1:1 + dec_batch, :].reshape(DEPTH, dec_batch, 1, N_MOD * D_MODEL)

    cache = _cache_kv(cache_ckv, jnp.swapaxes(cache_krope, 2, 3), wts["w_kv"])
    rope_tabs = _rope_tables(dec_seq)

    xp = x_prompt.reshape(batch * seq, D_MODEL)
    xs = x_sample.reshape(dec_batch * dec_seq, D_MODEL)
    new_ckv = new_kr = None
    for l in range(DEPTH):
        wl = dict(wts, **dict(zip(BIG_WEIGHTS, big)))
        xp, new_ckv, new_kr, *big = _context_layer(
            xp, mods_p[l], wl, l, new_ckv, new_kr, raw_big if l + 1 < DEPTH else None,
            seq_len=seq, seqs_per_tile=2)
        q, k, v, obc = _latent_pre(xs, mods_s[l], wl, l, rope_tabs, seq_len=dec_seq)
        xs = _latent_post(q, k, v, obc, xs, mods_s[l], cache, wl, l, seq_len=dec_seq, tq=512)

    return (xp.reshape(batch, seq, D_MODEL), xs.reshape(dec_batch, dec_seq, D_MODEL),
            new_ckv, jnp.swapaxes(new_kr, 2, 3))
```

```python
import functools
import math

import jax
import jax.numpy as jnp
import numpy as np
from jax import lax
from jax.experimental import pallas as pl
from jax.experimental.pallas import tpu as pltpu

F32 = jnp.float32
BF16 = jnp.bfloat16

D_MODEL = 1024
DEPTH = 4
GRID_W = 64
N_HEADS = 4
QK_NOPE = 128
QK_ROPE = 64
V_HEAD = 128
Q_LORA = 384
KV_LORA = 256
WIDTH_A = N_HEADS * V_HEAD
ROPE_THETA = 10000.0
WIDTH_B = 256
N_HEADS_B = 4
HEAD_B = WIDTH_B // N_HEADS_B
CHUNK = 128
WIDTH_C = 256
D_FF = 4 * D_MODEL
N_MOD = 6
EPS = 1e-6

LANES = 128
QK_PAD = 2 * LANES
QK_WIDTH = N_HEADS * QK_PAD
KV_COLS = N_HEADS * (QK_NOPE + V_HEAD)
BC_WIDTH = WIDTH_B + WIDTH_C
OFF_Q = 0
OFF_CKV = OFF_Q + Q_LORA
OFF_U = OFF_CKV + KV_LORA
OFF_V = OFF_U + WIDTH_B
OFF_BG = OFF_V + WIDTH_B
OFF_CG = OFF_BG + WIDTH_C
OFF_HH = OFF_CG + WIDTH_C
OFF_KR = OFF_HH + WIDTH_C
IN_COLS_PAD = OFF_KR + LANES
COND_ROWS = 8
VMEM_LIMIT = 56 * 1024 * 1024
SM_SCALE_LOG2 = math.log2(math.e) / math.sqrt(QK_NOPE + QK_ROPE)
FF_CHUNK = 1024
OUT_SPLIT = 2

PRE_WEIGHTS = ("g_pre_mix", "w_in", "g_q", "w_q", "g_kv", "w_kv", "g_v", "w_s", "b_s", "w_conv")
POST_WEIGHTS = ("g_post_mix", "w_out", "g_pre_ffn", "w_ff1", "w_ff2", "g_post_ffn")
BIG_WEIGHTS = ("w_in", "w_out", "w_ff1", "w_ff2")
BIG_SHAPES = ((D_MODEL, IN_COLS_PAD), (D_MODEL, D_MODEL), (D_MODEL, D_FF), (D_FF, D_MODEL))
CAST_STEPS = 16


def _rms(x, g):
    return x * lax.rsqrt(jnp.mean(x * x, axis=-1, keepdims=True) + EPS) * g


def _dot(a, b):
    return jnp.dot(a, b, preferred_element_type=F32)


def _dot_nt(a, b):
    return lax.dot_general(a, b, (((1,), (1,)), ((), ())), preferred_element_type=F32)


def _params(n_axes, sequential=False):
    semantics = "arbitrary" if sequential else "parallel"
    return pltpu.CompilerParams(
        dimension_semantics=(semantics,) * n_axes, vmem_limit_bytes=VMEM_LIMIT)


def _layer_spec(shape, l):
    if len(shape) == 2:
        return pl.BlockSpec(tuple(shape), lambda *_: (0, 0), pipeline_mode=pl.Buffered(1))
    return pl.BlockSpec((None,) + tuple(shape[1:]), lambda *_: (l, 0, 0),
                        pipeline_mode=pl.Buffered(1))


def _mod_spec(tiles_per_mod):
    return pl.BlockSpec((None, 1, N_MOD * D_MODEL), lambda i: (i // tiles_per_mod, 0, 0))


def _mod_kernel(cond_ref, w_ref, b_ref, o_ref):
    c = cond_ref[...]
    s = c / (1.0 + jnp.exp(-c))
    o_ref[...] = _dot(s.astype(BF16), w_ref[...].astype(BF16)) + b_ref[...]


def _modulations(cond, w_ada, b_ada):
    tn = 1536
    n_cols = N_MOD * D_MODEL
    return pl.pallas_call(
        _mod_kernel,
        grid=(DEPTH, n_cols // tn),
        in_specs=[
            pl.BlockSpec((COND_ROWS, D_MODEL), lambda l, j: (0, 0)),
            pl.BlockSpec((None, D_MODEL, tn), lambda l, j: (l, 0, j)),
            pl.BlockSpec((None, 1, tn), lambda l, j: (l, 0, j)),
        ],
        out_specs=pl.BlockSpec((None, COND_ROWS, tn), lambda l, j: (l, 0, j)),
        out_shape=jax.ShapeDtypeStruct((DEPTH, COND_ROWS, n_cols), F32),
        compiler_params=_params(2),
        name="modulation",
    )(cond, w_ada, b_ada.reshape(DEPTH, 1, n_cols))


def _cache_kv_kernel(ckv_ref, kr_ref, wkv_ref, k_ref, v_ref):
    kv = _dot(ckv_ref[...].astype(BF16), wkv_ref[...])
    kr_t = kr_ref[...]
    pad = jnp.zeros((QK_PAD - QK_NOPE - QK_ROPE, kr_t.shape[1]), F32)
    krz = jnp.concatenate([kr_t, pad], axis=0).T.astype(BF16)
    for h in range(N_HEADS):
        lo = h * QK_PAD
        k_ref[:, lo:lo + QK_NOPE] = kv[:, h * QK_NOPE:(h + 1) * QK_NOPE].astype(BF16)
        k_ref[:, lo + QK_NOPE:lo + QK_PAD] = krz
    v_ref[...] = kv[:, N_HEADS * QK_NOPE:].astype(BF16)


def _cache_kv(cache_ckv, cache_krope_t, w_kv):
    nb, _, past, _ = cache_ckv.shape
    return pl.pallas_call(
        _cache_kv_kernel,
        grid=(DEPTH, nb),
        in_specs=[
            pl.BlockSpec((None, None, past, KV_LORA), lambda l, b: (b, l, 0, 0)),
            pl.BlockSpec((None, None, QK_ROPE, past), lambda l, b: (b, l, 0, 0)),
            pl.BlockSpec((None, KV_LORA, KV_COLS), lambda l, b: (l, 0, 0)),
        ],
        out_specs=[
            pl.BlockSpec((None, None, past, QK_WIDTH), lambda l, b: (l, b, 0, 0)),
            pl.BlockSpec((None, None, past, WIDTH_A), lambda l, b: (l, b, 0, 0)),
        ],
        out_shape=[
            jax.ShapeDtypeStruct((DEPTH, nb, past, QK_WIDTH), BF16),
            jax.ShapeDtypeStruct((DEPTH, nb, past, WIDTH_A), BF16),
        ],
        compiler_params=_params(2),
        name="cache_kv",
    )(cache_ckv, cache_krope_t, w_kv)


IN_HALF = IN_COLS_PAD // CAST_STEPS // 2


def _cast_stage(in_refs, out_refs):
    win_a, win_b, wout, w1, w2 = in_refs
    win_o, wout_o, w1_o, w2_o = out_refs
    last = pl.program_id(0) == CAST_STEPS - 1
    b = jnp.where(last, 0.0, win_b[...])
    win_o[...] = jnp.concatenate([win_a[...], b], axis=0).T.astype(BF16)
    wout_o[...] = wout[...].astype(BF16)
    w1_o[...] = w1[...].astype(BF16)
    w2_o[...] = w2[...].astype(BF16)


def _w_in_block(i, half):
    kr_block = (Q_LORA + KV_LORA) // IN_HALF
    n_front = kr_block // 2
    shifted = jnp.where(i < CAST_STEPS - 1, 2 * i + 1 + half, kr_block)
    return jnp.where(i < n_front, 2 * i + half, shifted)


def _cast_specs(raw, l, chunk_of=lambda i: i):
    w_in_t = raw[0]
    in_specs = [pl.BlockSpec((None, IN_HALF, D_MODEL),
                             lambda i, h=h: (l, _w_in_block(chunk_of(i), h), 0))
                for h in range(2)]
    out_specs = [pl.BlockSpec((D_MODEL, IN_COLS_PAD // CAST_STEPS), lambda i: (0, chunk_of(i)))]
    out_shape = [jax.ShapeDtypeStruct(BIG_SHAPES[0], BF16)]
    for a, (rows, cols) in zip(raw[1:], BIG_SHAPES[1:]):
        chunk = rows // CAST_STEPS
        in_specs.append(pl.BlockSpec((None, chunk, cols), lambda i: (l, chunk_of(i), 0)))
        out_specs.append(pl.BlockSpec((chunk, cols), lambda i: (chunk_of(i), 0)))
        out_shape.append(jax.ShapeDtypeStruct((rows, cols), BF16))
    return in_specs, out_specs, out_shape, [w_in_t, w_in_t] + list(raw[1:])


def _cast_kernel(*refs):
    _cast_stage(refs[:5], refs[5:])


def _cast_layer(raw, l):
    in_specs, out_specs, out_shape, args = _cast_specs(raw, l)
    return pl.pallas_call(
        _cast_kernel,
        grid=(CAST_STEPS,),
        in_specs=in_specs,
        out_specs=out_specs,
        out_shape=out_shape,
        compiler_params=_params(1),
        name="cast_weights",
    )(*args)


def _swap_halves(x):
    lane = lax.broadcasted_iota(jnp.int32, x.shape, 1)
    quarter = QK_ROPE // 4
    first_half = (lane & (2 * quarter - 1)) < quarter
    return jnp.where(first_half, pltpu.roll(x, LANES - quarter, 1), pltpu.roll(x, quarter, 1))


def _pre_stage(x, mod, w, rope, seq_len, q_ref, k_ref, v_ref, ob_ref, oc_ref, cache_refs):
    tm = x.shape[0]
    sh1 = mod[:, 0:D_MODEL]
    sc1 = mod[:, D_MODEL:2 * D_MODEL]
    h = _rms(x, w["g_pre_mix"][...]) * (1.0 + sc1) + sh1
    z = _dot(h.astype(BF16), w["w_in"][...])

    ckv = _rms(z[:, OFF_CKV:OFF_CKV + KV_LORA], w["g_kv"][...])
    krz = z[:, OFF_KR:OFF_KR + LANES]
    if cache_refs is not None:
        ckv_ref, kr_ref = cache_refs
        ckv_ref[...] = ckv.reshape(ckv_ref.shape)
        for s in range(tm // seq_len):
            kr_ref[s] = krz[s * seq_len:(s + 1) * seq_len, :].T[0:QK_ROPE, :]
    qn = _rms(z[:, OFF_Q:OFF_Q + Q_LORA], w["g_q"][...])
    q = _dot(qn.astype(BF16), w["w_q"][...]) * SM_SCALE_LOG2
    kv = _dot(ckv.astype(BF16), w["w_kv"][...])
    if rope is not None:
        cos, sin = rope
        krz = krz * cos + _swap_halves(krz) * sin
    krz = krz.astype(BF16)
    for hd in range(N_HEADS):
        lo = hd * QK_PAD
        q_ref[:, lo:lo + QK_NOPE] = q[:, lo:lo + QK_NOPE].astype(BF16)
        qr = q[:, lo + QK_NOPE:lo + QK_PAD]
        if rope is not None:
            qr = qr * cos + _swap_halves(qr) * sin
        q_ref[:, lo + QK_NOPE:lo + QK_PAD] = qr.astype(BF16)
        k_ref[:, lo:lo + QK_NOPE] = kv[:, hd * QK_NOPE:(hd + 1) * QK_NOPE].astype(BF16)
        k_ref[:, lo + QK_NOPE:lo + QK_PAD] = krz
    v_ref[...] = kv[:, N_HEADS * QK_NOPE:].astype(BF16)

    u = jax.nn.gelu(z[:, OFF_U:OFF_U + WIDTH_B])
    vn = _rms(jax.nn.gelu(z[:, OFF_V:OFF_V + WIDTH_B]), w["g_v"][...]).astype(BF16)
    lane = lax.broadcasted_iota(jnp.int32, (CHUNK, WIDTH_B), 1)
    ws = w["w_s"][...]
    bs = w["b_s"][...]
    for c in range(tm // CHUNK):
        rows = slice(c * CHUNK, (c + 1) * CHUNK)
        r = _dot(ws, vn[rows, :])
        mixed = r[(N_HEADS_B - 1) * CHUNK:, :]
        for hb in range(N_HEADS_B - 2, -1, -1):
            mixed = jnp.where(lane < (hb + 1) * HEAD_B, r[hb * CHUNK:(hb + 1) * CHUNK, :], mixed)
        ob_ref[rows, :] = (u[rows, :] * (mixed + bs)).astype(BF16)

    zc = z[:, OFF_CG:OFF_CG + WIDTH_C] * z[:, OFF_HH:OFF_HH + WIDTH_C]
    pos = lax.broadcasted_iota(jnp.int32, (tm, WIDTH_C), 0) & (seq_len - 1)
    z_prev = jnp.where(pos == 0, 0.0, pltpu.roll(zc, 1, 0))
    z_next = jnp.where(pos == seq_len - 1, 0.0, pltpu.roll(zc, tm - 1, 0))
    wc = w["w_conv"][...]
    y = z_prev * wc[0:1, :] + zc * wc[1:2, :] + z_next * wc[2:3, :]
    oc_ref[...] = (z[:, OFF_BG:OFF_BG + WIDTH_C] * y).astype(BF16)


def _attn_head(q_ref, k_ref, v_ref, cache, qrows, krows, hd, oa_ref):
    qk_cols = slice(hd * QK_PAD, (hd + 1) * QK_PAD)
    v_cols = slice(hd * V_HEAD, (hd + 1) * V_HEAD)
    qh = q_ref[qrows, qk_cols]
    s_lat = _dot_nt(qh, k_ref[krows, qk_cols])
    mx = jnp.max(s_lat, axis=-1, keepdims=True)
    if cache is not None:
        kc_ref, vc_ref = cache
        s_ctx = _dot_nt(qh, kc_ref[:, qk_cols])
        mx = jnp.maximum(mx, jnp.max(s_ctx, axis=-1, keepdims=True))
    p_lat = jnp.exp2(s_lat - mx)
    den = jnp.sum(p_lat, axis=-1, keepdims=True)
    o = _dot(p_lat.astype(BF16), v_ref[krows, v_cols])
    if cache is not None:
        p_ctx = jnp.exp2(s_ctx - mx)
        den = den + jnp.sum(p_ctx, axis=-1, keepdims=True)
        o = o + _dot(p_ctx.astype(BF16), vc_ref[:, v_cols])
    oa_ref[qrows, v_cols] = (o * (1.0 / den)).astype(BF16)


def _post_stage(x_ref, mod, mix_ref, w, o_ref):
    ga1 = mod[:, 2 * D_MODEL:3 * D_MODEL]
    sh2 = mod[:, 3 * D_MODEL:4 * D_MODEL]
    sc2 = mod[:, 4 * D_MODEL:5 * D_MODEL]
    ga2 = mod[:, 5 * D_MODEL:6 * D_MODEL]
    group = x_ref.shape[0] // OUT_SPLIT
    for r in range(OUT_SPLIT):
        rows = slice(r * group, (r + 1) * group)
        mo = _dot(mix_ref[rows, :], w["w_out"][...])
        x1 = x_ref[rows, :] + ga1 * _rms(mo, w["g_post_mix"][...])
        o_ref[rows, :] = x1
        mix_ref[rows, :] = (_rms(x1, w["g_pre_ffn"][...]) * (1.0 + sc2) + sh2).astype(BF16)
    h2 = mix_ref[...]
    f = None
    for j in range(D_FF // FF_CHUNK):
        cols = slice(j * FF_CHUNK, (j + 1) * FF_CHUNK)
        a = jnp.square(jnp.maximum(_dot(h2, w["w_ff1"][:, cols]), 0.0)).astype(BF16)
        part = _dot(a, w["w_ff2"][cols, :])
        f = part if f is None else f + part
    o_ref[...] = o_ref[...] + ga2 * _rms(f, w["g_post_ffn"][...])


def _context_kernel(*refs, seq_len, aliased, cast_next):
    n_w = len(PRE_WEIGHTS) + len(POST_WEIGHTS)
    n_cast_out = len(BIG_WEIGHTS) if cast_next else 0
    n_cast_in = n_cast_out + 1 if cast_next else 0
    x_ref, mod_ref = refs[:2]
    w = dict(zip(PRE_WEIGHTS + POST_WEIGHTS, refs[2:2 + n_w]))
    cast_in = refs[2 + n_w:2 + n_w + n_cast_in]
    n_in = 2 + n_w + n_cast_in + (2 if aliased else 0)
    o_ref, ckv_ref, kr_ref = refs[n_in:n_in + 3]
    cast_out = refs[n_in + 3:n_in + 3 + n_cast_out]
    q_ref, k_ref, v_ref, mix_ref = refs[n_in + 3 + n_cast_out:]
    if cast_next:
        _cast_stage(cast_in, cast_out)
    x = x_ref[...]
    mod = mod_ref[...]
    _pre_stage(x, mod, w, None, seq_len, q_ref, k_ref, v_ref,
               mix_ref.at[:, WIDTH_A:WIDTH_A + WIDTH_B], mix_ref.at[:, WIDTH_A + WIDTH_B:],
               (ckv_ref, kr_ref))
    for s in range(x.shape[0] // seq_len):
        rows = slice(s * seq_len, (s + 1) * seq_len)
        for hd in range(N_HEADS):
            _attn_head(q_ref, k_ref, v_ref, None, rows, rows, hd, mix_ref)
    _post_stage(x_ref, mod, mix_ref, w, o_ref)


def _context_layer(x, mods, wts, l, new_ckv, new_kr, raw_big, *, seq_len, seqs_per_tile):
    t = x.shape[0]
    tm = seq_len * seqs_per_tile
    n_seq_total = t // seq_len
    aliased = new_ckv is not None
    assert aliased or l == 0
    cast_next = raw_big is not None
    n_tiles = t // tm
    row_spec = pl.BlockSpec((tm, D_MODEL), lambda i: (i, 0))
    ckv_spec = pl.BlockSpec((seqs_per_tile, None, seq_len, KV_LORA), lambda i: (i, l, 0, 0))
    kr_spec = pl.BlockSpec((seqs_per_tile, None, QK_ROPE, seq_len), lambda i: (i, l, 0, 0))
    in_specs = [row_spec, _mod_spec(n_tiles)]
    args = [x, mods]
    for n in PRE_WEIGHTS + POST_WEIGHTS:
        in_specs.append(_layer_spec(wts[n].shape, l))
        args.append(wts[n])
    out_specs = [row_spec, ckv_spec, kr_spec]
    out_shape = [
        jax.ShapeDtypeStruct((t, D_MODEL), F32),
        jax.ShapeDtypeStruct((n_seq_total, DEPTH, seq_len, KV_LORA), F32),
        jax.ShapeDtypeStruct((n_seq_total, DEPTH, QK_ROPE, seq_len), F32),
    ]
    if cast_next:
        assert n_tiles == CAST_STEPS
        c_in, c_out, c_shape, c_args = _cast_specs(raw_big, l + 1)
        in_specs += c_in
        args += c_args
        out_specs += c_out
        out_shape += c_shape
    aliases = {}
    if aliased:
        in_specs += [pl.BlockSpec(memory_space=pl.ANY)] * 2
        aliases = {len(args): 1, len(args) + 1: 2}
        args += [new_ckv, new_kr]
    return pl.pallas_call(
        functools.partial(_context_kernel, seq_len=seq_len, aliased=aliased, cast_next=cast_next),
        grid=(n_tiles,),
        in_specs=in_specs,
        out_specs=out_specs,
        out_shape=out_shape,
        scratch_shapes=[
            pltpu.VMEM((tm, QK_WIDTH), BF16), pltpu.VMEM((tm, QK_WIDTH), BF16),
            pltpu.VMEM((tm, WIDTH_A), BF16), pltpu.VMEM((tm, D_MODEL), BF16),
        ],
        input_output_aliases=aliases,
        compiler_params=_params(1),
        name="context_layer",
    )(*args)


def _latent_pre_kernel(*refs, seq_len):
    x_ref, mod_ref = refs[:2]
    w = dict(zip(PRE_WEIGHTS, refs[2:2 + len(PRE_WEIGHTS)]))
    cos_ref, sin_ref, q_ref, k_ref, v_ref, obc_ref = refs[2 + len(PRE_WEIGHTS):]
    _pre_stage(x_ref[...], mod_ref[...], w, (cos_ref[...], sin_ref[...]), seq_len,
               q_ref, k_ref, v_ref, obc_ref.at[:, 0:WIDTH_B], obc_ref.at[:, WIDTH_B:], None)


def _latent_pre(x, mods, wts, l, rope_tabs, *, seq_len):
    t = x.shape[0]
    tm = seq_len
    row_spec = lambda w: pl.BlockSpec((tm, w), lambda i: (i, 0))
    in_specs = [row_spec(D_MODEL), _mod_spec(1)]
    args = [x, mods]
    for n in PRE_WEIGHTS:
        in_specs.append(_layer_spec(wts[n].shape, l))
        args.append(wts[n])
    in_specs += [pl.BlockSpec((tm, LANES), lambda i: (0, 0), pipeline_mode=pl.Buffered(1))] * 2
    args += list(rope_tabs)
    return pl.pallas_call(
        functools.partial(_latent_pre_kernel, seq_len=seq_len),
        grid=(t // tm,),
        in_specs=in_specs,
        out_specs=[row_spec(QK_WIDTH), row_spec(QK_WIDTH), row_spec(WIDTH_A), row_spec(BC_WIDTH)],
        out_shape=[
            jax.ShapeDtypeStruct((t, QK_WIDTH), BF16),
            jax.ShapeDtypeStruct((t, QK_WIDTH), BF16),
            jax.ShapeDtypeStruct((t, WIDTH_A), BF16),
            jax.ShapeDtypeStruct((t, BC_WIDTH), BF16),
        ],
        compiler_params=_params(1),
        name="latent_pre",
    )(*args)


def _latent_post_kernel(*refs):
    q_ref, k_ref, v_ref, kc_ref, vc_ref, obc_ref, x_ref, mod_ref = refs[:8]
    w = dict(zip(POST_WEIGHTS, refs[8:8 + len(POST_WEIGHTS)]))
    o_ref, mix_ref = refs[8 + len(POST_WEIGHTS):]
    qrows = slice(0, q_ref.shape[0])
    krows = slice(0, k_ref.shape[0])
    for hd in range(N_HEADS):
        _attn_head(q_ref, k_ref, v_ref, (kc_ref, vc_ref), qrows, krows, hd, mix_ref)
    mix_ref[:, WIDTH_A:] = obc_ref[...]
    _post_stage(x_ref, mod_ref[...], mix_ref, w, o_ref)


def _latent_post(q, k, v, obc, x, mods, cache, wts, l, *, seq_len, tq):
    t = x.shape[0]
    tiles_per_seq = seq_len // tq
    kc, vc = cache
    past = kc.shape[2]
    seq_of = lambda i: i // tiles_per_seq
    row_spec = lambda w: pl.BlockSpec((tq, w), lambda i: (i, 0))
    in_specs = [
        row_spec(QK_WIDTH),
        pl.BlockSpec((seq_len, QK_WIDTH), lambda i: (seq_of(i), 0)),
        pl.BlockSpec((seq_len, WIDTH_A), lambda i: (seq_of(i), 0)),
        pl.BlockSpec((None, None, past, QK_WIDTH), lambda i: (l, seq_of(i), 0, 0)),
        pl.BlockSpec((None, None, past, WIDTH_A), lambda i: (l, seq_of(i), 0, 0)),
        row_spec(BC_WIDTH),
        row_spec(D_MODEL),
        _mod_spec(tiles_per_seq),
    ]
    args = [q, k, v, kc, vc, obc, x, mods]
    for n in POST_WEIGHTS:
        in_specs.append(_layer_spec(wts[n].shape, l))
        args.append(wts[n])
    return pl.pallas_call(
        _latent_post_kernel,
        grid=(t // tq,),
        in_specs=in_specs,
        out_specs=row_spec(D_MODEL),
        out_shape=jax.ShapeDtypeStruct((t, D_MODEL), F32),
        scratch_shapes=[pltpu.VMEM((tq, D_MODEL), BF16)],
        compiler_params=_params(1),
        name="latent_post",
    )(*args)


def _rope_tables(n_tokens):
    rows = n_tokens // GRID_W
    row = np.repeat(np.arange(rows, dtype=np.float64), GRID_W)
    col = np.tile(np.arange(GRID_W, dtype=np.float64), rows)
    nf = QK_ROPE // 4
    inv = ROPE_THETA ** (-np.arange(nf, dtype=np.float64) / nf)
    ang_r = row[:, None] * inv
    ang_c = col[:, None] * inv
    zeros = np.zeros((n_tokens, LANES - QK_ROPE))
    cos = np.concatenate([np.cos(ang_r), np.cos(ang_r), np.cos(ang_c), np.cos(ang_c), zeros], axis=1)
    sin = np.concatenate([-np.sin(ang_r), np.sin(ang_r), -np.sin(ang_c), np.sin(ang_c), zeros], axis=1)
    return jnp.asarray(cos, F32), jnp.asarray(sin, F32)


def _prepare_weights(w_uq, w_ukv, w_s, b_s, w_conv, gains):
    w_q = jnp.pad(w_uq, ((0, 0), (0, 0), (0, 0), (0, QK_PAD - QK_NOPE - QK_ROPE)))
    w_q = w_q.reshape(DEPTH, Q_LORA, QK_WIDTH).astype(BF16)
    w_kv = jnp.concatenate(
        [w_ukv[..., :QK_NOPE].reshape(DEPTH, KV_LORA, N_HEADS * QK_NOPE),
         w_ukv[..., QK_NOPE:].reshape(DEPTH, KV_LORA, N_HEADS * V_HEAD)], axis=-1).astype(BF16)
    wts = {
        "w_q": w_q, "w_kv": w_kv,
        "w_s": w_s.reshape(DEPTH, N_HEADS_B * CHUNK, CHUNK).astype(BF16),
        "b_s": jnp.repeat(jnp.swapaxes(b_s, 1, 2), HEAD_B, axis=-1),
        "w_conv": w_conv,
    }
    for name, g in gains.items():
        wts[name] = g.reshape(DEPTH, 1, g.shape[-1])
    return wts


def kernel(x_prompt, x_sample, cache_ckv, cache_krope, c, c_ctx, w_ada, b_ada, g_pre_mix, w_in, g_q, w_uq, g_kv, w_ukv, g_v, w_s, b_s, w_conv, w_out, g_post_mix, g_pre_ffn, w_ff1, w_ff2, g_post_ffn):
    batch, seq, _ = x_prompt.shape
    dec_batch, dec_seq, _ = x_sample.shape

    wts = _prepare_weights(
        w_uq, w_ukv, w_s, b_s, w_conv,
        {"g_pre_mix": g_pre_mix, "g_q": g_q, "g_kv": g_kv, "g_v": g_v,
         "g_post_mix": g_post_mix, "g_pre_ffn": g_pre_ffn, "g_post_ffn": g_post_ffn})
    raw_big = (jnp.swapaxes(w_in, 1, 2), w_out, w_ff1, w_ff2)
    big = _cast_layer(raw_big, 0)

    cond = jnp.concatenate(
        [c_ctx[None, :], c, jnp.zeros((COND_ROWS - 1 - dec_batch, D_MODEL), F32)], axis=0)
    mods = _modulations(cond, w_ada, b_ada)
    mods_p = mods[:, 0:1, :].reshape(DEPTH, 1, 1, N_MOD * D_MODEL)
    mods_s = mods[:, 1:1 + dec_batch, :].reshape(DEPTH, dec_batch, 1, N_MOD * D_MODEL)

    cache = _cache_kv(cache_ckv, jnp.swapaxes(cache_krope, 2, 3), wts["w_kv"])
    rope_tabs = _rope_tables(dec_seq)

    xp = x_prompt.reshape(batch * seq, D_MODEL)
    xs = x_sample.reshape(dec_batch * dec_seq, D_MODEL)
    new_ckv = new_kr = None
    for l in range(DEPTH):
        wl = dict(wts, **dict(zip(BIG_WEIGHTS, big)))
        xp, new_ckv, new_kr, *big = _context_layer(
            xp, mods_p[l], wl, l, new_ckv, new_kr, raw_big if l + 1 < DEPTH else None,
            seq_len=seq, seqs_per_tile=2)
        q, k, v, obc = _latent_pre(xs, mods_s[l], wl, l, rope_tabs, seq_len=dec_seq)
        xs = _latent_post(q, k, v, obc, xs, mods_s[l], cache, wl, l, seq_len=dec_seq, tq=512)

    return (xp.reshape(batch, seq, D_MODEL), xs.reshape(dec_batch, dec_seq, D_MODEL),
            new_ckv, jnp.swapaxes(new_kr, 2, 3))
```

```python
import functools
import math

import jax
import jax.numpy as jnp
import numpy as np
from jax import lax
from jax.experimental import pallas as pl
from jax.experimental.pallas import tpu as pltpu

F32 = jnp.float32
BF16 = jnp.bfloat16

D_MODEL = 1024
DEPTH = 4
GRID_W = 64
N_HEADS = 4
QK_NOPE = 128
QK_ROPE = 64
V_HEAD = 128
Q_LORA = 384
KV_LORA = 256
WIDTH_A = N_HEADS * V_HEAD
ROPE_THETA = 10000.0
WIDTH_B = 256
N_HEADS_B = 4
HEAD_B = WIDTH_B // N_HEADS_B
CHUNK = 128
WIDTH_C = 256
D_FF = 4 * D_MODEL
N_MOD = 6
EPS = 1e-6

LANES = 128
QK_PAD = 2 * LANES
QK_WIDTH = N_HEADS * QK_PAD
KV_COLS = N_HEADS * (QK_NOPE + V_HEAD)
BC_WIDTH = WIDTH_B + WIDTH_C
OFF_Q = 0
OFF_CKV = OFF_Q + Q_LORA
OFF_U = OFF_CKV + KV_LORA
OFF_V = OFF_U + WIDTH_B
OFF_BG = OFF_V + WIDTH_B
OFF_CG = OFF_BG + WIDTH_C
OFF_HH = OFF_CG + WIDTH_C
OFF_KR = OFF_HH + WIDTH_C
IN_COLS_PAD = OFF_KR + LANES
COND_ROWS = 8
VMEM_LIMIT = 56 * 1024 * 1024
SM_SCALE_LOG2 = math.log2(math.e) / math.sqrt(QK_NOPE + QK_ROPE)
FF_CHUNK = 1024
OUT_SPLIT = 2

PRE_WEIGHTS = ("g_pre_mix", "w_in", "g_q", "w_q", "g_kv", "w_kv", "g_v", "w_s", "b_s", "w_conv")
POST_WEIGHTS = ("g_post_mix", "w_out", "g_pre_ffn", "w_ff1", "w_ff2", "g_post_ffn")
BIG_WEIGHTS = ("w_in", "w_out", "w_ff1", "w_ff2")
BIG_SHAPES = ((D_MODEL, IN_COLS_PAD), (D_MODEL, D_MODEL), (D_MODEL, D_FF), (D_FF, D_MODEL))
CAST_STEPS = 16


def _rms(x, g):
    return x * lax.rsqrt(jnp.mean(x * x, axis=-1, keepdims=True) + EPS) * g


def _dot(a, b):
    return jnp.dot(a, b, preferred_element_type=F32)


def _dot_nt(a, b):
    return lax.dot_general(a, b, (((1,), (1,)), ((), ())), preferred_element_type=F32)


def _params(n_axes, sequential=False):
    semantics = "arbitrary" if sequential else "parallel"
    return pltpu.CompilerParams(
        dimension_semantics=(semantics,) * n_axes, vmem_limit_bytes=VMEM_LIMIT)


def _layer_spec(shape, l):
    if len(shape) == 2:
        return pl.BlockSpec(tuple(shape), lambda *_: (0, 0), pipeline_mode=pl.Buffered(1))
    return pl.BlockSpec((None,) + tuple(shape[1:]), lambda *_: (l, 0, 0),
                        pipeline_mode=pl.Buffered(1))


def _mod_spec(tiles_per_mod):
    return pl.BlockSpec((None, 1, N_MOD * D_MODEL), lambda i: (i // tiles_per_mod, 0, 0))


def _mod_kernel(cond_ref, w_ref, b_ref, o_ref):
    c = cond_ref[...]
    s = c / (1.0 + jnp.exp(-c))
    o_ref[...] = _dot(s.astype(BF16), w_ref[...].astype(BF16)) + b_ref[...]


def _modulations(cond, w_ada, b_ada):
    tn = 1536
    n_cols = N_MOD * D_MODEL
    return pl.pallas_call(
        _mod_kernel,
        grid=(DEPTH, n_cols // tn),
        in_specs=[
            pl.BlockSpec((COND_ROWS, D_MODEL), lambda l, j: (0, 0)),
            pl.BlockSpec((None, D_MODEL, tn), lambda l, j: (l, 0, j)),
            pl.BlockSpec((None, 1, tn), lambda l, j: (l, 0, j)),
        ],
        out_specs=pl.BlockSpec((None, COND_ROWS, tn), lambda l, j: (l, 0, j)),
        out_shape=jax.ShapeDtypeStruct((DEPTH, COND_ROWS, n_cols), F32),
        compiler_params=_params(2),
        name="modulation",
    )(cond, w_ada, b_ada.reshape(DEPTH, 1, n_cols))


def _cache_kv_kernel(ckv_ref, kr_ref, wkv_ref, k_ref, v_ref):
    nb, past, _ = ckv_ref.shape
    kv = _dot(ckv_ref[...].reshape(nb * past, KV_LORA).astype(BF16), wkv_ref[...])
    pad = jnp.zeros((QK_PAD - QK_NOPE - QK_ROPE, past), F32)
    for b in range(nb):
        rows = slice(b * past, (b + 1) * past)
        krz = jnp.concatenate([kr_ref[b], pad], axis=0).T.astype(BF16)
        for h in range(N_HEADS):
            lo = h * QK_PAD
            k_ref[b, :, lo:lo + QK_NOPE] = kv[rows, h * QK_NOPE:(h + 1) * QK_NOPE].astype(BF16)
            k_ref[b, :, lo + QK_NOPE:lo + QK_PAD] = krz
        v_ref[b] = kv[rows, N_HEADS * QK_NOPE:].astype(BF16)


def _cache_kv(cache_ckv, cache_krope_t, w_kv):
    nb, _, past, _ = cache_ckv.shape
    return pl.pallas_call(
        _cache_kv_kernel,
        grid=(DEPTH,),
        in_specs=[
            pl.BlockSpec((nb, None, past, KV_LORA), lambda l: (0, l, 0, 0)),
            pl.BlockSpec((nb, None, QK_ROPE, past), lambda l: (0, l, 0, 0)),
            pl.BlockSpec((None, KV_LORA, KV_COLS), lambda l: (l, 0, 0)),
        ],
        out_specs=[
            pl.BlockSpec((None, nb, past, QK_WIDTH), lambda l: (l, 0, 0, 0)),
            pl.BlockSpec((None, nb, past, WIDTH_A), lambda l: (l, 0, 0, 0)),
        ],
        out_shape=[
            jax.ShapeDtypeStruct((DEPTH, nb, past, QK_WIDTH), BF16),
            jax.ShapeDtypeStruct((DEPTH, nb, past, WIDTH_A), BF16),
        ],
        compiler_params=_params(1),
        name="cache_kv",
    )(cache_ckv, cache_krope_t, w_kv)


IN_HALF = IN_COLS_PAD // CAST_STEPS // 2


def _cast_stage(in_refs, out_refs):
    win_a, win_b, wout, w1, w2 = in_refs
    win_o, wout_o, w1_o, w2_o = out_refs
    last = pl.program_id(0) == CAST_STEPS - 1
    b = jnp.where(last, 0.0, win_b[...])
    win_o[...] = jnp.concatenate([win_a[...], b], axis=0).T.astype(BF16)
    wout_o[...] = wout[...].astype(BF16)
    w1_o[...] = w1[...].astype(BF16)
    w2_o[...] = w2[...].astype(BF16)


def _w_in_block(i, half):
    kr_block = (Q_LORA + KV_LORA) // IN_HALF
    n_front = kr_block // 2
    shifted = jnp.where(i < CAST_STEPS - 1, 2 * i + 1 + half, kr_block)
    return jnp.where(i < n_front, 2 * i + half, shifted)


def _cast_specs(raw, l, chunk_of=lambda i: i):
    w_in_t = raw[0]
    in_specs = [pl.BlockSpec((None, IN_HALF, D_MODEL),
                             lambda i, h=h: (l, _w_in_block(chunk_of(i), h), 0))
                for h in range(2)]
    out_specs = [pl.BlockSpec((D_MODEL, IN_COLS_PAD // CAST_STEPS), lambda i: (0, chunk_of(i)))]
    out_shape = [jax.ShapeDtypeStruct(BIG_SHAPES[0], BF16)]
    for a, (rows, cols) in zip(raw[1:], BIG_SHAPES[1:]):
        chunk = rows // CAST_STEPS
        in_specs.append(pl.BlockSpec((None, chunk, cols), lambda i: (l, chunk_of(i), 0)))
        out_specs.append(pl.BlockSpec((chunk, cols), lambda i: (chunk_of(i), 0)))
        out_shape.append(jax.ShapeDtypeStruct((rows, cols), BF16))
    return in_specs, out_specs, out_shape, [w_in_t, w_in_t] + list(raw[1:])


def _cast_kernel(*refs):
    _cast_stage(refs[:5], refs[5:])


def _cast_layer(raw, l):
    in_specs, out_specs, out_shape, args = _cast_specs(raw, l)
    return pl.pallas_call(
        _cast_kernel,
        grid=(CAST_STEPS,),
        in_specs=in_specs,
        out_specs=out_specs,
        out_shape=out_shape,
        compiler_params=_params(1),
        name="cast_weights",
    )(*args)


def _swap_halves(x):
    lane = lax.broadcasted_iota(jnp.int32, x.shape, 1)
    quarter = QK_ROPE // 4
    first_half = (lane & (2 * quarter - 1)) < quarter
    return jnp.where(first_half, pltpu.roll(x, LANES - quarter, 1), pltpu.roll(x, quarter, 1))


def _pre_stage(x, mod, w, rope, seq_len, q_ref, k_ref, v_ref, ob_ref, oc_ref, cache_refs):
    tm = x.shape[0]
    sh1 = mod[:, 0:D_MODEL]
    sc1 = mod[:, D_MODEL:2 * D_MODEL]
    h = _rms(x, w["g_pre_mix"][...]) * (1.0 + sc1) + sh1
    z = _dot(h.astype(BF16), w["w_in"][...])

    ckv = _rms(z[:, OFF_CKV:OFF_CKV + KV_LORA], w["g_kv"][...])
    krz = z[:, OFF_KR:OFF_KR + LANES]
    if cache_refs is not None:
        ckv_ref, kr_ref = cache_refs
        ckv_ref[...] = ckv.reshape(ckv_ref.shape)
        for s in range(tm // seq_len):
            kr_ref[s] = krz[s * seq_len:(s + 1) * seq_len, :].T[0:QK_ROPE, :]
    qn = _rms(z[:, OFF_Q:OFF_Q + Q_LORA], w["g_q"][...])
    q = _dot(qn.astype(BF16), w["w_q"][...]) * SM_SCALE_LOG2
    kv = _dot(ckv.astype(BF16), w["w_kv"][...])
    if rope is not None:
        cos, sin = rope
        krz = krz * cos + _swap_halves(krz) * sin
    krz = krz.astype(BF16)
    for hd in range(N_HEADS):
        lo = hd * QK_PAD
        q_ref[:, lo:lo + QK_NOPE] = q[:, lo:lo + QK_NOPE].astype(BF16)
        qr = q[:, lo + QK_NOPE:lo + QK_PAD]
        if rope is not None:
            qr = qr * cos + _swap_halves(qr) * sin
        q_ref[:, lo + QK_NOPE:lo + QK_PAD] = qr.astype(BF16)
        k_ref[:, lo:lo + QK_NOPE] = kv[:, hd * QK_NOPE:(hd + 1) * QK_NOPE].astype(BF16)
        k_ref[:, lo + QK_NOPE:lo + QK_PAD] = krz
    v_ref[...] = kv[:, N_HEADS * QK_NOPE:].astype(BF16)

    u = jax.nn.gelu(z[:, OFF_U:OFF_U + WIDTH_B])
    vn = _rms(jax.nn.gelu(z[:, OFF_V:OFF_V + WIDTH_B]), w["g_v"][...]).astype(BF16)
    lane = lax.broadcasted_iota(jnp.int32, (CHUNK, WIDTH_B), 1)
    ws = w["w_s"][...]
    bs = w["b_s"][...]
    for c in range(tm // CHUNK):
        rows = slice(c * CHUNK, (c + 1) * CHUNK)
        r = _dot(ws, vn[rows, :])
        mixed = r[(N_HEADS_B - 1) * CHUNK:, :]
        for hb in range(N_HEADS_B - 2, -1, -1):
            mixed = jnp.where(lane < (hb + 1) * HEAD_B, r[hb * CHUNK:(hb + 1) * CHUNK, :], mixed)
        ob_ref[rows, :] = (u[rows, :] * (mixed + bs)).astype(BF16)

    zc = z[:, OFF_CG:OFF_CG + WIDTH_C] * z[:, OFF_HH:OFF_HH + WIDTH_C]
    pos = lax.broadcasted_iota(jnp.int32, (tm, WIDTH_C), 0) & (seq_len - 1)
    z_prev = jnp.where(pos == 0, 0.0, pltpu.roll(zc, 1, 0))
    z_next = jnp.where(pos == seq_len - 1, 0.0, pltpu.roll(zc, tm - 1, 0))
    wc = w["w_conv"][...]
    y = z_prev * wc[0:1, :] + zc * wc[1:2, :] + z_next * wc[2:3, :]
    oc_ref[...] = (z[:, OFF_BG:OFF_BG + WIDTH_C] * y).astype(BF16)


def _attn_head(q_ref, k_ref, v_ref, cache, qrows, krows, hd, oa_ref):
    qk_cols = slice(hd * QK_PAD, (hd + 1) * QK_PAD)
    v_cols = slice(hd * V_HEAD, (hd + 1) * V_HEAD)
    qh = q_ref[qrows, qk_cols]
    s_lat = _dot_nt(qh, k_ref[krows, qk_cols])
    mx = jnp.max(s_lat, axis=-1, keepdims=True)
    if cache is not None:
        kc_ref, vc_ref = cache
        s_ctx = _dot_nt(qh, kc_ref[:, qk_cols])
        mx = jnp.maximum(mx, jnp.max(s_ctx, axis=-1, keepdims=True))
    p_lat = jnp.exp2(s_lat - mx)
    den = jnp.sum(p_lat, axis=-1, keepdims=True)
    o = _dot(p_lat.astype(BF16), v_ref[krows, v_cols])
    if cache is not None:
        p_ctx = jnp.exp2(s_ctx - mx)
        den = den + jnp.sum(p_ctx, axis=-1, keepdims=True)
        o = o + _dot(p_ctx.astype(BF16), vc_ref[:, v_cols])
    oa_ref[qrows, v_cols] = (o * (1.0 / den)).astype(BF16)


def _post_stage(x_ref, mod, mix_ref, w, o_ref):
    ga1 = mod[:, 2 * D_MODEL:3 * D_MODEL]
    sh2 = mod[:, 3 * D_MODEL:4 * D_MODEL]
    sc2 = mod[:, 4 * D_MODEL:5 * D_MODEL]
    ga2 = mod[:, 5 * D_MODEL:6 * D_MODEL]
    group = x_ref.shape[0] // OUT_SPLIT
    for r in range(OUT_SPLIT):
        rows = slice(r * group, (r + 1) * group)
        mo = _dot(mix_ref[rows, :], w["w_out"][...])
        x1 = x_ref[rows, :] + ga1 * _rms(mo, w["g_post_mix"][...])
        o_ref[rows, :] = x1
        mix_ref[rows, :] = (_rms(x1, w["g_pre_ffn"][...]) * (1.0 + sc2) + sh2).astype(BF16)
    h2 = mix_ref[...]
    f = None
    for j in range(D_FF // FF_CHUNK):
        cols = slice(j * FF_CHUNK, (j + 1) * FF_CHUNK)
        a = jnp.square(jnp.maximum(_dot(h2, w["w_ff1"][:, cols]), 0.0)).astype(BF16)
        part = _dot(a, w["w_ff2"][cols, :])
        f = part if f is None else f + part
    o_ref[...] = o_ref[...] + ga2 * _rms(f, w["g_post_ffn"][...])


def _context_kernel(*refs, seq_len, aliased, cast_next):
    n_w = len(PRE_WEIGHTS) + len(POST_WEIGHTS)
    n_cast_out = len(BIG_WEIGHTS) if cast_next else 0
    n_cast_in = n_cast_out + 1 if cast_next else 0
    x_ref, mod_ref = refs[:2]
    w = dict(zip(PRE_WEIGHTS + POST_WEIGHTS, refs[2:2 + n_w]))
    cast_in = refs[2 + n_w:2 + n_w + n_cast_in]
    n_in = 2 + n_w + n_cast_in + (2 if aliased else 0)
    o_ref, ckv_ref, kr_ref = refs[n_in:n_in + 3]
    cast_out = refs[n_in + 3:n_in + 3 + n_cast_out]
    q_ref, k_ref, v_ref, mix_ref = refs[n_in + 3 + n_cast_out:]
    if cast_next:
        _cast_stage(cast_in, cast_out)
    x = x_ref[...]
    mod = mod_ref[...]
    _pre_stage(x, mod, w, None, seq_len, q_ref, k_ref, v_ref,
               mix_ref.at[:, WIDTH_A:WIDTH_A + WIDTH_B], mix_ref.at[:, WIDTH_A + WIDTH_B:],
               (ckv_ref, kr_ref))
    for s in range(x.shape[0] // seq_len):
        rows = slice(s * seq_len, (s + 1) * seq_len)
        for hd in range(N_HEADS):
            _attn_head(q_ref, k_ref, v_ref, None, rows, rows, hd, mix_ref)
    _post_stage(x_ref, mod, mix_ref, w, o_ref)


def _context_layer(x, mods, wts, l, new_ckv, new_kr, raw_big, *, seq_len, seqs_per_tile):
    t = x.shape[0]
    tm = seq_len * seqs_per_tile
    n_seq_total = t // seq_len
    aliased = new_ckv is not None
    assert aliased or l == 0
    cast_next = raw_big is not None
    n_tiles = t // tm
    row_spec = pl.BlockSpec((tm, D_MODEL), lambda i: (i, 0))
    ckv_spec = pl.BlockSpec((seqs_per_tile, None, seq_len, KV_LORA), lambda i: (i, l, 0, 0))
    kr_spec = pl.BlockSpec((seqs_per_tile, None, QK_ROPE, seq_len), lambda i: (i, l, 0, 0))
    in_specs = [row_spec, _mod_spec(n_tiles)]
    args = [x, mods]
    for n in PRE_WEIGHTS + POST_WEIGHTS:
        in_specs.append(_layer_spec(wts[n].shape, l))
        args.append(wts[n])
    out_specs = [row_spec, ckv_spec, kr_spec]
    out_shape = [
        jax.ShapeDtypeStruct((t, D_MODEL), F32),
        jax.ShapeDtypeStruct((n_seq_total, DEPTH, seq_len, KV_LORA), F32),
        jax.ShapeDtypeStruct((n_seq_total, DEPTH, QK_ROPE, seq_len), F32),
    ]
    if cast_next:
        assert n_tiles == CAST_STEPS
        c_in, c_out, c_shape, c_args = _cast_specs(raw_big, l + 1)
        in_specs += c_in
        args += c_args
        out_specs += c_out
        out_shape += c_shape
    aliases = {}
    if aliased:
        in_specs += [pl.BlockSpec(memory_space=pl.ANY)] * 2
        aliases = {len(args): 1, len(args) + 1: 2}
        args += [new_ckv, new_kr]
    return pl.pallas_call(
        functools.partial(_context_kernel, seq_len=seq_len, aliased=aliased, cast_next=cast_next),
        grid=(n_tiles,),
        in_specs=in_specs,
        out_specs=out_specs,
        out_shape=out_shape,
        scratch_shapes=[
            pltpu.VMEM((tm, QK_WIDTH), BF16), pltpu.VMEM((tm, QK_WIDTH), BF16),
            pltpu.VMEM((tm, WIDTH_A), BF16), pltpu.VMEM((tm, D_MODEL), BF16),
        ],
        input_output_aliases=aliases,
        compiler_params=_params(1),
        name="context_layer",
    )(*args)


def _latent_pre_kernel(*refs, seq_len):
    x_ref, mod_ref = refs[:2]
    w = dict(zip(PRE_WEIGHTS, refs[2:2 + len(PRE_WEIGHTS)]))
    cos_ref, sin_ref, q_ref, k_ref, v_ref, obc_ref = refs[2 + len(PRE_WEIGHTS):]
    _pre_stage(x_ref[...], mod_ref[...], w, (cos_ref[...], sin_ref[...]), seq_len,
               q_ref, k_ref, v_ref, obc_ref.at[:, 0:WIDTH_B], obc_ref.at[:, WIDTH_B:], None)


def _latent_pre(x, mods, wts, l, rope_tabs, *, seq_len):
    t = x.shape[0]
    tm = seq_len
    row_spec = lambda w: pl.BlockSpec((tm, w), lambda i: (i, 0))
    in_specs = [row_spec(D_MODEL), _mod_spec(1)]
    args = [x, mods]
    for n in PRE_WEIGHTS:
        in_specs.append(_layer_spec(wts[n].shape, l))
        args.append(wts[n])
    in_specs += [pl.BlockSpec((tm, LANES), lambda i: (0, 0), pipeline_mode=pl.Buffered(1))] * 2
    args += list(rope_tabs)
    return pl.pallas_call(
        functools.partial(_latent_pre_kernel, seq_len=seq_len),
        grid=(t // tm,),
        in_specs=in_specs,
        out_specs=[row_spec(QK_WIDTH), row_spec(QK_WIDTH), row_spec(WIDTH_A), row_spec(BC_WIDTH)],
        out_shape=[
            jax.ShapeDtypeStruct((t, QK_WIDTH), BF16),
            jax.ShapeDtypeStruct((t, QK_WIDTH), BF16),
            jax.ShapeDtypeStruct((t, WIDTH_A), BF16),
            jax.ShapeDtypeStruct((t, BC_WIDTH), BF16),
        ],
        compiler_params=_params(1),
        name="latent_pre",
    )(*args)


def _latent_post_kernel(*refs):
    q_ref, k_ref, v_ref, kc_ref, vc_ref, obc_ref, x_ref, mod_ref = refs[:8]
    w = dict(zip(POST_WEIGHTS, refs[8:8 + len(POST_WEIGHTS)]))
    o_ref, mix_ref = refs[8 + len(POST_WEIGHTS):]
    qrows = slice(0, q_ref.shape[0])
    krows = slice(0, k_ref.shape[0])
    for hd in range(N_HEADS):
        _attn_head(q_ref, k_ref, v_ref, (kc_ref, vc_ref), qrows, krows, hd, mix_ref)
    mix_ref[:, WIDTH_A:] = obc_ref[...]
    _post_stage(x_ref, mod_ref[...], mix_ref, w, o_ref)


def _latent_post(q, k, v, obc, x, mods, cache, wts, l, *, seq_len, tq):
    t = x.shape[0]
    tiles_per_seq = seq_len // tq
    kc, vc = cache
    past = kc.shape[2]
    seq_of = lambda i: i // tiles_per_seq
    row_spec = lambda w: pl.BlockSpec((tq, w), lambda i: (i, 0))
    in_specs = [
        row_spec(QK_WIDTH),
        pl.BlockSpec((seq_len, QK_WIDTH), lambda i: (seq_of(i), 0)),
        pl.BlockSpec((seq_len, WIDTH_A), lambda i: (seq_of(i), 0)),
        pl.BlockSpec((None, None, past, QK_WIDTH), lambda i: (l, seq_of(i), 0, 0)),
        pl.BlockSpec((None, None, past, WIDTH_A), lambda i: (l, seq_of(i), 0, 0)),
        row_spec(BC_WIDTH),
        row_spec(D_MODEL),
        _mod_spec(tiles_per_seq),
    ]
    args = [q, k, v, kc, vc, obc, x, mods]
    for n in POST_WEIGHTS:
        in_specs.append(_layer_spec(wts[n].shape, l))
        args.append(wts[n])
    return pl.pallas_call(
        _latent_post_kernel,
        grid=(t // tq,),
        in_specs=in_specs,
        out_specs=row_spec(D_MODEL),
        out_shape=jax.ShapeDtypeStruct((t, D_MODEL), F32),
        scratch_shapes=[pltpu.VMEM((tq, D_MODEL), BF16)],
        compiler_params=_params(1),
        name="latent_post",
    )(*args)


def _rope_tables(n_tokens):
    rows = n_tokens // GRID_W
    row = np.repeat(np.arange(rows, dtype=np.float64), GRID_W)
    col = np.tile(np.arange(GRID_W, dtype=np.float64), rows)
    nf = QK_ROPE // 4
    inv = ROPE_THETA ** (-np.arange(nf, dtype=np.float64) / nf)
    ang_r = row[:, None] * inv
    ang_c = col[:, None] * inv
    zeros = np.zeros((n_tokens, LANES - QK_ROPE))
    cos = np.concatenate([np.cos(ang_r), np.cos(ang_r), np.cos(ang_c), np.cos(ang_c), zeros], axis=1)
    sin = np.concatenate([-np.sin(ang_r), np.sin(ang_r), -np.sin(ang_c), np.sin(ang_c), zeros], axis=1)
    return jnp.asarray(cos, F32), jnp.asarray(sin, F32)


def _prepare_weights(w_uq, w_ukv, w_s, b_s, w_conv, gains):
    w_q = jnp.pad(w_uq, ((0, 0), (0, 0), (0, 0), (0, QK_PAD - QK_NOPE - QK_ROPE)))
    w_q = w_q.reshape(DEPTH, Q_LORA, QK_WIDTH).astype(BF16)
    w_kv = jnp.concatenate(
        [w_ukv[..., :QK_NOPE].reshape(DEPTH, KV_LORA, N_HEADS * QK_NOPE),
         w_ukv[..., QK_NOPE:].reshape(DEPTH, KV_LORA, N_HEADS * V_HEAD)], axis=-1).astype(BF16)
    wts = {
        "w_q": w_q, "w_kv": w_kv,
        "w_s": w_s.reshape(DEPTH, N_HEADS_B * CHUNK, CHUNK).astype(BF16),
        "b_s": jnp.repeat(jnp.swapaxes(b_s, 1, 2), HEAD_B, axis=-1),
        "w_conv": w_conv,
    }
    for name, g in gains.items():
        wts[name] = g.reshape(DEPTH, 1, g.shape[-1])
    return wts


def kernel(x_prompt, x_sample, cache_ckv, cache_krope, c, c_ctx, w_ada, b_ada, g_pre_mix, w_in, g_q, w_uq, g_kv, w_ukv, g_v, w_s, b_s, w_conv, w_out, g_post_mix, g_pre_ffn, w_ff1, w_ff2, g_post_ffn):
    batch, seq, _ = x_prompt.shape
    dec_batch, dec_seq, _ = x_sample.shape

    wts = _prepare_weights(
        w_uq, w_ukv, w_s, b_s, w_conv,
        {"g_pre_mix": g_pre_mix, "g_q": g_q, "g_kv": g_kv, "g_v": g_v,
         "g_post_mix": g_post_mix, "g_pre_ffn": g_pre_ffn, "g_post_ffn": g_post_ffn})
    raw_big = (jnp.swapaxes(w_in, 1, 2), w_out, w_ff1, w_ff2)
    big = _cast_layer(raw_big, 0)

    cond = jnp.concatenate(
        [c_ctx[None, :], c, jnp.zeros((COND_ROWS - 1 - dec_batch, D_MODEL), F32)], axis=0)
    mods = _modulations(cond, w_ada, b_ada)
    mods_p = mods[:, 0:1, :].reshape(DEPTH, 1, 1, N_MOD * D_MODEL)
    mods_s = mods[:, 1:1 + dec_batch, :].reshape(DEPTH, dec_batch, 1, N_MOD * D_MODEL)

    cache = _cache_kv(cache_ckv, jnp.swapaxes(cache_krope, 2, 3), wts["w_kv"])
    rope_tabs = _rope_tables(dec_seq)

    xp = x_prompt.reshape(batch * seq, D_MODEL)
    xs = x_sample.reshape(dec_batch * dec_seq, D_MODEL)
    new_ckv = new_kr = None
    for l in range(DEPTH):
        wl = dict(wts, **dict(zip(BIG_WEIGHTS, big)))
        xp, new_ckv, new_kr, *big = _context_layer(
            xp, mods_p[l], wl, l, new_ckv, new_kr, raw_big if l + 1 < DEPTH else None,
            seq_len=seq, seqs_per_tile=2)
        q, k, v, obc = _latent_pre(xs, mods_s[l], wl, l, rope_tabs, seq_len=dec_seq)
        xs = _latent_post(q, k, v, obc, xs, mods_s[l], cache, wl, l, seq_len=dec_seq, tq=512)

    return (xp.reshape(batch, seq, D_MODEL), xs.reshape(dec_batch, dec_seq, D_MODEL),
            new_ckv, jnp.swapaxes(new_kr, 2, 3))
```

```python
import functools
import math

import jax
import jax.numpy as jnp
import numpy as np
from jax import lax
from jax.experimental import pallas as pl
from jax.experimental.pallas import tpu as pltpu

F32 = jnp.float32
BF16 = jnp.bfloat16

D_MODEL = 1024
DEPTH = 4
GRID_W = 64
N_HEADS = 4
QK_NOPE = 128
QK_ROPE = 64
V_HEAD = 128
Q_LORA = 384
KV_LORA = 256
WIDTH_A = N_HEADS * V_HEAD
ROPE_THETA = 10000.0
WIDTH_B = 256
N_HEADS_B = 4
HEAD_B = WIDTH_B // N_HEADS_B
CHUNK = 128
WIDTH_C = 256
D_FF = 4 * D_MODEL
N_MOD = 6
EPS = 1e-6

LANES = 128
QK_PAD = 2 * LANES
QK_WIDTH = N_HEADS * QK_PAD
KV_COLS = N_HEADS * (QK_NOPE + V_HEAD)
BC_WIDTH = WIDTH_B + WIDTH_C
OFF_Q = 0
OFF_CKV = OFF_Q + Q_LORA
OFF_U = OFF_CKV + KV_LORA
OFF_V = OFF_U + WIDTH_B
OFF_BG = OFF_V + WIDTH_B
OFF_CG = OFF_BG + WIDTH_C
OFF_HH = OFF_CG + WIDTH_C
OFF_KR = OFF_HH + WIDTH_C
IN_COLS_PAD = OFF_KR + LANES
COND_ROWS = 8
VMEM_LIMIT = 56 * 1024 * 1024
SM_SCALE_LOG2 = math.log2(math.e) / math.sqrt(QK_NOPE + QK_ROPE)
FF_CHUNK = 1024
OUT_SPLIT = 2

PRE_WEIGHTS = ("g_pre_mix", "w_in", "g_q", "w_q", "g_kv", "w_kv", "g_v", "w_s", "b_s", "w_conv")
POST_WEIGHTS = ("g_post_mix", "w_out", "g_pre_ffn", "w_ff1", "w_ff2", "g_post_ffn")
BIG_WEIGHTS = ("w_in", "w_out", "w_ff1", "w_ff2")
BIG_SHAPES = ((D_MODEL, IN_COLS_PAD), (D_MODEL, D_MODEL), (D_MODEL, D_FF), (D_FF, D_MODEL))
CAST_STEPS = 16


def _rms(x, g):
    return x * lax.rsqrt(jnp.mean(x * x, axis=-1, keepdims=True) + EPS) * g


def _dot(a, b):
    return jnp.dot(a, b, preferred_element_type=F32)


def _dot_nt(a, b):
    return lax.dot_general(a, b, (((1,), (1,)), ((), ())), preferred_element_type=F32)


def _params(n_axes, sequential=False):
    semantics = "arbitrary" if sequential else "parallel"
    return pltpu.CompilerParams(
        dimension_semantics=(semantics,) * n_axes, vmem_limit_bytes=VMEM_LIMIT)


def _layer_spec(shape, l):
    if len(shape) == 2:
        return pl.BlockSpec(tuple(shape), lambda *_: (0, 0), pipeline_mode=pl.Buffered(1))
    return pl.BlockSpec((None,) + tuple(shape[1:]), lambda *_: (l, 0, 0),
                        pipeline_mode=pl.Buffered(1))


class _LayerRow:
    def __init__(self, ref, layer):
        self.ref, self.layer = ref, layer

    def __getitem__(self, idx):
        assert idx is Ellipsis
        return self.ref[self.layer:self.layer + 1, :]


def _weight_refs(names, refs, layer):
    return {n: _LayerRow(r, layer) if n.startswith("g_") else r for n, r in zip(names, refs)}


def _mod_spec(l):
    return pl.BlockSpec((None, COND_ROWS, N_MOD * D_MODEL), lambda *_: (l, 0, 0),
                        pipeline_mode=pl.Buffered(1))


def _mod_kernel(cond_ref, w_ref, b_ref, o_ref):
    c = cond_ref[...]
    s = c / (1.0 + jnp.exp(-c))
    o_ref[...] = _dot(s.astype(BF16), w_ref[...].astype(BF16)) + b_ref[...]


def _modulations(cond, w_ada, b_ada):
    tn = 1536
    n_cols = N_MOD * D_MODEL
    return pl.pallas_call(
        _mod_kernel,
        grid=(DEPTH, n_cols // tn),
        in_specs=[
            pl.BlockSpec((COND_ROWS, D_MODEL), lambda l, j: (0, 0)),
            pl.BlockSpec((None, D_MODEL, tn), lambda l, j: (l, 0, j)),
            pl.BlockSpec((None, 1, tn), lambda l, j: (l, 0, j)),
        ],
        out_specs=pl.BlockSpec((None, COND_ROWS, tn), lambda l, j: (l, 0, j)),
        out_shape=jax.ShapeDtypeStruct((DEPTH, COND_ROWS, n_cols), F32),
        compiler_params=_params(2),
        name="modulation",
    )(cond, w_ada, b_ada.reshape(DEPTH, 1, n_cols))


def _cache_kv_kernel(ckv_ref, kr_ref, wkv_ref, k_ref, v_ref):
    nb, past, _ = ckv_ref.shape
    kv = _dot(ckv_ref[...].reshape(nb * past, KV_LORA).astype(BF16), wkv_ref[...])
    pad = jnp.zeros((QK_PAD - QK_NOPE - QK_ROPE, past), F32)
    for b in range(nb):
        rows = slice(b * past, (b + 1) * past)
        krz = jnp.concatenate([kr_ref[b], pad], axis=0).T.astype(BF16)
        for h in range(N_HEADS):
            lo = h * QK_PAD
            k_ref[b, :, lo:lo + QK_NOPE] = kv[rows, h * QK_NOPE:(h + 1) * QK_NOPE].astype(BF16)
            k_ref[b, :, lo + QK_NOPE:lo + QK_PAD] = krz
        v_ref[b] = kv[rows, N_HEADS * QK_NOPE:].astype(BF16)


def _cache_kv(cache_ckv, cache_krope_t, w_kv):
    nb, _, past, _ = cache_ckv.shape
    return pl.pallas_call(
        _cache_kv_kernel,
        grid=(DEPTH,),
        in_specs=[
            pl.BlockSpec((nb, None, past, KV_LORA), lambda l: (0, l, 0, 0)),
            pl.BlockSpec((nb, None, QK_ROPE, past), lambda l: (0, l, 0, 0)),
            pl.BlockSpec((None, KV_LORA, KV_COLS), lambda l: (l, 0, 0)),
        ],
        out_specs=[
            pl.BlockSpec((None, nb, past, QK_WIDTH), lambda l: (l, 0, 0, 0)),
            pl.BlockSpec((None, nb, past, WIDTH_A), lambda l: (l, 0, 0, 0)),
        ],
        out_shape=[
            jax.ShapeDtypeStruct((DEPTH, nb, past, QK_WIDTH), BF16),
            jax.ShapeDtypeStruct((DEPTH, nb, past, WIDTH_A), BF16),
        ],
        compiler_params=_params(1),
        name="cache_kv",
    )(cache_ckv, cache_krope_t, w_kv)


IN_HALF = IN_COLS_PAD // CAST_STEPS // 2


def _cast_stage(in_refs, out_refs):
    win_a, win_b, wout, w1, w2 = in_refs
    win_o, wout_o, w1_o, w2_o = out_refs
    last = pl.program_id(0) == CAST_STEPS - 1
    b = jnp.where(last, 0.0, win_b[...])
    win_o[...] = jnp.concatenate([win_a[...], b], axis=0).T.astype(BF16)
    wout_o[...] = wout[...].astype(BF16)
    w1_o[...] = w1[...].astype(BF16)
    w2_o[...] = w2[...].astype(BF16)


def _w_in_block(i, half):
    kr_block = (Q_LORA + KV_LORA) // IN_HALF
    n_front = kr_block // 2
    shifted = jnp.where(i < CAST_STEPS - 1, 2 * i + 1 + half, kr_block)
    return jnp.where(i < n_front, 2 * i + half, shifted)


def _cast_specs(raw, l, chunk_of=lambda i: i):
    w_in_t = raw[0]
    in_specs = [pl.BlockSpec((None, IN_HALF, D_MODEL),
                             lambda i, h=h: (l, _w_in_block(chunk_of(i), h), 0))
                for h in range(2)]
    out_specs = [pl.BlockSpec((D_MODEL, IN_COLS_PAD // CAST_STEPS), lambda i: (0, chunk_of(i)))]
    out_shape = [jax.ShapeDtypeStruct(BIG_SHAPES[0], BF16)]
    for a, (rows, cols) in zip(raw[1:], BIG_SHAPES[1:]):
        chunk = rows // CAST_STEPS
        in_specs.append(pl.BlockSpec((None, chunk, cols), lambda i: (l, chunk_of(i), 0)))
        out_specs.append(pl.BlockSpec((chunk, cols), lambda i: (chunk_of(i), 0)))
        out_shape.append(jax.ShapeDtypeStruct((rows, cols), BF16))
    return in_specs, out_specs, out_shape, [w_in_t, w_in_t] + list(raw[1:])


def _cast_kernel(*refs):
    _cast_stage(refs[:5], refs[5:])


def _cast_layer(raw, l):
    in_specs, out_specs, out_shape, args = _cast_specs(raw, l)
    return pl.pallas_call(
        _cast_kernel,
        grid=(CAST_STEPS,),
        in_specs=in_specs,
        out_specs=out_specs,
        out_shape=out_shape,
        compiler_params=_params(1),
        name="cast_weights",
    )(*args)


def _swap_halves(x):
    lane = lax.broadcasted_iota(jnp.int32, x.shape, 1)
    quarter = QK_ROPE // 4
    first_half = (lane & (2 * quarter - 1)) < quarter
    return jnp.where(first_half, pltpu.roll(x, LANES - quarter, 1), pltpu.roll(x, quarter, 1))


def _pre_stage(x, mod, w, rope, seq_len, q_ref, k_ref, v_ref, ob_ref, oc_ref, cache_refs):
    tm = x.shape[0]
    sh1 = mod[:, 0:D_MODEL]
    sc1 = mod[:, D_MODEL:2 * D_MODEL]
    h = _rms(x, w["g_pre_mix"][...]) * (1.0 + sc1) + sh1
    z = _dot(h.astype(BF16), w["w_in"][...])

    ckv = _rms(z[:, OFF_CKV:OFF_CKV + KV_LORA], w["g_kv"][...])
    krz = z[:, OFF_KR:OFF_KR + LANES]
    if cache_refs is not None:
        ckv_ref, kr_ref = cache_refs
        ckv_ref[...] = ckv.reshape(ckv_ref.shape)
        for s in range(tm // seq_len):
            kr_ref[s] = krz[s * seq_len:(s + 1) * seq_len, :].T[0:QK_ROPE, :]
    qn = _rms(z[:, OFF_Q:OFF_Q + Q_LORA], w["g_q"][...])
    q = _dot(qn.astype(BF16), w["w_q"][...]) * SM_SCALE_LOG2
    kv = _dot(ckv.astype(BF16), w["w_kv"][...])
    if rope is not None:
        cos, sin = rope
        krz = krz * cos + _swap_halves(krz) * sin
    krz = krz.astype(BF16)
    for hd in range(N_HEADS):
        lo = hd * QK_PAD
        q_ref[:, lo:lo + QK_NOPE] = q[:, lo:lo + QK_NOPE].astype(BF16)
        qr = q[:, lo + QK_NOPE:lo + QK_PAD]
        if rope is not None:
            qr = qr * cos + _swap_halves(qr) * sin
        q_ref[:, lo + QK_NOPE:lo + QK_PAD] = qr.astype(BF16)
        k_ref[:, lo:lo + QK_NOPE] = kv[:, hd * QK_NOPE:(hd + 1) * QK_NOPE].astype(BF16)
        k_ref[:, lo + QK_NOPE:lo + QK_PAD] = krz
    v_ref[...] = kv[:, N_HEADS * QK_NOPE:].astype(BF16)

    u = jax.nn.gelu(z[:, OFF_U:OFF_U + WIDTH_B])
    vn = _rms(jax.nn.gelu(z[:, OFF_V:OFF_V + WIDTH_B]), w["g_v"][...]).astype(BF16)
    lane = lax.broadcasted_iota(jnp.int32, (CHUNK, WIDTH_B), 1)
    ws = w["w_s"][...]
    bs = w["b_s"][...]
    for c in range(tm // CHUNK):
        rows = slice(c * CHUNK, (c + 1) * CHUNK)
        r = _dot(ws, vn[rows, :])
        mixed = r[(N_HEADS_B - 1) * CHUNK:, :]
        for hb in range(N_HEADS_B - 2, -1, -1):
            mixed = jnp.where(lane < (hb + 1) * HEAD_B, r[hb * CHUNK:(hb + 1) * CHUNK, :], mixed)
        ob_ref[rows, :] = (u[rows, :] * (mixed + bs)).astype(BF16)

    zc = z[:, OFF_CG:OFF_CG + WIDTH_C] * z[:, OFF_HH:OFF_HH + WIDTH_C]
    pos = lax.broadcasted_iota(jnp.int32, (tm, WIDTH_C), 0) & (seq_len - 1)
    z_prev = jnp.where(pos == 0, 0.0, pltpu.roll(zc, 1, 0))
    z_next = jnp.where(pos == seq_len - 1, 0.0, pltpu.roll(zc, tm - 1, 0))
    wc = w["w_conv"][...]
    y = z_prev * wc[0:1, :] + zc * wc[1:2, :] + z_next * wc[2:3, :]
    oc_ref[...] = (z[:, OFF_BG:OFF_BG + WIDTH_C] * y).astype(BF16)


def _attn_head(q_ref, k_ref, v_ref, cache, qrows, krows, hd, oa_ref):
    qk_cols = slice(hd * QK_PAD, (hd + 1) * QK_PAD)
    v_cols = slice(hd * V_HEAD, (hd + 1) * V_HEAD)
    qh = q_ref[qrows, qk_cols]
    s_lat = _dot_nt(qh, k_ref[krows, qk_cols])
    mx = jnp.max(s_lat, axis=-1, keepdims=True)
    if cache is not None:
        kc_ref, vc_ref = cache
        s_ctx = _dot_nt(qh, kc_ref[:, qk_cols])
        mx = jnp.maximum(mx, jnp.max(s_ctx, axis=-1, keepdims=True))
    p_lat = jnp.exp2(s_lat - mx)
    den = jnp.sum(p_lat, axis=-1, keepdims=True)
    o = _dot(p_lat.astype(BF16), v_ref[krows, v_cols])
    if cache is not None:
        p_ctx = jnp.exp2(s_ctx - mx)
        den = den + jnp.sum(p_ctx, axis=-1, keepdims=True)
        o = o + _dot(p_ctx.astype(BF16), vc_ref[:, v_cols])
    oa_ref[qrows, v_cols] = (o * (1.0 / den)).astype(BF16)


def _post_stage(x_ref, mod, mix_ref, w, o_ref):
    ga1 = mod[:, 2 * D_MODEL:3 * D_MODEL]
    sh2 = mod[:, 3 * D_MODEL:4 * D_MODEL]
    sc2 = mod[:, 4 * D_MODEL:5 * D_MODEL]
    ga2 = mod[:, 5 * D_MODEL:6 * D_MODEL]
    group = x_ref.shape[0] // OUT_SPLIT
    for r in range(OUT_SPLIT):
        rows = slice(r * group, (r + 1) * group)
        mo = _dot(mix_ref[rows, :], w["w_out"][...])
        x1 = x_ref[rows, :] + ga1 * _rms(mo, w["g_post_mix"][...])
        o_ref[rows, :] = x1
        mix_ref[rows, :] = (_rms(x1, w["g_pre_ffn"][...]) * (1.0 + sc2) + sh2).astype(BF16)
    h2 = mix_ref[...]
    f = None
    for j in range(D_FF // FF_CHUNK):
        cols = slice(j * FF_CHUNK, (j + 1) * FF_CHUNK)
        a = jnp.square(jnp.maximum(_dot(h2, w["w_ff1"][:, cols]), 0.0)).astype(BF16)
        part = _dot(a, w["w_ff2"][cols, :])
        f = part if f is None else f + part
    o_ref[...] = o_ref[...] + ga2 * _rms(f, w["g_post_ffn"][...])


def _context_kernel(*refs, layer, seq_len, aliased, cast_next):
    n_w = len(PRE_WEIGHTS) + len(POST_WEIGHTS)
    n_cast_out = len(BIG_WEIGHTS) if cast_next else 0
    n_cast_in = n_cast_out + 1 if cast_next else 0
    x_ref, mod_ref = refs[:2]
    w = _weight_refs(PRE_WEIGHTS + POST_WEIGHTS, refs[2:2 + n_w], layer)
    cast_in = refs[2 + n_w:2 + n_w + n_cast_in]
    n_in = 2 + n_w + n_cast_in + (2 if aliased else 0)
    o_ref, ckv_ref, kr_ref = refs[n_in:n_in + 3]
    cast_out = refs[n_in + 3:n_in + 3 + n_cast_out]
    q_ref, k_ref, v_ref, mix_ref = refs[n_in + 3 + n_cast_out:]
    if cast_next:
        _cast_stage(cast_in, cast_out)
    x = x_ref[...]
    mod = mod_ref[0:1, :]
    _pre_stage(x, mod, w, None, seq_len, q_ref, k_ref, v_ref,
               mix_ref.at[:, WIDTH_A:WIDTH_A + WIDTH_B], mix_ref.at[:, WIDTH_A + WIDTH_B:],
               (ckv_ref, kr_ref))
    for s in range(x.shape[0] // seq_len):
        rows = slice(s * seq_len, (s + 1) * seq_len)
        for hd in range(N_HEADS):
            _attn_head(q_ref, k_ref, v_ref, None, rows, rows, hd, mix_ref)
    _post_stage(x_ref, mod, mix_ref, w, o_ref)


def _context_layer(x, mods, wts, l, new_ckv, new_kr, raw_big, *, seq_len, seqs_per_tile):
    t = x.shape[0]
    tm = seq_len * seqs_per_tile
    n_seq_total = t // seq_len
    aliased = new_ckv is not None
    assert aliased or l == 0
    cast_next = raw_big is not None
    n_tiles = t // tm
    row_spec = pl.BlockSpec((tm, D_MODEL), lambda i: (i, 0))
    ckv_spec = pl.BlockSpec((seqs_per_tile, None, seq_len, KV_LORA), lambda i: (i, l, 0, 0))
    kr_spec = pl.BlockSpec((seqs_per_tile, None, QK_ROPE, seq_len), lambda i: (i, l, 0, 0))
    in_specs = [row_spec, _mod_spec(l)]
    args = [x, mods]
    for n in PRE_WEIGHTS + POST_WEIGHTS:
        in_specs.append(_layer_spec(wts[n].shape, l))
        args.append(wts[n])
    out_specs = [row_spec, ckv_spec, kr_spec]
    out_shape = [
        jax.ShapeDtypeStruct((t, D_MODEL), F32),
        jax.ShapeDtypeStruct((n_seq_total, DEPTH, seq_len, KV_LORA), F32),
        jax.ShapeDtypeStruct((n_seq_total, DEPTH, QK_ROPE, seq_len), F32),
    ]
    if cast_next:
        assert n_tiles == CAST_STEPS
        c_in, c_out, c_shape, c_args = _cast_specs(raw_big, l + 1)
        in_specs += c_in
        args += c_args
        out_specs += c_out
        out_shape += c_shape
    aliases = {}
    if aliased:
        in_specs += [pl.BlockSpec(memory_space=pl.ANY)] * 2
        aliases = {len(args): 1, len(args) + 1: 2}
        args += [new_ckv, new_kr]
    return pl.pallas_call(
        functools.partial(_context_kernel, layer=l, seq_len=seq_len, aliased=aliased,
                          cast_next=cast_next),
        grid=(n_tiles,),
        in_specs=in_specs,
        out_specs=out_specs,
        out_shape=out_shape,
        scratch_shapes=[
            pltpu.VMEM((tm, QK_WIDTH), BF16), pltpu.VMEM((tm, QK_WIDTH), BF16),
            pltpu.VMEM((tm, WIDTH_A), BF16), pltpu.VMEM((tm, D_MODEL), BF16),
        ],
        input_output_aliases=aliases,
        compiler_params=_params(1),
        name="context_layer",
    )(*args)


def _latent_pre_kernel(*refs, layer, seq_len):
    x_ref, mod_ref = refs[:2]
    w = _weight_refs(PRE_WEIGHTS, refs[2:2 + len(PRE_WEIGHTS)], layer)
    cos_ref, sin_ref, q_ref, k_ref, v_ref, obc_ref = refs[2 + len(PRE_WEIGHTS):]
    mod = mod_ref[pl.ds(1 + pl.program_id(0), 1), :]
    _pre_stage(x_ref[...], mod, w, (cos_ref[...], sin_ref[...]), seq_len,
               q_ref, k_ref, v_ref, obc_ref.at[:, 0:WIDTH_B], obc_ref.at[:, WIDTH_B:], None)


def _latent_pre(x, mods, wts, l, rope_tabs, *, seq_len):
    t = x.shape[0]
    tm = seq_len
    row_spec = lambda w: pl.BlockSpec((tm, w), lambda i: (i, 0))
    in_specs = [row_spec(D_MODEL), _mod_spec(l)]
    args = [x, mods]
    for n in PRE_WEIGHTS:
        in_specs.append(_layer_spec(wts[n].shape, l))
        args.append(wts[n])
    in_specs += [pl.BlockSpec((tm, LANES), lambda i: (0, 0), pipeline_mode=pl.Buffered(1))] * 2
    args += list(rope_tabs)
    return pl.pallas_call(
        functools.partial(_latent_pre_kernel, layer=l, seq_len=seq_len),
        grid=(t // tm,),
        in_specs=in_specs,
        out_specs=[row_spec(QK_WIDTH), row_spec(QK_WIDTH), row_spec(WIDTH_A), row_spec(BC_WIDTH)],
        out_shape=[
            jax.ShapeDtypeStruct((t, QK_WIDTH), BF16),
            jax.ShapeDtypeStruct((t, QK_WIDTH), BF16),
            jax.ShapeDtypeStruct((t, WIDTH_A), BF16),
            jax.ShapeDtypeStruct((t, BC_WIDTH), BF16),
        ],
        compiler_params=_params(1),
        name="latent_pre",
    )(*args)


def _latent_post_kernel(*refs, layer, tiles_per_seq):
    q_ref, k_ref, v_ref, kc_ref, vc_ref, obc_ref, x_ref, mod_ref = refs[:8]
    w = _weight_refs(POST_WEIGHTS, refs[8:8 + len(POST_WEIGHTS)], layer)
    o_ref, mix_ref = refs[8 + len(POST_WEIGHTS):]
    qrows = slice(0, q_ref.shape[0])
    krows = slice(0, k_ref.shape[0])
    for hd in range(N_HEADS):
        _attn_head(q_ref, k_ref, v_ref, (kc_ref, vc_ref), qrows, krows, hd, mix_ref)
    mix_ref[:, WIDTH_A:] = obc_ref[...]
    mod = mod_ref[pl.ds(1 + pl.program_id(0) // tiles_per_seq, 1), :]
    _post_stage(x_ref, mod, mix_ref, w, o_ref)


def _latent_post(q, k, v, obc, x, mods, cache, wts, l, *, seq_len, tq):
    t = x.shape[0]
    tiles_per_seq = seq_len // tq
    kc, vc = cache
    past = kc.shape[2]
    seq_of = lambda i: i // tiles_per_seq
    row_spec = lambda w: pl.BlockSpec((tq, w), lambda i: (i, 0))
    in_specs = [
        row_spec(QK_WIDTH),
        pl.BlockSpec((seq_len, QK_WIDTH), lambda i: (seq_of(i), 0)),
        pl.BlockSpec((seq_len, WIDTH_A), lambda i: (seq_of(i), 0)),
        pl.BlockSpec((None, None, past, QK_WIDTH), lambda i: (l, seq_of(i), 0, 0)),
        pl.BlockSpec((None, None, past, WIDTH_A), lambda i: (l, seq_of(i), 0, 0)),
        row_spec(BC_WIDTH),
        row_spec(D_MODEL),
        _mod_spec(l),
    ]
    args = [q, k, v, kc, vc, obc, x, mods]
    for n in POST_WEIGHTS:
        in_specs.append(_layer_spec(wts[n].shape, l))
        args.append(wts[n])
    return pl.pallas_call(
        functools.partial(_latent_post_kernel, layer=l, tiles_per_seq=tiles_per_seq),
        grid=(t // tq,),
        in_specs=in_specs,
        out_specs=row_spec(D_MODEL),
        out_shape=jax.ShapeDtypeStruct((t, D_MODEL), F32),
        scratch_shapes=[pltpu.VMEM((tq, D_MODEL), BF16)],
        compiler_params=_params(1),
        name="latent_post",
    )(*args)


def _rope_tables(n_tokens):
    rows = n_tokens // GRID_W
    row = np.repeat(np.arange(rows, dtype=np.float64), GRID_W)
    col = np.tile(np.arange(GRID_W, dtype=np.float64), rows)
    nf = QK_ROPE // 4
    inv = ROPE_THETA ** (-np.arange(nf, dtype=np.float64) / nf)
    ang_r = row[:, None] * inv
    ang_c = col[:, None] * inv
    zeros = np.zeros((n_tokens, LANES - QK_ROPE))
    cos = np.concatenate([np.cos(ang_r), np.cos(ang_r), np.cos(ang_c), np.cos(ang_c), zeros], axis=1)
    sin = np.concatenate([-np.sin(ang_r), np.sin(ang_r), -np.sin(ang_c), np.sin(ang_c), zeros], axis=1)
    return jnp.asarray(cos, F32), jnp.asarray(sin, F32)


def _prepare_weights(w_uq, w_ukv, w_s, b_s, w_conv, gains):
    w_q = jnp.pad(w_uq, ((0, 0), (0, 0), (0, 0), (0, QK_PAD - QK_NOPE - QK_ROPE)))
    w_q = w_q.reshape(DEPTH, Q_LORA, QK_WIDTH).astype(BF16)
    w_kv = jnp.concatenate(
        [w_ukv[..., :QK_NOPE].reshape(DEPTH, KV_LORA, N_HEADS * QK_NOPE),
         w_ukv[..., QK_NOPE:].reshape(DEPTH, KV_LORA, N_HEADS * V_HEAD)], axis=-1).astype(BF16)
    wts = {
        "w_q": w_q, "w_kv": w_kv,
        "w_s": w_s.reshape(DEPTH, N_HEADS_B * CHUNK, CHUNK).astype(BF16),
        "b_s": jnp.repeat(jnp.swapaxes(b_s, 1, 2), HEAD_B, axis=-1),
        "w_conv": w_conv,
    }
    for name, g in gains.items():
        wts[name] = g
    return wts


def kernel(x_prompt, x_sample, cache_ckv, cache_krope, c, c_ctx, w_ada, b_ada, g_pre_mix, w_in, g_q, w_uq, g_kv, w_ukv, g_v, w_s, b_s, w_conv, w_out, g_post_mix, g_pre_ffn, w_ff1, w_ff2, g_post_ffn):
    batch, seq, _ = x_prompt.shape
    dec_batch, dec_seq, _ = x_sample.shape

    wts = _prepare_weights(
        w_uq, w_ukv, w_s, b_s, w_conv,
        {"g_pre_mix": g_pre_mix, "g_q": g_q, "g_kv": g_kv, "g_v": g_v,
         "g_post_mix": g_post_mix, "g_pre_ffn": g_pre_ffn, "g_post_ffn": g_post_ffn})
    raw_big = (jnp.swapaxes(w_in, 1, 2), w_out, w_ff1, w_ff2)
    big = _cast_layer(raw_big, 0)

    cond = jnp.concatenate(
        [c_ctx[None, :], c, jnp.zeros((COND_ROWS - 1 - dec_batch, D_MODEL), F32)], axis=0)
    mods = _modulations(cond, w_ada, b_ada)

    cache = _cache_kv(cache_ckv, jnp.swapaxes(cache_krope, 2, 3), wts["w_kv"])
    rope_tabs = _rope_tables(dec_seq)

    xp = x_prompt.reshape(batch * seq, D_MODEL)
    xs = x_sample.reshape(dec_batch * dec_seq, D_MODEL)
    new_ckv = new_kr = None
    for l in range(DEPTH):
        wl = dict(wts, **dict(zip(BIG_WEIGHTS, big)))
        xp, new_ckv, new_kr, *big = _context_layer(
            xp, mods, wl, l, new_ckv, new_kr, raw_big if l + 1 < DEPTH else None,
            seq_len=seq, seqs_per_tile=2)
        q, k, v, obc = _latent_pre(xs, mods, wl, l, rope_tabs, seq_len=dec_seq)
        xs = _latent_post(q, k, v, obc, xs, mods, cache, wl, l, seq_len=dec_seq, tq=512)

    return (xp.reshape(batch, seq, D_MODEL), xs.reshape(dec_batch, dec_seq, D_MODEL),
            new_ckv, jnp.swapaxes(new_kr, 2, 3))
```

```python
import functools
import math

import jax
import jax.numpy as jnp
import numpy as np
from jax import lax
from jax.experimental import pallas as pl
from jax.experimental.pallas import tpu as pltpu

F32 = jnp.float32
BF16 = jnp.bfloat16

D_MODEL = 1024
DEPTH = 4
GRID_W = 64
N_HEADS = 4
QK_NOPE = 128
QK_ROPE = 64
V_HEAD = 128
Q_LORA = 384
KV_LORA = 256
WIDTH_A = N_HEADS * V_HEAD
ROPE_THETA = 10000.0
WIDTH_B = 256
N_HEADS_B = 4
HEAD_B = WIDTH_B // N_HEADS_B
CHUNK = 128
WIDTH_C = 256
D_FF = 4 * D_MODEL
N_MOD = 6
EPS = 1e-6

LANES = 128
QK_PAD = 2 * LANES
QK_WIDTH = N_HEADS * QK_PAD
KV_COLS = N_HEADS * (QK_NOPE + V_HEAD)
BC_WIDTH = WIDTH_B + WIDTH_C
OFF_Q = 0
OFF_CKV = OFF_Q + Q_LORA
OFF_U = OFF_CKV + KV_LORA
OFF_V = OFF_U + WIDTH_B
OFF_BG = OFF_V + WIDTH_B
OFF_CG = OFF_BG + WIDTH_C
OFF_HH = OFF_CG + WIDTH_C
OFF_KR = OFF_HH + WIDTH_C
IN_COLS_PAD = OFF_KR + LANES
COND_ROWS = 8
VMEM_LIMIT = 56 * 1024 * 1024
SM_SCALE_LOG2 = math.log2(math.e) / math.sqrt(QK_NOPE + QK_ROPE)
FF_CHUNK = 1024
OUT_SPLIT = 2

PRE_WEIGHTS = ("g_pre_mix", "w_in", "g_q", "w_q", "g_kv", "w_kv", "g_v", "w_s", "b_s", "w_conv")
POST_WEIGHTS = ("g_post_mix", "w_out", "g_pre_ffn", "w_ff1", "w_ff2", "g_post_ffn")
BIG_WEIGHTS = ("w_in", "w_out", "w_ff1", "w_ff2")
BIG_SHAPES = ((D_MODEL, IN_COLS_PAD), (D_MODEL, D_MODEL), (D_MODEL, D_FF), (D_FF, D_MODEL))
CAST_STEPS = 16


def _rms(x, g):
    return x * lax.rsqrt(jnp.mean(x * x, axis=-1, keepdims=True) + EPS) * g


def _dot(a, b):
    return jnp.dot(a, b, preferred_element_type=F32)


def _dot_nt(a, b):
    return lax.dot_general(a, b, (((1,), (1,)), ((), ())), preferred_element_type=F32)


def _params(n_axes):
    return pltpu.CompilerParams(
        dimension_semantics=("parallel",) * n_axes, vmem_limit_bytes=VMEM_LIMIT)


def _layer_spec(shape, l):
    if len(shape) == 2:
        return pl.BlockSpec(tuple(shape), lambda *_: (0, 0), pipeline_mode=pl.Buffered(1))
    return pl.BlockSpec((None,) + tuple(shape[1:]), lambda *_: (l, 0, 0),
                        pipeline_mode=pl.Buffered(1))


class _LayerRow:
    def __init__(self, ref, layer):
        self.ref, self.layer = ref, layer

    def __getitem__(self, idx):
        assert idx is Ellipsis
        return self.ref[self.layer:self.layer + 1, :]


def _weight_refs(names, refs, layer):
    return {n: _LayerRow(r, layer) if n.startswith("g_") else r for n, r in zip(names, refs)}


def _mod_spec(l):
    return pl.BlockSpec((None, COND_ROWS, N_MOD * D_MODEL), lambda *_: (l, 0, 0),
                        pipeline_mode=pl.Buffered(1))


def _mod_kernel(cond_ref, w_ref, b_ref, o_ref):
    c = cond_ref[...]
    s = c / (1.0 + jnp.exp(-c))
    o_ref[...] = _dot(s.astype(BF16), w_ref[...].astype(BF16)) + b_ref[...]


def _modulations(cond, w_ada, b_ada):
    tn = 1536
    n_cols = N_MOD * D_MODEL
    return pl.pallas_call(
        _mod_kernel,
        grid=(DEPTH, n_cols // tn),
        in_specs=[
            pl.BlockSpec((COND_ROWS, D_MODEL), lambda l, j: (0, 0)),
            pl.BlockSpec((None, D_MODEL, tn), lambda l, j: (l, 0, j)),
            pl.BlockSpec((None, 1, tn), lambda l, j: (l, 0, j)),
        ],
        out_specs=pl.BlockSpec((None, COND_ROWS, tn), lambda l, j: (l, 0, j)),
        out_shape=jax.ShapeDtypeStruct((DEPTH, COND_ROWS, n_cols), F32),
        compiler_params=_params(2),
        name="modulation",
    )(cond, w_ada, b_ada.reshape(DEPTH, 1, n_cols))


def _cache_kv_kernel(ckv_ref, kr_ref, wkv_ref, k_ref, v_ref):
    nb, past, _ = ckv_ref.shape
    kv = _dot(ckv_ref[...].reshape(nb * past, KV_LORA).astype(BF16), wkv_ref[...])
    pad = jnp.zeros((QK_PAD - QK_NOPE - QK_ROPE, past), F32)
    for b in range(nb):
        rows = slice(b * past, (b + 1) * past)
        krz = jnp.concatenate([kr_ref[b], pad], axis=0).T.astype(BF16)
        for h in range(N_HEADS):
            lo = h * QK_PAD
            k_ref[b, :, lo:lo + QK_NOPE] = kv[rows, h * QK_NOPE:(h + 1) * QK_NOPE].astype(BF16)
            k_ref[b, :, lo + QK_NOPE:lo + QK_PAD] = krz
        v_ref[b] = kv[rows, N_HEADS * QK_NOPE:].astype(BF16)


def _cache_kv(cache_ckv, cache_krope_t, w_kv):
    nb, _, past, _ = cache_ckv.shape
    return pl.pallas_call(
        _cache_kv_kernel,
        grid=(DEPTH,),
        in_specs=[
            pl.BlockSpec((nb, None, past, KV_LORA), lambda l: (0, l, 0, 0)),
            pl.BlockSpec((nb, None, QK_ROPE, past), lambda l: (0, l, 0, 0)),
            pl.BlockSpec((None, KV_LORA, KV_COLS), lambda l: (l, 0, 0)),
        ],
        out_specs=[
            pl.BlockSpec((None, nb, past, QK_WIDTH), lambda l: (l, 0, 0, 0)),
            pl.BlockSpec((None, nb, past, WIDTH_A), lambda l: (l, 0, 0, 0)),
        ],
        out_shape=[
            jax.ShapeDtypeStruct((DEPTH, nb, past, QK_WIDTH), BF16),
            jax.ShapeDtypeStruct((DEPTH, nb, past, WIDTH_A), BF16),
        ],
        compiler_params=_params(1),
        name="cache_kv",
    )(cache_ckv, cache_krope_t, w_kv)


IN_HALF = IN_COLS_PAD // CAST_STEPS // 2


def _cast_stage(in_refs, out_refs):
    win_a, win_b, wout, w1, w2 = in_refs
    win_o, wout_o, w1_o, w2_o = out_refs
    last = pl.program_id(0) == CAST_STEPS - 1
    b = jnp.where(last, 0.0, win_b[...])
    win_o[...] = jnp.concatenate([win_a[...], b], axis=0).T.astype(BF16)
    wout_o[...] = wout[...].astype(BF16)
    w1_o[...] = w1[...].astype(BF16)
    w2_o[...] = w2[...].astype(BF16)


def _w_in_block(i, half):
    kr_block = (Q_LORA + KV_LORA) // IN_HALF
    n_front = kr_block // 2
    shifted = jnp.where(i < CAST_STEPS - 1, 2 * i + 1 + half, kr_block)
    return jnp.where(i < n_front, 2 * i + half, shifted)


def _cast_specs(raw, l):
    w_in_t = raw[0]
    in_specs = [pl.BlockSpec((None, IN_HALF, D_MODEL), lambda i, h=h: (l, _w_in_block(i, h), 0))
                for h in range(2)]
    out_specs = [pl.BlockSpec((D_MODEL, IN_COLS_PAD // CAST_STEPS), lambda i: (0, i))]
    out_shape = [jax.ShapeDtypeStruct(BIG_SHAPES[0], BF16)]
    for a, (rows, cols) in zip(raw[1:], BIG_SHAPES[1:]):
        chunk = rows // CAST_STEPS
        in_specs.append(pl.BlockSpec((None, chunk, cols), lambda i: (l, i, 0)))
        out_specs.append(pl.BlockSpec((chunk, cols), lambda i: (i, 0)))
        out_shape.append(jax.ShapeDtypeStruct((rows, cols), BF16))
    return in_specs, out_specs, out_shape, [w_in_t, w_in_t] + list(raw[1:])


def _cast_kernel(*refs):
    _cast_stage(refs[:5], refs[5:])


def _cast_layer(raw, l):
    in_specs, out_specs, out_shape, args = _cast_specs(raw, l)
    return pl.pallas_call(
        _cast_kernel,
        grid=(CAST_STEPS,),
        in_specs=in_specs,
        out_specs=out_specs,
        out_shape=out_shape,
        compiler_params=_params(1),
        name="cast_weights",
    )(*args)


def _swap_halves(x):
    lane = lax.broadcasted_iota(jnp.int32, x.shape, 1)
    quarter = QK_ROPE // 4
    first_half = (lane & (2 * quarter - 1)) < quarter
    return jnp.where(first_half, pltpu.roll(x, LANES - quarter, 1), pltpu.roll(x, quarter, 1))


def _pre_stage(x, mod, w, rope, seq_len, q_ref, k_ref, v_ref, ob_ref, oc_ref, cache_refs):
    tm = x.shape[0]
    sh1 = mod[:, 0:D_MODEL]
    sc1 = mod[:, D_MODEL:2 * D_MODEL]
    h = _rms(x, w["g_pre_mix"][...]) * (1.0 + sc1) + sh1
    z = _dot(h.astype(BF16), w["w_in"][...])

    ckv = _rms(z[:, OFF_CKV:OFF_CKV + KV_LORA], w["g_kv"][...])
    krz = z[:, OFF_KR:OFF_KR + LANES]
    if cache_refs is not None:
        ckv_ref, kr_ref = cache_refs
        ckv_ref[...] = ckv.reshape(ckv_ref.shape)
        for s in range(tm // seq_len):
            kr_ref[s] = krz[s * seq_len:(s + 1) * seq_len, :].T[0:QK_ROPE, :]
    qn = _rms(z[:, OFF_Q:OFF_Q + Q_LORA], w["g_q"][...])
    q = _dot(qn.astype(BF16), w["w_q"][...]) * SM_SCALE_LOG2
    kv = _dot(ckv.astype(BF16), w["w_kv"][...])
    if rope is not None:
        cos, sin = rope
        krz = krz * cos + _swap_halves(krz) * sin
    krz = krz.astype(BF16)
    for hd in range(N_HEADS):
        lo = hd * QK_PAD
        q_ref[:, lo:lo + QK_NOPE] = q[:, lo:lo + QK_NOPE].astype(BF16)
        qr = q[:, lo + QK_NOPE:lo + QK_PAD]
        if rope is not None:
            qr = qr * cos + _swap_halves(qr) * sin
        q_ref[:, lo + QK_NOPE:lo + QK_PAD] = qr.astype(BF16)
        k_ref[:, lo:lo + QK_NOPE] = kv[:, hd * QK_NOPE:(hd + 1) * QK_NOPE].astype(BF16)
        k_ref[:, lo + QK_NOPE:lo + QK_PAD] = krz
    v_ref[...] = kv[:, N_HEADS * QK_NOPE:].astype(BF16)

    u = jax.nn.gelu(z[:, OFF_U:OFF_U + WIDTH_B])
    vn = _rms(jax.nn.gelu(z[:, OFF_V:OFF_V + WIDTH_B]), w["g_v"][...]).astype(BF16)
    lane = lax.broadcasted_iota(jnp.int32, (CHUNK, WIDTH_B), 1)
    ws = w["w_s"][...]
    bs = w["b_s"][...]
    for c in range(tm // CHUNK):
        rows = slice(c * CHUNK, (c + 1) * CHUNK)
        r = _dot(ws, vn[rows, :])
        mixed = r[(N_HEADS_B - 1) * CHUNK:, :]
        for hb in range(N_HEADS_B - 2, -1, -1):
            mixed = jnp.where(lane < (hb + 1) * HEAD_B, r[hb * CHUNK:(hb + 1) * CHUNK, :], mixed)
        ob_ref[rows, :] = (u[rows, :] * (mixed + bs)).astype(BF16)

    zc = z[:, OFF_CG:OFF_CG + WIDTH_C] * z[:, OFF_HH:OFF_HH + WIDTH_C]
    pos = lax.broadcasted_iota(jnp.int32, (tm, WIDTH_C), 0) & (seq_len - 1)
    z_prev = jnp.where(pos == 0, 0.0, pltpu.roll(zc, 1, 0))
    z_next = jnp.where(pos == seq_len - 1, 0.0, pltpu.roll(zc, tm - 1, 0))
    wc = w["w_conv"][...]
    y = z_prev * wc[0:1, :] + zc * wc[1:2, :] + z_next * wc[2:3, :]
    oc_ref[...] = (z[:, OFF_BG:OFF_BG + WIDTH_C] * y).astype(BF16)


def _attn_head(q_ref, k_ref, v_ref, cache, qrows, krows, hd, oa_ref):
    qk_cols = slice(hd * QK_PAD, (hd + 1) * QK_PAD)
    v_cols = slice(hd * V_HEAD, (hd + 1) * V_HEAD)
    qh = q_ref[qrows, qk_cols]
    s_lat = _dot_nt(qh, k_ref[krows, qk_cols])
    mx = jnp.max(s_lat, axis=-1, keepdims=True)
    if cache is not None:
        kc_ref, vc_ref = cache
        s_ctx = _dot_nt(qh, kc_ref[:, qk_cols])
        mx = jnp.maximum(mx, jnp.max(s_ctx, axis=-1, keepdims=True))
    p_lat = jnp.exp2(s_lat - mx)
    den = jnp.sum(p_lat, axis=-1, keepdims=True)
    o = _dot(p_lat.astype(BF16), v_ref[krows, v_cols])
    if cache is not None:
        p_ctx = jnp.exp2(s_ctx - mx)
        den = den + jnp.sum(p_ctx, axis=-1, keepdims=True)
        o = o + _dot(p_ctx.astype(BF16), vc_ref[:, v_cols])
    oa_ref[qrows, v_cols] = (o * (1.0 / den)).astype(BF16)


def _post_stage(x_ref, mod, mix_ref, w, o_ref):
    ga1 = mod[:, 2 * D_MODEL:3 * D_MODEL]
    sh2 = mod[:, 3 * D_MODEL:4 * D_MODEL]
    sc2 = mod[:, 4 * D_MODEL:5 * D_MODEL]
    ga2 = mod[:, 5 * D_MODEL:6 * D_MODEL]
    group = x_ref.shape[0] // OUT_SPLIT
    for r in range(OUT_SPLIT):
        rows = slice(r * group, (r + 1) * group)
        mo = _dot(mix_ref[rows, :], w["w_out"][...])
        x1 = x_ref[rows, :] + ga1 * _rms(mo, w["g_post_mix"][...])
        o_ref[rows, :] = x1
        mix_ref[rows, :] = (_rms(x1, w["g_pre_ffn"][...]) * (1.0 + sc2) + sh2).astype(BF16)
    h2 = mix_ref[...]
    f = None
    for j in range(D_FF // FF_CHUNK):
        cols = slice(j * FF_CHUNK, (j + 1) * FF_CHUNK)
        a = jnp.square(jnp.maximum(_dot(h2, w["w_ff1"][:, cols]), 0.0)).astype(BF16)
        part = _dot(a, w["w_ff2"][cols, :])
        f = part if f is None else f + part
    o_ref[...] = o_ref[...] + ga2 * _rms(f, w["g_post_ffn"][...])


def _context_kernel(*refs, layer, seq_len, aliased, cast_next):
    n_w = len(PRE_WEIGHTS) + len(POST_WEIGHTS)
    n_cast_out = len(BIG_WEIGHTS) if cast_next else 0
    n_cast_in = n_cast_out + 1 if cast_next else 0
    x_ref, mod_ref = refs[:2]
    w = _weight_refs(PRE_WEIGHTS + POST_WEIGHTS, refs[2:2 + n_w], layer)
    cast_in = refs[2 + n_w:2 + n_w + n_cast_in]
    n_in = 2 + n_w + n_cast_in + (2 if aliased else 0)
    o_ref, ckv_ref, kr_ref = refs[n_in:n_in + 3]
    cast_out = refs[n_in + 3:n_in + 3 + n_cast_out]
    q_ref, k_ref, v_ref, mix_ref = refs[n_in + 3 + n_cast_out:]
    if cast_next:
        _cast_stage(cast_in, cast_out)
    x = x_ref[...]
    mod = mod_ref[0:1, :]
    _pre_stage(x, mod, w, None, seq_len, q_ref, k_ref, v_ref,
               mix_ref.at[:, WIDTH_A:WIDTH_A + WIDTH_B], mix_ref.at[:, WIDTH_A + WIDTH_B:],
               (ckv_ref, kr_ref))
    for s in range(x.shape[0] // seq_len):
        rows = slice(s * seq_len, (s + 1) * seq_len)
        for hd in range(N_HEADS):
            _attn_head(q_ref, k_ref, v_ref, None, rows, rows, hd, mix_ref)
    _post_stage(x_ref, mod, mix_ref, w, o_ref)


def _context_layer(x, mods, wts, l, new_ckv, new_kr, raw_big, *, seq_len, seqs_per_tile):
    t = x.shape[0]
    tm = seq_len * seqs_per_tile
    n_seq_total = t // seq_len
    aliased = new_ckv is not None
    assert aliased or l == 0
    cast_next = raw_big is not None
    n_tiles = t // tm
    row_spec = pl.BlockSpec((tm, D_MODEL), lambda i: (i, 0))
    ckv_spec = pl.BlockSpec((seqs_per_tile, None, seq_len, KV_LORA), lambda i: (i, l, 0, 0))
    kr_spec = pl.BlockSpec((seqs_per_tile, None, QK_ROPE, seq_len), lambda i: (i, l, 0, 0))
    in_specs = [row_spec, _mod_spec(l)]
    args = [x, mods]
    for n in PRE_WEIGHTS + POST_WEIGHTS:
        in_specs.append(_layer_spec(wts[n].shape, l))
        args.append(wts[n])
    out_specs = [row_spec, ckv_spec, kr_spec]
    out_shape = [
        jax.ShapeDtypeStruct((t, D_MODEL), F32),
        jax.ShapeDtypeStruct((n_seq_total, DEPTH, seq_len, KV_LORA), F32),
        jax.ShapeDtypeStruct((n_seq_total, DEPTH, QK_ROPE, seq_len), F32),
    ]
    if cast_next:
        assert n_tiles == CAST_STEPS
        c_in, c_out, c_shape, c_args = _cast_specs(raw_big, l + 1)
        in_specs += c_in
        args += c_args
        out_specs += c_out
        out_shape += c_shape
    aliases = {}
    if aliased:
        in_specs += [pl.BlockSpec(memory_space=pl.ANY)] * 2
        aliases = {len(args): 1, len(args) + 1: 2}
        args += [new_ckv, new_kr]
    return pl.pallas_call(
        functools.partial(_context_kernel, layer=l, seq_len=seq_len, aliased=aliased,
                          cast_next=cast_next),
        grid=(n_tiles,),
        in_specs=in_specs,
        out_specs=out_specs,
        out_shape=out_shape,
        scratch_shapes=[
            pltpu.VMEM((tm, QK_WIDTH), BF16), pltpu.VMEM((tm, QK_WIDTH), BF16),
            pltpu.VMEM((tm, WIDTH_A), BF16), pltpu.VMEM((tm, D_MODEL), BF16),
        ],
        input_output_aliases=aliases,
        compiler_params=_params(1),
        name="context_layer",
    )(*args)


def _latent_pre_kernel(*refs, layer, seq_len):
    x_ref, mod_ref = refs[:2]
    w = _weight_refs(PRE_WEIGHTS, refs[2:2 + len(PRE_WEIGHTS)], layer)
    cos_ref, sin_ref, q_ref, k_ref, v_ref, obc_ref = refs[2 + len(PRE_WEIGHTS):]
    mod = mod_ref[pl.ds(1 + pl.program_id(0), 1), :]
    _pre_stage(x_ref[...], mod, w, (cos_ref[...], sin_ref[...]), seq_len,
               q_ref, k_ref, v_ref, obc_ref.at[:, 0:WIDTH_B], obc_ref.at[:, WIDTH_B:], None)


def _latent_pre(x, mods, wts, l, rope_tabs, *, seq_len):
    t = x.shape[0]
    tm = seq_len
    row_spec = lambda w: pl.BlockSpec((tm, w), lambda i: (i, 0))
    in_specs = [row_spec(D_MODEL), _mod_spec(l)]
    args = [x, mods]
    for n in PRE_WEIGHTS:
        in_specs.append(_layer_spec(wts[n].shape, l))
        args.append(wts[n])
    in_specs += [pl.BlockSpec((tm, LANES), lambda i: (0, 0), pipeline_mode=pl.Buffered(1))] * 2
    args += list(rope_tabs)
    return pl.pallas_call(
        functools.partial(_latent_pre_kernel, layer=l, seq_len=seq_len),
        grid=(t // tm,),
        in_specs=in_specs,
        out_specs=[row_spec(QK_WIDTH), row_spec(QK_WIDTH), row_spec(WIDTH_A), row_spec(BC_WIDTH)],
        out_shape=[
            jax.ShapeDtypeStruct((t, QK_WIDTH), BF16),
            jax.ShapeDtypeStruct((t, QK_WIDTH), BF16),
            jax.ShapeDtypeStruct((t, WIDTH_A), BF16),
            jax.ShapeDtypeStruct((t, BC_WIDTH), BF16),
        ],
        compiler_params=_params(1),
        name="latent_pre",
    )(*args)


def _latent_post_kernel(*refs, layer, tiles_per_seq):
    q_ref, k_ref, v_ref, kc_ref, vc_ref, obc_ref, x_ref, mod_ref = refs[:8]
    w = _weight_refs(POST_WEIGHTS, refs[8:8 + len(POST_WEIGHTS)], layer)
    o_ref, mix_ref = refs[8 + len(POST_WEIGHTS):]
    qrows = slice(0, q_ref.shape[0])
    krows = slice(0, k_ref.shape[0])
    for hd in range(N_HEADS):
        _attn_head(q_ref, k_ref, v_ref, (kc_ref, vc_ref), qrows, krows, hd, mix_ref)
    mix_ref[:, WIDTH_A:] = obc_ref[...]
    mod = mod_ref[pl.ds(1 + pl.program_id(0) // tiles_per_seq, 1), :]
    _post_stage(x_ref, mod, mix_ref, w, o_ref)


def _latent_post(q, k, v, obc, x, mods, cache, wts, l, *, seq_len, tq):
    t = x.shape[0]
    tiles_per_seq = seq_len // tq
    kc, vc = cache
    past = kc.shape[2]
    seq_of = lambda i: i // tiles_per_seq
    row_spec = lambda w: pl.BlockSpec((tq, w), lambda i: (i, 0))
    in_specs = [
        row_spec(QK_WIDTH),
        pl.BlockSpec((seq_len, QK_WIDTH), lambda i: (seq_of(i), 0)),
        pl.BlockSpec((seq_len, WIDTH_A), lambda i: (seq_of(i), 0)),
        pl.BlockSpec((None, None, past, QK_WIDTH), lambda i: (l, seq_of(i), 0, 0)),
        pl.BlockSpec((None, None, past, WIDTH_A), lambda i: (l, seq_of(i), 0, 0)),
        row_spec(BC_WIDTH),
        row_spec(D_MODEL),
        _mod_spec(l),
    ]
    args = [q, k, v, kc, vc, obc, x, mods]
    for n in POST_WEIGHTS:
        in_specs.append(_layer_spec(wts[n].shape, l))
        args.append(wts[n])
    return pl.pallas_call(
        functools.partial(_latent_post_kernel, layer=l, tiles_per_seq=tiles_per_seq),
        grid=(t // tq,),
        in_specs=in_specs,
        out_specs=row_spec(D_MODEL),
        out_shape=jax.ShapeDtypeStruct((t, D_MODEL), F32),
        scratch_shapes=[pltpu.VMEM((tq, D_MODEL), BF16)],
        compiler_params=_params(1),
        name="latent_post",
    )(*args)


def _rope_tables(n_tokens):
    rows = n_tokens // GRID_W
    row = np.repeat(np.arange(rows, dtype=np.float64), GRID_W)
    col = np.tile(np.arange(GRID_W, dtype=np.float64), rows)
    nf = QK_ROPE // 4
    inv = ROPE_THETA ** (-np.arange(nf, dtype=np.float64) / nf)
    ang_r = row[:, None] * inv
    ang_c = col[:, None] * inv
    zeros = np.zeros((n_tokens, LANES - QK_ROPE))
    cos = np.concatenate([np.cos(ang_r), np.cos(ang_r), np.cos(ang_c), np.cos(ang_c), zeros], axis=1)
    sin = np.concatenate([-np.sin(ang_r), np.sin(ang_r), -np.sin(ang_c), np.sin(ang_c), zeros], axis=1)
    return jnp.asarray(cos, F32), jnp.asarray(sin, F32)


def _prepare_weights(w_uq, w_ukv, w_s, b_s, w_conv, gains):
    w_q = jnp.pad(w_uq, ((0, 0), (0, 0), (0, 0), (0, QK_PAD - QK_NOPE - QK_ROPE)))
    w_q = w_q.reshape(DEPTH, Q_LORA, QK_WIDTH).astype(BF16)
    w_kv = jnp.concatenate(
        [w_ukv[..., :QK_NOPE].reshape(DEPTH, KV_LORA, N_HEADS * QK_NOPE),
         w_ukv[..., QK_NOPE:].reshape(DEPTH, KV_LORA, N_HEADS * V_HEAD)], axis=-1).astype(BF16)
    wts = {
        "w_q": w_q, "w_kv": w_kv,
        "w_s": w_s.reshape(DEPTH, N_HEADS_B * CHUNK, CHUNK).astype(BF16),
        "b_s": jnp.repeat(jnp.swapaxes(b_s, 1, 2), HEAD_B, axis=-1),
        "w_conv": w_conv,
    }
    for name, g in gains.items():
        wts[name] = g
    return wts


def kernel(x_prompt, x_sample, cache_ckv, cache_krope, c, c_ctx, w_ada, b_ada, g_pre_mix, w_in, g_q, w_uq, g_kv, w_ukv, g_v, w_s, b_s, w_conv, w_out, g_post_mix, g_pre_ffn, w_ff1, w_ff2, g_post_ffn):
    batch, seq, _ = x_prompt.shape
    dec_batch, dec_seq, _ = x_sample.shape

    wts = _prepare_weights(
        w_uq, w_ukv, w_s, b_s, w_conv,
        {"g_pre_mix": g_pre_mix, "g_q": g_q, "g_kv": g_kv, "g_v": g_v,
         "g_post_mix": g_post_mix, "g_pre_ffn": g_pre_ffn, "g_post_ffn": g_post_ffn})
    raw_big = (jnp.swapaxes(w_in, 1, 2), w_out, w_ff1, w_ff2)
    big = _cast_layer(raw_big, 0)

    cond = jnp.concatenate(
        [c_ctx[None, :], c, jnp.zeros((COND_ROWS - 1 - dec_batch, D_MODEL), F32)], axis=0)
    mods = _modulations(cond, w_ada, b_ada)

    cache = _cache_kv(cache_ckv, jnp.swapaxes(cache_krope, 2, 3), wts["w_kv"])
    rope_tabs = _rope_tables(dec_seq)

    xp = x_prompt.reshape(batch * seq, D_MODEL)
    xs = x_sample.reshape(dec_batch * dec_seq, D_MODEL)
    new_ckv = new_kr = None
    for l in range(DEPTH):
        wl = dict(wts, **dict(zip(BIG_WEIGHTS, big)))
        xp, new_ckv, new_kr, *big = _context_layer(
            xp, mods, wl, l, new_ckv, new_kr, raw_big if l + 1 < DEPTH else None,
            seq_len=seq, seqs_per_tile=2)
        q, k, v, obc = _latent_pre(xs, mods, wl, l, rope_tabs, seq_len=dec_seq)
        xs = _latent_post(q, k, v, obc, xs, mods, cache, wl, l, seq_len=dec_seq, tq=512)

    return (xp.reshape(batch, seq, D_MODEL), xs.reshape(dec_batch, dec_seq, D_MODEL),
            new_ckv, jnp.swapaxes(new_kr, 2, 3))
```

```python
import functools
import math

import jax
import jax.numpy as jnp
import numpy as np
from jax import lax
from jax.experimental import pallas as pl
from jax.experimental.pallas import tpu as pltpu

F32 = jnp.float32
BF16 = jnp.bfloat16

D_MODEL = 1024
DEPTH = 4
GRID_W = 64
N_HEADS = 4
QK_NOPE = 128
QK_ROPE = 64
V_HEAD = 128
Q_LORA = 384
KV_LORA = 256
WIDTH_A = N_HEADS * V_HEAD
ROPE_THETA = 10000.0
WIDTH_B = 256
N_HEADS_B = 4
HEAD_B = WIDTH_B // N_HEADS_B
CHUNK = 128
WIDTH_C = 256
D_FF = 4 * D_MODEL
N_MOD = 6
EPS = 1e-6

LANES = 128
QK_PAD = 2 * LANES
QK_WIDTH = N_HEADS * QK_PAD
KV_COLS = N_HEADS * (QK_NOPE + V_HEAD)
BC_WIDTH = WIDTH_B + WIDTH_C
OFF_Q = 0
OFF_CKV = OFF_Q + Q_LORA
OFF_U = OFF_CKV + KV_LORA
OFF_V = OFF_U + WIDTH_B
OFF_BG = OFF_V + WIDTH_B
OFF_CG = OFF_BG + WIDTH_C
OFF_HH = OFF_CG + WIDTH_C
OFF_KR = OFF_HH + WIDTH_C
IN_COLS_PAD = OFF_KR + LANES
COND_ROWS = 8
VMEM_LIMIT = 56 * 1024 * 1024
SM_SCALE_LOG2 = math.log2(math.e) / math.sqrt(QK_NOPE + QK_ROPE)
FF_CHUNK = 1024
OUT_SPLIT = 2

PRE_WEIGHTS = ("g_pre_mix", "w_in", "g_q", "w_q", "g_kv", "w_kv", "g_v", "w_s", "b_s", "w_conv")
POST_WEIGHTS = ("g_post_mix", "w_out", "g_pre_ffn", "w_ff1", "w_ff2", "g_post_ffn")
BIG_WEIGHTS = ("w_in", "w_out", "w_ff1", "w_ff2")
BIG_SHAPES = ((D_MODEL, IN_COLS_PAD), (D_MODEL, D_MODEL), (D_MODEL, D_FF), (D_FF, D_MODEL))
CAST_STEPS = 16


def _rms(x, g):
    return x * lax.rsqrt(jnp.mean(x * x, axis=-1, keepdims=True) + EPS) * g


def _dot(a, b):
    return jnp.dot(a, b, preferred_element_type=F32)


def _dot_nt(a, b):
    return lax.dot_general(a, b, (((1,), (1,)), ((), ())), preferred_element_type=F32)


def _params(n_axes):
    return pltpu.CompilerParams(
        dimension_semantics=("parallel",) * n_axes, vmem_limit_bytes=VMEM_LIMIT)


def _layer_spec(shape, l):
    if len(shape) == 2:
        return pl.BlockSpec(tuple(shape), lambda *_: (0, 0), pipeline_mode=pl.Buffered(1))
    return pl.BlockSpec((None,) + tuple(shape[1:]), lambda *_: (l, 0, 0),
                        pipeline_mode=pl.Buffered(1))


class _LayerRow:
    def __init__(self, ref, layer):
        self.ref, self.layer = ref, layer

    def __getitem__(self, idx):
        assert idx is Ellipsis
        return self.ref[self.layer:self.layer + 1, :]


def _weight_refs(names, refs, layer):
    return {n: _LayerRow(r, layer) if n.startswith("g_") else r for n, r in zip(names, refs)}


def _mod_spec(l):
    return pl.BlockSpec((None, COND_ROWS, N_MOD * D_MODEL), lambda *_: (l, 0, 0),
                        pipeline_mode=pl.Buffered(1))


def _mod_kernel(cond_ref, w_ref, b_ref, o_ref):
    c = cond_ref[...]
    s = c / (1.0 + jnp.exp(-c))
    o_ref[...] = _dot(s.astype(BF16), w_ref[...].astype(BF16)) + b_ref[...]


def _modulations(cond, w_ada, b_ada):
    tn = 1536
    n_cols = N_MOD * D_MODEL
    return pl.pallas_call(
        _mod_kernel,
        grid=(DEPTH, n_cols // tn),
        in_specs=[
            pl.BlockSpec((COND_ROWS, D_MODEL), lambda l, j: (0, 0)),
            pl.BlockSpec((None, D_MODEL, tn), lambda l, j: (l, 0, j)),
            pl.BlockSpec((None, 1, tn), lambda l, j: (l, 0, j)),
        ],
        out_specs=pl.BlockSpec((None, COND_ROWS, tn), lambda l, j: (l, 0, j)),
        out_shape=jax.ShapeDtypeStruct((DEPTH, COND_ROWS, n_cols), F32),
        compiler_params=_params(2),
        name="modulation",
    )(cond, w_ada, b_ada.reshape(DEPTH, 1, n_cols))


def _cache_kv_kernel(ckv_ref, kr_ref, wkv_ref, k_ref, v_ref):
    nb, past, _ = ckv_ref.shape
    kv = _dot(ckv_ref[...].reshape(nb * past, KV_LORA).astype(BF16), wkv_ref[...])
    pad = jnp.zeros((QK_PAD - QK_NOPE - QK_ROPE, past), F32)
    for b in range(nb):
        rows = slice(b * past, (b + 1) * past)
        krz = jnp.concatenate([kr_ref[b], pad], axis=0).T.astype(BF16)
        for h in range(N_HEADS):
            lo = h * QK_PAD
            k_ref[b, :, lo:lo + QK_NOPE] = kv[rows, h * QK_NOPE:(h + 1) * QK_NOPE].astype(BF16)
            k_ref[b, :, lo + QK_NOPE:lo + QK_PAD] = krz
        v_ref[b] = kv[rows, N_HEADS * QK_NOPE:].astype(BF16)


def _cache_kv(cache_ckv, cache_krope_t, w_kv):
    nb, _, past, _ = cache_ckv.shape
    return pl.pallas_call(
        _cache_kv_kernel,
        grid=(DEPTH,),
        in_specs=[
            pl.BlockSpec((nb, None, past, KV_LORA), lambda l: (0, l, 0, 0)),
            pl.BlockSpec((nb, None, QK_ROPE, past), lambda l: (0, l, 0, 0)),
            pl.BlockSpec((None, KV_LORA, KV_COLS), lambda l: (l, 0, 0)),
        ],
        out_specs=[
            pl.BlockSpec((None, nb, past, QK_WIDTH), lambda l: (l, 0, 0, 0)),
            pl.BlockSpec((None, nb, past, WIDTH_A), lambda l: (l, 0, 0, 0)),
        ],
        out_shape=[
            jax.ShapeDtypeStruct((DEPTH, nb, past, QK_WIDTH), BF16),
            jax.ShapeDtypeStruct((DEPTH, nb, past, WIDTH_A), BF16),
        ],
        compiler_params=_params(1),
        name="cache_kv",
    )(cache_ckv, cache_krope_t, w_kv)


IN_HALF = IN_COLS_PAD // CAST_STEPS // 2


def _cast_stage(in_refs, out_refs):
    win_a, win_b, wout, w1, w2 = in_refs
    win_o, wout_o, w1_o, w2_o = out_refs
    last = pl.program_id(0) == CAST_STEPS - 1
    b = jnp.where(last, 0.0, win_b[...])
    win_o[...] = jnp.concatenate([win_a[...], b], axis=0).T.astype(BF16)
    wout_o[...] = wout[...].astype(BF16)
    w1_o[...] = w1[...].astype(BF16)
    w2_o[...] = w2[...].astype(BF16)


def _w_in_block(i, half):
    kr_block = (Q_LORA + KV_LORA) // IN_HALF
    n_front = kr_block // 2
    shifted = jnp.where(i < CAST_STEPS - 1, 2 * i + 1 + half, kr_block)
    return jnp.where(i < n_front, 2 * i + half, shifted)


def _cast_specs(raw, l):
    w_in_t = raw[0]
    in_specs = [pl.BlockSpec((None, IN_HALF, D_MODEL), lambda i, h=h: (l, _w_in_block(i, h), 0))
                for h in range(2)]
    out_specs = [pl.BlockSpec((D_MODEL, IN_COLS_PAD // CAST_STEPS), lambda i: (0, i))]
    out_shape = [jax.ShapeDtypeStruct(BIG_SHAPES[0], BF16)]
    for a, (rows, cols) in zip(raw[1:], BIG_SHAPES[1:]):
        chunk = rows // CAST_STEPS
        in_specs.append(pl.BlockSpec((None, chunk, cols), lambda i: (l, i, 0)))
        out_specs.append(pl.BlockSpec((chunk, cols), lambda i: (i, 0)))
        out_shape.append(jax.ShapeDtypeStruct((rows, cols), BF16))
    return in_specs, out_specs, out_shape, [w_in_t, w_in_t] + list(raw[1:])


def _cast_kernel(*refs):
    _cast_stage(refs[:5], refs[5:])


def _cast_layer(raw, l):
    in_specs, out_specs, out_shape, args = _cast_specs(raw, l)
    return pl.pallas_call(
        _cast_kernel,
        grid=(CAST_STEPS,),
        in_specs=in_specs,
        out_specs=out_specs,
        out_shape=out_shape,
        compiler_params=_params(1),
        name="cast_weights",
    )(*args)


def _swap_halves(x):
    lane = lax.broadcasted_iota(jnp.int32, x.shape, 1)
    quarter = QK_ROPE // 4
    first_half = (lane & (2 * quarter - 1)) < quarter
    return jnp.where(first_half, pltpu.roll(x, LANES - quarter, 1), pltpu.roll(x, quarter, 1))


def _pre_stage(x, mod, w, rope, seq_len, q_ref, k_ref, v_ref, ob_ref, oc_ref, cache_refs):
    tm = x.shape[0]
    sh1 = mod[:, 0:D_MODEL]
    sc1 = mod[:, D_MODEL:2 * D_MODEL]
    h = _rms(x, w["g_pre_mix"][...]) * (1.0 + sc1) + sh1
    z = _dot(h.astype(BF16), w["w_in"][...])

    ckv = _rms(z[:, OFF_CKV:OFF_CKV + KV_LORA], w["g_kv"][...])
    krz = z[:, OFF_KR:OFF_KR + LANES]
    if cache_refs is not None:
        ckv_ref, kr_ref = cache_refs
        if len(ckv_ref.shape) == 4:
            ckv_ref[:, 1:] = jnp.zeros((ckv_ref.shape[0], DEPTH - 1) + ckv_ref.shape[2:], F32)
            kr_ref[:, 1:] = jnp.zeros((kr_ref.shape[0], DEPTH - 1) + kr_ref.shape[2:], F32)
            ckv_ref, kr_ref = ckv_ref.at[:, 0], kr_ref.at[:, 0]
        ckv_ref[...] = ckv.reshape(ckv_ref.shape)
        for s in range(tm // seq_len):
            kr_ref[s] = krz[s * seq_len:(s + 1) * seq_len, :].T[0:QK_ROPE, :]
    qn = _rms(z[:, OFF_Q:OFF_Q + Q_LORA], w["g_q"][...])
    q = _dot(qn.astype(BF16), w["w_q"][...]) * SM_SCALE_LOG2
    kv = _dot(ckv.astype(BF16), w["w_kv"][...])
    if rope is not None:
        cos, sin = rope
        krz = krz * cos + _swap_halves(krz) * sin
    krz = krz.astype(BF16)
    for hd in range(N_HEADS):
        lo = hd * QK_PAD
        q_ref[:, lo:lo + QK_NOPE] = q[:, lo:lo + QK_NOPE].astype(BF16)
        qr = q[:, lo + QK_NOPE:lo + QK_PAD]
        if rope is not None:
            qr = qr * cos + _swap_halves(qr) * sin
        q_ref[:, lo + QK_NOPE:lo + QK_PAD] = qr.astype(BF16)
        k_ref[:, lo:lo + QK_NOPE] = kv[:, hd * QK_NOPE:(hd + 1) * QK_NOPE].astype(BF16)
        k_ref[:, lo + QK_NOPE:lo + QK_PAD] = krz
    v_ref[...] = kv[:, N_HEADS * QK_NOPE:].astype(BF16)

    u = jax.nn.gelu(z[:, OFF_U:OFF_U + WIDTH_B])
    vn = _rms(jax.nn.gelu(z[:, OFF_V:OFF_V + WIDTH_B]), w["g_v"][...]).astype(BF16)
    lane = lax.broadcasted_iota(jnp.int32, (CHUNK, WIDTH_B), 1)
    ws = w["w_s"][...]
    bs = w["b_s"][...]
    for c in range(tm // CHUNK):
        rows = slice(c * CHUNK, (c + 1) * CHUNK)
        r = _dot(ws, vn[rows, :])
        mixed = r[(N_HEADS_B - 1) * CHUNK:, :]
        for hb in range(N_HEADS_B - 2, -1, -1):
            mixed = jnp.where(lane < (hb + 1) * HEAD_B, r[hb * CHUNK:(hb + 1) * CHUNK, :], mixed)
        ob_ref[rows, :] = (u[rows, :] * (mixed + bs)).astype(BF16)

    zc = z[:, OFF_CG:OFF_CG + WIDTH_C] * z[:, OFF_HH:OFF_HH + WIDTH_C]
    pos = lax.broadcasted_iota(jnp.int32, (tm, WIDTH_C), 0) & (seq_len - 1)
    z_prev = jnp.where(pos == 0, 0.0, pltpu.roll(zc, 1, 0))
    z_next = jnp.where(pos == seq_len - 1, 0.0, pltpu.roll(zc, tm - 1, 0))
    wc = w["w_conv"][...]
    y = z_prev * wc[0:1, :] + zc * wc[1:2, :] + z_next * wc[2:3, :]
    oc_ref[...] = (z[:, OFF_BG:OFF_BG + WIDTH_C] * y).astype(BF16)


def _attn_head(q_ref, k_ref, v_ref, cache, qrows, krows, hd, oa_ref):
    qk_cols = slice(hd * QK_PAD, (hd + 1) * QK_PAD)
    v_cols = slice(hd * V_HEAD, (hd + 1) * V_HEAD)
    qh = q_ref[qrows, qk_cols]
    s_lat = _dot_nt(qh, k_ref[krows, qk_cols])
    mx = jnp.max(s_lat, axis=-1, keepdims=True)
    if cache is not None:
        kc_ref, vc_ref = cache
        s_ctx = _dot_nt(qh, kc_ref[:, qk_cols])
        mx = jnp.maximum(mx, jnp.max(s_ctx, axis=-1, keepdims=True))
    p_lat = jnp.exp2(s_lat - mx)
    den = jnp.sum(p_lat, axis=-1, keepdims=True)
    o = _dot(p_lat.astype(BF16), v_ref[krows, v_cols])
    if cache is not None:
        p_ctx = jnp.exp2(s_ctx - mx)
        den = den + jnp.sum(p_ctx, axis=-1, keepdims=True)
        o = o + _dot(p_ctx.astype(BF16), vc_ref[:, v_cols])
    oa_ref[qrows, v_cols] = (o * (1.0 / den)).astype(BF16)


def _post_stage(x_ref, mod, mix_ref, w, o_ref):
    ga1 = mod[:, 2 * D_MODEL:3 * D_MODEL]
    sh2 = mod[:, 3 * D_MODEL:4 * D_MODEL]
    sc2 = mod[:, 4 * D_MODEL:5 * D_MODEL]
    ga2 = mod[:, 5 * D_MODEL:6 * D_MODEL]
    group = x_ref.shape[0] // OUT_SPLIT
    for r in range(OUT_SPLIT):
        rows = slice(r * group, (r + 1) * group)
        mo = _dot(mix_ref[rows, :], w["w_out"][...])
        x1 = x_ref[rows, :] + ga1 * _rms(mo, w["g_post_mix"][...])
        o_ref[rows, :] = x1
        mix_ref[rows, :] = (_rms(x1, w["g_pre_ffn"][...]) * (1.0 + sc2) + sh2).astype(BF16)
    h2 = mix_ref[...]
    f = None
    for j in range(D_FF // FF_CHUNK):
        cols = slice(j * FF_CHUNK, (j + 1) * FF_CHUNK)
        a = jnp.square(jnp.maximum(_dot(h2, w["w_ff1"][:, cols]), 0.0)).astype(BF16)
        part = _dot(a, w["w_ff2"][cols, :])
        f = part if f is None else f + part
    o_ref[...] = o_ref[...] + ga2 * _rms(f, w["g_post_ffn"][...])


def _context_kernel(*refs, layer, seq_len, aliased, cast_next):
    n_w = len(PRE_WEIGHTS) + len(POST_WEIGHTS)
    n_cast_out = len(BIG_WEIGHTS) if cast_next else 0
    n_cast_in = n_cast_out + 1 if cast_next else 0
    x_ref, mod_ref = refs[:2]
    w = _weight_refs(PRE_WEIGHTS + POST_WEIGHTS, refs[2:2 + n_w], layer)
    cast_in = refs[2 + n_w:2 + n_w + n_cast_in]
    n_in = 2 + n_w + n_cast_in + (2 if aliased else 0)
    o_ref, ckv_ref, kr_ref = refs[n_in:n_in + 3]
    cast_out = refs[n_in + 3:n_in + 3 + n_cast_out]
    q_ref, k_ref, v_ref, mix_ref = refs[n_in + 3 + n_cast_out:]
    if cast_next:
        _cast_stage(cast_in, cast_out)
    x = x_ref[...]
    mod = mod_ref[0:1, :]
    _pre_stage(x, mod, w, None, seq_len, q_ref, k_ref, v_ref,
               mix_ref.at[:, WIDTH_A:WIDTH_A + WIDTH_B], mix_ref.at[:, WIDTH_A + WIDTH_B:],
               (ckv_ref, kr_ref))
    for s in range(x.shape[0] // seq_len):
        rows = slice(s * seq_len, (s + 1) * seq_len)
        for hd in range(N_HEADS):
            _attn_head(q_ref, k_ref, v_ref, None, rows, rows, hd, mix_ref)
    _post_stage(x_ref, mod, mix_ref, w, o_ref)


def _context_layer(x, mods, wts, l, new_ckv, new_kr, raw_big, *, seq_len, seqs_per_tile):
    t = x.shape[0]
    tm = seq_len * seqs_per_tile
    n_seq_total = t // seq_len
    aliased = new_ckv is not None
    assert aliased or l == 0
    cast_next = raw_big is not None
    n_tiles = t // tm
    row_spec = pl.BlockSpec((tm, D_MODEL), lambda i: (i, 0))
    layer_dim, layer_idx = (None, l) if aliased else (DEPTH, 0)
    ckv_spec = pl.BlockSpec((seqs_per_tile, layer_dim, seq_len, KV_LORA),
                            lambda i: (i, layer_idx, 0, 0))
    kr_spec = pl.BlockSpec((seqs_per_tile, layer_dim, QK_ROPE, seq_len),
                           lambda i: (i, layer_idx, 0, 0))
    in_specs = [row_spec, _mod_spec(l)]
    args = [x, mods]
    for n in PRE_WEIGHTS + POST_WEIGHTS:
        in_specs.append(_layer_spec(wts[n].shape, l))
        args.append(wts[n])
    out_specs = [row_spec, ckv_spec, kr_spec]
    out_shape = [
        jax.ShapeDtypeStruct((t, D_MODEL), F32),
        jax.ShapeDtypeStruct((n_seq_total, DEPTH, seq_len, KV_LORA), F32),
        jax.ShapeDtypeStruct((n_seq_total, DEPTH, QK_ROPE, seq_len), F32),
    ]
    if cast_next:
        assert n_tiles == CAST_STEPS
        c_in, c_out, c_shape, c_args = _cast_specs(raw_big, l + 1)
        in_specs += c_in
        args += c_args
        out_specs += c_out
        out_shape += c_shape
    aliases = {}
    if aliased:
        in_specs += [pl.BlockSpec(memory_space=pl.ANY)] * 2
        aliases = {len(args): 1, len(args) + 1: 2}
        args += [new_ckv, new_kr]
    return pl.pallas_call(
        functools.partial(_context_kernel, layer=l, seq_len=seq_len, aliased=aliased,
                          cast_next=cast_next),
        grid=(n_tiles,),
        in_specs=in_specs,
        out_specs=out_specs,
        out_shape=out_shape,
        scratch_shapes=[
            pltpu.VMEM((tm, QK_WIDTH), BF16), pltpu.VMEM((tm, QK_WIDTH), BF16),
            pltpu.VMEM((tm, WIDTH_A), BF16), pltpu.VMEM((tm, D_MODEL), BF16),
        ],
        input_output_aliases=aliases,
        compiler_params=_params(1),
        name="context_layer",
    )(*args)


def _latent_pre_kernel(*refs, layer, seq_len):
    x_ref, mod_ref = refs[:2]
    w = _weight_refs(PRE_WEIGHTS, refs[2:2 + len(PRE_WEIGHTS)], layer)
    cos_ref, sin_ref, q_ref, k_ref, v_ref, obc_ref = refs[2 + len(PRE_WEIGHTS):]
    mod = mod_ref[pl.ds(1 + pl.program_id(0), 1), :]
    _pre_stage(x_ref[...], mod, w, (cos_ref[...], sin_ref[...]), seq_len,
               q_ref, k_ref, v_ref, obc_ref.at[:, 0:WIDTH_B], obc_ref.at[:, WIDTH_B:], None)


def _latent_pre(x, mods, wts, l, rope_tabs, *, seq_len):
    t = x.shape[0]
    tm = seq_len
    row_spec = lambda w: pl.BlockSpec((tm, w), lambda i: (i, 0))
    in_specs = [row_spec(D_MODEL), _mod_spec(l)]
    args = [x, mods]
    for n in PRE_WEIGHTS:
        in_specs.append(_layer_spec(wts[n].shape, l))
        args.append(wts[n])
    in_specs += [pl.BlockSpec((tm, LANES), lambda i: (0, 0), pipeline_mode=pl.Buffered(1))] * 2
    args += list(rope_tabs)
    return pl.pallas_call(
        functools.partial(_latent_pre_kernel, layer=l, seq_len=seq_len),
        grid=(t // tm,),
        in_specs=in_specs,
        out_specs=[row_spec(QK_WIDTH), row_spec(QK_WIDTH), row_spec(WIDTH_A), row_spec(BC_WIDTH)],
        out_shape=[
            jax.ShapeDtypeStruct((t, QK_WIDTH), BF16),
            jax.ShapeDtypeStruct((t, QK_WIDTH), BF16),
            jax.ShapeDtypeStruct((t, WIDTH_A), BF16),
            jax.ShapeDtypeStruct((t, BC_WIDTH), BF16),
        ],
        compiler_params=_params(1),
        name="latent_pre",
    )(*args)


def _latent_post_kernel(*refs, layer, tiles_per_seq):
    q_ref, k_ref, v_ref, kc_ref, vc_ref, obc_ref, x_ref, mod_ref = refs[:8]
    w = _weight_refs(POST_WEIGHTS, refs[8:8 + len(POST_WEIGHTS)], layer)
    o_ref, mix_ref = refs[8 + len(POST_WEIGHTS):]
    qrows = slice(0, q_ref.shape[0])
    krows = slice(0, k_ref.shape[0])
    for hd in range(N_HEADS):
        _attn_head(q_ref, k_ref, v_ref, (kc_ref, vc_ref), qrows, krows, hd, mix_ref)
    mix_ref[:, WIDTH_A:] = obc_ref[...]
    mod = mod_ref[pl.ds(1 + pl.program_id(0) // tiles_per_seq, 1), :]
    _post_stage(x_ref, mod, mix_ref, w, o_ref)


def _latent_post(q, k, v, obc, x, mods, cache, wts, l, *, seq_len, tq):
    t = x.shape[0]
    tiles_per_seq = seq_len // tq
    kc, vc = cache
    past = kc.shape[2]
    seq_of = lambda i: i // tiles_per_seq
    row_spec = lambda w: pl.BlockSpec((tq, w), lambda i: (i, 0))
    in_specs = [
        row_spec(QK_WIDTH),
        pl.BlockSpec((seq_len, QK_WIDTH), lambda i: (seq_of(i), 0)),
        pl.BlockSpec((seq_len, WIDTH_A), lambda i: (seq_of(i), 0)),
        pl.BlockSpec((None, None, past, QK_WIDTH), lambda i: (l, seq_of(i), 0, 0)),
        pl.BlockSpec((None, None, past, WIDTH_A), lambda i: (l, seq_of(i), 0, 0)),
        row_spec(BC_WIDTH),
        row_spec(D_MODEL),
        _mod_spec(l),
    ]
    args = [q, k, v, kc, vc, obc, x, mods]
    for n in POST_WEIGHTS:
        in_specs.append(_layer_spec(wts[n].shape, l))
        args.append(wts[n])
    return pl.pallas_call(
        functools.partial(_latent_post_kernel, layer=l, tiles_per_seq=tiles_per_seq),
        grid=(t // tq,),
        in_specs=in_specs,
        out_specs=row_spec(D_MODEL),
        out_shape=jax.ShapeDtypeStruct((t, D_MODEL), F32),
        scratch_shapes=[pltpu.VMEM((tq, D_MODEL), BF16)],
        compiler_params=_params(1),
        name="latent_post",
    )(*args)


def _rope_tables(n_tokens):
    rows = n_tokens // GRID_W
    row = np.repeat(np.arange(rows, dtype=np.float64), GRID_W)
    col = np.tile(np.arange(GRID_W, dtype=np.float64), rows)
    nf = QK_ROPE // 4
    inv = ROPE_THETA ** (-np.arange(nf, dtype=np.float64) / nf)
    ang_r = row[:, None] * inv
    ang_c = col[:, None] * inv
    zeros = np.zeros((n_tokens, LANES - QK_ROPE))
    cos = np.concatenate([np.cos(ang_r), np.cos(ang_r), np.cos(ang_c), np.cos(ang_c), zeros], axis=1)
    sin = np.concatenate([-np.sin(ang_r), np.sin(ang_r), -np.sin(ang_c), np.sin(ang_c), zeros], axis=1)
    return jnp.asarray(cos, F32), jnp.asarray(sin, F32)


def _prepare_weights(w_uq, w_ukv, w_s, b_s, w_conv, gains):
    w_q = jnp.pad(w_uq, ((0, 0), (0, 0), (0, 0), (0, QK_PAD - QK_NOPE - QK_ROPE)))
    w_q = w_q.reshape(DEPTH, Q_LORA, QK_WIDTH).astype(BF16)
    w_kv = jnp.concatenate(
        [w_ukv[..., :QK_NOPE].reshape(DEPTH, KV_LORA, N_HEADS * QK_NOPE),
         w_ukv[..., QK_NOPE:].reshape(DEPTH, KV_LORA, N_HEADS * V_HEAD)], axis=-1).astype(BF16)
    wts = {
        "w_q": w_q, "w_kv": w_kv,
        "w_s": w_s.reshape(DEPTH, N_HEADS_B * CHUNK, CHUNK).astype(BF16),
        "b_s": jnp.repeat(jnp.swapaxes(b_s, 1, 2), HEAD_B, axis=-1),
        "w_conv": w_conv,
    }
    for name, g in gains.items():
        wts[name] = g
    return wts


def kernel(x_prompt, x_sample, cache_ckv, cache_krope, c, c_ctx, w_ada, b_ada, g_pre_mix, w_in, g_q, w_uq, g_kv, w_ukv, g_v, w_s, b_s, w_conv, w_out, g_post_mix, g_pre_ffn, w_ff1, w_ff2, g_post_ffn):
    batch, seq, _ = x_prompt.shape
    dec_batch, dec_seq, _ = x_sample.shape

    wts = _prepare_weights(
        w_uq, w_ukv, w_s, b_s, w_conv,
        {"g_pre_mix": g_pre_mix, "g_q": g_q, "g_kv": g_kv, "g_v": g_v,
         "g_post_mix": g_post_mix, "g_pre_ffn": g_pre_ffn, "g_post_ffn": g_post_ffn})
    raw_big = (jnp.swapaxes(w_in, 1, 2), w_out, w_ff1, w_ff2)
    big = _cast_layer(raw_big, 0)

    cond = jnp.concatenate(
        [c_ctx[None, :], c, jnp.zeros((COND_ROWS - 1 - dec_batch, D_MODEL), F32)], axis=0)
    mods = _modulations(cond, w_ada, b_ada)

    cache = _cache_kv(cache_ckv, jnp.swapaxes(cache_krope, 2, 3), wts["w_kv"])
    rope_tabs = _rope_tables(dec_seq)

    xp = x_prompt.reshape(batch * seq, D_MODEL)
    xs = x_sample.reshape(dec_batch * dec_seq, D_MODEL)
    new_ckv = new_kr = None
    for l in range(DEPTH):
        wl = dict(wts, **dict(zip(BIG_WEIGHTS, big)))
        xp, new_ckv, new_kr, *big = _context_layer(
            xp, mods, wl, l, new_ckv, new_kr, raw_big if l + 1 < DEPTH else None,
            seq_len=seq, seqs_per_tile=2)
        q, k, v, obc = _latent_pre(xs, mods, wl, l, rope_tabs, seq_len=dec_seq)
        xs = _latent_post(q, k, v, obc, xs, mods, cache, wl, l, seq_len=dec_seq, tq=512)

    return (xp.reshape(batch, seq, D_MODEL), xs.reshape(dec_batch, dec_seq, D_MODEL),
            new_ckv, jnp.swapaxes(new_kr, 2, 3))
```

```python
import functools
import math

import jax
import jax.numpy as jnp
import numpy as np
from jax import lax
from jax.experimental import pallas as pl
from jax.experimental.pallas import tpu as pltpu

F32 = jnp.float32
BF16 = jnp.bfloat16

D_MODEL = 1024
DEPTH = 4
GRID_W = 64
N_HEADS = 4
QK_NOPE = 128
QK_ROPE = 64
V_HEAD = 128
Q_LORA = 384
KV_LORA = 256
WIDTH_A = N_HEADS * V_HEAD
ROPE_THETA = 10000.0
WIDTH_B = 256
N_HEADS_B = 4
HEAD_B = WIDTH_B // N_HEADS_B
CHUNK = 128
WIDTH_C = 256
D_FF = 4 * D_MODEL
N_MOD = 6
EPS = 1e-6

LANES = 128
QK_PAD = 2 * LANES
QK_WIDTH = N_HEADS * QK_PAD
KV_COLS = N_HEADS * (QK_NOPE + V_HEAD)
BC_WIDTH = WIDTH_B + WIDTH_C
OFF_Q = 0
OFF_CKV = OFF_Q + Q_LORA
OFF_U = OFF_CKV + KV_LORA
OFF_V = OFF_U + WIDTH_B
OFF_BG = OFF_V + WIDTH_B
OFF_CG = OFF_BG + WIDTH_C
OFF_HH = OFF_CG + WIDTH_C
OFF_KR = OFF_HH + WIDTH_C
IN_COLS_PAD = OFF_KR + LANES
COND_ROWS = 8
VMEM_LIMIT = 56 * 1024 * 1024
SM_SCALE_LOG2 = math.log2(math.e) / math.sqrt(QK_NOPE + QK_ROPE)
FF_CHUNK = 1024
OUT_SPLIT = 2

PRE_WEIGHTS = ("g_pre_mix", "w_in", "g_q", "w_q", "g_kv", "w_kv", "g_v", "w_s", "b_s", "w_conv")
POST_WEIGHTS = ("g_post_mix", "w_out", "g_pre_ffn", "w_ff1", "w_ff2", "g_post_ffn")
BIG_WEIGHTS = ("w_in", "w_out", "w_ff1", "w_ff2")
BIG_SHAPES = ((D_MODEL, IN_COLS_PAD), (D_MODEL, D_MODEL), (D_MODEL, D_FF), (D_FF, D_MODEL))
CAST_STEPS = 16


def _rms(x, g):
    return x * lax.rsqrt(jnp.mean(x * x, axis=-1, keepdims=True) + EPS) * g


def _dot(a, b):
    return jnp.dot(a, b, preferred_element_type=F32)


def _dot_nt(a, b):
    return lax.dot_general(a, b, (((1,), (1,)), ((), ())), preferred_element_type=F32)


def _params(n_axes):
    return pltpu.CompilerParams(
        dimension_semantics=("parallel",) * n_axes, vmem_limit_bytes=VMEM_LIMIT)


def _layer_spec(shape, l):
    if len(shape) == 2:
        return pl.BlockSpec(tuple(shape), lambda *_: (0, 0), pipeline_mode=pl.Buffered(1))
    return pl.BlockSpec((None,) + tuple(shape[1:]), lambda *_: (l, 0, 0),
                        pipeline_mode=pl.Buffered(1))


class _LayerRow:
    def __init__(self, ref, layer):
        self.ref, self.layer = ref, layer

    def __getitem__(self, idx):
        assert idx is Ellipsis
        return self.ref[self.layer:self.layer + 1, :]


def _weight_refs(names, refs, layer):
    return {n: _LayerRow(r, layer) if n.startswith("g_") else r for n, r in zip(names, refs)}


def _mod_spec(l):
    return pl.BlockSpec((None, COND_ROWS, N_MOD * D_MODEL), lambda *_: (l, 0, 0),
                        pipeline_mode=pl.Buffered(1))


def _mod_kernel(cond_ref, w_ref, b_ref, o_ref):
    c = cond_ref[...]
    s = c / (1.0 + jnp.exp(-c))
    o_ref[...] = _dot(s.astype(BF16), w_ref[...].astype(BF16)) + b_ref[...]


def _modulations(cond, w_ada, b_ada):
    tn = 1536
    n_cols = N_MOD * D_MODEL
    return pl.pallas_call(
        _mod_kernel,
        grid=(DEPTH, n_cols // tn),
        in_specs=[
            pl.BlockSpec((COND_ROWS, D_MODEL), lambda l, j: (0, 0)),
            pl.BlockSpec((None, D_MODEL, tn), lambda l, j: (l, 0, j)),
            pl.BlockSpec((None, 1, tn), lambda l, j: (l, 0, j)),
        ],
        out_specs=pl.BlockSpec((None, COND_ROWS, tn), lambda l, j: (l, 0, j)),
        out_shape=jax.ShapeDtypeStruct((DEPTH, COND_ROWS, n_cols), F32),
        compiler_params=_params(2),
        name="modulation",
    )(cond, w_ada, b_ada.reshape(DEPTH, 1, n_cols))


def _cache_kv_kernel(ckv_ref, kr_ref, wkv_ref, k_ref, v_ref):
    nb, past, _ = ckv_ref.shape
    kv = _dot(ckv_ref[...].reshape(nb * past, KV_LORA).astype(BF16), wkv_ref[...])
    pad = jnp.zeros((QK_PAD - QK_NOPE - QK_ROPE, past), F32)
    for b in range(nb):
        rows = slice(b * past, (b + 1) * past)
        krz = jnp.concatenate([kr_ref[b], pad], axis=0).T.astype(BF16)
        for h in range(N_HEADS):
            lo = h * QK_PAD
            k_ref[b, :, lo:lo + QK_NOPE] = kv[rows, h * QK_NOPE:(h + 1) * QK_NOPE].astype(BF16)
            k_ref[b, :, lo + QK_NOPE:lo + QK_PAD] = krz
        v_ref[b] = kv[rows, N_HEADS * QK_NOPE:].astype(BF16)


def _cache_kv(cache_ckv, cache_krope_t, w_kv):
    nb, _, past, _ = cache_ckv.shape
    return pl.pallas_call(
        _cache_kv_kernel,
        grid=(DEPTH,),
        in_specs=[
            pl.BlockSpec((nb, None, past, KV_LORA), lambda l: (0, l, 0, 0)),
            pl.BlockSpec((nb, None, QK_ROPE, past), lambda l: (0, l, 0, 0)),
            pl.BlockSpec((None, KV_LORA, KV_COLS), lambda l: (l, 0, 0)),
        ],
        out_specs=[
            pl.BlockSpec((None, nb, past, QK_WIDTH), lambda l: (l, 0, 0, 0)),
            pl.BlockSpec((None, nb, past, WIDTH_A), lambda l: (l, 0, 0, 0)),
        ],
        out_shape=[
            jax.ShapeDtypeStruct((DEPTH, nb, past, QK_WIDTH), BF16),
            jax.ShapeDtypeStruct((DEPTH, nb, past, WIDTH_A), BF16),
        ],
        compiler_params=_params(1),
        name="cache_kv",
    )(cache_ckv, cache_krope_t, w_kv)


IN_HALF = IN_COLS_PAD // CAST_STEPS // 2


def _cast_stage(in_refs, out_refs):
    win_a, win_b, wout, w1, w2 = in_refs
    win_o, wout_o, w1_o, w2_o = out_refs
    last = pl.program_id(0) == CAST_STEPS - 1
    b = jnp.where(last, 0.0, win_b[...])
    win_o[...] = jnp.concatenate([win_a[...], b], axis=0).T.astype(BF16)
    wout_o[...] = wout[...].astype(BF16)
    w1_o[...] = w1[...].astype(BF16)
    w2_o[...] = w2[...].astype(BF16)


def _w_in_block(i, half):
    kr_block = (Q_LORA + KV_LORA) // IN_HALF
    n_front = kr_block // 2
    shifted = jnp.where(i < CAST_STEPS - 1, 2 * i + 1 + half, kr_block)
    return jnp.where(i < n_front, 2 * i + half, shifted)


def _cast_specs(raw, l):
    w_in_t = raw[0]
    in_specs = [pl.BlockSpec((None, IN_HALF, D_MODEL), lambda i, h=h: (l, _w_in_block(i, h), 0))
                for h in range(2)]
    out_specs = [pl.BlockSpec((D_MODEL, IN_COLS_PAD // CAST_STEPS), lambda i: (0, i))]
    out_shape = [jax.ShapeDtypeStruct(BIG_SHAPES[0], BF16)]
    for a, (rows, cols) in zip(raw[1:], BIG_SHAPES[1:]):
        chunk = rows // CAST_STEPS
        in_specs.append(pl.BlockSpec((None, chunk, cols), lambda i: (l, i, 0)))
        out_specs.append(pl.BlockSpec((chunk, cols), lambda i: (i, 0)))
        out_shape.append(jax.ShapeDtypeStruct((rows, cols), BF16))
    return in_specs, out_specs, out_shape, [w_in_t, w_in_t] + list(raw[1:])


def _cast_kernel(*refs):
    _cast_stage(refs[:5], refs[5:])


def _cast_layer(raw, l):
    in_specs, out_specs, out_shape, args = _cast_specs(raw, l)
    return pl.pallas_call(
        _cast_kernel,
        grid=(CAST_STEPS,),
        in_specs=in_specs,
        out_specs=out_specs,
        out_shape=out_shape,
        compiler_params=_params(1),
        name="cast_weights",
    )(*args)


def _swap_halves(x):
    lane = lax.broadcasted_iota(jnp.int32, x.shape, 1)
    quarter = QK_ROPE // 4
    first_half = (lane & (2 * quarter - 1)) < quarter
    return jnp.where(first_half, pltpu.roll(x, LANES - quarter, 1), pltpu.roll(x, quarter, 1))


def _pre_stage(x, mod, w, rope, seq_len, q_ref, k_ref, v_ref, ob_ref, oc_ref, cache_refs):
    tm = x.shape[0]
    sh1 = mod[:, 0:D_MODEL]
    sc1 = mod[:, D_MODEL:2 * D_MODEL]
    h = _rms(x, w["g_pre_mix"][...] * (1.0 + sc1)) + sh1
    z = _dot(h.astype(BF16), w["w_in"][...])

    ckv = _rms(z[:, OFF_CKV:OFF_CKV + KV_LORA], w["g_kv"][...])
    krz = z[:, OFF_KR:OFF_KR + LANES]
    if cache_refs is not None:
        ckv_ref, kr_ref = cache_refs
        if len(ckv_ref.shape) == 4:
            ckv_ref[:, 1:] = jnp.zeros((ckv_ref.shape[0], DEPTH - 1) + ckv_ref.shape[2:], F32)
            kr_ref[:, 1:] = jnp.zeros((kr_ref.shape[0], DEPTH - 1) + kr_ref.shape[2:], F32)
            ckv_ref, kr_ref = ckv_ref.at[:, 0], kr_ref.at[:, 0]
        ckv_ref[...] = ckv.reshape(ckv_ref.shape)
        for s in range(tm // seq_len):
            kr_ref[s] = krz[s * seq_len:(s + 1) * seq_len, :].T[0:QK_ROPE, :]
    qn = _rms(z[:, OFF_Q:OFF_Q + Q_LORA], w["g_q"][...])
    q = _dot(qn.astype(BF16), w["w_q"][...]) * SM_SCALE_LOG2
    kv = _dot(ckv.astype(BF16), w["w_kv"][...])
    if rope is not None:
        cos, sin = rope
        krz = krz * cos + _swap_halves(krz) * sin
    krz = krz.astype(BF16)
    for hd in range(N_HEADS):
        lo = hd * QK_PAD
        q_ref[:, lo:lo + QK_NOPE] = q[:, lo:lo + QK_NOPE].astype(BF16)
        qr = q[:, lo + QK_NOPE:lo + QK_PAD]
        if rope is not None:
            qr = qr * cos + _swap_halves(qr) * sin
        q_ref[:, lo + QK_NOPE:lo + QK_PAD] = qr.astype(BF16)
        k_ref[:, lo:lo + QK_NOPE] = kv[:, hd * QK_NOPE:(hd + 1) * QK_NOPE].astype(BF16)
        k_ref[:, lo + QK_NOPE:lo + QK_PAD] = krz
    v_ref[...] = kv[:, N_HEADS * QK_NOPE:].astype(BF16)

    u = jax.nn.gelu(z[:, OFF_U:OFF_U + WIDTH_B])
    vn = _rms(jax.nn.gelu(z[:, OFF_V:OFF_V + WIDTH_B]), w["g_v"][...]).astype(BF16)
    lane = lax.broadcasted_iota(jnp.int32, (CHUNK, WIDTH_B), 1)
    ws = w["w_s"][...]
    bs = w["b_s"][...]
    for c in range(tm // CHUNK):
        rows = slice(c * CHUNK, (c + 1) * CHUNK)
        r = _dot(ws, vn[rows, :])
        mixed = r[(N_HEADS_B - 1) * CHUNK:, :]
        for hb in range(N_HEADS_B - 2, -1, -1):
            mixed = jnp.where(lane < (hb + 1) * HEAD_B, r[hb * CHUNK:(hb + 1) * CHUNK, :], mixed)
        ob_ref[rows, :] = (u[rows, :] * (mixed + bs)).astype(BF16)

    zc = z[:, OFF_CG:OFF_CG + WIDTH_C] * z[:, OFF_HH:OFF_HH + WIDTH_C]
    pos = lax.broadcasted_iota(jnp.int32, (tm, WIDTH_C), 0) & (seq_len - 1)
    z_prev = jnp.where(pos == 0, 0.0, pltpu.roll(zc, 1, 0))
    z_next = jnp.where(pos == seq_len - 1, 0.0, pltpu.roll(zc, tm - 1, 0))
    wc = w["w_conv"][...]
    y = z_prev * wc[0:1, :] + zc * wc[1:2, :] + z_next * wc[2:3, :]
    oc_ref[...] = (z[:, OFF_BG:OFF_BG + WIDTH_C] * y).astype(BF16)


def _attn_head(q_ref, k_ref, v_ref, cache, qrows, krows, hd, oa_ref):
    qk_cols = slice(hd * QK_PAD, (hd + 1) * QK_PAD)
    v_cols = slice(hd * V_HEAD, (hd + 1) * V_HEAD)
    qh = q_ref[qrows, qk_cols]
    s_lat = _dot_nt(qh, k_ref[krows, qk_cols])
    mx = jnp.max(s_lat, axis=-1, keepdims=True)
    if cache is not None:
        kc_ref, vc_ref = cache
        s_ctx = _dot_nt(qh, kc_ref[:, qk_cols])
        mx = jnp.maximum(mx, jnp.max(s_ctx, axis=-1, keepdims=True))
    p_lat = jnp.exp2(s_lat - mx)
    den = jnp.sum(p_lat, axis=-1, keepdims=True)
    o = _dot(p_lat.astype(BF16), v_ref[krows, v_cols])
    if cache is not None:
        p_ctx = jnp.exp2(s_ctx - mx)
        den = den + jnp.sum(p_ctx, axis=-1, keepdims=True)
        o = o + _dot(p_ctx.astype(BF16), vc_ref[:, v_cols])
    oa_ref[qrows, v_cols] = (o * (1.0 / den)).astype(BF16)


def _post_stage(x_ref, mod, mix_ref, w, o_ref):
    ga1 = mod[:, 2 * D_MODEL:3 * D_MODEL]
    sh2 = mod[:, 3 * D_MODEL:4 * D_MODEL]
    sc2 = mod[:, 4 * D_MODEL:5 * D_MODEL]
    ga2 = mod[:, 5 * D_MODEL:6 * D_MODEL]
    post_mix_row = ga1 * w["g_post_mix"][...]
    pre_ffn_row = w["g_pre_ffn"][...] * (1.0 + sc2)
    post_ffn_row = ga2 * w["g_post_ffn"][...]
    group = x_ref.shape[0] // OUT_SPLIT
    for r in range(OUT_SPLIT):
        rows = slice(r * group, (r + 1) * group)
        mo = _dot(mix_ref[rows, :], w["w_out"][...])
        x1 = x_ref[rows, :] + _rms(mo, post_mix_row)
        o_ref[rows, :] = x1
        mix_ref[rows, :] = (_rms(x1, pre_ffn_row) + sh2).astype(BF16)
    h2 = mix_ref[...]
    f = None
    for j in range(D_FF // FF_CHUNK):
        cols = slice(j * FF_CHUNK, (j + 1) * FF_CHUNK)
        a = jnp.square(jnp.maximum(_dot(h2, w["w_ff1"][:, cols]), 0.0)).astype(BF16)
        part = _dot(a, w["w_ff2"][cols, :])
        f = part if f is None else f + part
    o_ref[...] = o_ref[...] + _rms(f, post_ffn_row)


def _context_kernel(*refs, layer, seq_len, aliased, cast_next):
    n_w = len(PRE_WEIGHTS) + len(POST_WEIGHTS)
    n_cast_out = len(BIG_WEIGHTS) if cast_next else 0
    n_cast_in = n_cast_out + 1 if cast_next else 0
    x_ref, mod_ref = refs[:2]
    w = _weight_refs(PRE_WEIGHTS + POST_WEIGHTS, refs[2:2 + n_w], layer)
    cast_in = refs[2 + n_w:2 + n_w + n_cast_in]
    n_in = 2 + n_w + n_cast_in + (2 if aliased else 0)
    o_ref, ckv_ref, kr_ref = refs[n_in:n_in + 3]
    cast_out = refs[n_in + 3:n_in + 3 + n_cast_out]
    q_ref, k_ref, v_ref, mix_ref = refs[n_in + 3 + n_cast_out:]
    if cast_next:
        _cast_stage(cast_in, cast_out)
    x = x_ref[...]
    mod = mod_ref[0:1, :]
    _pre_stage(x, mod, w, None, seq_len, q_ref, k_ref, v_ref,
               mix_ref.at[:, WIDTH_A:WIDTH_A + WIDTH_B], mix_ref.at[:, WIDTH_A + WIDTH_B:],
               (ckv_ref, kr_ref))
    for s in range(x.shape[0] // seq_len):
        rows = slice(s * seq_len, (s + 1) * seq_len)
        for hd in range(N_HEADS):
            _attn_head(q_ref, k_ref, v_ref, None, rows, rows, hd, mix_ref)
    _post_stage(x_ref, mod, mix_ref, w, o_ref)


def _context_layer(x, mods, wts, l, new_ckv, new_kr, raw_big, *, seq_len, seqs_per_tile):
    t = x.shape[0]
    tm = seq_len * seqs_per_tile
    n_seq_total = t // seq_len
    aliased = new_ckv is not None
    assert aliased or l == 0
    cast_next = raw_big is not None
    n_tiles = t // tm
    row_spec = pl.BlockSpec((tm, D_MODEL), lambda i: (i, 0))
    layer_dim, layer_idx = (None, l) if aliased else (DEPTH, 0)
    ckv_spec = pl.BlockSpec((seqs_per_tile, layer_dim, seq_len, KV_LORA),
                            lambda i: (i, layer_idx, 0, 0))
    kr_spec = pl.BlockSpec((seqs_per_tile, layer_dim, QK_ROPE, seq_len),
                           lambda i: (i, layer_idx, 0, 0))
    in_specs = [row_spec, _mod_spec(l)]
    args = [x, mods]
    for n in PRE_WEIGHTS + POST_WEIGHTS:
        in_specs.append(_layer_spec(wts[n].shape, l))
        args.append(wts[n])
    out_specs = [row_spec, ckv_spec, kr_spec]
    out_shape = [
        jax.ShapeDtypeStruct((t, D_MODEL), F32),
        jax.ShapeDtypeStruct((n_seq_total, DEPTH, seq_len, KV_LORA), F32),
        jax.ShapeDtypeStruct((n_seq_total, DEPTH, QK_ROPE, seq_len), F32),
    ]
    if cast_next:
        assert n_tiles == CAST_STEPS
        c_in, c_out, c_shape, c_args = _cast_specs(raw_big, l + 1)
        in_specs += c_in
        args += c_args
        out_specs += c_out
        out_shape += c_shape
    aliases = {}
    if aliased:
        in_specs += [pl.BlockSpec(memory_space=pl.ANY)] * 2
        aliases = {len(args): 1, len(args) + 1: 2}
        args += [new_ckv, new_kr]
    return pl.pallas_call(
        functools.partial(_context_kernel, layer=l, seq_len=seq_len, aliased=aliased,
                          cast_next=cast_next),
        grid=(n_tiles,),
        in_specs=in_specs,
        out_specs=out_specs,
        out_shape=out_shape,
        scratch_shapes=[
            pltpu.VMEM((tm, QK_WIDTH), BF16), pltpu.VMEM((tm, QK_WIDTH), BF16),
            pltpu.VMEM((tm, WIDTH_A), BF16), pltpu.VMEM((tm, D_MODEL), BF16),
        ],
        input_output_aliases=aliases,
        compiler_params=_params(1),
        name="context_layer",
    )(*args)


def _latent_pre_kernel(*refs, layer, seq_len):
    x_ref, mod_ref = refs[:2]
    w = _weight_refs(PRE_WEIGHTS, refs[2:2 + len(PRE_WEIGHTS)], layer)
    cos_ref, sin_ref, q_ref, k_ref, v_ref, obc_ref = refs[2 + len(PRE_WEIGHTS):]
    mod = mod_ref[pl.ds(1 + pl.program_id(0), 1), :]
    _pre_stage(x_ref[...], mod, w, (cos_ref[...], sin_ref[...]), seq_len,
               q_ref, k_ref, v_ref, obc_ref.at[:, 0:WIDTH_B], obc_ref.at[:, WIDTH_B:], None)


def _latent_pre(x, mods, wts, l, rope_tabs, *, seq_len):
    t = x.shape[0]
    tm = seq_len
    row_spec = lambda w: pl.BlockSpec((tm, w), lambda i: (i, 0))
    in_specs = [row_spec(D_MODEL), _mod_spec(l)]
    args = [x, mods]
    for n in PRE_WEIGHTS:
        in_specs.append(_layer_spec(wts[n].shape, l))
        args.append(wts[n])
    in_specs += [pl.BlockSpec((tm, LANES), lambda i: (0, 0), pipeline_mode=pl.Buffered(1))] * 2
    args += list(rope_tabs)
    return pl.pallas_call(
        functools.partial(_latent_pre_kernel, layer=l, seq_len=seq_len),
        grid=(t // tm,),
        in_specs=in_specs,
        out_specs=[row_spec(QK_WIDTH), row_spec(QK_WIDTH), row_spec(WIDTH_A), row_spec(BC_WIDTH)],
        out_shape=[
            jax.ShapeDtypeStruct((t, QK_WIDTH), BF16),
            jax.ShapeDtypeStruct((t, QK_WIDTH), BF16),
            jax.ShapeDtypeStruct((t, WIDTH_A), BF16),
            jax.ShapeDtypeStruct((t, BC_WIDTH), BF16),
        ],
        compiler_params=_params(1),
        name="latent_pre",
    )(*args)


def _latent_post_kernel(*refs, layer, tiles_per_seq):
    q_ref, k_ref, v_ref, kc_ref, vc_ref, obc_ref, x_ref, mod_ref = refs[:8]
    w = _weight_refs(POST_WEIGHTS, refs[8:8 + len(POST_WEIGHTS)], layer)
    o_ref, mix_ref = refs[8 + len(POST_WEIGHTS):]
    qrows = slice(0, q_ref.shape[0])
    krows = slice(0, k_ref.shape[0])
    for hd in range(N_HEADS):
        _attn_head(q_ref, k_ref, v_ref, (kc_ref, vc_ref), qrows, krows, hd, mix_ref)
    mix_ref[:, WIDTH_A:] = obc_ref[...]
    mod = mod_ref[pl.ds(1 + pl.program_id(0) // tiles_per_seq, 1), :]
    _post_stage(x_ref, mod, mix_ref, w, o_ref)


def _latent_post(q, k, v, obc, x, mods, cache, wts, l, *, seq_len, tq):
    t = x.shape[0]
    tiles_per_seq = seq_len // tq
    kc, vc = cache
    past = kc.shape[2]
    seq_of = lambda i: i // tiles_per_seq
    row_spec = lambda w: pl.BlockSpec((tq, w), lambda i: (i, 0))
    in_specs = [
        row_spec(QK_WIDTH),
        pl.BlockSpec((seq_len, QK_WIDTH), lambda i: (seq_of(i), 0)),
        pl.BlockSpec((seq_len, WIDTH_A), lambda i: (seq_of(i), 0)),
        pl.BlockSpec((None, None, past, QK_WIDTH), lambda i: (l, seq_of(i), 0, 0)),
        pl.BlockSpec((None, None, past, WIDTH_A), lambda i: (l, seq_of(i), 0, 0)),
        row_spec(BC_WIDTH),
        row_spec(D_MODEL),
        _mod_spec(l),
    ]
    args = [q, k, v, kc, vc, obc, x, mods]
    for n in POST_WEIGHTS:
        in_specs.append(_layer_spec(wts[n].shape, l))
        args.append(wts[n])
    return pl.pallas_call(
        functools.partial(_latent_post_kernel, layer=l, tiles_per_seq=tiles_per_seq),
        grid=(t // tq,),
        in_specs=in_specs,
        out_specs=row_spec(D_MODEL),
        out_shape=jax.ShapeDtypeStruct((t, D_MODEL), F32),
        scratch_shapes=[pltpu.VMEM((tq, D_MODEL), BF16)],
        compiler_params=_params(1),
        name="latent_post",
    )(*args)


def _rope_tables(n_tokens):
    rows = n_tokens // GRID_W
    row = np.repeat(np.arange(rows, dtype=np.float64), GRID_W)
    col = np.tile(np.arange(GRID_W, dtype=np.float64), rows)
    nf = QK_ROPE // 4
    inv = ROPE_THETA ** (-np.arange(nf, dtype=np.float64) / nf)
    ang_r = row[:, None] * inv
    ang_c = col[:, None] * inv
    zeros = np.zeros((n_tokens, LANES - QK_ROPE))
    cos = np.concatenate([np.cos(ang_r), np.cos(ang_r), np.cos(ang_c), np.cos(ang_c), zeros], axis=1)
    sin = np.concatenate([-np.sin(ang_r), np.sin(ang_r), -np.sin(ang_c), np.sin(ang_c), zeros], axis=1)
    return jnp.asarray(cos, F32), jnp.asarray(sin, F32)


def _prepare_weights(w_uq, w_ukv, w_s, b_s, w_conv, gains):
    w_q = jnp.pad(w_uq, ((0, 0), (0, 0), (0, 0), (0, QK_PAD - QK_NOPE - QK_ROPE)))
    w_q = w_q.reshape(DEPTH, Q_LORA, QK_WIDTH).astype(BF16)
    w_kv = jnp.concatenate(
        [w_ukv[..., :QK_NOPE].reshape(DEPTH, KV_LORA, N_HEADS * QK_NOPE),
         w_ukv[..., QK_NOPE:].reshape(DEPTH, KV_LORA, N_HEADS * V_HEAD)], axis=-1).astype(BF16)
    wts = {
        "w_q": w_q, "w_kv": w_kv,
        "w_s": w_s.reshape(DEPTH, N_HEADS_B * CHUNK, CHUNK).astype(BF16),
        "b_s": jnp.repeat(jnp.swapaxes(b_s, 1, 2), HEAD_B, axis=-1),
        "w_conv": w_conv,
    }
    for name, g in gains.items():
        wts[name] = g
    return wts


def kernel(x_prompt, x_sample, cache_ckv, cache_krope, c, c_ctx, w_ada, b_ada, g_pre_mix, w_in, g_q, w_uq, g_kv, w_ukv, g_v, w_s, b_s, w_conv, w_out, g_post_mix, g_pre_ffn, w_ff1, w_ff2, g_post_ffn):
    batch, seq, _ = x_prompt.shape
    dec_batch, dec_seq, _ = x_sample.shape

    wts = _prepare_weights(
        w_uq, w_ukv, w_s, b_s, w_conv,
        {"g_pre_mix": g_pre_mix, "g_q": g_q, "g_kv": g_kv, "g_v": g_v,
         "g_post_mix": g_post_mix, "g_pre_ffn": g_pre_ffn, "g_post_ffn": g_post_ffn})
    raw_big = (jnp.swapaxes(w_in, 1, 2), w_out, w_ff1, w_ff2)
    big = _cast_layer(raw_big, 0)

    cond = jnp.concatenate(
        [c_ctx[None, :], c, jnp.zeros((COND_ROWS - 1 - dec_batch, D_MODEL), F32)], axis=0)
    mods = _modulations(cond, w_ada, b_ada)

    cache = _cache_kv(cache_ckv, jnp.swapaxes(cache_krope, 2, 3), wts["w_kv"])
    rope_tabs = _rope_tables(dec_seq)

    xp = x_prompt.reshape(batch * seq, D_MODEL)
    xs = x_sample.reshape(dec_batch * dec_seq, D_MODEL)
    new_ckv = new_kr = None
    for l in range(DEPTH):
        wl = dict(wts, **dict(zip(BIG_WEIGHTS, big)))
        xp, new_ckv, new_kr, *big = _context_layer(
            xp, mods, wl, l, new_ckv, new_kr, raw_big if l + 1 < DEPTH else None,
            seq_len=seq, seqs_per_tile=2)
        q, k, v, obc = _latent_pre(xs, mods, wl, l, rope_tabs, seq_len=dec_seq)
        xs = _latent_post(q, k, v, obc, xs, mods, cache, wl, l, seq_len=dec_seq, tq=512)

    return (xp.reshape(batch, seq, D_MODEL), xs.reshape(dec_batch, dec_seq, D_MODEL),
            new_ckv, jnp.swapaxes(new_kr, 2, 3))
```

```python
import functools
import math

import jax
import jax.numpy as jnp
import numpy as np
from jax import lax
from jax.experimental import pallas as pl
from jax.experimental.pallas import tpu as pltpu

F32 = jnp.float32
BF16 = jnp.bfloat16

D_MODEL = 1024
DEPTH = 4
GRID_W = 64
N_HEADS = 4
QK_NOPE = 128
QK_ROPE = 64
V_HEAD = 128
Q_LORA = 384
KV_LORA = 256
WIDTH_A = N_HEADS * V_HEAD
ROPE_THETA = 10000.0
WIDTH_B = 256
N_HEADS_B = 4
HEAD_B = WIDTH_B // N_HEADS_B
CHUNK = 128
WIDTH_C = 256
D_FF = 4 * D_MODEL
N_MOD = 6
EPS = 1e-6

LANES = 128
QK_PAD = 2 * LANES
QK_WIDTH = N_HEADS * QK_PAD
KV_COLS = N_HEADS * (QK_NOPE + V_HEAD)
BC_WIDTH = WIDTH_B + WIDTH_C
OFF_Q = 0
OFF_CKV = OFF_Q + Q_LORA
OFF_U = OFF_CKV + KV_LORA
OFF_V = OFF_U + WIDTH_B
OFF_BG = OFF_V + WIDTH_B
OFF_CG = OFF_BG + WIDTH_C
OFF_HH = OFF_CG + WIDTH_C
OFF_KR = OFF_HH + WIDTH_C
IN_COLS_PAD = OFF_KR + LANES
COND_ROWS = 8
VMEM_LIMIT = 56 * 1024 * 1024
SM_SCALE_LOG2 = math.log2(math.e) / math.sqrt(QK_NOPE + QK_ROPE)
FF_CHUNK = 1024
LATENT_OUT_GROUP = 128

PRE_WEIGHTS = ("g_pre_mix", "w_in", "g_q", "w_q", "g_kv", "w_kv", "g_v", "w_s", "b_s", "w_conv")
POST_WEIGHTS = ("g_post_mix", "w_out", "g_pre_ffn", "w_ff1", "w_ff2", "g_post_ffn")
BIG_WEIGHTS = ("w_in", "w_out", "w_ff1", "w_ff2")
BIG_SHAPES = ((D_MODEL, IN_COLS_PAD), (D_MODEL, D_MODEL), (D_MODEL, D_FF), (D_FF, D_MODEL))
CAST_STEPS = 16


def _rms(x, g):
    return x * lax.rsqrt(jnp.mean(x * x, axis=-1, keepdims=True) + EPS) * g


def _dot(a, b):
    return jnp.dot(a, b, preferred_element_type=F32)


def _dot_nt(a, b):
    return lax.dot_general(a, b, (((1,), (1,)), ((), ())), preferred_element_type=F32)


def _params(n_axes):
    return pltpu.CompilerParams(
        dimension_semantics=("parallel",) * n_axes, vmem_limit_bytes=VMEM_LIMIT)


def _layer_spec(shape, l):
    if len(shape) == 2:
        return pl.BlockSpec(tuple(shape), lambda *_: (0, 0), pipeline_mode=pl.Buffered(1))
    return pl.BlockSpec((None,) + tuple(shape[1:]), lambda *_: (l, 0, 0),
                        pipeline_mode=pl.Buffered(1))


class _LayerRow:
    def __init__(self, ref, layer):
        self.ref, self.layer = ref, layer

    def __getitem__(self, idx):
        assert idx is Ellipsis
        return self.ref[self.layer:self.layer + 1, :]


def _weight_refs(names, refs, layer):
    return {n: _LayerRow(r, layer) if n.startswith("g_") else r for n, r in zip(names, refs)}


def _mod_spec(l):
    return pl.BlockSpec((None, COND_ROWS, N_MOD * D_MODEL), lambda *_: (l, 0, 0),
                        pipeline_mode=pl.Buffered(1))


def _mod_kernel(cond_ref, w_ref, b_ref, o_ref):
    c = cond_ref[...]
    s = c / (1.0 + jnp.exp(-c))
    o_ref[...] = _dot(s.astype(BF16), w_ref[...].astype(BF16)) + b_ref[...]


def _modulations(cond, w_ada, b_ada):
    tn = 1536
    n_cols = N_MOD * D_MODEL
    return pl.pallas_call(
        _mod_kernel,
        grid=(DEPTH, n_cols // tn),
        in_specs=[
            pl.BlockSpec((COND_ROWS, D_MODEL), lambda l, j: (0, 0)),
            pl.BlockSpec((None, D_MODEL, tn), lambda l, j: (l, 0, j)),
            pl.BlockSpec((None, 1, tn), lambda l, j: (l, 0, j)),
        ],
        out_specs=pl.BlockSpec((None, COND_ROWS, tn), lambda l, j: (l, 0, j)),
        out_shape=jax.ShapeDtypeStruct((DEPTH, COND_ROWS, n_cols), F32),
        compiler_params=_params(2),
        name="modulation",
    )(cond, w_ada, b_ada.reshape(DEPTH, 1, n_cols))


def _cache_kv_kernel(ckv_ref, kr_ref, wkv_ref, k_ref, v_ref):
    nb, past, _ = ckv_ref.shape
    kv = _dot(ckv_ref[...].reshape(nb * past, KV_LORA).astype(BF16), wkv_ref[...])
    pad = jnp.zeros((QK_PAD - QK_NOPE - QK_ROPE, past), F32)
    for b in range(nb):
        rows = slice(b * past, (b + 1) * past)
        krz = jnp.concatenate([kr_ref[b], pad], axis=0).T.astype(BF16)
        for h in range(N_HEADS):
            lo = h * QK_PAD
            k_ref[b, :, lo:lo + QK_NOPE] = kv[rows, h * QK_NOPE:(h + 1) * QK_NOPE].astype(BF16)
            k_ref[b, :, lo + QK_NOPE:lo + QK_PAD] = krz
        v_ref[b] = kv[rows, N_HEADS * QK_NOPE:].astype(BF16)


def _cache_kv(cache_ckv, cache_krope_t, w_kv):
    nb, _, past, _ = cache_ckv.shape
    return pl.pallas_call(
        _cache_kv_kernel,
        grid=(DEPTH,),
        in_specs=[
            pl.BlockSpec((nb, None, past, KV_LORA), lambda l: (0, l, 0, 0)),
            pl.BlockSpec((nb, None, QK_ROPE, past), lambda l: (0, l, 0, 0)),
            pl.BlockSpec((None, KV_LORA, KV_COLS), lambda l: (l, 0, 0)),
        ],
        out_specs=[
            pl.BlockSpec((None, nb, past, QK_WIDTH), lambda l: (l, 0, 0, 0)),
            pl.BlockSpec((None, nb, past, WIDTH_A), lambda l: (l, 0, 0, 0)),
        ],
        out_shape=[
            jax.ShapeDtypeStruct((DEPTH, nb, past, QK_WIDTH), BF16),
            jax.ShapeDtypeStruct((DEPTH, nb, past, WIDTH_A), BF16),
        ],
        compiler_params=_params(1),
        name="cache_kv",
    )(cache_ckv, cache_krope_t, w_kv)


IN_HALF = IN_COLS_PAD // CAST_STEPS // 2


def _cast_stage(in_refs, out_refs):
    win_a, win_b, wout, w1, w2 = in_refs
    win_o, wout_o, w1_o, w2_o = out_refs
    last = pl.program_id(0) == CAST_STEPS - 1
    b = jnp.where(last, 0.0, win_b[...])
    win_o[...] = jnp.concatenate([win_a[...], b], axis=0).T.astype(BF16)
    wout_o[...] = wout[...].astype(BF16)
    w1_o[...] = w1[...].astype(BF16)
    w2_o[...] = w2[...].astype(BF16)


def _w_in_block(i, half):
    kr_block = (Q_LORA + KV_LORA) // IN_HALF
    n_front = kr_block // 2
    shifted = jnp.where(i < CAST_STEPS - 1, 2 * i + 1 + half, kr_block)
    return jnp.where(i < n_front, 2 * i + half, shifted)


def _cast_specs(raw, l):
    w_in_t = raw[0]
    in_specs = [pl.BlockSpec((None, IN_HALF, D_MODEL), lambda i, h=h: (l, _w_in_block(i, h), 0))
                for h in range(2)]
    out_specs = [pl.BlockSpec((D_MODEL, IN_COLS_PAD // CAST_STEPS), lambda i: (0, i))]
    out_shape = [jax.ShapeDtypeStruct(BIG_SHAPES[0], BF16)]
    for a, (rows, cols) in zip(raw[1:], BIG_SHAPES[1:]):
        chunk = rows // CAST_STEPS
        in_specs.append(pl.BlockSpec((None, chunk, cols), lambda i: (l, i, 0)))
        out_specs.append(pl.BlockSpec((chunk, cols), lambda i: (i, 0)))
        out_shape.append(jax.ShapeDtypeStruct((rows, cols), BF16))
    return in_specs, out_specs, out_shape, [w_in_t, w_in_t] + list(raw[1:])


def _cast_kernel(*refs):
    _cast_stage(refs[:5], refs[5:])


def _cast_layer(raw, l):
    in_specs, out_specs, out_shape, args = _cast_specs(raw, l)
    return pl.pallas_call(
        _cast_kernel,
        grid=(CAST_STEPS,),
        in_specs=in_specs,
        out_specs=out_specs,
        out_shape=out_shape,
        compiler_params=_params(1),
        name="cast_weights",
    )(*args)


def _swap_halves(x):
    lane = lax.broadcasted_iota(jnp.int32, x.shape, 1)
    quarter = QK_ROPE // 4
    first_half = (lane & (2 * quarter - 1)) < quarter
    return jnp.where(first_half, pltpu.roll(x, LANES - quarter, 1), pltpu.roll(x, quarter, 1))


def _pre_stage(x, mod, w, rope, seq_len, q_ref, k_ref, v_ref, ob_ref, oc_ref, cache_refs):
    tm = x.shape[0]
    sh1 = mod[:, 0:D_MODEL]
    sc1 = mod[:, D_MODEL:2 * D_MODEL]
    h = _rms(x, w["g_pre_mix"][...] * (1.0 + sc1)) + sh1
    z = _dot(h.astype(BF16), w["w_in"][...])

    ckv = _rms(z[:, OFF_CKV:OFF_CKV + KV_LORA], w["g_kv"][...])
    krz = z[:, OFF_KR:OFF_KR + LANES]
    if cache_refs is not None:
        ckv_ref, kr_ref = cache_refs
        if len(ckv_ref.shape) == 4:
            ckv_ref[:, 1:] = jnp.zeros((ckv_ref.shape[0], DEPTH - 1) + ckv_ref.shape[2:], F32)
            kr_ref[:, 1:] = jnp.zeros((kr_ref.shape[0], DEPTH - 1) + kr_ref.shape[2:], F32)
            ckv_ref, kr_ref = ckv_ref.at[:, 0], kr_ref.at[:, 0]
        ckv_ref[...] = ckv.reshape(ckv_ref.shape)
        for s in range(tm // seq_len):
            kr_ref[s] = krz[s * seq_len:(s + 1) * seq_len, :].T[0:QK_ROPE, :]
    qn = _rms(z[:, OFF_Q:OFF_Q + Q_LORA], w["g_q"][...] * SM_SCALE_LOG2)
    q = _dot(qn.astype(BF16), w["w_q"][...])
    kv = _dot(ckv.astype(BF16), w["w_kv"][...])
    if rope is not None:
        cos, sin = rope
        krz = krz * cos + _swap_halves(krz) * sin
    krz = krz.astype(BF16)
    for hd in range(N_HEADS):
        lo = hd * QK_PAD
        q_ref[:, lo:lo + QK_NOPE] = q[:, lo:lo + QK_NOPE].astype(BF16)
        qr = q[:, lo + QK_NOPE:lo + QK_PAD]
        if rope is not None:
            qr = qr * cos + _swap_halves(qr) * sin
        q_ref[:, lo + QK_NOPE:lo + QK_PAD] = qr.astype(BF16)
        k_ref[:, lo:lo + QK_NOPE] = kv[:, hd * QK_NOPE:(hd + 1) * QK_NOPE].astype(BF16)
        k_ref[:, lo + QK_NOPE:lo + QK_PAD] = krz
    v_ref[...] = kv[:, N_HEADS * QK_NOPE:].astype(BF16)

    u = jax.nn.gelu(z[:, OFF_U:OFF_U + WIDTH_B])
    vn = _rms(jax.nn.gelu(z[:, OFF_V:OFF_V + WIDTH_B]), w["g_v"][...]).astype(BF16)
    lane = lax.broadcasted_iota(jnp.int32, (CHUNK, WIDTH_B), 1)
    ws = w["w_s"][...]
    bs = w["b_s"][...]
    for c in range(tm // CHUNK):
        rows = slice(c * CHUNK, (c + 1) * CHUNK)
        r = _dot(ws, vn[rows, :])
        mixed = r[(N_HEADS_B - 1) * CHUNK:, :]
        for hb in range(N_HEADS_B - 2, -1, -1):
            mixed = jnp.where(lane < (hb + 1) * HEAD_B, r[hb * CHUNK:(hb + 1) * CHUNK, :], mixed)
        ob_ref[rows, :] = (u[rows, :] * (mixed + bs)).astype(BF16)

    zc = z[:, OFF_CG:OFF_CG + WIDTH_C] * z[:, OFF_HH:OFF_HH + WIDTH_C]
    pos = lax.broadcasted_iota(jnp.int32, (tm, WIDTH_C), 0) & (seq_len - 1)
    z_prev = jnp.where(pos == 0, 0.0, pltpu.roll(zc, 1, 0))
    z_next = jnp.where(pos == seq_len - 1, 0.0, pltpu.roll(zc, tm - 1, 0))
    wc = w["w_conv"][...]
    y = z_prev * wc[0:1, :] + zc * wc[1:2, :] + z_next * wc[2:3, :]
    oc_ref[...] = (z[:, OFF_BG:OFF_BG + WIDTH_C] * y).astype(BF16)


def _attn_head(q_ref, k_ref, v_ref, cache, qrows, krows, hd, oa_ref):
    qk_cols = slice(hd * QK_PAD, (hd + 1) * QK_PAD)
    v_cols = slice(hd * V_HEAD, (hd + 1) * V_HEAD)
    qh = q_ref[qrows, qk_cols]
    s_lat = _dot_nt(qh, k_ref[krows, qk_cols])
    mx = jnp.max(s_lat, axis=-1, keepdims=True)
    if cache is not None:
        kc_ref, vc_ref = cache
        s_ctx = _dot_nt(qh, kc_ref[:, qk_cols])
        mx = jnp.maximum(mx, jnp.max(s_ctx, axis=-1, keepdims=True))
    p_lat = jnp.exp2(s_lat - mx)
    den = jnp.sum(p_lat, axis=-1, keepdims=True)
    o = _dot(p_lat.astype(BF16), v_ref[krows, v_cols])
    if cache is not None:
        p_ctx = jnp.exp2(s_ctx - mx)
        den = den + jnp.sum(p_ctx, axis=-1, keepdims=True)
        o = o + _dot(p_ctx.astype(BF16), vc_ref[:, v_cols])
    oa_ref[qrows, v_cols] = (o * (1.0 / den)).astype(BF16)


def _post_stage(x_ref, mod, mix_ref, w, o_ref, group):
    ga1 = mod[:, 2 * D_MODEL:3 * D_MODEL]
    sh2 = mod[:, 3 * D_MODEL:4 * D_MODEL]
    sc2 = mod[:, 4 * D_MODEL:5 * D_MODEL]
    ga2 = mod[:, 5 * D_MODEL:6 * D_MODEL]
    post_mix_row = ga1 * w["g_post_mix"][...]
    pre_ffn_row = w["g_pre_ffn"][...] * (1.0 + sc2)
    post_ffn_row = ga2 * w["g_post_ffn"][...]
    for r in range(x_ref.shape[0] // group):
        rows = slice(r * group, (r + 1) * group)
        mo = _dot(mix_ref[rows, :], w["w_out"][...])
        x1 = x_ref[rows, :] + _rms(mo, post_mix_row)
        o_ref[rows, :] = x1
        mix_ref[rows, :] = (_rms(x1, pre_ffn_row) + sh2).astype(BF16)
    h2 = mix_ref[...]
    f = None
    for j in range(D_FF // FF_CHUNK):
        cols = slice(j * FF_CHUNK, (j + 1) * FF_CHUNK)
        a = jnp.square(jnp.maximum(_dot(h2, w["w_ff1"][:, cols]), 0.0)).astype(BF16)
        part = _dot(a, w["w_ff2"][cols, :])
        f = part if f is None else f + part
    o_ref[...] = o_ref[...] + _rms(f, post_ffn_row)


def _context_kernel(*refs, layer, seq_len, aliased, cast_next):
    n_w = len(PRE_WEIGHTS) + len(POST_WEIGHTS)
    n_cast_out = len(BIG_WEIGHTS) if cast_next else 0
    n_cast_in = n_cast_out + 1 if cast_next else 0
    x_ref, mod_ref = refs[:2]
    w = _weight_refs(PRE_WEIGHTS + POST_WEIGHTS, refs[2:2 + n_w], layer)
    cast_in = refs[2 + n_w:2 + n_w + n_cast_in]
    n_in = 2 + n_w + n_cast_in + (2 if aliased else 0)
    o_ref, ckv_ref, kr_ref = refs[n_in:n_in + 3]
    cast_out = refs[n_in + 3:n_in + 3 + n_cast_out]
    q_ref, k_ref, v_ref, mix_ref = refs[n_in + 3 + n_cast_out:]
    if cast_next:
        _cast_stage(cast_in, cast_out)
    x = x_ref[...]
    mod = mod_ref[0:1, :]
    _pre_stage(x, mod, w, None, seq_len, q_ref, k_ref, v_ref,
               mix_ref.at[:, WIDTH_A:WIDTH_A + WIDTH_B], mix_ref.at[:, WIDTH_A + WIDTH_B:],
               (ckv_ref, kr_ref))
    for s in range(x.shape[0] // seq_len):
        rows = slice(s * seq_len, (s + 1) * seq_len)
        for hd in range(N_HEADS):
            _attn_head(q_ref, k_ref, v_ref, None, rows, rows, hd, mix_ref)
    _post_stage(x_ref, mod, mix_ref, w, o_ref, seq_len)


def _context_layer(x, mods, wts, l, new_ckv, new_kr, raw_big, *, seq_len, seqs_per_tile):
    t = x.shape[0]
    tm = seq_len * seqs_per_tile
    n_seq_total = t // seq_len
    aliased = new_ckv is not None
    assert aliased or l == 0
    cast_next = raw_big is not None
    n_tiles = t // tm
    row_spec = pl.BlockSpec((tm, D_MODEL), lambda i: (i, 0))
    layer_dim, layer_idx = (None, l) if aliased else (DEPTH, 0)
    ckv_spec = pl.BlockSpec((seqs_per_tile, layer_dim, seq_len, KV_LORA),
                            lambda i: (i, layer_idx, 0, 0))
    kr_spec = pl.BlockSpec((seqs_per_tile, layer_dim, QK_ROPE, seq_len),
                           lambda i: (i, layer_idx, 0, 0))
    in_specs = [row_spec, _mod_spec(l)]
    args = [x, mods]
    for n in PRE_WEIGHTS + POST_WEIGHTS:
        in_specs.append(_layer_spec(wts[n].shape, l))
        args.append(wts[n])
    out_specs = [row_spec, ckv_spec, kr_spec]
    out_shape = [
        jax.ShapeDtypeStruct((t, D_MODEL), F32),
        jax.ShapeDtypeStruct((n_seq_total, DEPTH, seq_len, KV_LORA), F32),
        jax.ShapeDtypeStruct((n_seq_total, DEPTH, QK_ROPE, seq_len), F32),
    ]
    if cast_next:
        assert n_tiles == CAST_STEPS
        c_in, c_out, c_shape, c_args = _cast_specs(raw_big, l + 1)
        in_specs += c_in
        args += c_args
        out_specs += c_out
        out_shape += c_shape
    aliases = {}
    if aliased:
        in_specs += [pl.BlockSpec(memory_space=pl.ANY)] * 2
        aliases = {len(args): 1, len(args) + 1: 2}
        args += [new_ckv, new_kr]
    return pl.pallas_call(
        functools.partial(_context_kernel, layer=l, seq_len=seq_len, aliased=aliased,
                          cast_next=cast_next),
        grid=(n_tiles,),
        in_specs=in_specs,
        out_specs=out_specs,
        out_shape=out_shape,
        scratch_shapes=[
            pltpu.VMEM((tm, QK_WIDTH), BF16), pltpu.VMEM((tm, QK_WIDTH), BF16),
            pltpu.VMEM((tm, WIDTH_A), BF16), pltpu.VMEM((tm, D_MODEL), BF16),
        ],
        input_output_aliases=aliases,
        compiler_params=_params(1),
        name="context_layer",
    )(*args)


def _latent_pre_kernel(*refs, layer, seq_len):
    x_ref, mod_ref = refs[:2]
    w = _weight_refs(PRE_WEIGHTS, refs[2:2 + len(PRE_WEIGHTS)], layer)
    cos_ref, sin_ref, q_ref, k_ref, v_ref, obc_ref = refs[2 + len(PRE_WEIGHTS):]
    mod = mod_ref[pl.ds(1 + pl.program_id(0), 1), :]
    _pre_stage(x_ref[...], mod, w, (cos_ref[...], sin_ref[...]), seq_len,
               q_ref, k_ref, v_ref, obc_ref.at[:, 0:WIDTH_B], obc_ref.at[:, WIDTH_B:], None)


def _latent_pre(x, mods, wts, l, rope_tabs, *, seq_len):
    t = x.shape[0]
    tm = seq_len
    row_spec = lambda w: pl.BlockSpec((tm, w), lambda i: (i, 0))
    in_specs = [row_spec(D_MODEL), _mod_spec(l)]
    args = [x, mods]
    for n in PRE_WEIGHTS:
        in_specs.append(_layer_spec(wts[n].shape, l))
        args.append(wts[n])
    in_specs += [pl.BlockSpec((tm, LANES), lambda i: (0, 0), pipeline_mode=pl.Buffered(1))] * 2
    args += list(rope_tabs)
    return pl.pallas_call(
        functools.partial(_latent_pre_kernel, layer=l, seq_len=seq_len),
        grid=(t // tm,),
        in_specs=in_specs,
        out_specs=[row_spec(QK_WIDTH), row_spec(QK_WIDTH), row_spec(WIDTH_A), row_spec(BC_WIDTH)],
        out_shape=[
            jax.ShapeDtypeStruct((t, QK_WIDTH), BF16),
            jax.ShapeDtypeStruct((t, QK_WIDTH), BF16),
            jax.ShapeDtypeStruct((t, WIDTH_A), BF16),
            jax.ShapeDtypeStruct((t, BC_WIDTH), BF16),
        ],
        compiler_params=_params(1),
        name="latent_pre",
    )(*args)


def _latent_post_kernel(*refs, layer, tiles_per_seq):
    q_ref, k_ref, v_ref, kc_ref, vc_ref, obc_ref, x_ref, mod_ref = refs[:8]
    w = _weight_refs(POST_WEIGHTS, refs[8:8 + len(POST_WEIGHTS)], layer)
    o_ref, mix_ref = refs[8 + len(POST_WEIGHTS):]
    qrows = slice(0, q_ref.shape[0])
    krows = slice(0, k_ref.shape[0])
    for hd in range(N_HEADS):
        _attn_head(q_ref, k_ref, v_ref, (kc_ref, vc_ref), qrows, krows, hd, mix_ref)
    mix_ref[:, WIDTH_A:] = obc_ref[...]
    mod = mod_ref[pl.ds(1 + pl.program_id(0) // tiles_per_seq, 1), :]
    _post_stage(x_ref, mod, mix_ref, w, o_ref, LATENT_OUT_GROUP)


def _latent_post(q, k, v, obc, x, mods, cache, wts, l, *, seq_len, tq):
    t = x.shape[0]
    tiles_per_seq = seq_len // tq
    kc, vc = cache
    past = kc.shape[2]
    seq_of = lambda i: i // tiles_per_seq
    row_spec = lambda w: pl.BlockSpec((tq, w), lambda i: (i, 0))
    in_specs = [
        row_spec(QK_WIDTH),
        pl.BlockSpec((seq_len, QK_WIDTH), lambda i: (seq_of(i), 0)),
        pl.BlockSpec((seq_len, WIDTH_A), lambda i: (seq_of(i), 0)),
        pl.BlockSpec((None, None, past, QK_WIDTH), lambda i: (l, seq_of(i), 0, 0)),
        pl.BlockSpec((None, None, past, WIDTH_A), lambda i: (l, seq_of(i), 0, 0)),
        row_spec(BC_WIDTH),
        row_spec(D_MODEL),
        _mod_spec(l),
    ]
    args = [q, k, v, kc, vc, obc, x, mods]
    for n in POST_WEIGHTS:
        in_specs.append(_layer_spec(wts[n].shape, l))
        args.append(wts[n])
    return pl.pallas_call(
        functools.partial(_latent_post_kernel, layer=l, tiles_per_seq=tiles_per_seq),
        grid=(t // tq,),
        in_specs=in_specs,
        out_specs=row_spec(D_MODEL),
        out_shape=jax.ShapeDtypeStruct((t, D_MODEL), F32),
        scratch_shapes=[pltpu.VMEM((tq, D_MODEL), BF16)],
        compiler_params=_params(1),
        name="latent_post",
    )(*args)


def _rope_tables(n_tokens):
    rows = n_tokens // GRID_W
    row = np.repeat(np.arange(rows, dtype=np.float64), GRID_W)
    col = np.tile(np.arange(GRID_W, dtype=np.float64), rows)
    nf = QK_ROPE // 4
    inv = ROPE_THETA ** (-np.arange(nf, dtype=np.float64) / nf)
    ang_r = row[:, None] * inv
    ang_c = col[:, None] * inv
    zeros = np.zeros((n_tokens, LANES - QK_ROPE))
    cos = np.concatenate([np.cos(ang_r), np.cos(ang_r), np.cos(ang_c), np.cos(ang_c), zeros], axis=1)
    sin = np.concatenate([-np.sin(ang_r), np.sin(ang_r), -np.sin(ang_c), np.sin(ang_c), zeros], axis=1)
    return jnp.asarray(cos, F32), jnp.asarray(sin, F32)


def _prepare_weights(w_uq, w_ukv, w_s, b_s, w_conv, gains):
    w_q = jnp.pad(w_uq, ((0, 0), (0, 0), (0, 0), (0, QK_PAD - QK_NOPE - QK_ROPE)))
    w_q = w_q.reshape(DEPTH, Q_LORA, QK_WIDTH).astype(BF16)
    w_kv = jnp.concatenate(
        [w_ukv[..., :QK_NOPE].reshape(DEPTH, KV_LORA, N_HEADS * QK_NOPE),
         w_ukv[..., QK_NOPE:].reshape(DEPTH, KV_LORA, N_HEADS * V_HEAD)], axis=-1).astype(BF16)
    wts = {
        "w_q": w_q, "w_kv": w_kv,
        "w_s": w_s.reshape(DEPTH, N_HEADS_B * CHUNK, CHUNK).astype(BF16),
        "b_s": jnp.repeat(jnp.swapaxes(b_s, 1, 2), HEAD_B, axis=-1),
        "w_conv": w_conv,
    }
    for name, g in gains.items():
        wts[name] = g
    return wts


def kernel(x_prompt, x_sample, cache_ckv, cache_krope, c, c_ctx, w_ada, b_ada, g_pre_mix, w_in, g_q, w_uq, g_kv, w_ukv, g_v, w_s, b_s, w_conv, w_out, g_post_mix, g_pre_ffn, w_ff1, w_ff2, g_post_ffn):
    batch, seq, _ = x_prompt.shape
    dec_batch, dec_seq, _ = x_sample.shape

    wts = _prepare_weights(
        w_uq, w_ukv, w_s, b_s, w_conv,
        {"g_pre_mix": g_pre_mix, "g_q": g_q, "g_kv": g_kv, "g_v": g_v,
         "g_post_mix": g_post_mix, "g_pre_ffn": g_pre_ffn, "g_post_ffn": g_post_ffn})
    raw_big = (jnp.swapaxes(w_in, 1, 2), w_out, w_ff1, w_ff2)
    big = _cast_layer(raw_big, 0)

    cond = jnp.concatenate(
        [c_ctx[None, :], c, jnp.zeros((COND_ROWS - 1 - dec_batch, D_MODEL), F32)], axis=0)
    mods = _modulations(cond, w_ada, b_ada)

    cache = _cache_kv(cache_ckv, jnp.swapaxes(cache_krope, 2, 3), wts["w_kv"])
    rope_tabs = _rope_tables(dec_seq)

    xp = x_prompt.reshape(batch * seq, D_MODEL)
    xs = x_sample.reshape(dec_batch * dec_seq, D_MODEL)
    new_ckv = new_kr = None
    for l in range(DEPTH):
        wl = dict(wts, **dict(zip(BIG_WEIGHTS, big)))
        xp, new_ckv, new_kr, *big = _context_layer(
            xp, mods, wl, l, new_ckv, new_kr, raw_big if l + 1 < DEPTH else None,
            seq_len=seq, seqs_per_tile=2)
        q, k, v, obc = _latent_pre(xs, mods, wl, l, rope_tabs, seq_len=dec_seq)
        xs = _latent_post(q, k, v, obc, xs, mods, cache, wl, l, seq_len=dec_seq, tq=512)

    return (xp.reshape(batch, seq, D_MODEL), xs.reshape(dec_batch, dec_seq, D_MODEL),
            new_ckv, jnp.swapaxes(new_kr, 2, 3))
```

```python
import functools
import math

import jax
import jax.numpy as jnp
import numpy as np
from jax import lax
from jax.experimental import pallas as pl
from jax.experimental.pallas import tpu as pltpu

F32 = jnp.float32
BF16 = jnp.bfloat16

D_MODEL = 1024
DEPTH = 4
GRID_W = 64
N_HEADS = 4
QK_NOPE = 128
QK_ROPE = 64
V_HEAD = 128
Q_LORA = 384
KV_LORA = 256
WIDTH_A = N_HEADS * V_HEAD
ROPE_THETA = 10000.0
WIDTH_B = 256
N_HEADS_B = 4
HEAD_B = WIDTH_B // N_HEADS_B
CHUNK = 128
WIDTH_C = 256
D_FF = 4 * D_MODEL
N_MOD = 6
EPS = 1e-6

LANES = 128
QK_PAD = 2 * LANES
QK_WIDTH = N_HEADS * QK_PAD
KV_COLS = N_HEADS * (QK_NOPE + V_HEAD)
BC_WIDTH = WIDTH_B + WIDTH_C
OFF_Q = 0
OFF_CKV = OFF_Q + Q_LORA
OFF_U = OFF_CKV + KV_LORA
OFF_V = OFF_U + WIDTH_B
OFF_BG = OFF_V + WIDTH_B
OFF_CG = OFF_BG + WIDTH_C
OFF_HH = OFF_CG + WIDTH_C
OFF_KR = OFF_HH + WIDTH_C
IN_COLS_PAD = OFF_KR + LANES
COND_ROWS = 8
VMEM_LIMIT = 56 * 1024 * 1024
SM_SCALE_LOG2 = math.log2(math.e) / math.sqrt(QK_NOPE + QK_ROPE)
FF_CHUNK = 1024
OUT_SPLIT = 2

PRE_WEIGHTS = ("g_pre_mix", "w_in", "g_q", "w_q", "g_kv", "w_kv", "g_v", "w_s", "b_s", "w_conv")
POST_WEIGHTS = ("g_post_mix", "w_out", "g_pre_ffn", "w_ff1", "w_ff2", "g_post_ffn")
BIG_WEIGHTS = ("w_in", "w_out", "w_ff1", "w_ff2")
BIG_SHAPES = ((D_MODEL, IN_COLS_PAD), (D_MODEL, D_MODEL), (D_MODEL, D_FF), (D_FF, D_MODEL))
CAST_STEPS = 16


def _rms(x, g):
    return x * lax.rsqrt(jnp.mean(x * x, axis=-1, keepdims=True) + EPS) * g


def _dot(a, b):
    return jnp.dot(a, b, preferred_element_type=F32)


def _dot_nt(a, b):
    return lax.dot_general(a, b, (((1,), (1,)), ((), ())), preferred_element_type=F32)


def _params(n_axes):
    return pltpu.CompilerParams(
        dimension_semantics=("parallel",) * n_axes, vmem_limit_bytes=VMEM_LIMIT)


def _layer_spec(shape, l):
    if len(shape) == 2:
        return pl.BlockSpec(tuple(shape), lambda *_: (0, 0), pipeline_mode=pl.Buffered(1))
    return pl.BlockSpec((None,) + tuple(shape[1:]), lambda *_: (l, 0, 0),
                        pipeline_mode=pl.Buffered(1))


class _LayerRow:
    def __init__(self, ref, layer):
        self.ref, self.layer = ref, layer

    def __getitem__(self, idx):
        assert idx is Ellipsis
        return self.ref[self.layer:self.layer + 1, :]


def _weight_refs(names, refs, layer):
    return {n: _LayerRow(r, layer) if n.startswith("g_") else r for n, r in zip(names, refs)}


def _mod_spec():
    return pl.BlockSpec((COND_ROWS, N_MOD * D_MODEL), lambda *_: (0, 0),
                        pipeline_mode=pl.Buffered(1))


MOD_COLS = N_MOD * D_MODEL


def _mod_stage(cond_ref, w_ref, b_ref, o_ref):
    c = cond_ref[...]
    s = c / (1.0 + jnp.exp(-c))
    o_ref[...] = _dot(s.astype(BF16), w_ref[...].astype(BF16)) + b_ref[...]


def _mod_specs(l, n_steps):
    tn = MOD_COLS // n_steps
    in_specs = [
        pl.BlockSpec((COND_ROWS, D_MODEL), lambda i: (0, 0), pipeline_mode=pl.Buffered(1)),
        pl.BlockSpec((None, D_MODEL, tn), lambda i: (l, 0, i)),
        pl.BlockSpec((None, 1, tn), lambda i: (l, 0, i)),
    ]
    return (in_specs, pl.BlockSpec((COND_ROWS, tn), lambda i: (0, i)),
            jax.ShapeDtypeStruct((COND_ROWS, MOD_COLS), F32))


def _modulation_layer(cond, w_ada, b_ada, l):
    in_specs, out_spec, out_shape = _mod_specs(l, 4)
    return pl.pallas_call(
        _mod_stage,
        grid=(4,),
        in_specs=in_specs,
        out_specs=out_spec,
        out_shape=out_shape,
        compiler_params=_params(1),
        name="modulation",
    )(cond, w_ada, b_ada)


def _cache_kv_kernel(ckv_ref, kr_ref, wkv_ref, k_ref, v_ref):
    nb, past, _ = ckv_ref.shape
    kv = _dot(ckv_ref[...].reshape(nb * past, KV_LORA).astype(BF16), wkv_ref[...])
    pad = jnp.zeros((QK_PAD - QK_NOPE - QK_ROPE, past), F32)
    for b in range(nb):
        rows = slice(b * past, (b + 1) * past)
        krz = jnp.concatenate([kr_ref[b], pad], axis=0).T.astype(BF16)
        for h in range(N_HEADS):
            lo = h * QK_PAD
            k_ref[b, :, lo:lo + QK_NOPE] = kv[rows, h * QK_NOPE:(h + 1) * QK_NOPE].astype(BF16)
            k_ref[b, :, lo + QK_NOPE:lo + QK_PAD] = krz
        v_ref[b] = kv[rows, N_HEADS * QK_NOPE:].astype(BF16)


def _cache_kv(cache_ckv, cache_krope_t, w_kv):
    nb, _, past, _ = cache_ckv.shape
    return pl.pallas_call(
        _cache_kv_kernel,
        grid=(DEPTH,),
        in_specs=[
            pl.BlockSpec((nb, None, past, KV_LORA), lambda l: (0, l, 0, 0)),
            pl.BlockSpec((nb, None, QK_ROPE, past), lambda l: (0, l, 0, 0)),
            pl.BlockSpec((None, KV_LORA, KV_COLS), lambda l: (l, 0, 0)),
        ],
        out_specs=[
            pl.BlockSpec((None, nb, past, QK_WIDTH), lambda l: (l, 0, 0, 0)),
            pl.BlockSpec((None, nb, past, WIDTH_A), lambda l: (l, 0, 0, 0)),
        ],
        out_shape=[
            jax.ShapeDtypeStruct((DEPTH, nb, past, QK_WIDTH), BF16),
            jax.ShapeDtypeStruct((DEPTH, nb, past, WIDTH_A), BF16),
        ],
        compiler_params=_params(1),
        name="cache_kv",
    )(cache_ckv, cache_krope_t, w_kv)


IN_HALF = IN_COLS_PAD // CAST_STEPS // 2


def _cast_stage(in_refs, out_refs):
    win_a, win_b, wout, w1, w2 = in_refs
    win_o, wout_o, w1_o, w2_o = out_refs
    last = pl.program_id(0) == CAST_STEPS - 1
    b = jnp.where(last, 0.0, win_b[...])
    win_o[...] = jnp.concatenate([win_a[...], b], axis=0).T.astype(BF16)
    wout_o[...] = wout[...].astype(BF16)
    w1_o[...] = w1[...].astype(BF16)
    w2_o[...] = w2[...].astype(BF16)


def _w_in_block(i, half):
    kr_block = (Q_LORA + KV_LORA) // IN_HALF
    n_front = kr_block // 2
    shifted = jnp.where(i < CAST_STEPS - 1, 2 * i + 1 + half, kr_block)
    return jnp.where(i < n_front, 2 * i + half, shifted)


def _cast_specs(raw, l):
    w_in_t = raw[0]
    in_specs = [pl.BlockSpec((None, IN_HALF, D_MODEL), lambda i, h=h: (l, _w_in_block(i, h), 0))
                for h in range(2)]
    out_specs = [pl.BlockSpec((D_MODEL, IN_COLS_PAD // CAST_STEPS), lambda i: (0, i))]
    out_shape = [jax.ShapeDtypeStruct(BIG_SHAPES[0], BF16)]
    for a, (rows, cols) in zip(raw[1:], BIG_SHAPES[1:]):
        chunk = rows // CAST_STEPS
        in_specs.append(pl.BlockSpec((None, chunk, cols), lambda i: (l, i, 0)))
        out_specs.append(pl.BlockSpec((chunk, cols), lambda i: (i, 0)))
        out_shape.append(jax.ShapeDtypeStruct((rows, cols), BF16))
    return in_specs, out_specs, out_shape, [w_in_t, w_in_t] + list(raw[1:])


def _cast_kernel(*refs):
    _cast_stage(refs[:5], refs[5:])


def _cast_layer(raw, l):
    in_specs, out_specs, out_shape, args = _cast_specs(raw, l)
    return pl.pallas_call(
        _cast_kernel,
        grid=(CAST_STEPS,),
        in_specs=in_specs,
        out_specs=out_specs,
        out_shape=out_shape,
        compiler_params=_params(1),
        name="cast_weights",
    )(*args)


def _swap_halves(x):
    lane = lax.broadcasted_iota(jnp.int32, x.shape, 1)
    quarter = QK_ROPE // 4
    first_half = (lane & (2 * quarter - 1)) < quarter
    return jnp.where(first_half, pltpu.roll(x, LANES - quarter, 1), pltpu.roll(x, quarter, 1))


def _pre_stage(x, mod, w, rope, seq_len, q_ref, k_ref, v_ref, ob_ref, oc_ref, cache_refs):
    tm = x.shape[0]
    sh1 = mod[:, 0:D_MODEL]
    sc1 = mod[:, D_MODEL:2 * D_MODEL]
    h = _rms(x, w["g_pre_mix"][...] * (1.0 + sc1)) + sh1
    z = _dot(h.astype(BF16), w["w_in"][...])

    ckv = _rms(z[:, OFF_CKV:OFF_CKV + KV_LORA], w["g_kv"][...])
    krz = z[:, OFF_KR:OFF_KR + LANES]
    if cache_refs is not None:
        ckv_ref, kr_ref = cache_refs
        if len(ckv_ref.shape) == 4:
            ckv_ref[:, 1:] = jnp.zeros((ckv_ref.shape[0], DEPTH - 1) + ckv_ref.shape[2:], F32)
            kr_ref[:, 1:] = jnp.zeros((kr_ref.shape[0], DEPTH - 1) + kr_ref.shape[2:], F32)
            ckv_ref, kr_ref = ckv_ref.at[:, 0], kr_ref.at[:, 0]
        ckv_ref[...] = ckv.reshape(ckv_ref.shape)
        for s in range(tm // seq_len):
            kr_ref[s] = krz[s * seq_len:(s + 1) * seq_len, :].T[0:QK_ROPE, :]
    qn = _rms(z[:, OFF_Q:OFF_Q + Q_LORA], w["g_q"][...])
    q = _dot(qn.astype(BF16), w["w_q"][...]) * SM_SCALE_LOG2
    kv = _dot(ckv.astype(BF16), w["w_kv"][...])
    if rope is not None:
        cos, sin = rope
        krz = krz * cos + _swap_halves(krz) * sin
    krz = krz.astype(BF16)
    for hd in range(N_HEADS):
        lo = hd * QK_PAD
        q_ref[:, lo:lo + QK_NOPE] = q[:, lo:lo + QK_NOPE].astype(BF16)
        qr = q[:, lo + QK_NOPE:lo + QK_PAD]
        if rope is not None:
            qr = qr * cos + _swap_halves(qr) * sin
        q_ref[:, lo + QK_NOPE:lo + QK_PAD] = qr.astype(BF16)
        k_ref[:, lo:lo + QK_NOPE] = kv[:, hd * QK_NOPE:(hd + 1) * QK_NOPE].astype(BF16)
        k_ref[:, lo + QK_NOPE:lo + QK_PAD] = krz
    v_ref[...] = kv[:, N_HEADS * QK_NOPE:].astype(BF16)

    u = jax.nn.gelu(z[:, OFF_U:OFF_U + WIDTH_B])
    vn = _rms(jax.nn.gelu(z[:, OFF_V:OFF_V + WIDTH_B]), w["g_v"][...]).astype(BF16)
    lane = lax.broadcasted_iota(jnp.int32, (CHUNK, WIDTH_B), 1)
    ws = w["w_s"][...]
    bs = w["b_s"][...]
    for c in range(tm // CHUNK):
        rows = slice(c * CHUNK, (c + 1) * CHUNK)
        r = _dot(ws, vn[rows, :])
        mixed = r[(N_HEADS_B - 1) * CHUNK:, :]
        for hb in range(N_HEADS_B - 2, -1, -1):
            mixed = jnp.where(lane < (hb + 1) * HEAD_B, r[hb * CHUNK:(hb + 1) * CHUNK, :], mixed)
        ob_ref[rows, :] = (u[rows, :] * (mixed + bs)).astype(BF16)

    zc = z[:, OFF_CG:OFF_CG + WIDTH_C] * z[:, OFF_HH:OFF_HH + WIDTH_C]
    pos = lax.broadcasted_iota(jnp.int32, (tm, WIDTH_C), 0) & (seq_len - 1)
    z_prev = jnp.where(pos == 0, 0.0, pltpu.roll(zc, 1, 0))
    z_next = jnp.where(pos == seq_len - 1, 0.0, pltpu.roll(zc, tm - 1, 0))
    wc = w["w_conv"][...]
    y = z_prev * wc[0:1, :] + zc * wc[1:2, :] + z_next * wc[2:3, :]
    oc_ref[...] = (z[:, OFF_BG:OFF_BG + WIDTH_C] * y).astype(BF16)


def _attn_head(q_ref, k_ref, v_ref, cache, qrows, krows, hd, oa_ref):
    qk_cols = slice(hd * QK_PAD, (hd + 1) * QK_PAD)
    v_cols = slice(hd * V_HEAD, (hd + 1) * V_HEAD)
    qh = q_ref[qrows, qk_cols]
    s_lat = _dot_nt(qh, k_ref[krows, qk_cols])
    mx = jnp.max(s_lat, axis=-1, keepdims=True)
    if cache is not None:
        kc_ref, vc_ref = cache
        s_ctx = _dot_nt(qh, kc_ref[:, qk_cols])
        mx = jnp.maximum(mx, jnp.max(s_ctx, axis=-1, keepdims=True))
    p_lat = jnp.exp2(s_lat - mx)
    den = jnp.sum(p_lat, axis=-1, keepdims=True)
    o = _dot(p_lat.astype(BF16), v_ref[krows, v_cols])
    if cache is not None:
        p_ctx = jnp.exp2(s_ctx - mx)
        den = den + jnp.sum(p_ctx, axis=-1, keepdims=True)
        o = o + _dot(p_ctx.astype(BF16), vc_ref[:, v_cols])
    oa_ref[qrows, v_cols] = (o * (1.0 / den)).astype(BF16)


def _post_stage(x_ref, mod, mix_ref, w, o_ref):
    ga1 = mod[:, 2 * D_MODEL:3 * D_MODEL]
    sh2 = mod[:, 3 * D_MODEL:4 * D_MODEL]
    sc2 = mod[:, 4 * D_MODEL:5 * D_MODEL]
    ga2 = mod[:, 5 * D_MODEL:6 * D_MODEL]
    post_mix_row = ga1 * w["g_post_mix"][...]
    pre_ffn_row = w["g_pre_ffn"][...] * (1.0 + sc2)
    post_ffn_row = ga2 * w["g_post_ffn"][...]
    group = x_ref.shape[0] // OUT_SPLIT
    for r in range(OUT_SPLIT):
        rows = slice(r * group, (r + 1) * group)
        mo = _dot(mix_ref[rows, :], w["w_out"][...])
        x1 = x_ref[rows, :] + _rms(mo, post_mix_row)
        o_ref[rows, :] = x1
        mix_ref[rows, :] = (_rms(x1, pre_ffn_row) + sh2).astype(BF16)
    h2 = mix_ref[...]
    f = None
    for j in range(D_FF // FF_CHUNK):
        cols = slice(j * FF_CHUNK, (j + 1) * FF_CHUNK)
        a = jnp.square(jnp.maximum(_dot(h2, w["w_ff1"][:, cols]), 0.0)).astype(BF16)
        part = _dot(a, w["w_ff2"][cols, :])
        f = part if f is None else f + part
    o_ref[...] = o_ref[...] + _rms(f, post_ffn_row)


def _context_kernel(*refs, layer, seq_len, aliased, cast_next, mod_next):
    n_w = len(PRE_WEIGHTS) + len(POST_WEIGHTS)
    n_cast_out = len(BIG_WEIGHTS) if cast_next else 0
    n_cast_in = n_cast_out + 1 if cast_next else 0
    n_side_out = n_cast_out + (1 if mod_next else 0)
    n_side_in = n_cast_in + (3 if mod_next else 0)
    x_ref, mod_ref = refs[:2]
    w = _weight_refs(PRE_WEIGHTS + POST_WEIGHTS, refs[2:2 + n_w], layer)
    side_in = refs[2 + n_w:2 + n_w + n_side_in]
    n_in = 2 + n_w + n_side_in + (2 if aliased else 0)
    o_ref, ckv_ref, kr_ref = refs[n_in:n_in + 3]
    side_out = refs[n_in + 3:n_in + 3 + n_side_out]
    q_ref, k_ref, v_ref, mix_ref = refs[n_in + 3 + n_side_out:]
    if cast_next:
        _cast_stage(side_in[:n_cast_in], side_out[:n_cast_out])
    if mod_next:
        _mod_stage(*side_in[n_cast_in:], side_out[n_cast_out])
    x = x_ref[...]
    mod = mod_ref[0:1, :]
    _pre_stage(x, mod, w, None, seq_len, q_ref, k_ref, v_ref,
               mix_ref.at[:, WIDTH_A:WIDTH_A + WIDTH_B], mix_ref.at[:, WIDTH_A + WIDTH_B:],
               (ckv_ref, kr_ref))
    for s in range(x.shape[0] // seq_len):
        rows = slice(s * seq_len, (s + 1) * seq_len)
        for hd in range(N_HEADS):
            _attn_head(q_ref, k_ref, v_ref, None, rows, rows, hd, mix_ref)
    _post_stage(x_ref, mod, mix_ref, w, o_ref)


def _context_layer(x, mods, wts, l, new_ckv, new_kr, raw_big, mod_args, *, seq_len, seqs_per_tile):
    t = x.shape[0]
    tm = seq_len * seqs_per_tile
    n_seq_total = t // seq_len
    aliased = new_ckv is not None
    assert aliased or l == 0
    cast_next = raw_big is not None
    mod_next = mod_args is not None
    n_tiles = t // tm
    row_spec = pl.BlockSpec((tm, D_MODEL), lambda i: (i, 0))
    layer_dim, layer_idx = (None, l) if aliased else (DEPTH, 0)
    ckv_spec = pl.BlockSpec((seqs_per_tile, layer_dim, seq_len, KV_LORA),
                            lambda i: (i, layer_idx, 0, 0))
    kr_spec = pl.BlockSpec((seqs_per_tile, layer_dim, QK_ROPE, seq_len),
                           lambda i: (i, layer_idx, 0, 0))
    in_specs = [row_spec, _mod_spec()]
    args = [x, mods]
    for n in PRE_WEIGHTS + POST_WEIGHTS:
        in_specs.append(_layer_spec(wts[n].shape, l))
        args.append(wts[n])
    out_specs = [row_spec, ckv_spec, kr_spec]
    out_shape = [
        jax.ShapeDtypeStruct((t, D_MODEL), F32),
        jax.ShapeDtypeStruct((n_seq_total, DEPTH, seq_len, KV_LORA), F32),
        jax.ShapeDtypeStruct((n_seq_total, DEPTH, QK_ROPE, seq_len), F32),
    ]
    if cast_next:
        assert n_tiles == CAST_STEPS
        c_in, c_out, c_shape, c_args = _cast_specs(raw_big, l + 1)
        in_specs += c_in
        args += c_args
        out_specs += c_out
        out_shape += c_shape
    if mod_next:
        m_in, m_out, m_shape = _mod_specs(l + 1, n_tiles)
        in_specs += m_in
        args += list(mod_args)
        out_specs.append(m_out)
        out_shape.append(m_shape)
    aliases = {}
    if aliased:
        in_specs += [pl.BlockSpec(memory_space=pl.ANY)] * 2
        aliases = {len(args): 1, len(args) + 1: 2}
        args += [new_ckv, new_kr]
    return pl.pallas_call(
        functools.partial(_context_kernel, layer=l, seq_len=seq_len, aliased=aliased,
                          cast_next=cast_next, mod_next=mod_next),
        grid=(n_tiles,),
        in_specs=in_specs,
        out_specs=out_specs,
        out_shape=out_shape,
        scratch_shapes=[
            pltpu.VMEM((tm, QK_WIDTH), BF16), pltpu.VMEM((tm, QK_WIDTH), BF16),
            pltpu.VMEM((tm, WIDTH_A), BF16), pltpu.VMEM((tm, D_MODEL), BF16),
        ],
        input_output_aliases=aliases,
        compiler_params=_params(1),
        name="context_layer",
    )(*args)


def _latent_pre_kernel(*refs, layer, seq_len):
    x_ref, mod_ref = refs[:2]
    w = _weight_refs(PRE_WEIGHTS, refs[2:2 + len(PRE_WEIGHTS)], layer)
    cos_ref, sin_ref, q_ref, k_ref, v_ref, obc_ref = refs[2 + len(PRE_WEIGHTS):]
    mod = mod_ref[pl.ds(1 + pl.program_id(0), 1), :]
    _pre_stage(x_ref[...], mod, w, (cos_ref[...], sin_ref[...]), seq_len,
               q_ref, k_ref, v_ref, obc_ref.at[:, 0:WIDTH_B], obc_ref.at[:, WIDTH_B:], None)


def _latent_pre(x, mods, wts, l, rope_tabs, *, seq_len):
    t = x.shape[0]
    tm = seq_len
    row_spec = lambda w: pl.BlockSpec((tm, w), lambda i: (i, 0))
    in_specs = [row_spec(D_MODEL), _mod_spec()]
    args = [x, mods]
    for n in PRE_WEIGHTS:
        in_specs.append(_layer_spec(wts[n].shape, l))
        args.append(wts[n])
    in_specs += [pl.BlockSpec((tm, LANES), lambda i: (0, 0), pipeline_mode=pl.Buffered(1))] * 2
    args += list(rope_tabs)
    return pl.pallas_call(
        functools.partial(_latent_pre_kernel, layer=l, seq_len=seq_len),
        grid=(t // tm,),
        in_specs=in_specs,
        out_specs=[row_spec(QK_WIDTH), row_spec(QK_WIDTH), row_spec(WIDTH_A), row_spec(BC_WIDTH)],
        out_shape=[
            jax.ShapeDtypeStruct((t, QK_WIDTH), BF16),
            jax.ShapeDtypeStruct((t, QK_WIDTH), BF16),
            jax.ShapeDtypeStruct((t, WIDTH_A), BF16),
            jax.ShapeDtypeStruct((t, BC_WIDTH), BF16),
        ],
        compiler_params=_params(1),
        name="latent_pre",
    )(*args)


def _latent_post_kernel(*refs, layer, tiles_per_seq):
    q_ref, k_ref, v_ref, kc_ref, vc_ref, obc_ref, x_ref, mod_ref = refs[:8]
    w = _weight_refs(POST_WEIGHTS, refs[8:8 + len(POST_WEIGHTS)], layer)
    o_ref, mix_ref = refs[8 + len(POST_WEIGHTS):]
    qrows = slice(0, q_ref.shape[0])
    krows = slice(0, k_ref.shape[0])
    for hd in range(N_HEADS):
        _attn_head(q_ref, k_ref, v_ref, (kc_ref, vc_ref), qrows, krows, hd, mix_ref)
    mix_ref[:, WIDTH_A:] = obc_ref[...]
    mod = mod_ref[pl.ds(1 + pl.program_id(0) // tiles_per_seq, 1), :]
    _post_stage(x_ref, mod, mix_ref, w, o_ref)


def _latent_post(q, k, v, obc, x, mods, cache, wts, l, *, seq_len, tq):
    t = x.shape[0]
    tiles_per_seq = seq_len // tq
    kc, vc = cache
    past = kc.shape[2]
    seq_of = lambda i: i // tiles_per_seq
    row_spec = lambda w: pl.BlockSpec((tq, w), lambda i: (i, 0))
    in_specs = [
        row_spec(QK_WIDTH),
        pl.BlockSpec((seq_len, QK_WIDTH), lambda i: (seq_of(i), 0)),
        pl.BlockSpec((seq_len, WIDTH_A), lambda i: (seq_of(i), 0)),
        pl.BlockSpec((None, None, past, QK_WIDTH), lambda i: (l, seq_of(i), 0, 0)),
        pl.BlockSpec((None, None, past, WIDTH_A), lambda i: (l, seq_of(i), 0, 0)),
        row_spec(BC_WIDTH),
        row_spec(D_MODEL),
        _mod_spec(),
    ]
    args = [q, k, v, kc, vc, obc, x, mods]
    for n in POST_WEIGHTS:
        in_specs.append(_layer_spec(wts[n].shape, l))
        args.append(wts[n])
    return pl.pallas_call(
        functools.partial(_latent_post_kernel, layer=l, tiles_per_seq=tiles_per_seq),
        grid=(t // tq,),
        in_specs=in_specs,
        out_specs=row_spec(D_MODEL),
        out_shape=jax.ShapeDtypeStruct((t, D_MODEL), F32),
        scratch_shapes=[pltpu.VMEM((tq, D_MODEL), BF16)],
        compiler_params=_params(1),
        name="latent_post",
    )(*args)


def _rope_tables(n_tokens):
    rows = n_tokens // GRID_W
    row = np.repeat(np.arange(rows, dtype=np.float64), GRID_W)
    col = np.tile(np.arange(GRID_W, dtype=np.float64), rows)
    nf = QK_ROPE // 4
    inv = ROPE_THETA ** (-np.arange(nf, dtype=np.float64) / nf)
    ang_r = row[:, None] * inv
    ang_c = col[:, None] * inv
    zeros = np.zeros((n_tokens, LANES - QK_ROPE))
    cos = np.concatenate([np.cos(ang_r), np.cos(ang_r), np.cos(ang_c), np.cos(ang_c), zeros], axis=1)
    sin = np.concatenate([-np.sin(ang_r), np.sin(ang_r), -np.sin(ang_c), np.sin(ang_c), zeros], axis=1)
    return jnp.asarray(cos, F32), jnp.asarray(sin, F32)


def _prepare_weights(w_uq, w_ukv, w_s, b_s, w_conv, gains):
    w_q = jnp.pad(w_uq, ((0, 0), (0, 0), (0, 0), (0, QK_PAD - QK_NOPE - QK_ROPE)))
    w_q = w_q.reshape(DEPTH, Q_LORA, QK_WIDTH).astype(BF16)
    w_kv = jnp.concatenate(
        [w_ukv[..., :QK_NOPE].reshape(DEPTH, KV_LORA, N_HEADS * QK_NOPE),
         w_ukv[..., QK_NOPE:].reshape(DEPTH, KV_LORA, N_HEADS * V_HEAD)], axis=-1).astype(BF16)
    wts = {
        "w_q": w_q, "w_kv": w_kv,
        "w_s": w_s.reshape(DEPTH, N_HEADS_B * CHUNK, CHUNK).astype(BF16),
        "b_s": jnp.repeat(jnp.swapaxes(b_s, 1, 2), HEAD_B, axis=-1),
        "w_conv": w_conv,
    }
    for name, g in gains.items():
        wts[name] = g
    return wts


def kernel(x_prompt, x_sample, cache_ckv, cache_krope, c, c_ctx, w_ada, b_ada, g_pre_mix, w_in, g_q, w_uq, g_kv, w_ukv, g_v, w_s, b_s, w_conv, w_out, g_post_mix, g_pre_ffn, w_ff1, w_ff2, g_post_ffn):
    batch, seq, _ = x_prompt.shape
    dec_batch, dec_seq, _ = x_sample.shape

    wts = _prepare_weights(
        w_uq, w_ukv, w_s, b_s, w_conv,
        {"g_pre_mix": g_pre_mix, "g_q": g_q, "g_kv": g_kv, "g_v": g_v,
         "g_post_mix": g_post_mix, "g_pre_ffn": g_pre_ffn, "g_post_ffn": g_post_ffn})
    raw_big = (jnp.swapaxes(w_in, 1, 2), w_out, w_ff1, w_ff2)
    big = _cast_layer(raw_big, 0)

    cond = jnp.concatenate(
        [c_ctx[None, :], c, jnp.zeros((COND_ROWS - 1 - dec_batch, D_MODEL), F32)], axis=0)
    mod_args = (cond, w_ada, b_ada.reshape(DEPTH, 1, MOD_COLS))
    mods, mods_next = _modulation_layer(*mod_args, 0), _modulation_layer(*mod_args, 1)

    cache = _cache_kv(cache_ckv, jnp.swapaxes(cache_krope, 2, 3), wts["w_kv"])
    rope_tabs = _rope_tables(dec_seq)

    xp = x_prompt.reshape(batch * seq, D_MODEL)
    xs = x_sample.reshape(dec_batch * dec_seq, D_MODEL)
    new_ckv = new_kr = None
    for l in range(DEPTH):
        wl = dict(wts, **dict(zip(BIG_WEIGHTS, big)))
        side_mods = 1 <= l < DEPTH - 1
        xp, new_ckv, new_kr, *side = _context_layer(
            xp, mods, wl, l, new_ckv, new_kr, raw_big if l + 1 < DEPTH else None,
            mod_args if side_mods else None, seq_len=seq, seqs_per_tile=2)
        q, k, v, obc = _latent_pre(xs, mods, wl, l, rope_tabs, seq_len=dec_seq)
        xs = _latent_post(q, k, v, obc, xs, mods, cache, wl, l, seq_len=dec_seq, tq=512)
        big, mods = side[:len(BIG_WEIGHTS)], (side[-1] if side_mods else mods_next)

    return (xp.reshape(batch, seq, D_MODEL), xs.reshape(dec_batch, dec_seq, D_MODEL),
            new_ckv, jnp.swapaxes(new_kr, 2, 3))
```

```python
import functools
import math

import jax
import jax.numpy as jnp
import numpy as np
from jax import lax
from jax.experimental import pallas as pl
from jax.experimental.pallas import tpu as pltpu

F32 = jnp.float32
BF16 = jnp.bfloat16

D_MODEL = 1024
DEPTH = 4
GRID_W = 64
N_HEADS = 4
QK_NOPE = 128
QK_ROPE = 64
V_HEAD = 128
Q_LORA = 384
KV_LORA = 256
WIDTH_A = N_HEADS * V_HEAD
ROPE_THETA = 10000.0
WIDTH_B = 256
N_HEADS_B = 4
HEAD_B = WIDTH_B // N_HEADS_B
CHUNK = 128
WIDTH_C = 256
D_FF = 4 * D_MODEL
N_MOD = 6
EPS = 1e-6

LANES = 128
QK_PAD = 2 * LANES
QK_WIDTH = N_HEADS * QK_PAD
KV_COLS = N_HEADS * (QK_NOPE + V_HEAD)
BC_WIDTH = WIDTH_B + WIDTH_C
OFF_Q = 0
OFF_CKV = OFF_Q + Q_LORA
OFF_U = OFF_CKV + KV_LORA
OFF_V = OFF_U + WIDTH_B
OFF_BG = OFF_V + WIDTH_B
OFF_CG = OFF_BG + WIDTH_C
OFF_HH = OFF_CG + WIDTH_C
OFF_KR = OFF_HH + WIDTH_C
IN_COLS_PAD = OFF_KR + LANES
COND_ROWS = 8
VMEM_LIMIT = 56 * 1024 * 1024
SM_SCALE_LOG2 = math.log2(math.e) / math.sqrt(QK_NOPE + QK_ROPE)
FF_CHUNK = 1024
OUT_SPLIT = 2

PRE_WEIGHTS = ("g_pre_mix", "w_in", "g_q", "w_q", "g_kv", "w_kv", "g_v", "w_s", "b_s", "w_conv")
POST_WEIGHTS = ("g_post_mix", "w_out", "g_pre_ffn", "w_ff1", "w_ff2", "g_post_ffn")
BIG_WEIGHTS = ("w_in", "w_out", "w_ff1", "w_ff2")
BIG_SHAPES = ((D_MODEL, IN_COLS_PAD), (D_MODEL, D_MODEL), (D_MODEL, D_FF), (D_FF, D_MODEL))
CAST_STEPS = 16


def _rms(x, g):
    return x * lax.rsqrt(jnp.mean(x * x, axis=-1, keepdims=True) + EPS) * g


def _dot(a, b):
    return jnp.dot(a, b, preferred_element_type=F32)


def _dot_nt(a, b):
    return lax.dot_general(a, b, (((1,), (1,)), ((), ())), preferred_element_type=F32)


def _params(n_axes):
    return pltpu.CompilerParams(
        dimension_semantics=("parallel",) * n_axes, vmem_limit_bytes=VMEM_LIMIT)


def _layer_spec(shape, l):
    if len(shape) == 2:
        return pl.BlockSpec(tuple(shape), lambda *_: (0, 0), pipeline_mode=pl.Buffered(1))
    return pl.BlockSpec((None,) + tuple(shape[1:]), lambda *_: (l, 0, 0),
                        pipeline_mode=pl.Buffered(1))


class _LayerRow:
    def __init__(self, ref, layer):
        self.ref, self.layer = ref, layer

    def __getitem__(self, idx):
        assert idx is Ellipsis
        return self.ref[self.layer:self.layer + 1, :]


def _weight_refs(names, refs, layer):
    return {n: _LayerRow(r, layer) if n.startswith("g_") else r for n, r in zip(names, refs)}


def _mod_spec():
    return pl.BlockSpec((COND_ROWS, N_MOD * D_MODEL), lambda *_: (0, 0),
                        pipeline_mode=pl.Buffered(1))


MOD_COLS = N_MOD * D_MODEL


def _mod_stage(cond_ref, w_ref, b_ref, o_ref):
    c = cond_ref[...]
    s = c / (1.0 + jnp.exp(-c))
    o_ref[...] = _dot(s.astype(BF16), w_ref[...].astype(BF16)) + b_ref[...]


def _mod_specs(l, n_steps):
    tn = MOD_COLS // n_steps
    in_specs = [
        pl.BlockSpec((COND_ROWS, D_MODEL), lambda i: (0, 0), pipeline_mode=pl.Buffered(1)),
        pl.BlockSpec((None, D_MODEL, tn), lambda i: (l, 0, i)),
        pl.BlockSpec((None, 1, tn), lambda i: (l, 0, i)),
    ]
    return (in_specs, pl.BlockSpec((COND_ROWS, tn), lambda i: (0, i)),
            jax.ShapeDtypeStruct((COND_ROWS, MOD_COLS), F32))


def _modulation_layers(cond, w_ada, b_ada, n_layers):
    n_chunks = 4
    tn = MOD_COLS // n_chunks
    return pl.pallas_call(
        _mod_stage,
        grid=(n_layers, n_chunks),
        in_specs=[
            pl.BlockSpec((COND_ROWS, D_MODEL), lambda l, j: (0, 0)),
            pl.BlockSpec((None, D_MODEL, tn), lambda l, j: (l, 0, j)),
            pl.BlockSpec((None, 1, tn), lambda l, j: (l, 0, j)),
        ],
        out_specs=pl.BlockSpec((None, COND_ROWS, tn), lambda l, j: (l, 0, j)),
        out_shape=jax.ShapeDtypeStruct((n_layers, COND_ROWS, MOD_COLS), F32),
        compiler_params=_params(2),
        name="modulation",
    )(cond, w_ada, b_ada)


def _cache_kv_kernel(ckv_ref, kr_ref, wkv_ref, k_ref, v_ref):
    nb, past, _ = ckv_ref.shape
    kv = _dot(ckv_ref[...].reshape(nb * past, KV_LORA).astype(BF16), wkv_ref[...])
    pad = jnp.zeros((QK_PAD - QK_NOPE - QK_ROPE, past), F32)
    for b in range(nb):
        rows = slice(b * past, (b + 1) * past)
        krz = jnp.concatenate([kr_ref[b], pad], axis=0).T.astype(BF16)
        for h in range(N_HEADS):
            lo = h * QK_PAD
            k_ref[b, :, lo:lo + QK_NOPE] = kv[rows, h * QK_NOPE:(h + 1) * QK_NOPE].astype(BF16)
            k_ref[b, :, lo + QK_NOPE:lo + QK_PAD] = krz
        v_ref[b] = kv[rows, N_HEADS * QK_NOPE:].astype(BF16)


def _cache_kv(cache_ckv, cache_krope_t, w_kv):
    nb, _, past, _ = cache_ckv.shape
    return pl.pallas_call(
        _cache_kv_kernel,
        grid=(DEPTH,),
        in_specs=[
            pl.BlockSpec((nb, None, past, KV_LORA), lambda l: (0, l, 0, 0)),
            pl.BlockSpec((nb, None, QK_ROPE, past), lambda l: (0, l, 0, 0)),
            pl.BlockSpec((None, KV_LORA, KV_COLS), lambda l: (l, 0, 0)),
        ],
        out_specs=[
            pl.BlockSpec((None, nb, past, QK_WIDTH), lambda l: (l, 0, 0, 0)),
            pl.BlockSpec((None, nb, past, WIDTH_A), lambda l: (l, 0, 0, 0)),
        ],
        out_shape=[
            jax.ShapeDtypeStruct((DEPTH, nb, past, QK_WIDTH), BF16),
            jax.ShapeDtypeStruct((DEPTH, nb, past, WIDTH_A), BF16),
        ],
        compiler_params=_params(1),
        name="cache_kv",
    )(cache_ckv, cache_krope_t, w_kv)


IN_HALF = IN_COLS_PAD // CAST_STEPS // 2


def _cast_stage(in_refs, out_refs):
    win_a, win_b, wout, w1, w2 = in_refs
    win_o, wout_o, w1_o, w2_o = out_refs
    last = pl.program_id(0) == CAST_STEPS - 1
    b = jnp.where(last, 0.0, win_b[...])
    win_o[...] = jnp.concatenate([win_a[...], b], axis=0).T.astype(BF16)
    wout_o[...] = wout[...].astype(BF16)
    w1_o[...] = w1[...].astype(BF16)
    w2_o[...] = w2[...].astype(BF16)


def _w_in_block(i, half):
    kr_block = (Q_LORA + KV_LORA) // IN_HALF
    n_front = kr_block // 2
    shifted = jnp.where(i < CAST_STEPS - 1, 2 * i + 1 + half, kr_block)
    return jnp.where(i < n_front, 2 * i + half, shifted)


def _cast_specs(raw, l):
    w_in_t = raw[0]
    in_specs = [pl.BlockSpec((None, IN_HALF, D_MODEL), lambda i, h=h: (l, _w_in_block(i, h), 0))
                for h in range(2)]
    out_specs = [pl.BlockSpec((D_MODEL, IN_COLS_PAD // CAST_STEPS), lambda i: (0, i))]
    out_shape = [jax.ShapeDtypeStruct(BIG_SHAPES[0], BF16)]
    for a, (rows, cols) in zip(raw[1:], BIG_SHAPES[1:]):
        chunk = rows // CAST_STEPS
        in_specs.append(pl.BlockSpec((None, chunk, cols), lambda i: (l, i, 0)))
        out_specs.append(pl.BlockSpec((chunk, cols), lambda i: (i, 0)))
        out_shape.append(jax.ShapeDtypeStruct((rows, cols), BF16))
    return in_specs, out_specs, out_shape, [w_in_t, w_in_t] + list(raw[1:])


def _cast_kernel(*refs):
    _cast_stage(refs[:5], refs[5:])


def _cast_layer(raw, l):
    in_specs, out_specs, out_shape, args = _cast_specs(raw, l)
    return pl.pallas_call(
        _cast_kernel,
        grid=(CAST_STEPS,),
        in_specs=in_specs,
        out_specs=out_specs,
        out_shape=out_shape,
        compiler_params=_params(1),
        name="cast_weights",
    )(*args)


def _swap_halves(x):
    lane = lax.broadcasted_iota(jnp.int32, x.shape, 1)
    quarter = QK_ROPE // 4
    first_half = (lane & (2 * quarter - 1)) < quarter
    return jnp.where(first_half, pltpu.roll(x, LANES - quarter, 1), pltpu.roll(x, quarter, 1))


def _pre_stage(x, mod, w, rope, seq_len, q_ref, k_ref, v_ref, ob_ref, oc_ref, cache_refs):
    tm = x.shape[0]
    sh1 = mod[:, 0:D_MODEL]
    sc1 = mod[:, D_MODEL:2 * D_MODEL]
    h = _rms(x, w["g_pre_mix"][...] * (1.0 + sc1)) + sh1
    z = _dot(h.astype(BF16), w["w_in"][...])

    ckv = _rms(z[:, OFF_CKV:OFF_CKV + KV_LORA], w["g_kv"][...])
    krz = z[:, OFF_KR:OFF_KR + LANES]
    if cache_refs is not None:
        ckv_ref, kr_ref = cache_refs
        if len(ckv_ref.shape) == 4:
            ckv_ref[:, 1:] = jnp.zeros((ckv_ref.shape[0], DEPTH - 1) + ckv_ref.shape[2:], F32)
            kr_ref[:, 1:] = jnp.zeros((kr_ref.shape[0], DEPTH - 1) + kr_ref.shape[2:], F32)
            ckv_ref, kr_ref = ckv_ref.at[:, 0], kr_ref.at[:, 0]
        ckv_ref[...] = ckv.reshape(ckv_ref.shape)
        for s in range(tm // seq_len):
            kr_ref[s] = krz[s * seq_len:(s + 1) * seq_len, :].T[0:QK_ROPE, :]
    qn = _rms(z[:, OFF_Q:OFF_Q + Q_LORA], w["g_q"][...])
    q = _dot(qn.astype(BF16), w["w_q"][...]) * SM_SCALE_LOG2
    kv = _dot(ckv.astype(BF16), w["w_kv"][...])
    if rope is not None:
        cos, sin = rope
        krz = krz * cos + _swap_halves(krz) * sin
    krz = krz.astype(BF16)
    for hd in range(N_HEADS):
        lo = hd * QK_PAD
        q_ref[:, lo:lo + QK_NOPE] = q[:, lo:lo + QK_NOPE].astype(BF16)
        qr = q[:, lo + QK_NOPE:lo + QK_PAD]
        if rope is not None:
            qr = qr * cos + _swap_halves(qr) * sin
        q_ref[:, lo + QK_NOPE:lo + QK_PAD] = qr.astype(BF16)
        k_ref[:, lo:lo + QK_NOPE] = kv[:, hd * QK_NOPE:(hd + 1) * QK_NOPE].astype(BF16)
        k_ref[:, lo + QK_NOPE:lo + QK_PAD] = krz
    v_ref[...] = kv[:, N_HEADS * QK_NOPE:].astype(BF16)

    u = jax.nn.gelu(z[:, OFF_U:OFF_U + WIDTH_B])
    vn = _rms(jax.nn.gelu(z[:, OFF_V:OFF_V + WIDTH_B]), w["g_v"][...]).astype(BF16)
    lane = lax.broadcasted_iota(jnp.int32, (CHUNK, WIDTH_B), 1)
    ws = w["w_s"][...]
    bs = w["b_s"][...]
    for c in range(tm // CHUNK):
        rows = slice(c * CHUNK, (c + 1) * CHUNK)
        r = _dot(ws, vn[rows, :])
        mixed = r[(N_HEADS_B - 1) * CHUNK:, :]
        for hb in range(N_HEADS_B - 2, -1, -1):
            mixed = jnp.where(lane < (hb + 1) * HEAD_B, r[hb * CHUNK:(hb + 1) * CHUNK, :], mixed)
        ob_ref[rows, :] = (u[rows, :] * (mixed + bs)).astype(BF16)

    zc = z[:, OFF_CG:OFF_CG + WIDTH_C] * z[:, OFF_HH:OFF_HH + WIDTH_C]
    pos = lax.broadcasted_iota(jnp.int32, (tm, WIDTH_C), 0) & (seq_len - 1)
    z_prev = jnp.where(pos == 0, 0.0, pltpu.roll(zc, 1, 0))
    z_next = jnp.where(pos == seq_len - 1, 0.0, pltpu.roll(zc, tm - 1, 0))
    wc = w["w_conv"][...]
    y = z_prev * wc[0:1, :] + zc * wc[1:2, :] + z_next * wc[2:3, :]
    oc_ref[...] = (z[:, OFF_BG:OFF_BG + WIDTH_C] * y).astype(BF16)


def _attn_head(q_ref, k_ref, v_ref, cache, qrows, krows, hd, oa_ref):
    qk_cols = slice(hd * QK_PAD, (hd + 1) * QK_PAD)
    v_cols = slice(hd * V_HEAD, (hd + 1) * V_HEAD)
    qh = q_ref[qrows, qk_cols]
    s_lat = _dot_nt(qh, k_ref[krows, qk_cols])
    mx = jnp.max(s_lat, axis=-1, keepdims=True)
    if cache is not None:
        kc_ref, vc_ref = cache
        s_ctx = _dot_nt(qh, kc_ref[:, qk_cols])
        mx = jnp.maximum(mx, jnp.max(s_ctx, axis=-1, keepdims=True))
    p_lat = jnp.exp2(s_lat - mx)
    den = jnp.sum(p_lat, axis=-1, keepdims=True)
    o = _dot(p_lat.astype(BF16), v_ref[krows, v_cols])
    if cache is not None:
        p_ctx = jnp.exp2(s_ctx - mx)
        den = den + jnp.sum(p_ctx, axis=-1, keepdims=True)
        o = o + _dot(p_ctx.astype(BF16), vc_ref[:, v_cols])
    oa_ref[qrows, v_cols] = (o * (1.0 / den)).astype(BF16)


def _post_stage(x_ref, mod, mix_ref, w, o_ref):
    ga1 = mod[:, 2 * D_MODEL:3 * D_MODEL]
    sh2 = mod[:, 3 * D_MODEL:4 * D_MODEL]
    sc2 = mod[:, 4 * D_MODEL:5 * D_MODEL]
    ga2 = mod[:, 5 * D_MODEL:6 * D_MODEL]
    post_mix_row = ga1 * w["g_post_mix"][...]
    pre_ffn_row = w["g_pre_ffn"][...] * (1.0 + sc2)
    post_ffn_row = ga2 * w["g_post_ffn"][...]
    group = x_ref.shape[0] // OUT_SPLIT
    for r in range(OUT_SPLIT):
        rows = slice(r * group, (r + 1) * group)
        mo = _dot(mix_ref[rows, :], w["w_out"][...])
        x1 = x_ref[rows, :] + _rms(mo, post_mix_row)
        o_ref[rows, :] = x1
        mix_ref[rows, :] = (_rms(x1, pre_ffn_row) + sh2).astype(BF16)
    h2 = mix_ref[...]
    f = None
    for j in range(D_FF // FF_CHUNK):
        cols = slice(j * FF_CHUNK, (j + 1) * FF_CHUNK)
        a = jnp.square(jnp.maximum(_dot(h2, w["w_ff1"][:, cols]), 0.0)).astype(BF16)
        part = _dot(a, w["w_ff2"][cols, :])
        f = part if f is None else f + part
    o_ref[...] = o_ref[...] + _rms(f, post_ffn_row)


def _context_kernel(*refs, layer, seq_len, aliased, cast_next, mod_next):
    n_w = len(PRE_WEIGHTS) + len(POST_WEIGHTS)
    n_cast_out = len(BIG_WEIGHTS) if cast_next else 0
    n_cast_in = n_cast_out + 1 if cast_next else 0
    n_side_out = n_cast_out + (1 if mod_next else 0)
    n_side_in = n_cast_in + (3 if mod_next else 0)
    x_ref, mod_ref = refs[:2]
    w = _weight_refs(PRE_WEIGHTS + POST_WEIGHTS, refs[2:2 + n_w], layer)
    side_in = refs[2 + n_w:2 + n_w + n_side_in]
    n_in = 2 + n_w + n_side_in + (2 if aliased else 0)
    o_ref, ckv_ref, kr_ref = refs[n_in:n_in + 3]
    side_out = refs[n_in + 3:n_in + 3 + n_side_out]
    q_ref, k_ref, v_ref, mix_ref = refs[n_in + 3 + n_side_out:]
    if cast_next:
        _cast_stage(side_in[:n_cast_in], side_out[:n_cast_out])
    if mod_next:
        _mod_stage(*side_in[n_cast_in:], side_out[n_cast_out])
    x = x_ref[...]
    mod = mod_ref[0:1, :]
    _pre_stage(x, mod, w, None, seq_len, q_ref, k_ref, v_ref,
               mix_ref.at[:, WIDTH_A:WIDTH_A + WIDTH_B], mix_ref.at[:, WIDTH_A + WIDTH_B:],
               (ckv_ref, kr_ref))
    for s in range(x.shape[0] // seq_len):
        rows = slice(s * seq_len, (s + 1) * seq_len)
        for hd in range(N_HEADS):
            _attn_head(q_ref, k_ref, v_ref, None, rows, rows, hd, mix_ref)
    _post_stage(x_ref, mod, mix_ref, w, o_ref)


def _context_layer(x, mods, wts, l, new_ckv, new_kr, raw_big, mod_args, *, seq_len, seqs_per_tile):
    t = x.shape[0]
    tm = seq_len * seqs_per_tile
    n_seq_total = t // seq_len
    aliased = new_ckv is not None
    assert aliased or l == 0
    cast_next = raw_big is not None
    mod_next = mod_args is not None
    n_tiles = t // tm
    row_spec = pl.BlockSpec((tm, D_MODEL), lambda i: (i, 0))
    layer_dim, layer_idx = (None, l) if aliased else (DEPTH, 0)
    ckv_spec = pl.BlockSpec((seqs_per_tile, layer_dim, seq_len, KV_LORA),
                            lambda i: (i, layer_idx, 0, 0))
    kr_spec = pl.BlockSpec((seqs_per_tile, layer_dim, QK_ROPE, seq_len),
                           lambda i: (i, layer_idx, 0, 0))
    in_specs = [row_spec, _mod_spec()]
    args = [x, mods]
    for n in PRE_WEIGHTS + POST_WEIGHTS:
        in_specs.append(_layer_spec(wts[n].shape, l))
        args.append(wts[n])
    out_specs = [row_spec, ckv_spec, kr_spec]
    out_shape = [
        jax.ShapeDtypeStruct((t, D_MODEL), F32),
        jax.ShapeDtypeStruct((n_seq_total, DEPTH, seq_len, KV_LORA), F32),
        jax.ShapeDtypeStruct((n_seq_total, DEPTH, QK_ROPE, seq_len), F32),
    ]
    if cast_next:
        assert n_tiles == CAST_STEPS
        c_in, c_out, c_shape, c_args = _cast_specs(raw_big, l + 1)
        in_specs += c_in
        args += c_args
        out_specs += c_out
        out_shape += c_shape
    if mod_next:
        m_in, m_out, m_shape = _mod_specs(l + 1, n_tiles)
        in_specs += m_in
        args += list(mod_args)
        out_specs.append(m_out)
        out_shape.append(m_shape)
    aliases = {}
    if aliased:
        in_specs += [pl.BlockSpec(memory_space=pl.ANY)] * 2
        aliases = {len(args): 1, len(args) + 1: 2}
        args += [new_ckv, new_kr]
    return pl.pallas_call(
        functools.partial(_context_kernel, layer=l, seq_len=seq_len, aliased=aliased,
                          cast_next=cast_next, mod_next=mod_next),
        grid=(n_tiles,),
        in_specs=in_specs,
        out_specs=out_specs,
        out_shape=out_shape,
        scratch_shapes=[
            pltpu.VMEM((tm, QK_WIDTH), BF16), pltpu.VMEM((tm, QK_WIDTH), BF16),
            pltpu.VMEM((tm, WIDTH_A), BF16), pltpu.VMEM((tm, D_MODEL), BF16),
        ],
        input_output_aliases=aliases,
        compiler_params=_params(1),
        name="context_layer",
    )(*args)


def _latent_pre_kernel(*refs, layer, seq_len, mod_next):
    x_ref, mod_ref = refs[:2]
    w = _weight_refs(PRE_WEIGHTS, refs[2:2 + len(PRE_WEIGHTS)], layer)
    refs = refs[2 + len(PRE_WEIGHTS):]
    cos_ref, sin_ref = refs[:2]
    if mod_next:
        _mod_stage(*refs[2:5], refs[-1])
        refs = refs[:2] + refs[5:-1]
    q_ref, k_ref, v_ref, obc_ref = refs[2:]
    mod = mod_ref[pl.ds(1 + pl.program_id(0), 1), :]
    _pre_stage(x_ref[...], mod, w, (cos_ref[...], sin_ref[...]), seq_len,
               q_ref, k_ref, v_ref, obc_ref.at[:, 0:WIDTH_B], obc_ref.at[:, WIDTH_B:], None)


def _latent_pre(x, mods, wts, l, rope_tabs, mod_args, *, seq_len):
    t = x.shape[0]
    tm = seq_len
    row_spec = lambda w: pl.BlockSpec((tm, w), lambda i: (i, 0))
    in_specs = [row_spec(D_MODEL), _mod_spec()]
    args = [x, mods]
    for n in PRE_WEIGHTS:
        in_specs.append(_layer_spec(wts[n].shape, l))
        args.append(wts[n])
    in_specs += [pl.BlockSpec((tm, LANES), lambda i: (0, 0), pipeline_mode=pl.Buffered(1))] * 2
    args += list(rope_tabs)
    out_specs = [row_spec(QK_WIDTH), row_spec(QK_WIDTH), row_spec(WIDTH_A), row_spec(BC_WIDTH)]
    out_shape = [
        jax.ShapeDtypeStruct((t, QK_WIDTH), BF16),
        jax.ShapeDtypeStruct((t, QK_WIDTH), BF16),
        jax.ShapeDtypeStruct((t, WIDTH_A), BF16),
        jax.ShapeDtypeStruct((t, BC_WIDTH), BF16),
    ]
    if mod_args is not None:
        m_in, m_out, m_shape = _mod_specs(l + 1, t // tm)
        in_specs += m_in
        args += list(mod_args)
        out_specs.append(m_out)
        out_shape.append(m_shape)
    return pl.pallas_call(
        functools.partial(_latent_pre_kernel, layer=l, seq_len=seq_len,
                          mod_next=mod_args is not None),
        grid=(t // tm,),
        in_specs=in_specs,
        out_specs=out_specs,
        out_shape=out_shape,
        compiler_params=_params(1),
        name="latent_pre",
    )(*args)


def _latent_post_kernel(*refs, layer, tiles_per_seq):
    q_ref, k_ref, v_ref, kc_ref, vc_ref, obc_ref, x_ref, mod_ref = refs[:8]
    w = _weight_refs(POST_WEIGHTS, refs[8:8 + len(POST_WEIGHTS)], layer)
    o_ref, mix_ref = refs[8 + len(POST_WEIGHTS):]
    qrows = slice(0, q_ref.shape[0])
    krows = slice(0, k_ref.shape[0])
    for hd in range(N_HEADS):
        _attn_head(q_ref, k_ref, v_ref, (kc_ref, vc_ref), qrows, krows, hd, mix_ref)
    mix_ref[:, WIDTH_A:] = obc_ref[...]
    mod = mod_ref[pl.ds(1 + pl.program_id(0) // tiles_per_seq, 1), :]
    _post_stage(x_ref, mod, mix_ref, w, o_ref)


def _latent_post(q, k, v, obc, x, mods, cache, wts, l, *, seq_len, tq):
    t = x.shape[0]
    tiles_per_seq = seq_len // tq
    kc, vc = cache
    past = kc.shape[2]
    seq_of = lambda i: i // tiles_per_seq
    row_spec = lambda w: pl.BlockSpec((tq, w), lambda i: (i, 0))
    in_specs = [
        row_spec(QK_WIDTH),
        pl.BlockSpec((seq_len, QK_WIDTH), lambda i: (seq_of(i), 0)),
        pl.BlockSpec((seq_len, WIDTH_A), lambda i: (seq_of(i), 0)),
        pl.BlockSpec((None, None, past, QK_WIDTH), lambda i: (l, seq_of(i), 0, 0)),
        pl.BlockSpec((None, None, past, WIDTH_A), lambda i: (l, seq_of(i), 0, 0)),
        row_spec(BC_WIDTH),
        row_spec(D_MODEL),
        _mod_spec(),
    ]
    args = [q, k, v, kc, vc, obc, x, mods]
    for n in POST_WEIGHTS:
        in_specs.append(_layer_spec(wts[n].shape, l))
        args.append(wts[n])
    return pl.pallas_call(
        functools.partial(_latent_post_kernel, layer=l, tiles_per_seq=tiles_per_seq),
        grid=(t // tq,),
        in_specs=in_specs,
        out_specs=row_spec(D_MODEL),
        out_shape=jax.ShapeDtypeStruct((t, D_MODEL), F32),
        scratch_shapes=[pltpu.VMEM((tq, D_MODEL), BF16)],
        compiler_params=_params(1),
        name="latent_post",
    )(*args)


def _rope_tables(n_tokens):
    rows = n_tokens // GRID_W
    row = np.repeat(np.arange(rows, dtype=np.float64), GRID_W)
    col = np.tile(np.arange(GRID_W, dtype=np.float64), rows)
    nf = QK_ROPE // 4
    inv = ROPE_THETA ** (-np.arange(nf, dtype=np.float64) / nf)
    ang_r = row[:, None] * inv
    ang_c = col[:, None] * inv
    zeros = np.zeros((n_tokens, LANES - QK_ROPE))
    cos = np.concatenate([np.cos(ang_r), np.cos(ang_r), np.cos(ang_c), np.cos(ang_c), zeros], axis=1)
    sin = np.concatenate([-np.sin(ang_r), np.sin(ang_r), -np.sin(ang_c), np.sin(ang_c), zeros], axis=1)
    return jnp.asarray(cos, F32), jnp.asarray(sin, F32)


def _prepare_weights(w_uq, w_ukv, w_s, b_s, w_conv, gains):
    w_q = jnp.pad(w_uq, ((0, 0), (0, 0), (0, 0), (0, QK_PAD - QK_NOPE - QK_ROPE)))
    w_q = w_q.reshape(DEPTH, Q_LORA, QK_WIDTH).astype(BF16)
    w_kv = jnp.concatenate(
        [w_ukv[..., :QK_NOPE].reshape(DEPTH, KV_LORA, N_HEADS * QK_NOPE),
         w_ukv[..., QK_NOPE:].reshape(DEPTH, KV_LORA, N_HEADS * V_HEAD)], axis=-1).astype(BF16)
    wts = {
        "w_q": w_q, "w_kv": w_kv,
        "w_s": w_s.reshape(DEPTH, N_HEADS_B * CHUNK, CHUNK).astype(BF16),
        "b_s": jnp.repeat(jnp.swapaxes(b_s, 1, 2), HEAD_B, axis=-1),
        "w_conv": w_conv,
    }
    for name, g in gains.items():
        wts[name] = g
    return wts


def kernel(x_prompt, x_sample, cache_ckv, cache_krope, c, c_ctx, w_ada, b_ada, g_pre_mix, w_in, g_q, w_uq, g_kv, w_ukv, g_v, w_s, b_s, w_conv, w_out, g_post_mix, g_pre_ffn, w_ff1, w_ff2, g_post_ffn):
    batch, seq, _ = x_prompt.shape
    dec_batch, dec_seq, _ = x_sample.shape

    wts = _prepare_weights(
        w_uq, w_ukv, w_s, b_s, w_conv,
        {"g_pre_mix": g_pre_mix, "g_q": g_q, "g_kv": g_kv, "g_v": g_v,
         "g_post_mix": g_post_mix, "g_pre_ffn": g_pre_ffn, "g_post_ffn": g_post_ffn})
    raw_big = (jnp.swapaxes(w_in, 1, 2), w_out, w_ff1, w_ff2)
    big = _cast_layer(raw_big, 0)

    cond = jnp.concatenate(
        [c_ctx[None, :], c, jnp.zeros((COND_ROWS - 1 - dec_batch, D_MODEL), F32)], axis=0)
    mod_args = (cond, w_ada, b_ada.reshape(DEPTH, 1, MOD_COLS))
    mods = _modulation_layers(*mod_args, 1)[0]

    cache = _cache_kv(cache_ckv, jnp.swapaxes(cache_krope, 2, 3), wts["w_kv"])
    rope_tabs = _rope_tables(dec_seq)

    xp = x_prompt.reshape(batch * seq, D_MODEL)
    xs = x_sample.reshape(dec_batch * dec_seq, D_MODEL)
    new_ckv = new_kr = None
    for l in range(DEPTH):
        wl = dict(wts, **dict(zip(BIG_WEIGHTS, big)))
        side_mods = 1 <= l < DEPTH - 1
        xp, new_ckv, new_kr, *side = _context_layer(
            xp, mods, wl, l, new_ckv, new_kr, raw_big if l + 1 < DEPTH else None,
            mod_args if side_mods else None, seq_len=seq, seqs_per_tile=2)
        q, k, v, obc, *pre_side = _latent_pre(
            xs, mods, wl, l, rope_tabs, mod_args if l == 0 else None, seq_len=dec_seq)
        xs = _latent_post(q, k, v, obc, xs, mods, cache, wl, l, seq_len=dec_seq, tq=512)
        big = side[:len(BIG_WEIGHTS)]
        mods = side[-1] if side_mods else (pre_side[0] if pre_side else None)

    return (xp.reshape(batch, seq, D_MODEL), xs.reshape(dec_batch, dec_seq, D_MODEL),
            new_ckv, jnp.swapaxes(new_kr, 2, 3))
```

```python
import functools
import math

import jax
import jax.numpy as jnp
import numpy as np
from jax import lax
from jax.experimental import pallas as pl
from jax.experimental.pallas import tpu as pltpu

F32 = jnp.float32
BF16 = jnp.bfloat16

D_MODEL = 1024
DEPTH = 4
GRID_W = 64
N_HEADS = 4
QK_NOPE = 128
QK_ROPE = 64
V_HEAD = 128
Q_LORA = 384
KV_LORA = 256
WIDTH_A = N_HEADS * V_HEAD
ROPE_THETA = 10000.0
WIDTH_B = 256
N_HEADS_B = 4
HEAD_B = WIDTH_B // N_HEADS_B
CHUNK = 128
WIDTH_C = 256
D_FF = 4 * D_MODEL
N_MOD = 6
EPS = 1e-6

LANES = 128
QK_PAD = 2 * LANES
QK_WIDTH = N_HEADS * QK_PAD
KV_COLS = N_HEADS * (QK_NOPE + V_HEAD)
BC_WIDTH = WIDTH_B + WIDTH_C
OFF_Q = 0
OFF_CKV = OFF_Q + Q_LORA
OFF_U = OFF_CKV + KV_LORA
OFF_V = OFF_U + WIDTH_B
OFF_BG = OFF_V + WIDTH_B
OFF_CG = OFF_BG + WIDTH_C
OFF_HH = OFF_CG + WIDTH_C
OFF_KR = OFF_HH + WIDTH_C
IN_COLS_PAD = OFF_KR + LANES
COND_ROWS = 8
VMEM_LIMIT = 56 * 1024 * 1024
SM_SCALE_LOG2 = math.log2(math.e) / math.sqrt(QK_NOPE + QK_ROPE)
FF_CHUNK = 1024
OUT_SPLIT = 2

PRE_WEIGHTS = ("g_pre_mix", "w_in", "g_q", "w_q", "g_kv", "w_kv", "g_v", "w_s", "b_s", "w_conv")
POST_WEIGHTS = ("g_post_mix", "w_out", "g_pre_ffn", "w_ff1", "w_ff2", "g_post_ffn")
BIG_WEIGHTS = ("w_in", "w_out", "w_ff1", "w_ff2")
BIG_SHAPES = ((D_MODEL, IN_COLS_PAD), (D_MODEL, D_MODEL), (D_MODEL, D_FF), (D_FF, D_MODEL))
CAST_STEPS = 16


def _rms(x, g):
    return x * lax.rsqrt(jnp.mean(x * x, axis=-1, keepdims=True) + EPS) * g


def _dot(a, b):
    return jnp.dot(a, b, preferred_element_type=F32)


def _dot_nt(a, b):
    return lax.dot_general(a, b, (((1,), (1,)), ((), ())), preferred_element_type=F32)


def _params(n_axes):
    return pltpu.CompilerParams(
        dimension_semantics=("parallel",) * n_axes, vmem_limit_bytes=VMEM_LIMIT)


def _layer_spec(shape, l):
    if len(shape) == 2:
        return pl.BlockSpec(tuple(shape), lambda *_: (0, 0), pipeline_mode=pl.Buffered(1))
    return pl.BlockSpec((None,) + tuple(shape[1:]), lambda *_: (l, 0, 0),
                        pipeline_mode=pl.Buffered(1))


class _LayerRow:
    def __init__(self, ref, layer):
        self.ref, self.layer = ref, layer

    def __getitem__(self, idx):
        assert idx is Ellipsis
        return self.ref[self.layer:self.layer + 1, :]


def _weight_refs(names, refs, layer):
    return {n: _LayerRow(r, layer) if n.startswith("g_") else r for n, r in zip(names, refs)}


def _mod_spec():
    return pl.BlockSpec((COND_ROWS, N_MOD * D_MODEL), lambda *_: (0, 0),
                        pipeline_mode=pl.Buffered(1))


MOD_COLS = N_MOD * D_MODEL


def _mod_stage(cond_ref, w_ref, b_ref, o_ref):
    c = cond_ref[...]
    s = c / (1.0 + jnp.exp(-c))
    o_ref[...] = _dot(s.astype(BF16), w_ref[...].astype(BF16)) + b_ref[...]


def _mod_specs(l, n_steps):
    tn = MOD_COLS // n_steps
    in_specs = [
        pl.BlockSpec((COND_ROWS, D_MODEL), lambda i: (0, 0), pipeline_mode=pl.Buffered(1)),
        pl.BlockSpec((None, D_MODEL, tn), lambda i: (l, 0, i)),
        pl.BlockSpec((None, 1, tn), lambda i: (l, 0, i)),
    ]
    return (in_specs, pl.BlockSpec((COND_ROWS, tn), lambda i: (0, i)),
            jax.ShapeDtypeStruct((COND_ROWS, MOD_COLS), F32))


def _modulation_layers(cond, w_ada, b_ada, n_layers):
    n_chunks = 4
    tn = MOD_COLS // n_chunks
    return pl.pallas_call(
        _mod_stage,
        grid=(n_layers, n_chunks),
        in_specs=[
            pl.BlockSpec((COND_ROWS, D_MODEL), lambda l, j: (0, 0)),
            pl.BlockSpec((None, D_MODEL, tn), lambda l, j: (l, 0, j)),
            pl.BlockSpec((None, 1, tn), lambda l, j: (l, 0, j)),
        ],
        out_specs=pl.BlockSpec((None, COND_ROWS, tn), lambda l, j: (l, 0, j)),
        out_shape=jax.ShapeDtypeStruct((n_layers, COND_ROWS, MOD_COLS), F32),
        compiler_params=_params(2),
        name="modulation",
    )(cond, w_ada, b_ada)


def _cache_kv_kernel(ckv_ref, kr_ref, wkv_ref, k_ref, v_ref):
    nb, past, _ = ckv_ref.shape
    kv = _dot(ckv_ref[...].reshape(nb * past, KV_LORA).astype(BF16), wkv_ref[...])
    pad = jnp.zeros((QK_PAD - QK_NOPE - QK_ROPE, past), F32)
    for b in range(nb):
        rows = slice(b * past, (b + 1) * past)
        krz = jnp.concatenate([kr_ref[b], pad], axis=0).T.astype(BF16)
        for h in range(N_HEADS):
            lo = h * QK_PAD
            k_ref[b, :, lo:lo + QK_NOPE] = kv[rows, h * QK_NOPE:(h + 1) * QK_NOPE].astype(BF16)
            k_ref[b, :, lo + QK_NOPE:lo + QK_PAD] = krz
        v_ref[b] = kv[rows, N_HEADS * QK_NOPE:].astype(BF16)


def _cache_kv(cache_ckv, cache_krope_t, w_kv):
    nb, _, past, _ = cache_ckv.shape
    return pl.pallas_call(
        _cache_kv_kernel,
        grid=(DEPTH,),
        in_specs=[
            pl.BlockSpec((nb, None, past, KV_LORA), lambda l: (0, l, 0, 0)),
            pl.BlockSpec((nb, None, QK_ROPE, past), lambda l: (0, l, 0, 0)),
            pl.BlockSpec((None, KV_LORA, KV_COLS), lambda l: (l, 0, 0)),
        ],
        out_specs=[
            pl.BlockSpec((None, nb, past, QK_WIDTH), lambda l: (l, 0, 0, 0)),
            pl.BlockSpec((None, nb, past, WIDTH_A), lambda l: (l, 0, 0, 0)),
        ],
        out_shape=[
            jax.ShapeDtypeStruct((DEPTH, nb, past, QK_WIDTH), BF16),
            jax.ShapeDtypeStruct((DEPTH, nb, past, WIDTH_A), BF16),
        ],
        compiler_params=_params(1),
        name="cache_kv",
    )(cache_ckv, cache_krope_t, w_kv)


IN_HALF = IN_COLS_PAD // CAST_STEPS // 2


def _cast_stage(in_refs, out_refs):
    win_a, win_b, wout, w1, w2 = in_refs
    win_o, wout_o, w1_o, w2_o = out_refs
    last = pl.program_id(0) == CAST_STEPS - 1
    b = jnp.where(last, 0.0, win_b[...])
    win_o[...] = jnp.concatenate([win_a[...], b], axis=0).T.astype(BF16)
    wout_o[...] = wout[...].astype(BF16)
    w1_o[...] = w1[...].astype(BF16)
    w2_o[...] = w2[...].astype(BF16)


def _w_in_block(i, half):
    kr_block = (Q_LORA + KV_LORA) // IN_HALF
    n_front = kr_block // 2
    shifted = jnp.where(i < CAST_STEPS - 1, 2 * i + 1 + half, kr_block)
    return jnp.where(i < n_front, 2 * i + half, shifted)


def _cast_specs(raw, l):
    w_in_t = raw[0]
    in_specs = [pl.BlockSpec((None, IN_HALF, D_MODEL), lambda i, h=h: (l, _w_in_block(i, h), 0))
                for h in range(2)]
    out_specs = [pl.BlockSpec((D_MODEL, IN_COLS_PAD // CAST_STEPS), lambda i: (0, i))]
    out_shape = [jax.ShapeDtypeStruct(BIG_SHAPES[0], BF16)]
    for a, (rows, cols) in zip(raw[1:], BIG_SHAPES[1:]):
        chunk = rows // CAST_STEPS
        in_specs.append(pl.BlockSpec((None, chunk, cols), lambda i: (l, i, 0)))
        out_specs.append(pl.BlockSpec((chunk, cols), lambda i: (i, 0)))
        out_shape.append(jax.ShapeDtypeStruct((rows, cols), BF16))
    return in_specs, out_specs, out_shape, [w_in_t, w_in_t] + list(raw[1:])


def _cast_kernel(*refs):
    _cast_stage(refs[:5], refs[6:10])
    wuq_ref, wq_ref = refs[5], refs[10]
    pad = jnp.zeros((QK_PAD - wuq_ref.shape[0], Q_LORA), F32)
    wq_ref[...] = jnp.concatenate([wuq_ref[...], pad], axis=0).T.astype(BF16)


def _cast_layer(raw, w_uq):
    assert DEPTH * N_HEADS == CAST_STEPS
    in_specs, out_specs, out_shape, args = _cast_specs(raw, 0)
    head_dim = w_uq.shape[-1]
    w_uq_t = jnp.transpose(w_uq, (0, 2, 3, 1))
    in_specs.append(pl.BlockSpec((None, None, head_dim, Q_LORA),
                                 lambda i: (i // N_HEADS, i % N_HEADS, 0, 0)))
    out_specs.append(pl.BlockSpec((None, Q_LORA, QK_PAD), lambda i: (i // N_HEADS, 0, i % N_HEADS)))
    out_shape.append(jax.ShapeDtypeStruct((DEPTH, Q_LORA, QK_WIDTH), BF16))
    *big, w_q = pl.pallas_call(
        _cast_kernel,
        grid=(CAST_STEPS,),
        in_specs=in_specs,
        out_specs=out_specs,
        out_shape=out_shape,
        compiler_params=_params(1),
        name="cast_weights",
    )(*args, w_uq_t)
    return big, w_q


def _swap_halves(x):
    lane = lax.broadcasted_iota(jnp.int32, x.shape, 1)
    quarter = QK_ROPE // 4
    first_half = (lane & (2 * quarter - 1)) < quarter
    return jnp.where(first_half, pltpu.roll(x, LANES - quarter, 1), pltpu.roll(x, quarter, 1))


def _pre_stage(x, mod, w, rope, seq_len, q_ref, k_ref, v_ref, ob_ref, oc_ref, cache_refs):
    tm = x.shape[0]
    sh1 = mod[:, 0:D_MODEL]
    sc1 = mod[:, D_MODEL:2 * D_MODEL]
    h = _rms(x, w["g_pre_mix"][...] * (1.0 + sc1)) + sh1
    z = _dot(h.astype(BF16), w["w_in"][...])

    ckv = _rms(z[:, OFF_CKV:OFF_CKV + KV_LORA], w["g_kv"][...])
    krz = z[:, OFF_KR:OFF_KR + LANES]
    if cache_refs is not None:
        ckv_ref, kr_ref = cache_refs
        if len(ckv_ref.shape) == 4:
            ckv_ref[:, 1:] = jnp.zeros((ckv_ref.shape[0], DEPTH - 1) + ckv_ref.shape[2:], F32)
            kr_ref[:, 1:] = jnp.zeros((kr_ref.shape[0], DEPTH - 1) + kr_ref.shape[2:], F32)
            ckv_ref, kr_ref = ckv_ref.at[:, 0], kr_ref.at[:, 0]
        ckv_ref[...] = ckv.reshape(ckv_ref.shape)
        for s in range(tm // seq_len):
            kr_ref[s] = krz[s * seq_len:(s + 1) * seq_len, :].T[0:QK_ROPE, :]
    qn = _rms(z[:, OFF_Q:OFF_Q + Q_LORA], w["g_q"][...])
    q = _dot(qn.astype(BF16), w["w_q"][...]) * SM_SCALE_LOG2
    kv = _dot(ckv.astype(BF16), w["w_kv"][...])
    if rope is not None:
        cos, sin = rope
        krz = krz * cos + _swap_halves(krz) * sin
    krz = krz.astype(BF16)
    for hd in range(N_HEADS):
        lo = hd * QK_PAD
        q_ref[:, lo:lo + QK_NOPE] = q[:, lo:lo + QK_NOPE].astype(BF16)
        qr = q[:, lo + QK_NOPE:lo + QK_PAD]
        if rope is not None:
            qr = qr * cos + _swap_halves(qr) * sin
        q_ref[:, lo + QK_NOPE:lo + QK_PAD] = qr.astype(BF16)
        k_ref[:, lo:lo + QK_NOPE] = kv[:, hd * QK_NOPE:(hd + 1) * QK_NOPE].astype(BF16)
        k_ref[:, lo + QK_NOPE:lo + QK_PAD] = krz
    v_ref[...] = kv[:, N_HEADS * QK_NOPE:].astype(BF16)

    u = jax.nn.gelu(z[:, OFF_U:OFF_U + WIDTH_B])
    vn = _rms(jax.nn.gelu(z[:, OFF_V:OFF_V + WIDTH_B]), w["g_v"][...]).astype(BF16)
    lane = lax.broadcasted_iota(jnp.int32, (CHUNK, WIDTH_B), 1)
    ws = w["w_s"][...]
    bs = w["b_s"][...]
    for c in range(tm // CHUNK):
        rows = slice(c * CHUNK, (c + 1) * CHUNK)
        r = _dot(ws, vn[rows, :])
        mixed = r[(N_HEADS_B - 1) * CHUNK:, :]
        for hb in range(N_HEADS_B - 2, -1, -1):
            mixed = jnp.where(lane < (hb + 1) * HEAD_B, r[hb * CHUNK:(hb + 1) * CHUNK, :], mixed)
        ob_ref[rows, :] = (u[rows, :] * (mixed + bs)).astype(BF16)

    zc = z[:, OFF_CG:OFF_CG + WIDTH_C] * z[:, OFF_HH:OFF_HH + WIDTH_C]
    pos = lax.broadcasted_iota(jnp.int32, (tm, WIDTH_C), 0) & (seq_len - 1)
    z_prev = jnp.where(pos == 0, 0.0, pltpu.roll(zc, 1, 0))
    z_next = jnp.where(pos == seq_len - 1, 0.0, pltpu.roll(zc, tm - 1, 0))
    wc = w["w_conv"][...]
    y = z_prev * wc[0:1, :] + zc * wc[1:2, :] + z_next * wc[2:3, :]
    oc_ref[...] = (z[:, OFF_BG:OFF_BG + WIDTH_C] * y).astype(BF16)


def _attn_head(q_ref, k_ref, v_ref, cache, qrows, krows, hd, oa_ref):
    qk_cols = slice(hd * QK_PAD, (hd + 1) * QK_PAD)
    v_cols = slice(hd * V_HEAD, (hd + 1) * V_HEAD)
    qh = q_ref[qrows, qk_cols]
    s_lat = _dot_nt(qh, k_ref[krows, qk_cols])
    mx = jnp.max(s_lat, axis=-1, keepdims=True)
    if cache is not None:
        kc_ref, vc_ref = cache
        s_ctx = _dot_nt(qh, kc_ref[:, qk_cols])
        mx = jnp.maximum(mx, jnp.max(s_ctx, axis=-1, keepdims=True))
    p_lat = jnp.exp2(s_lat - mx)
    den = jnp.sum(p_lat, axis=-1, keepdims=True)
    o = _dot(p_lat.astype(BF16), v_ref[krows, v_cols])
    if cache is not None:
        p_ctx = jnp.exp2(s_ctx - mx)
        den = den + jnp.sum(p_ctx, axis=-1, keepdims=True)
        o = o + _dot(p_ctx.astype(BF16), vc_ref[:, v_cols])
    oa_ref[qrows, v_cols] = (o * (1.0 / den)).astype(BF16)


def _post_stage(x_ref, mod, mix_ref, w, o_ref):
    ga1 = mod[:, 2 * D_MODEL:3 * D_MODEL]
    sh2 = mod[:, 3 * D_MODEL:4 * D_MODEL]
    sc2 = mod[:, 4 * D_MODEL:5 * D_MODEL]
    ga2 = mod[:, 5 * D_MODEL:6 * D_MODEL]
    post_mix_row = ga1 * w["g_post_mix"][...]
    pre_ffn_row = w["g_pre_ffn"][...] * (1.0 + sc2)
    post_ffn_row = ga2 * w["g_post_ffn"][...]
    group = x_ref.shape[0] // OUT_SPLIT
    for r in range(OUT_SPLIT):
        rows = slice(r * group, (r + 1) * group)
        mo = _dot(mix_ref[rows, :], w["w_out"][...])
        x1 = x_ref[rows, :] + _rms(mo, post_mix_row)
        o_ref[rows, :] = x1
        mix_ref[rows, :] = (_rms(x1, pre_ffn_row) + sh2).astype(BF16)
    h2 = mix_ref[...]
    f = None
    for j in range(D_FF // FF_CHUNK):
        cols = slice(j * FF_CHUNK, (j + 1) * FF_CHUNK)
        a = jnp.square(jnp.maximum(_dot(h2, w["w_ff1"][:, cols]), 0.0)).astype(BF16)
        part = _dot(a, w["w_ff2"][cols, :])
        f = part if f is None else f + part
    o_ref[...] = o_ref[...] + _rms(f, post_ffn_row)


def _context_kernel(*refs, layer, seq_len, aliased, cast_next, mod_next):
    n_w = len(PRE_WEIGHTS) + len(POST_WEIGHTS)
    n_cast_out = len(BIG_WEIGHTS) if cast_next else 0
    n_cast_in = n_cast_out + 1 if cast_next else 0
    n_side_out = n_cast_out + (1 if mod_next else 0)
    n_side_in = n_cast_in + (3 if mod_next else 0)
    x_ref, mod_ref = refs[:2]
    w = _weight_refs(PRE_WEIGHTS + POST_WEIGHTS, refs[2:2 + n_w], layer)
    side_in = refs[2 + n_w:2 + n_w + n_side_in]
    n_in = 2 + n_w + n_side_in + (2 if aliased else 0)
    o_ref, ckv_ref, kr_ref = refs[n_in:n_in + 3]
    side_out = refs[n_in + 3:n_in + 3 + n_side_out]
    q_ref, k_ref, v_ref, mix_ref = refs[n_in + 3 + n_side_out:]
    if cast_next:
        _cast_stage(side_in[:n_cast_in], side_out[:n_cast_out])
    if mod_next:
        _mod_stage(*side_in[n_cast_in:], side_out[n_cast_out])
    x = x_ref[...]
    mod = mod_ref[0:1, :]
    _pre_stage(x, mod, w, None, seq_len, q_ref, k_ref, v_ref,
               mix_ref.at[:, WIDTH_A:WIDTH_A + WIDTH_B], mix_ref.at[:, WIDTH_A + WIDTH_B:],
               (ckv_ref, kr_ref))
    for s in range(x.shape[0] // seq_len):
        rows = slice(s * seq_len, (s + 1) * seq_len)
        for hd in range(N_HEADS):
            _attn_head(q_ref, k_ref, v_ref, None, rows, rows, hd, mix_ref)
    _post_stage(x_ref, mod, mix_ref, w, o_ref)


def _context_layer(x, mods, wts, l, new_ckv, new_kr, raw_big, mod_args, *, seq_len, seqs_per_tile):
    t = x.shape[0]
    tm = seq_len * seqs_per_tile
    n_seq_total = t // seq_len
    aliased = new_ckv is not None
    assert aliased or l == 0
    cast_next = raw_big is not None
    mod_next = mod_args is not None
    n_tiles = t // tm
    row_spec = pl.BlockSpec((tm, D_MODEL), lambda i: (i, 0))
    layer_dim, layer_idx = (None, l) if aliased else (DEPTH, 0)
    ckv_spec = pl.BlockSpec((seqs_per_tile, layer_dim, seq_len, KV_LORA),
                            lambda i: (i, layer_idx, 0, 0))
    kr_spec = pl.BlockSpec((seqs_per_tile, layer_dim, QK_ROPE, seq_len),
                           lambda i: (i, layer_idx, 0, 0))
    in_specs = [row_spec, _mod_spec()]
    args = [x, mods]
    for n in PRE_WEIGHTS + POST_WEIGHTS:
        in_specs.append(_layer_spec(wts[n].shape, l))
        args.append(wts[n])
    out_specs = [row_spec, ckv_spec, kr_spec]
    out_shape = [
        jax.ShapeDtypeStruct((t, D_MODEL), F32),
        jax.ShapeDtypeStruct((n_seq_total, DEPTH, seq_len, KV_LORA), F32),
        jax.ShapeDtypeStruct((n_seq_total, DEPTH, QK_ROPE, seq_len), F32),
    ]
    if cast_next:
        assert n_tiles == CAST_STEPS
        c_in, c_out, c_shape, c_args = _cast_specs(raw_big, l + 1)
        in_specs += c_in
        args += c_args
        out_specs += c_out
        out_shape += c_shape
    if mod_next:
        m_in, m_out, m_shape = _mod_specs(l + 1, n_tiles)
        in_specs += m_in
        args += list(mod_args)
        out_specs.append(m_out)
        out_shape.append(m_shape)
    aliases = {}
    if aliased:
        in_specs += [pl.BlockSpec(memory_space=pl.ANY)] * 2
        aliases = {len(args): 1, len(args) + 1: 2}
        args += [new_ckv, new_kr]
    return pl.pallas_call(
        functools.partial(_context_kernel, layer=l, seq_len=seq_len, aliased=aliased,
                          cast_next=cast_next, mod_next=mod_next),
        grid=(n_tiles,),
        in_specs=in_specs,
        out_specs=out_specs,
        out_shape=out_shape,
        scratch_shapes=[
            pltpu.VMEM((tm, QK_WIDTH), BF16), pltpu.VMEM((tm, QK_WIDTH), BF16),
            pltpu.VMEM((tm, WIDTH_A), BF16), pltpu.VMEM((tm, D_MODEL), BF16),
        ],
        input_output_aliases=aliases,
        compiler_params=_params(1),
        name="context_layer",
    )(*args)


def _latent_pre_kernel(*refs, layer, seq_len, mod_next):
    x_ref, mod_ref = refs[:2]
    w = _weight_refs(PRE_WEIGHTS, refs[2:2 + len(PRE_WEIGHTS)], layer)
    refs = refs[2 + len(PRE_WEIGHTS):]
    cos_ref, sin_ref = refs[:2]
    if mod_next:
        _mod_stage(*refs[2:5], refs[-1])
        refs = refs[:2] + refs[5:-1]
    q_ref, k_ref, v_ref, obc_ref = refs[2:]
    mod = mod_ref[pl.ds(1 + pl.program_id(0), 1), :]
    _pre_stage(x_ref[...], mod, w, (cos_ref[...], sin_ref[...]), seq_len,
               q_ref, k_ref, v_ref, obc_ref.at[:, 0:WIDTH_B], obc_ref.at[:, WIDTH_B:], None)


def _latent_pre(x, mods, wts, l, rope_tabs, mod_args, *, seq_len):
    t = x.shape[0]
    tm = seq_len
    row_spec = lambda w: pl.BlockSpec((tm, w), lambda i: (i, 0))
    in_specs = [row_spec(D_MODEL), _mod_spec()]
    args = [x, mods]
    for n in PRE_WEIGHTS:
        in_specs.append(_layer_spec(wts[n].shape, l))
        args.append(wts[n])
    in_specs += [pl.BlockSpec((tm, LANES), lambda i: (0, 0), pipeline_mode=pl.Buffered(1))] * 2
    args += list(rope_tabs)
    out_specs = [row_spec(QK_WIDTH), row_spec(QK_WIDTH), row_spec(WIDTH_A), row_spec(BC_WIDTH)]
    out_shape = [
        jax.ShapeDtypeStruct((t, QK_WIDTH), BF16),
        jax.ShapeDtypeStruct((t, QK_WIDTH), BF16),
        jax.ShapeDtypeStruct((t, WIDTH_A), BF16),
        jax.ShapeDtypeStruct((t, BC_WIDTH), BF16),
    ]
    if mod_args is not None:
        m_in, m_out, m_shape = _mod_specs(l + 1, t // tm)
        in_specs += m_in
        args += list(mod_args)
        out_specs.append(m_out)
        out_shape.append(m_shape)
    return pl.pallas_call(
        functools.partial(_latent_pre_kernel, layer=l, seq_len=seq_len,
                          mod_next=mod_args is not None),
        grid=(t // tm,),
        in_specs=in_specs,
        out_specs=out_specs,
        out_shape=out_shape,
        compiler_params=_params(1),
        name="latent_pre",
    )(*args)


def _latent_post_kernel(*refs, layer, tiles_per_seq):
    q_ref, k_ref, v_ref, kc_ref, vc_ref, obc_ref, x_ref, mod_ref = refs[:8]
    w = _weight_refs(POST_WEIGHTS, refs[8:8 + len(POST_WEIGHTS)], layer)
    o_ref, mix_ref = refs[8 + len(POST_WEIGHTS):]
    qrows = slice(0, q_ref.shape[0])
    krows = slice(0, k_ref.shape[0])
    for hd in range(N_HEADS):
        _attn_head(q_ref, k_ref, v_ref, (kc_ref, vc_ref), qrows, krows, hd, mix_ref)
    mix_ref[:, WIDTH_A:] = obc_ref[...]
    mod = mod_ref[pl.ds(1 + pl.program_id(0) // tiles_per_seq, 1), :]
    _post_stage(x_ref, mod, mix_ref, w, o_ref)


def _latent_post(q, k, v, obc, x, mods, cache, wts, l, *, seq_len, tq):
    t = x.shape[0]
    tiles_per_seq = seq_len // tq
    kc, vc = cache
    past = kc.shape[2]
    seq_of = lambda i: i // tiles_per_seq
    row_spec = lambda w: pl.BlockSpec((tq, w), lambda i: (i, 0))
    in_specs = [
        row_spec(QK_WIDTH),
        pl.BlockSpec((seq_len, QK_WIDTH), lambda i: (seq_of(i), 0)),
        pl.BlockSpec((seq_len, WIDTH_A), lambda i: (seq_of(i), 0)),
        pl.BlockSpec((None, None, past, QK_WIDTH), lambda i: (l, seq_of(i), 0, 0)),
        pl.BlockSpec((None, None, past, WIDTH_A), lambda i: (l, seq_of(i), 0, 0)),
        row_spec(BC_WIDTH),
        row_spec(D_MODEL),
        _mod_spec(),
    ]
    args = [q, k, v, kc, vc, obc, x, mods]
    for n in POST_WEIGHTS:
        in_specs.append(_layer_spec(wts[n].shape, l))
        args.append(wts[n])
    return pl.pallas_call(
        functools.partial(_latent_post_kernel, layer=l, tiles_per_seq=tiles_per_seq),
        grid=(t // tq,),
        in_specs=in_specs,
        out_specs=row_spec(D_MODEL),
        out_shape=jax.ShapeDtypeStruct((t, D_MODEL), F32),
        scratch_shapes=[pltpu.VMEM((tq, D_MODEL), BF16)],
        compiler_params=_params(1),
        name="latent_post",
    )(*args)


def _rope_tables(n_tokens):
    rows = n_tokens // GRID_W
    row = np.repeat(np.arange(rows, dtype=np.float64), GRID_W)
    col = np.tile(np.arange(GRID_W, dtype=np.float64), rows)
    nf = QK_ROPE // 4
    inv = ROPE_THETA ** (-np.arange(nf, dtype=np.float64) / nf)
    ang_r = row[:, None] * inv
    ang_c = col[:, None] * inv
    zeros = np.zeros((n_tokens, LANES - QK_ROPE))
    cos = np.concatenate([np.cos(ang_r), np.cos(ang_r), np.cos(ang_c), np.cos(ang_c), zeros], axis=1)
    sin = np.concatenate([-np.sin(ang_r), np.sin(ang_r), -np.sin(ang_c), np.sin(ang_c), zeros], axis=1)
    return jnp.asarray(cos, F32), jnp.asarray(sin, F32)


def _prepare_weights(w_ukv, w_s, b_s, w_conv, gains):
    w_kv = jnp.concatenate(
        [w_ukv[..., :QK_NOPE].reshape(DEPTH, KV_LORA, N_HEADS * QK_NOPE),
         w_ukv[..., QK_NOPE:].reshape(DEPTH, KV_LORA, N_HEADS * V_HEAD)], axis=-1).astype(BF16)
    wts = {
        "w_kv": w_kv,
        "w_s": w_s.reshape(DEPTH, N_HEADS_B * CHUNK, CHUNK).astype(BF16),
        "b_s": jnp.repeat(jnp.swapaxes(b_s, 1, 2), HEAD_B, axis=-1),
        "w_conv": w_conv,
    }
    for name, g in gains.items():
        wts[name] = g
    return wts


def kernel(x_prompt, x_sample, cache_ckv, cache_krope, c, c_ctx, w_ada, b_ada, g_pre_mix, w_in, g_q, w_uq, g_kv, w_ukv, g_v, w_s, b_s, w_conv, w_out, g_post_mix, g_pre_ffn, w_ff1, w_ff2, g_post_ffn):
    batch, seq, _ = x_prompt.shape
    dec_batch, dec_seq, _ = x_sample.shape

    wts = _prepare_weights(
        w_ukv, w_s, b_s, w_conv,
        {"g_pre_mix": g_pre_mix, "g_q": g_q, "g_kv": g_kv, "g_v": g_v,
         "g_post_mix": g_post_mix, "g_pre_ffn": g_pre_ffn, "g_post_ffn": g_post_ffn})
    raw_big = (jnp.swapaxes(w_in, 1, 2), w_out, w_ff1, w_ff2)
    big, wts["w_q"] = _cast_layer(raw_big, w_uq)

    cond = jnp.concatenate(
        [c_ctx[None, :], c, jnp.zeros((COND_ROWS - 1 - dec_batch, D_MODEL), F32)], axis=0)
    mod_args = (cond, w_ada, b_ada.reshape(DEPTH, 1, MOD_COLS))
    mods = _modulation_layers(*mod_args, 1)[0]

    cache = _cache_kv(cache_ckv, jnp.swapaxes(cache_krope, 2, 3), wts["w_kv"])
    rope_tabs = _rope_tables(dec_seq)

    xp = x_prompt.reshape(batch * seq, D_MODEL)
    xs = x_sample.reshape(dec_batch * dec_seq, D_MODEL)
    new_ckv = new_kr = None
    for l in range(DEPTH):
        wl = dict(wts, **dict(zip(BIG_WEIGHTS, big)))
        side_mods = 1 <= l < DEPTH - 1
        xp, new_ckv, new_kr, *side = _context_layer(
            xp, mods, wl, l, new_ckv, new_kr, raw_big if l + 1 < DEPTH else None,
            mod_args if side_mods else None, seq_len=seq, seqs_per_tile=2)
        q, k, v, obc, *pre_side = _latent_pre(
            xs, mods, wl, l, rope_tabs, mod_args if l == 0 else None, seq_len=dec_seq)
        xs = _latent_post(q, k, v, obc, xs, mods, cache, wl, l, seq_len=dec_seq, tq=512)
        big = side[:len(BIG_WEIGHTS)]
        mods = side[-1] if side_mods else (pre_side[0] if pre_side else None)

    return (xp.reshape(batch, seq, D_MODEL), xs.reshape(dec_batch, dec_seq, D_MODEL),
            new_ckv, jnp.swapaxes(new_kr, 2, 3))
```

```python
import functools
import math

import jax
import jax.numpy as jnp
import numpy as np
from jax import lax
from jax.experimental import pallas as pl
from jax.experimental.pallas import tpu as pltpu

F32 = jnp.float32
BF16 = jnp.bfloat16

D_MODEL = 1024
DEPTH = 4
GRID_W = 64
N_HEADS = 4
QK_NOPE = 128
QK_ROPE = 64
V_HEAD = 128
Q_LORA = 384
KV_LORA = 256
WIDTH_A = N_HEADS * V_HEAD
ROPE_THETA = 10000.0
WIDTH_B = 256
N_HEADS_B = 4
HEAD_B = WIDTH_B // N_HEADS_B
CHUNK = 128
WIDTH_C = 256
D_FF = 4 * D_MODEL
N_MOD = 6
EPS = 1e-6

LANES = 128
QK_PAD = 2 * LANES
QK_WIDTH = N_HEADS * QK_PAD
KV_COLS = N_HEADS * (QK_NOPE + V_HEAD)
BC_WIDTH = WIDTH_B + WIDTH_C
OFF_Q = 0
OFF_CKV = OFF_Q + Q_LORA
OFF_U = OFF_CKV + KV_LORA
OFF_V = OFF_U + WIDTH_B
OFF_BG = OFF_V + WIDTH_B
OFF_CG = OFF_BG + WIDTH_C
OFF_HH = OFF_CG + WIDTH_C
OFF_KR = OFF_HH + WIDTH_C
IN_COLS_PAD = OFF_KR + LANES
COND_ROWS = 8
VMEM_LIMIT = 56 * 1024 * 1024
SM_SCALE_LOG2 = math.log2(math.e) / math.sqrt(QK_NOPE + QK_ROPE)
FF_CHUNK = 1024
OUT_SPLIT = 2

PRE_WEIGHTS = ("g_pre_mix", "w_in", "g_q", "w_q", "g_kv", "w_kv", "g_v", "w_s", "b_s", "w_conv")
POST_WEIGHTS = ("g_post_mix", "w_out", "g_pre_ffn", "w_ff1", "w_ff2", "g_post_ffn")
BIG_WEIGHTS = ("w_in", "w_out", "w_ff1", "w_ff2")
BIG_SHAPES = ((D_MODEL, IN_COLS_PAD), (D_MODEL, D_MODEL), (D_MODEL, D_FF), (D_FF, D_MODEL))
CAST_STEPS = 16


def _rms(x, g):
    return x * lax.rsqrt(jnp.mean(x * x, axis=-1, keepdims=True) + EPS) * g


def _dot(a, b):
    return jnp.dot(a, b, preferred_element_type=F32)


def _dot_nt(a, b):
    return lax.dot_general(a, b, (((1,), (1,)), ((), ())), preferred_element_type=F32)


def _params(n_axes):
    return pltpu.CompilerParams(
        dimension_semantics=("parallel",) * n_axes, vmem_limit_bytes=VMEM_LIMIT)


def _layer_spec(shape, l):
    if len(shape) == 2:
        return pl.BlockSpec(tuple(shape), lambda *_: (0, 0), pipeline_mode=pl.Buffered(1))
    return pl.BlockSpec((None,) + tuple(shape[1:]), lambda *_: (l, 0, 0),
                        pipeline_mode=pl.Buffered(1))


class _LayerRow:
    def __init__(self, ref, layer):
        self.ref, self.layer = ref, layer

    def __getitem__(self, idx):
        assert idx is Ellipsis
        return self.ref[self.layer:self.layer + 1, :]


def _weight_refs(names, refs, layer):
    return {n: _LayerRow(r, layer) if n.startswith("g_") else r for n, r in zip(names, refs)}


def _mod_spec():
    return pl.BlockSpec((COND_ROWS, N_MOD * D_MODEL), lambda *_: (0, 0),
                        pipeline_mode=pl.Buffered(1))


MOD_COLS = N_MOD * D_MODEL


def _mod_stage(cond_ref, w_ref, b_ref, o_ref):
    c = cond_ref[...]
    s = c / (1.0 + jnp.exp(-c))
    o_ref[...] = _dot(s.astype(BF16), w_ref[...].astype(BF16)) + b_ref[...]


def _mod_specs(l, n_steps):
    tn = MOD_COLS // n_steps
    in_specs = [
        pl.BlockSpec((COND_ROWS, D_MODEL), lambda i: (0, 0), pipeline_mode=pl.Buffered(1)),
        pl.BlockSpec((None, D_MODEL, tn), lambda i: (l, 0, i)),
        pl.BlockSpec((None, 1, tn), lambda i: (l, 0, i)),
    ]
    return (in_specs, pl.BlockSpec((COND_ROWS, tn), lambda i: (0, i)),
            jax.ShapeDtypeStruct((COND_ROWS, MOD_COLS), F32))


def _modulation_layers(cond, w_ada, b_ada, n_layers):
    n_chunks = 4
    tn = MOD_COLS // n_chunks
    return pl.pallas_call(
        _mod_stage,
        grid=(n_layers, n_chunks),
        in_specs=[
            pl.BlockSpec((COND_ROWS, D_MODEL), lambda l, j: (0, 0)),
            pl.BlockSpec((None, D_MODEL, tn), lambda l, j: (l, 0, j)),
            pl.BlockSpec((None, 1, tn), lambda l, j: (l, 0, j)),
        ],
        out_specs=pl.BlockSpec((None, COND_ROWS, tn), lambda l, j: (l, 0, j)),
        out_shape=jax.ShapeDtypeStruct((n_layers, COND_ROWS, MOD_COLS), F32),
        compiler_params=_params(2),
        name="modulation",
    )(cond, w_ada, b_ada)


def _cache_kv_stage(ckv_ref, kr_ref, wkv_ref, k_ref, v_ref):
    past = ckv_ref.shape[0]
    kv = _dot(ckv_ref[...].astype(BF16), wkv_ref[...])
    pad = jnp.zeros((QK_PAD - QK_NOPE - QK_ROPE, past), F32)
    krz = jnp.concatenate([kr_ref[...], pad], axis=0).T.astype(BF16)
    for h in range(N_HEADS):
        lo = h * QK_PAD
        k_ref[:, lo:lo + QK_NOPE] = kv[:, h * QK_NOPE:(h + 1) * QK_NOPE].astype(BF16)
        k_ref[:, lo + QK_NOPE:lo + QK_PAD] = krz
    v_ref[...] = kv[:, N_HEADS * QK_NOPE:].astype(BF16)


IN_HALF = IN_COLS_PAD // CAST_STEPS // 2


def _cast_stage(in_refs, out_refs):
    win_a, win_b, wout, w1, w2 = in_refs
    win_o, wout_o, w1_o, w2_o = out_refs
    last = pl.program_id(0) == CAST_STEPS - 1
    b = jnp.where(last, 0.0, win_b[...])
    win_o[...] = jnp.concatenate([win_a[...], b], axis=0).T.astype(BF16)
    wout_o[...] = wout[...].astype(BF16)
    w1_o[...] = w1[...].astype(BF16)
    w2_o[...] = w2[...].astype(BF16)


def _w_in_block(i, half):
    kr_block = (Q_LORA + KV_LORA) // IN_HALF
    n_front = kr_block // 2
    shifted = jnp.where(i < CAST_STEPS - 1, 2 * i + 1 + half, kr_block)
    return jnp.where(i < n_front, 2 * i + half, shifted)


def _cast_specs(raw, l):
    w_in_t = raw[0]
    in_specs = [pl.BlockSpec((None, IN_HALF, D_MODEL), lambda i, h=h: (l, _w_in_block(i, h), 0))
                for h in range(2)]
    out_specs = [pl.BlockSpec((D_MODEL, IN_COLS_PAD // CAST_STEPS), lambda i: (0, i))]
    out_shape = [jax.ShapeDtypeStruct(BIG_SHAPES[0], BF16)]
    for a, (rows, cols) in zip(raw[1:], BIG_SHAPES[1:]):
        chunk = rows // CAST_STEPS
        in_specs.append(pl.BlockSpec((None, chunk, cols), lambda i: (l, i, 0)))
        out_specs.append(pl.BlockSpec((chunk, cols), lambda i: (i, 0)))
        out_shape.append(jax.ShapeDtypeStruct((rows, cols), BF16))
    return in_specs, out_specs, out_shape, [w_in_t, w_in_t] + list(raw[1:])


def _cast_kernel(*refs):
    _cast_stage(refs[:5], refs[5:])


def _cast_layer(raw, l):
    in_specs, out_specs, out_shape, args = _cast_specs(raw, l)
    return pl.pallas_call(
        _cast_kernel,
        grid=(CAST_STEPS,),
        in_specs=in_specs,
        out_specs=out_specs,
        out_shape=out_shape,
        compiler_params=_params(1),
        name="cast_weights",
    )(*args)


def _swap_halves(x):
    lane = lax.broadcasted_iota(jnp.int32, x.shape, 1)
    quarter = QK_ROPE // 4
    first_half = (lane & (2 * quarter - 1)) < quarter
    return jnp.where(first_half, pltpu.roll(x, LANES - quarter, 1), pltpu.roll(x, quarter, 1))


def _pre_stage(x, mod, w, rope, seq_len, q_ref, k_ref, v_ref, ob_ref, oc_ref, cache_refs):
    tm = x.shape[0]
    sh1 = mod[:, 0:D_MODEL]
    sc1 = mod[:, D_MODEL:2 * D_MODEL]
    h = _rms(x, w["g_pre_mix"][...] * (1.0 + sc1)) + sh1
    z = _dot(h.astype(BF16), w["w_in"][...])

    ckv = _rms(z[:, OFF_CKV:OFF_CKV + KV_LORA], w["g_kv"][...])
    krz = z[:, OFF_KR:OFF_KR + LANES]
    if cache_refs is not None:
        ckv_ref, kr_ref = cache_refs
        if len(ckv_ref.shape) == 4:
            ckv_ref[:, 1:] = jnp.zeros((ckv_ref.shape[0], DEPTH - 1) + ckv_ref.shape[2:], F32)
            kr_ref[:, 1:] = jnp.zeros((kr_ref.shape[0], DEPTH - 1) + kr_ref.shape[2:], F32)
            ckv_ref, kr_ref = ckv_ref.at[:, 0], kr_ref.at[:, 0]
        ckv_ref[...] = ckv.reshape(ckv_ref.shape)
        for s in range(tm // seq_len):
            kr_ref[s] = krz[s * seq_len:(s + 1) * seq_len, :].T[0:QK_ROPE, :]
    qn = _rms(z[:, OFF_Q:OFF_Q + Q_LORA], w["g_q"][...])
    q = _dot(qn.astype(BF16), w["w_q"][...]) * SM_SCALE_LOG2
    kv = _dot(ckv.astype(BF16), w["w_kv"][...])
    if rope is not None:
        cos, sin = rope
        krz = krz * cos + _swap_halves(krz) * sin
    krz = krz.astype(BF16)
    for hd in range(N_HEADS):
        lo = hd * QK_PAD
        q_ref[:, lo:lo + QK_NOPE] = q[:, lo:lo + QK_NOPE].astype(BF16)
        qr = q[:, lo + QK_NOPE:lo + QK_PAD]
        if rope is not None:
            qr = qr * cos + _swap_halves(qr) * sin
        q_ref[:, lo + QK_NOPE:lo + QK_PAD] = qr.astype(BF16)
        k_ref[:, lo:lo + QK_NOPE] = kv[:, hd * QK_NOPE:(hd + 1) * QK_NOPE].astype(BF16)
        k_ref[:, lo + QK_NOPE:lo + QK_PAD] = krz
    v_ref[...] = kv[:, N_HEADS * QK_NOPE:].astype(BF16)

    u = jax.nn.gelu(z[:, OFF_U:OFF_U + WIDTH_B])
    vn = _rms(jax.nn.gelu(z[:, OFF_V:OFF_V + WIDTH_B]), w["g_v"][...]).astype(BF16)
    lane = lax.broadcasted_iota(jnp.int32, (CHUNK, WIDTH_B), 1)
    ws = w["w_s"][...]
    bs = w["b_s"][...]
    for c in range(tm // CHUNK):
        rows = slice(c * CHUNK, (c + 1) * CHUNK)
        r = _dot(ws, vn[rows, :])
        mixed = r[(N_HEADS_B - 1) * CHUNK:, :]
        for hb in range(N_HEADS_B - 2, -1, -1):
            mixed = jnp.where(lane < (hb + 1) * HEAD_B, r[hb * CHUNK:(hb + 1) * CHUNK, :], mixed)
        ob_ref[rows, :] = (u[rows, :] * (mixed + bs)).astype(BF16)

    zc = z[:, OFF_CG:OFF_CG + WIDTH_C] * z[:, OFF_HH:OFF_HH + WIDTH_C]
    pos = lax.broadcasted_iota(jnp.int32, (tm, WIDTH_C), 0) & (seq_len - 1)
    z_prev = jnp.where(pos == 0, 0.0, pltpu.roll(zc, 1, 0))
    z_next = jnp.where(pos == seq_len - 1, 0.0, pltpu.roll(zc, tm - 1, 0))
    wc = w["w_conv"][...]
    y = z_prev * wc[0:1, :] + zc * wc[1:2, :] + z_next * wc[2:3, :]
    oc_ref[...] = (z[:, OFF_BG:OFF_BG + WIDTH_C] * y).astype(BF16)


def _attn_head(q_ref, k_ref, v_ref, cache, qrows, krows, hd, oa_ref):
    qk_cols = slice(hd * QK_PAD, (hd + 1) * QK_PAD)
    v_cols = slice(hd * V_HEAD, (hd + 1) * V_HEAD)
    qh = q_ref[qrows, qk_cols]
    s_lat = _dot_nt(qh, k_ref[krows, qk_cols])
    mx = jnp.max(s_lat, axis=-1, keepdims=True)
    if cache is not None:
        kc_ref, vc_ref = cache
        s_ctx = _dot_nt(qh, kc_ref[:, qk_cols])
        mx = jnp.maximum(mx, jnp.max(s_ctx, axis=-1, keepdims=True))
    p_lat = jnp.exp2(s_lat - mx)
    den = jnp.sum(p_lat, axis=-1, keepdims=True)
    o = _dot(p_lat.astype(BF16), v_ref[krows, v_cols])
    if cache is not None:
        p_ctx = jnp.exp2(s_ctx - mx)
        den = den + jnp.sum(p_ctx, axis=-1, keepdims=True)
        o = o + _dot(p_ctx.astype(BF16), vc_ref[:, v_cols])
    oa_ref[qrows, v_cols] = (o * (1.0 / den)).astype(BF16)


def _post_stage(x_ref, mod, mix_ref, w, o_ref):
    ga1 = mod[:, 2 * D_MODEL:3 * D_MODEL]
    sh2 = mod[:, 3 * D_MODEL:4 * D_MODEL]
    sc2 = mod[:, 4 * D_MODEL:5 * D_MODEL]
    ga2 = mod[:, 5 * D_MODEL:6 * D_MODEL]
    post_mix_row = ga1 * w["g_post_mix"][...]
    pre_ffn_row = w["g_pre_ffn"][...] * (1.0 + sc2)
    post_ffn_row = ga2 * w["g_post_ffn"][...]
    group = x_ref.shape[0] // OUT_SPLIT
    for r in range(OUT_SPLIT):
        rows = slice(r * group, (r + 1) * group)
        mo = _dot(mix_ref[rows, :], w["w_out"][...])
        x1 = x_ref[rows, :] + _rms(mo, post_mix_row)
        o_ref[rows, :] = x1
        mix_ref[rows, :] = (_rms(x1, pre_ffn_row) + sh2).astype(BF16)
    h2 = mix_ref[...]
    f = None
    for j in range(D_FF // FF_CHUNK):
        cols = slice(j * FF_CHUNK, (j + 1) * FF_CHUNK)
        a = jnp.square(jnp.maximum(_dot(h2, w["w_ff1"][:, cols]), 0.0)).astype(BF16)
        part = _dot(a, w["w_ff2"][cols, :])
        f = part if f is None else f + part
    o_ref[...] = o_ref[...] + _rms(f, post_ffn_row)


def _context_kernel(*refs, layer, seq_len, aliased, cast_next, mod_next):
    n_w = len(PRE_WEIGHTS) + len(POST_WEIGHTS)
    n_cast_out = len(BIG_WEIGHTS) if cast_next else 0
    n_cast_in = n_cast_out + 1 if cast_next else 0
    n_side_out = n_cast_out + (1 if mod_next else 0)
    n_side_in = n_cast_in + (3 if mod_next else 0)
    x_ref, mod_ref = refs[:2]
    w = _weight_refs(PRE_WEIGHTS + POST_WEIGHTS, refs[2:2 + n_w], layer)
    side_in = refs[2 + n_w:2 + n_w + n_side_in]
    n_in = 2 + n_w + n_side_in + (2 if aliased else 0)
    o_ref, ckv_ref, kr_ref = refs[n_in:n_in + 3]
    side_out = refs[n_in + 3:n_in + 3 + n_side_out]
    q_ref, k_ref, v_ref, mix_ref = refs[n_in + 3 + n_side_out:]
    if cast_next:
        _cast_stage(side_in[:n_cast_in], side_out[:n_cast_out])
    if mod_next:
        _mod_stage(*side_in[n_cast_in:], side_out[n_cast_out])
    x = x_ref[...]
    mod = mod_ref[0:1, :]
    _pre_stage(x, mod, w, None, seq_len, q_ref, k_ref, v_ref,
               mix_ref.at[:, WIDTH_A:WIDTH_A + WIDTH_B], mix_ref.at[:, WIDTH_A + WIDTH_B:],
               (ckv_ref, kr_ref))
    for s in range(x.shape[0] // seq_len):
        rows = slice(s * seq_len, (s + 1) * seq_len)
        for hd in range(N_HEADS):
            _attn_head(q_ref, k_ref, v_ref, None, rows, rows, hd, mix_ref)
    _post_stage(x_ref, mod, mix_ref, w, o_ref)


def _context_layer(x, mods, wts, l, new_ckv, new_kr, raw_big, mod_args, *, seq_len, seqs_per_tile):
    t = x.shape[0]
    tm = seq_len * seqs_per_tile
    n_seq_total = t // seq_len
    aliased = new_ckv is not None
    assert aliased or l == 0
    cast_next = raw_big is not None
    mod_next = mod_args is not None
    n_tiles = t // tm
    row_spec = pl.BlockSpec((tm, D_MODEL), lambda i: (i, 0))
    layer_dim, layer_idx = (None, l) if aliased else (DEPTH, 0)
    ckv_spec = pl.BlockSpec((seqs_per_tile, layer_dim, seq_len, KV_LORA),
                            lambda i: (i, layer_idx, 0, 0))
    kr_spec = pl.BlockSpec((seqs_per_tile, layer_dim, QK_ROPE, seq_len),
                           lambda i: (i, layer_idx, 0, 0))
    in_specs = [row_spec, _mod_spec()]
    args = [x, mods]
    for n in PRE_WEIGHTS + POST_WEIGHTS:
        in_specs.append(_layer_spec(wts[n].shape, l))
        args.append(wts[n])
    out_specs = [row_spec, ckv_spec, kr_spec]
    out_shape = [
        jax.ShapeDtypeStruct((t, D_MODEL), F32),
        jax.ShapeDtypeStruct((n_seq_total, DEPTH, seq_len, KV_LORA), F32),
        jax.ShapeDtypeStruct((n_seq_total, DEPTH, QK_ROPE, seq_len), F32),
    ]
    if cast_next:
        assert n_tiles == CAST_STEPS
        c_in, c_out, c_shape, c_args = _cast_specs(raw_big, l + 1)
        in_specs += c_in
        args += c_args
        out_specs += c_out
        out_shape += c_shape
    if mod_next:
        m_in, m_out, m_shape = _mod_specs(l + 1, n_tiles)
        in_specs += m_in
        args += list(mod_args)
        out_specs.append(m_out)
        out_shape.append(m_shape)
    aliases = {}
    if aliased:
        in_specs += [pl.BlockSpec(memory_space=pl.ANY)] * 2
        aliases = {len(args): 1, len(args) + 1: 2}
        args += [new_ckv, new_kr]
    return pl.pallas_call(
        functools.partial(_context_kernel, layer=l, seq_len=seq_len, aliased=aliased,
                          cast_next=cast_next, mod_next=mod_next),
        grid=(n_tiles,),
        in_specs=in_specs,
        out_specs=out_specs,
        out_shape=out_shape,
        scratch_shapes=[
            pltpu.VMEM((tm, QK_WIDTH), BF16), pltpu.VMEM((tm, QK_WIDTH), BF16),
            pltpu.VMEM((tm, WIDTH_A), BF16), pltpu.VMEM((tm, D_MODEL), BF16),
        ],
        input_output_aliases=aliases,
        compiler_params=_params(1),
        name="context_layer",
    )(*args)


def _latent_pre_kernel(*refs, layer, seq_len, mod_next):
    x_ref, mod_ref = refs[:2]
    w = _weight_refs(PRE_WEIGHTS, refs[2:2 + len(PRE_WEIGHTS)], layer)
    refs = refs[2 + len(PRE_WEIGHTS):]
    cos_ref, sin_ref, cache_ckv_ref, cache_kr_ref = refs[:4]
    if mod_next:
        _mod_stage(*refs[4:7], refs[-1])
        refs = refs[:4] + refs[7:-1]
    q_ref, k_ref, v_ref, obc_ref, kc_ref, vc_ref = refs[4:]
    _cache_kv_stage(cache_ckv_ref, cache_kr_ref, w["w_kv"], kc_ref, vc_ref)
    mod = mod_ref[pl.ds(1 + pl.program_id(0), 1), :]
    _pre_stage(x_ref[...], mod, w, (cos_ref[...], sin_ref[...]), seq_len,
               q_ref, k_ref, v_ref, obc_ref.at[:, 0:WIDTH_B], obc_ref.at[:, WIDTH_B:], None)


def _latent_pre(x, mods, wts, l, rope_tabs, cache_ckv, cache_krope_t, mod_args, *, seq_len):
    t = x.shape[0]
    tm = seq_len
    nb, _, past, _ = cache_ckv.shape
    assert nb == t // tm
    row_spec = lambda w: pl.BlockSpec((tm, w), lambda i: (i, 0))
    in_specs = [row_spec(D_MODEL), _mod_spec()]
    args = [x, mods]
    for n in PRE_WEIGHTS:
        in_specs.append(_layer_spec(wts[n].shape, l))
        args.append(wts[n])
    in_specs += [pl.BlockSpec((tm, LANES), lambda i: (0, 0), pipeline_mode=pl.Buffered(1))] * 2
    args += list(rope_tabs)
    in_specs += [pl.BlockSpec((None, None, past, KV_LORA), lambda i: (i, l, 0, 0)),
                 pl.BlockSpec((None, None, QK_ROPE, past), lambda i: (i, l, 0, 0))]
    args += [cache_ckv, cache_krope_t]
    out_specs = [row_spec(QK_WIDTH), row_spec(QK_WIDTH), row_spec(WIDTH_A), row_spec(BC_WIDTH),
                 pl.BlockSpec((None, past, QK_WIDTH), lambda i: (i, 0, 0)),
                 pl.BlockSpec((None, past, WIDTH_A), lambda i: (i, 0, 0))]
    out_shape = [
        jax.ShapeDtypeStruct((t, QK_WIDTH), BF16),
        jax.ShapeDtypeStruct((t, QK_WIDTH), BF16),
        jax.ShapeDtypeStruct((t, WIDTH_A), BF16),
        jax.ShapeDtypeStruct((t, BC_WIDTH), BF16),
        jax.ShapeDtypeStruct((nb, past, QK_WIDTH), BF16),
        jax.ShapeDtypeStruct((nb, past, WIDTH_A), BF16),
    ]
    if mod_args is not None:
        m_in, m_out, m_shape = _mod_specs(l + 1, t // tm)
        in_specs += m_in
        args += list(mod_args)
        out_specs.append(m_out)
        out_shape.append(m_shape)
    return pl.pallas_call(
        functools.partial(_latent_pre_kernel, layer=l, seq_len=seq_len,
                          mod_next=mod_args is not None),
        grid=(t // tm,),
        in_specs=in_specs,
        out_specs=out_specs,
        out_shape=out_shape,
        compiler_params=_params(1),
        name="latent_pre",
    )(*args)


def _latent_post_kernel(*refs, layer, tiles_per_seq):
    q_ref, k_ref, v_ref, kc_ref, vc_ref, obc_ref, x_ref, mod_ref = refs[:8]
    w = _weight_refs(POST_WEIGHTS, refs[8:8 + len(POST_WEIGHTS)], layer)
    o_ref, mix_ref = refs[8 + len(POST_WEIGHTS):]
    qrows = slice(0, q_ref.shape[0])
    krows = slice(0, k_ref.shape[0])
    for hd in range(N_HEADS):
        _attn_head(q_ref, k_ref, v_ref, (kc_ref, vc_ref), qrows, krows, hd, mix_ref)
    mix_ref[:, WIDTH_A:] = obc_ref[...]
    mod = mod_ref[pl.ds(1 + pl.program_id(0) // tiles_per_seq, 1), :]
    _post_stage(x_ref, mod, mix_ref, w, o_ref)


def _latent_post(q, k, v, obc, x, mods, cache, wts, l, *, seq_len, tq):
    t = x.shape[0]
    tiles_per_seq = seq_len // tq
    kc, vc = cache
    past = kc.shape[1]
    seq_of = lambda i: i // tiles_per_seq
    row_spec = lambda w: pl.BlockSpec((tq, w), lambda i: (i, 0))
    in_specs = [
        row_spec(QK_WIDTH),
        pl.BlockSpec((seq_len, QK_WIDTH), lambda i: (seq_of(i), 0)),
        pl.BlockSpec((seq_len, WIDTH_A), lambda i: (seq_of(i), 0)),
        pl.BlockSpec((None, past, QK_WIDTH), lambda i: (seq_of(i), 0, 0)),
        pl.BlockSpec((None, past, WIDTH_A), lambda i: (seq_of(i), 0, 0)),
        row_spec(BC_WIDTH),
        row_spec(D_MODEL),
        _mod_spec(),
    ]
    args = [q, k, v, kc, vc, obc, x, mods]
    for n in POST_WEIGHTS:
        in_specs.append(_layer_spec(wts[n].shape, l))
        args.append(wts[n])
    return pl.pallas_call(
        functools.partial(_latent_post_kernel, layer=l, tiles_per_seq=tiles_per_seq),
        grid=(t // tq,),
        in_specs=in_specs,
        out_specs=row_spec(D_MODEL),
        out_shape=jax.ShapeDtypeStruct((t, D_MODEL), F32),
        scratch_shapes=[pltpu.VMEM((tq, D_MODEL), BF16)],
        compiler_params=_params(1),
        name="latent_post",
    )(*args)


def _rope_tables(n_tokens):
    rows = n_tokens // GRID_W
    row = np.repeat(np.arange(rows, dtype=np.float64), GRID_W)
    col = np.tile(np.arange(GRID_W, dtype=np.float64), rows)
    nf = QK_ROPE // 4
    inv = ROPE_THETA ** (-np.arange(nf, dtype=np.float64) / nf)
    ang_r = row[:, None] * inv
    ang_c = col[:, None] * inv
    zeros = np.zeros((n_tokens, LANES - QK_ROPE))
    cos = np.concatenate([np.cos(ang_r), np.cos(ang_r), np.cos(ang_c), np.cos(ang_c), zeros], axis=1)
    sin = np.concatenate([-np.sin(ang_r), np.sin(ang_r), -np.sin(ang_c), np.sin(ang_c), zeros], axis=1)
    return jnp.asarray(cos, F32), jnp.asarray(sin, F32)


def _prepare_weights(w_uq, w_ukv, w_s, b_s, w_conv, gains):
    w_q = jnp.pad(w_uq, ((0, 0), (0, 0), (0, 0), (0, QK_PAD - QK_NOPE - QK_ROPE)))
    w_q = w_q.reshape(DEPTH, Q_LORA, QK_WIDTH).astype(BF16)
    w_kv = jnp.concatenate(
        [w_ukv[..., :QK_NOPE].reshape(DEPTH, KV_LORA, N_HEADS * QK_NOPE),
         w_ukv[..., QK_NOPE:].reshape(DEPTH, KV_LORA, N_HEADS * V_HEAD)], axis=-1).astype(BF16)
    wts = {
        "w_q": w_q, "w_kv": w_kv,
        "w_s": w_s.reshape(DEPTH, N_HEADS_B * CHUNK, CHUNK).astype(BF16),
        "b_s": jnp.repeat(jnp.swapaxes(b_s, 1, 2), HEAD_B, axis=-1),
        "w_conv": w_conv,
    }
    for name, g in gains.items():
        wts[name] = g
    return wts


def kernel(x_prompt, x_sample, cache_ckv, cache_krope, c, c_ctx, w_ada, b_ada, g_pre_mix, w_in, g_q, w_uq, g_kv, w_ukv, g_v, w_s, b_s, w_conv, w_out, g_post_mix, g_pre_ffn, w_ff1, w_ff2, g_post_ffn):
    batch, seq, _ = x_prompt.shape
    dec_batch, dec_seq, _ = x_sample.shape

    wts = _prepare_weights(
        w_uq, w_ukv, w_s, b_s, w_conv,
        {"g_pre_mix": g_pre_mix, "g_q": g_q, "g_kv": g_kv, "g_v": g_v,
         "g_post_mix": g_post_mix, "g_pre_ffn": g_pre_ffn, "g_post_ffn": g_post_ffn})
    raw_big = (jnp.swapaxes(w_in, 1, 2), w_out, w_ff1, w_ff2)
    big = _cast_layer(raw_big, 0)

    cond = jnp.concatenate(
        [c_ctx[None, :], c, jnp.zeros((COND_ROWS - 1 - dec_batch, D_MODEL), F32)], axis=0)
    mod_args = (cond, w_ada, b_ada.reshape(DEPTH, 1, MOD_COLS))
    mods = _modulation_layers(*mod_args, 1)[0]

    cache_krope_t = jnp.swapaxes(cache_krope, 2, 3)
    rope_tabs = _rope_tables(dec_seq)

    xp = x_prompt.reshape(batch * seq, D_MODEL)
    xs = x_sample.reshape(dec_batch * dec_seq, D_MODEL)
    new_ckv = new_kr = None
    for l in range(DEPTH):
        wl = dict(wts, **dict(zip(BIG_WEIGHTS, big)))
        side_mods = 1 <= l < DEPTH - 1
        xp, new_ckv, new_kr, *side = _context_layer(
            xp, mods, wl, l, new_ckv, new_kr, raw_big if l + 1 < DEPTH else None,
            mod_args if side_mods else None, seq_len=seq, seqs_per_tile=2)
        q, k, v, obc, kc, vc, *pre_side = _latent_pre(
            xs, mods, wl, l, rope_tabs, cache_ckv, cache_krope_t, mod_args if l == 0 else None,
            seq_len=dec_seq)
        xs = _latent_post(q, k, v, obc, xs, mods, (kc, vc), wl, l, seq_len=dec_seq, tq=512)
        big = side[:len(BIG_WEIGHTS)]
        mods = side[-1] if side_mods else (pre_side[0] if pre_side else None)

    return (xp.reshape(batch, seq, D_MODEL), xs.reshape(dec_batch, dec_seq, D_MODEL),
            new_ckv, jnp.swapaxes(new_kr, 2, 3))
```

```python
import functools
import math

import jax
import jax.numpy as jnp
import numpy as np
from jax import lax
from jax.experimental import pallas as pl
from jax.experimental.pallas import tpu as pltpu

F32 = jnp.float32
BF16 = jnp.bfloat16

D_MODEL = 1024
DEPTH = 4
GRID_W = 64
N_HEADS = 4
QK_NOPE = 128
QK_ROPE = 64
V_HEAD = 128
Q_LORA = 384
KV_LORA = 256
WIDTH_A = N_HEADS * V_HEAD
ROPE_THETA = 10000.0
WIDTH_B = 256
N_HEADS_B = 4
HEAD_B = WIDTH_B // N_HEADS_B
CHUNK = 128
WIDTH_C = 256
D_FF = 4 * D_MODEL
N_MOD = 6
EPS = 1e-6

LANES = 128
QK_PAD = 2 * LANES
QK_WIDTH = N_HEADS * QK_PAD
KV_COLS = N_HEADS * (QK_NOPE + V_HEAD)
BC_WIDTH = WIDTH_B + WIDTH_C
OFF_Q = 0
OFF_CKV = OFF_Q + Q_LORA
OFF_U = OFF_CKV + KV_LORA
OFF_V = OFF_U + WIDTH_B
OFF_BG = OFF_V + WIDTH_B
OFF_CG = OFF_BG + WIDTH_C
OFF_HH = OFF_CG + WIDTH_C
OFF_KR = OFF_HH + WIDTH_C
IN_COLS_PAD = OFF_KR + LANES
COND_ROWS = 8
VMEM_LIMIT = 56 * 1024 * 1024
SM_SCALE_LOG2 = math.log2(math.e) / math.sqrt(QK_NOPE + QK_ROPE)
FF_CHUNK = 1024
OUT_SPLIT = 2

PRE_WEIGHTS = ("g_pre_mix", "w_in", "g_q", "w_q", "g_kv", "w_kv", "g_v", "w_s", "b_s", "w_conv")
POST_WEIGHTS = ("g_post_mix", "w_out", "g_pre_ffn", "w_ff1", "w_ff2", "g_post_ffn")
BIG_WEIGHTS = ("w_in", "w_out", "w_ff1", "w_ff2")
BIG_SHAPES = ((D_MODEL, IN_COLS_PAD), (D_MODEL, D_MODEL), (D_MODEL, D_FF), (D_FF, D_MODEL))
CAST_STEPS = 16


def _rms(x, g):
    return x * lax.rsqrt(jnp.mean(x * x, axis=-1, keepdims=True) + EPS) * g


def _dot(a, b):
    return jnp.dot(a, b, preferred_element_type=F32)


def _dot_nt(a, b):
    return lax.dot_general(a, b, (((1,), (1,)), ((), ())), preferred_element_type=F32)


def _params(n_axes):
    return pltpu.CompilerParams(
        dimension_semantics=("parallel",) * n_axes, vmem_limit_bytes=VMEM_LIMIT)


def _layer_spec(shape, l):
    if len(shape) == 2:
        return pl.BlockSpec(tuple(shape), lambda *_: (0, 0), pipeline_mode=pl.Buffered(1))
    return pl.BlockSpec((None,) + tuple(shape[1:]), lambda *_: (l, 0, 0),
                        pipeline_mode=pl.Buffered(1))


class _LayerRow:
    def __init__(self, ref, layer):
        self.ref, self.layer = ref, layer

    def __getitem__(self, idx):
        assert idx is Ellipsis
        return self.ref[self.layer:self.layer + 1, :]


def _weight_refs(names, refs, layer):
    return {n: _LayerRow(r, layer) if n.startswith("g_") else r for n, r in zip(names, refs)}


def _mod_spec():
    return pl.BlockSpec((COND_ROWS, N_MOD * D_MODEL), lambda *_: (0, 0),
                        pipeline_mode=pl.Buffered(1))


MOD_COLS = N_MOD * D_MODEL


def _mod_stage(cond_ref, w_ref, b_ref, o_ref):
    c = cond_ref[...]
    s = c / (1.0 + jnp.exp(-c))
    o_ref[...] = _dot(s.astype(BF16), w_ref[...].astype(BF16)) + b_ref[...]


def _mod_specs(l, n_steps):
    tn = MOD_COLS // n_steps
    in_specs = [
        pl.BlockSpec((COND_ROWS, D_MODEL), lambda i: (0, 0), pipeline_mode=pl.Buffered(1)),
        pl.BlockSpec((None, D_MODEL, tn), lambda i: (l, 0, i)),
        pl.BlockSpec((None, 1, tn), lambda i: (l, 0, i)),
    ]
    return (in_specs, pl.BlockSpec((COND_ROWS, tn), lambda i: (0, i)),
            jax.ShapeDtypeStruct((COND_ROWS, MOD_COLS), F32))


def _cache_kv_stage(ckv_ref, kr_ref, wkv_ref, k_ref, v_ref):
    past = ckv_ref.shape[0]
    kv = _dot(ckv_ref[...].astype(BF16), wkv_ref[...])
    pad = jnp.zeros((QK_PAD - QK_NOPE - QK_ROPE, past), F32)
    krz = jnp.concatenate([kr_ref[...], pad], axis=0).T.astype(BF16)
    for h in range(N_HEADS):
        lo = h * QK_PAD
        k_ref[:, lo:lo + QK_NOPE] = kv[:, h * QK_NOPE:(h + 1) * QK_NOPE].astype(BF16)
        k_ref[:, lo + QK_NOPE:lo + QK_PAD] = krz
    v_ref[...] = kv[:, N_HEADS * QK_NOPE:].astype(BF16)


IN_HALF = IN_COLS_PAD // CAST_STEPS // 2


def _cast_stage(in_refs, out_refs):
    win_a, win_b, wout, w1, w2 = in_refs
    win_o, wout_o, w1_o, w2_o = out_refs
    last = pl.program_id(0) == CAST_STEPS - 1
    b = jnp.where(last, 0.0, win_b[...])
    win_o[...] = jnp.concatenate([win_a[...], b], axis=0).T.astype(BF16)
    wout_o[...] = wout[...].astype(BF16)
    w1_o[...] = w1[...].astype(BF16)
    w2_o[...] = w2[...].astype(BF16)


def _w_in_block(i, half):
    kr_block = (Q_LORA + KV_LORA) // IN_HALF
    n_front = kr_block // 2
    shifted = jnp.where(i < CAST_STEPS - 1, 2 * i + 1 + half, kr_block)
    return jnp.where(i < n_front, 2 * i + half, shifted)


def _cast_specs(raw, l):
    w_in_t = raw[0]
    in_specs = [pl.BlockSpec((None, IN_HALF, D_MODEL), lambda i, h=h: (l, _w_in_block(i, h), 0))
                for h in range(2)]
    out_specs = [pl.BlockSpec((D_MODEL, IN_COLS_PAD // CAST_STEPS), lambda i: (0, i))]
    out_shape = [jax.ShapeDtypeStruct(BIG_SHAPES[0], BF16)]
    for a, (rows, cols) in zip(raw[1:], BIG_SHAPES[1:]):
        chunk = rows // CAST_STEPS
        in_specs.append(pl.BlockSpec((None, chunk, cols), lambda i: (l, i, 0)))
        out_specs.append(pl.BlockSpec((chunk, cols), lambda i: (i, 0)))
        out_shape.append(jax.ShapeDtypeStruct((rows, cols), BF16))
    return in_specs, out_specs, out_shape, [w_in_t, w_in_t] + list(raw[1:])


def _first_layer_kernel(*refs):
    _cast_stage(refs[:5], refs[8:12])
    _mod_stage(*refs[5:8], refs[12])


def _first_layer_setup(raw, mod_args):
    in_specs, out_specs, out_shape, args = _cast_specs(raw, 0)
    m_in, m_out, m_shape = _mod_specs(0, CAST_STEPS)
    return pl.pallas_call(
        _first_layer_kernel,
        grid=(CAST_STEPS,),
        in_specs=in_specs + m_in,
        out_specs=out_specs + [m_out],
        out_shape=out_shape + [m_shape],
        compiler_params=_params(1),
        name="first_layer_setup",
    )(*args, *mod_args)


def _swap_halves(x):
    lane = lax.broadcasted_iota(jnp.int32, x.shape, 1)
    quarter = QK_ROPE // 4
    first_half = (lane & (2 * quarter - 1)) < quarter
    return jnp.where(first_half, pltpu.roll(x, LANES - quarter, 1), pltpu.roll(x, quarter, 1))


def _pre_stage(x, mod, w, rope, seq_len, q_ref, k_ref, v_ref, ob_ref, oc_ref, cache_refs):
    tm = x.shape[0]
    sh1 = mod[:, 0:D_MODEL]
    sc1 = mod[:, D_MODEL:2 * D_MODEL]
    h = _rms(x, w["g_pre_mix"][...] * (1.0 + sc1)) + sh1
    z = _dot(h.astype(BF16), w["w_in"][...])

    ckv = _rms(z[:, OFF_CKV:OFF_CKV + KV_LORA], w["g_kv"][...])
    krz = z[:, OFF_KR:OFF_KR + LANES]
    if cache_refs is not None:
        ckv_ref, kr_ref = cache_refs
        if len(ckv_ref.shape) == 4:
            ckv_ref[:, 1:] = jnp.zeros((ckv_ref.shape[0], DEPTH - 1) + ckv_ref.shape[2:], F32)
            kr_ref[:, 1:] = jnp.zeros((kr_ref.shape[0], DEPTH - 1) + kr_ref.shape[2:], F32)
            ckv_ref, kr_ref = ckv_ref.at[:, 0], kr_ref.at[:, 0]
        ckv_ref[...] = ckv.reshape(ckv_ref.shape)
        for s in range(tm // seq_len):
            kr_ref[s] = krz[s * seq_len:(s + 1) * seq_len, :].T[0:QK_ROPE, :]
    qn = _rms(z[:, OFF_Q:OFF_Q + Q_LORA], w["g_q"][...])
    q = _dot(qn.astype(BF16), w["w_q"][...]) * SM_SCALE_LOG2
    kv = _dot(ckv.astype(BF16), w["w_kv"][...])
    if rope is not None:
        cos, sin = rope
        krz = krz * cos + _swap_halves(krz) * sin
    krz = krz.astype(BF16)
    for hd in range(N_HEADS):
        lo = hd * QK_PAD
        q_ref[:, lo:lo + QK_NOPE] = q[:, lo:lo + QK_NOPE].astype(BF16)
        qr = q[:, lo + QK_NOPE:lo + QK_PAD]
        if rope is not None:
            qr = qr * cos + _swap_halves(qr) * sin
        q_ref[:, lo + QK_NOPE:lo + QK_PAD] = qr.astype(BF16)
        k_ref[:, lo:lo + QK_NOPE] = kv[:, hd * QK_NOPE:(hd + 1) * QK_NOPE].astype(BF16)
        k_ref[:, lo + QK_NOPE:lo + QK_PAD] = krz
    v_ref[...] = kv[:, N_HEADS * QK_NOPE:].astype(BF16)

    u = jax.nn.gelu(z[:, OFF_U:OFF_U + WIDTH_B])
    vn = _rms(jax.nn.gelu(z[:, OFF_V:OFF_V + WIDTH_B]), w["g_v"][...]).astype(BF16)
    lane = lax.broadcasted_iota(jnp.int32, (CHUNK, WIDTH_B), 1)
    ws = w["w_s"][...]
    bs = w["b_s"][...]
    for c in range(tm // CHUNK):
        rows = slice(c * CHUNK, (c + 1) * CHUNK)
        r = _dot(ws, vn[rows, :])
        mixed = r[(N_HEADS_B - 1) * CHUNK:, :]
        for hb in range(N_HEADS_B - 2, -1, -1):
            mixed = jnp.where(lane < (hb + 1) * HEAD_B, r[hb * CHUNK:(hb + 1) * CHUNK, :], mixed)
        ob_ref[rows, :] = (u[rows, :] * (mixed + bs)).astype(BF16)

    zc = z[:, OFF_CG:OFF_CG + WIDTH_C] * z[:, OFF_HH:OFF_HH + WIDTH_C]
    pos = lax.broadcasted_iota(jnp.int32, (tm, WIDTH_C), 0) & (seq_len - 1)
    z_prev = jnp.where(pos == 0, 0.0, pltpu.roll(zc, 1, 0))
    z_next = jnp.where(pos == seq_len - 1, 0.0, pltpu.roll(zc, tm - 1, 0))
    wc = w["w_conv"][...]
    y = z_prev * wc[0:1, :] + zc * wc[1:2, :] + z_next * wc[2:3, :]
    oc_ref[...] = (z[:, OFF_BG:OFF_BG + WIDTH_C] * y).astype(BF16)


def _attn_head(q_ref, k_ref, v_ref, cache, qrows, krows, hd, oa_ref):
    qk_cols = slice(hd * QK_PAD, (hd + 1) * QK_PAD)
    v_cols = slice(hd * V_HEAD, (hd + 1) * V_HEAD)
    qh = q_ref[qrows, qk_cols]
    s_lat = _dot_nt(qh, k_ref[krows, qk_cols])
    mx = jnp.max(s_lat, axis=-1, keepdims=True)
    if cache is not None:
        kc_ref, vc_ref = cache
        s_ctx = _dot_nt(qh, kc_ref[:, qk_cols])
        mx = jnp.maximum(mx, jnp.max(s_ctx, axis=-1, keepdims=True))
    p_lat = jnp.exp2(s_lat - mx)
    den = jnp.sum(p_lat, axis=-1, keepdims=True)
    o = _dot(p_lat.astype(BF16), v_ref[krows, v_cols])
    if cache is not None:
        p_ctx = jnp.exp2(s_ctx - mx)
        den = den + jnp.sum(p_ctx, axis=-1, keepdims=True)
        o = o + _dot(p_ctx.astype(BF16), vc_ref[:, v_cols])
    oa_ref[qrows, v_cols] = (o * (1.0 / den)).astype(BF16)


def _post_stage(x_ref, mod, mix_ref, w, o_ref):
    ga1 = mod[:, 2 * D_MODEL:3 * D_MODEL]
    sh2 = mod[:, 3 * D_MODEL:4 * D_MODEL]
    sc2 = mod[:, 4 * D_MODEL:5 * D_MODEL]
    ga2 = mod[:, 5 * D_MODEL:6 * D_MODEL]
    post_mix_row = ga1 * w["g_post_mix"][...]
    pre_ffn_row = w["g_pre_ffn"][...] * (1.0 + sc2)
    post_ffn_row = ga2 * w["g_post_ffn"][...]
    group = x_ref.shape[0] // OUT_SPLIT
    for r in range(OUT_SPLIT):
        rows = slice(r * group, (r + 1) * group)
        mo = _dot(mix_ref[rows, :], w["w_out"][...])
        x1 = x_ref[rows, :] + _rms(mo, post_mix_row)
        o_ref[rows, :] = x1
        mix_ref[rows, :] = (_rms(x1, pre_ffn_row) + sh2).astype(BF16)
    h2 = mix_ref[...]
    f = None
    for j in range(D_FF // FF_CHUNK):
        cols = slice(j * FF_CHUNK, (j + 1) * FF_CHUNK)
        a = jnp.square(jnp.maximum(_dot(h2, w["w_ff1"][:, cols]), 0.0)).astype(BF16)
        part = _dot(a, w["w_ff2"][cols, :])
        f = part if f is None else f + part
    o_ref[...] = o_ref[...] + _rms(f, post_ffn_row)


def _context_kernel(*refs, layer, seq_len, aliased, cast_next, mod_next):
    n_w = len(PRE_WEIGHTS) + len(POST_WEIGHTS)
    n_cast_out = len(BIG_WEIGHTS) if cast_next else 0
    n_cast_in = n_cast_out + 1 if cast_next else 0
    n_side_out = n_cast_out + (1 if mod_next else 0)
    n_side_in = n_cast_in + (3 if mod_next else 0)
    x_ref, mod_ref = refs[:2]
    w = _weight_refs(PRE_WEIGHTS + POST_WEIGHTS, refs[2:2 + n_w], layer)
    side_in = refs[2 + n_w:2 + n_w + n_side_in]
    n_in = 2 + n_w + n_side_in + (2 if aliased else 0)
    o_ref, ckv_ref, kr_ref = refs[n_in:n_in + 3]
    side_out = refs[n_in + 3:n_in + 3 + n_side_out]
    q_ref, k_ref, v_ref, mix_ref = refs[n_in + 3 + n_side_out:]
    if cast_next:
        _cast_stage(side_in[:n_cast_in], side_out[:n_cast_out])
    if mod_next:
        _mod_stage(*side_in[n_cast_in:], side_out[n_cast_out])
    x = x_ref[...]
    mod = mod_ref[0:1, :]
    _pre_stage(x, mod, w, None, seq_len, q_ref, k_ref, v_ref,
               mix_ref.at[:, WIDTH_A:WIDTH_A + WIDTH_B], mix_ref.at[:, WIDTH_A + WIDTH_B:],
               (ckv_ref, kr_ref))
    for s in range(x.shape[0] // seq_len):
        rows = slice(s * seq_len, (s + 1) * seq_len)
        for hd in range(N_HEADS):
            _attn_head(q_ref, k_ref, v_ref, None, rows, rows, hd, mix_ref)
    _post_stage(x_ref, mod, mix_ref, w, o_ref)


def _context_layer(x, mods, wts, l, new_ckv, new_kr, raw_big, mod_args, *, seq_len, seqs_per_tile):
    t = x.shape[0]
    tm = seq_len * seqs_per_tile
    n_seq_total = t // seq_len
    aliased = new_ckv is not None
    assert aliased or l == 0
    cast_next = raw_big is not None
    mod_next = mod_args is not None
    n_tiles = t // tm
    row_spec = pl.BlockSpec((tm, D_MODEL), lambda i: (i, 0))
    layer_dim, layer_idx = (None, l) if aliased else (DEPTH, 0)
    ckv_spec = pl.BlockSpec((seqs_per_tile, layer_dim, seq_len, KV_LORA),
                            lambda i: (i, layer_idx, 0, 0))
    kr_spec = pl.BlockSpec((seqs_per_tile, layer_dim, QK_ROPE, seq_len),
                           lambda i: (i, layer_idx, 0, 0))
    in_specs = [row_spec, _mod_spec()]
    args = [x, mods]
    for n in PRE_WEIGHTS + POST_WEIGHTS:
        in_specs.append(_layer_spec(wts[n].shape, l))
        args.append(wts[n])
    out_specs = [row_spec, ckv_spec, kr_spec]
    out_shape = [
        jax.ShapeDtypeStruct((t, D_MODEL), F32),
        jax.ShapeDtypeStruct((n_seq_total, DEPTH, seq_len, KV_LORA), F32),
        jax.ShapeDtypeStruct((n_seq_total, DEPTH, QK_ROPE, seq_len), F32),
    ]
    if cast_next:
        assert n_tiles == CAST_STEPS
        c_in, c_out, c_shape, c_args = _cast_specs(raw_big, l + 1)
        in_specs += c_in
        args += c_args
        out_specs += c_out
        out_shape += c_shape
    if mod_next:
        m_in, m_out, m_shape = _mod_specs(l + 1, n_tiles)
        in_specs += m_in
        args += list(mod_args)
        out_specs.append(m_out)
        out_shape.append(m_shape)
    aliases = {}
    if aliased:
        in_specs += [pl.BlockSpec(memory_space=pl.ANY)] * 2
        aliases = {len(args): 1, len(args) + 1: 2}
        args += [new_ckv, new_kr]
    return pl.pallas_call(
        functools.partial(_context_kernel, layer=l, seq_len=seq_len, aliased=aliased,
                          cast_next=cast_next, mod_next=mod_next),
        grid=(n_tiles,),
        in_specs=in_specs,
        out_specs=out_specs,
        out_shape=out_shape,
        scratch_shapes=[
            pltpu.VMEM((tm, QK_WIDTH), BF16), pltpu.VMEM((tm, QK_WIDTH), BF16),
            pltpu.VMEM((tm, WIDTH_A), BF16), pltpu.VMEM((tm, D_MODEL), BF16),
        ],
        input_output_aliases=aliases,
        compiler_params=_params(1),
        name="context_layer",
    )(*args)


def _latent_pre_kernel(*refs, layer, seq_len, mod_next):
    x_ref, mod_ref = refs[:2]
    w = _weight_refs(PRE_WEIGHTS, refs[2:2 + len(PRE_WEIGHTS)], layer)
    refs = refs[2 + len(PRE_WEIGHTS):]
    cos_ref, sin_ref, cache_ckv_ref, cache_kr_ref = refs[:4]
    if mod_next:
        _mod_stage(*refs[4:7], refs[-1])
        refs = refs[:4] + refs[7:-1]
    q_ref, k_ref, v_ref, obc_ref, kc_ref, vc_ref = refs[4:]
    _cache_kv_stage(cache_ckv_ref, cache_kr_ref, w["w_kv"], kc_ref, vc_ref)
    mod = mod_ref[pl.ds(1 + pl.program_id(0), 1), :]
    _pre_stage(x_ref[...], mod, w, (cos_ref[...], sin_ref[...]), seq_len,
               q_ref, k_ref, v_ref, obc_ref.at[:, 0:WIDTH_B], obc_ref.at[:, WIDTH_B:], None)


def _latent_pre(x, mods, wts, l, rope_tabs, cache_ckv, cache_krope_t, mod_args, *, seq_len):
    t = x.shape[0]
    tm = seq_len
    nb, _, past, _ = cache_ckv.shape
    assert nb == t // tm
    row_spec = lambda w: pl.BlockSpec((tm, w), lambda i: (i, 0))
    in_specs = [row_spec(D_MODEL), _mod_spec()]
    args = [x, mods]
    for n in PRE_WEIGHTS:
        in_specs.append(_layer_spec(wts[n].shape, l))
        args.append(wts[n])
    in_specs += [pl.BlockSpec((tm, LANES), lambda i: (0, 0), pipeline_mode=pl.Buffered(1))] * 2
    args += list(rope_tabs)
    in_specs += [pl.BlockSpec((None, None, past, KV_LORA), lambda i: (i, l, 0, 0)),
                 pl.BlockSpec((None, None, QK_ROPE, past), lambda i: (i, l, 0, 0))]
    args += [cache_ckv, cache_krope_t]
    out_specs = [row_spec(QK_WIDTH), row_spec(QK_WIDTH), row_spec(WIDTH_A), row_spec(BC_WIDTH),
                 pl.BlockSpec((None, past, QK_WIDTH), lambda i: (i, 0, 0)),
                 pl.BlockSpec((None, past, WIDTH_A), lambda i: (i, 0, 0))]
    out_shape = [
        jax.ShapeDtypeStruct((t, QK_WIDTH), BF16),
        jax.ShapeDtypeStruct((t, QK_WIDTH), BF16),
        jax.ShapeDtypeStruct((t, WIDTH_A), BF16),
        jax.ShapeDtypeStruct((t, BC_WIDTH), BF16),
        jax.ShapeDtypeStruct((nb, past, QK_WIDTH), BF16),
        jax.ShapeDtypeStruct((nb, past, WIDTH_A), BF16),
    ]
    if mod_args is not None:
        m_in, m_out, m_shape = _mod_specs(l + 1, t // tm)
        in_specs += m_in
        args += list(mod_args)
        out_specs.append(m_out)
        out_shape.append(m_shape)
    return pl.pallas_call(
        functools.partial(_latent_pre_kernel, layer=l, seq_len=seq_len,
                          mod_next=mod_args is not None),
        grid=(t // tm,),
        in_specs=in_specs,
        out_specs=out_specs,
        out_shape=out_shape,
        compiler_params=_params(1),
        name="latent_pre",
    )(*args)


def _latent_post_kernel(*refs, layer, tiles_per_seq):
    q_ref, k_ref, v_ref, kc_ref, vc_ref, obc_ref, x_ref, mod_ref = refs[:8]
    w = _weight_refs(POST_WEIGHTS, refs[8:8 + len(POST_WEIGHTS)], layer)
    o_ref, mix_ref = refs[8 + len(POST_WEIGHTS):]
    qrows = slice(0, q_ref.shape[0])
    krows = slice(0, k_ref.shape[0])
    for hd in range(N_HEADS):
        _attn_head(q_ref, k_ref, v_ref, (kc_ref, vc_ref), qrows, krows, hd, mix_ref)
    mix_ref[:, WIDTH_A:] = obc_ref[...]
    mod = mod_ref[pl.ds(1 + pl.program_id(0) // tiles_per_seq, 1), :]
    _post_stage(x_ref, mod, mix_ref, w, o_ref)


def _latent_post(q, k, v, obc, x, mods, cache, wts, l, *, seq_len, tq):
    t = x.shape[0]
    tiles_per_seq = seq_len // tq
    kc, vc = cache
    past = kc.shape[1]
    seq_of = lambda i: i // tiles_per_seq
    row_spec = lambda w: pl.BlockSpec((tq, w), lambda i: (i, 0))
    in_specs = [
        row_spec(QK_WIDTH),
        pl.BlockSpec((seq_len, QK_WIDTH), lambda i: (seq_of(i), 0)),
        pl.BlockSpec((seq_len, WIDTH_A), lambda i: (seq_of(i), 0)),
        pl.BlockSpec((None, past, QK_WIDTH), lambda i: (seq_of(i), 0, 0)),
        pl.BlockSpec((None, past, WIDTH_A), lambda i: (seq_of(i), 0, 0)),
        row_spec(BC_WIDTH),
        row_spec(D_MODEL),
        _mod_spec(),
    ]
    args = [q, k, v, kc, vc, obc, x, mods]
    for n in POST_WEIGHTS:
        in_specs.append(_layer_spec(wts[n].shape, l))
        args.append(wts[n])
    return pl.pallas_call(
        functools.partial(_latent_post_kernel, layer=l, tiles_per_seq=tiles_per_seq),
        grid=(t // tq,),
        in_specs=in_specs,
        out_specs=row_spec(D_MODEL),
        out_shape=jax.ShapeDtypeStruct((t, D_MODEL), F32),
        scratch_shapes=[pltpu.VMEM((tq, D_MODEL), BF16)],
        compiler_params=_params(1),
        name="latent_post",
    )(*args)


def _rope_tables(n_tokens):
    rows = n_tokens // GRID_W
    row = np.repeat(np.arange(rows, dtype=np.float64), GRID_W)
    col = np.tile(np.arange(GRID_W, dtype=np.float64), rows)
    nf = QK_ROPE // 4
    inv = ROPE_THETA ** (-np.arange(nf, dtype=np.float64) / nf)
    ang_r = row[:, None] * inv
    ang_c = col[:, None] * inv
    zeros = np.zeros((n_tokens, LANES - QK_ROPE))
    cos = np.concatenate([np.cos(ang_r), np.cos(ang_r), np.cos(ang_c), np.cos(ang_c), zeros], axis=1)
    sin = np.concatenate([-np.sin(ang_r), np.sin(ang_r), -np.sin(ang_c), np.sin(ang_c), zeros], axis=1)
    return jnp.asarray(cos, F32), jnp.asarray(sin, F32)


def _prepare_weights(w_uq, w_ukv, w_s, b_s, w_conv, gains):
    w_q = jnp.pad(w_uq, ((0, 0), (0, 0), (0, 0), (0, QK_PAD - QK_NOPE - QK_ROPE)))
    w_q = w_q.reshape(DEPTH, Q_LORA, QK_WIDTH).astype(BF16)
    w_kv = jnp.concatenate(
        [w_ukv[..., :QK_NOPE].reshape(DEPTH, KV_LORA, N_HEADS * QK_NOPE),
         w_ukv[..., QK_NOPE:].reshape(DEPTH, KV_LORA, N_HEADS * V_HEAD)], axis=-1).astype(BF16)
    wts = {
        "w_q": w_q, "w_kv": w_kv,
        "w_s": w_s.reshape(DEPTH, N_HEADS_B * CHUNK, CHUNK).astype(BF16),
        "b_s": jnp.repeat(jnp.swapaxes(b_s, 1, 2), HEAD_B, axis=-1),
        "w_conv": w_conv,
    }
    for name, g in gains.items():
        wts[name] = g
    return wts


def kernel(x_prompt, x_sample, cache_ckv, cache_krope, c, c_ctx, w_ada, b_ada, g_pre_mix, w_in, g_q, w_uq, g_kv, w_ukv, g_v, w_s, b_s, w_conv, w_out, g_post_mix, g_pre_ffn, w_ff1, w_ff2, g_post_ffn):
    batch, seq, _ = x_prompt.shape
    dec_batch, dec_seq, _ = x_sample.shape

    wts = _prepare_weights(
        w_uq, w_ukv, w_s, b_s, w_conv,
        {"g_pre_mix": g_pre_mix, "g_q": g_q, "g_kv": g_kv, "g_v": g_v,
         "g_post_mix": g_post_mix, "g_pre_ffn": g_pre_ffn, "g_post_ffn": g_post_ffn})
    raw_big = (jnp.swapaxes(w_in, 1, 2), w_out, w_ff1, w_ff2)

    cond = jnp.concatenate(
        [c_ctx[None, :], c, jnp.zeros((COND_ROWS - 1 - dec_batch, D_MODEL), F32)], axis=0)
    mod_args = (cond, w_ada, b_ada.reshape(DEPTH, 1, MOD_COLS))
    *big, mods = _first_layer_setup(raw_big, mod_args)

    cache_krope_t = jnp.swapaxes(cache_krope, 2, 3)
    rope_tabs = _rope_tables(dec_seq)

    xp = x_prompt.reshape(batch * seq, D_MODEL)
    xs = x_sample.reshape(dec_batch * dec_seq, D_MODEL)
    new_ckv = new_kr = None
    for l in range(DEPTH):
        wl = dict(wts, **dict(zip(BIG_WEIGHTS, big)))
        side_mods = 1 <= l < DEPTH - 1
        xp, new_ckv, new_kr, *side = _context_layer(
            xp, mods, wl, l, new_ckv, new_kr, raw_big if l + 1 < DEPTH else None,
            mod_args if side_mods else None, seq_len=seq, seqs_per_tile=2)
        q, k, v, obc, kc, vc, *pre_side = _latent_pre(
            xs, mods, wl, l, rope_tabs, cache_ckv, cache_krope_t, mod_args if l == 0 else None,
            seq_len=dec_seq)
        xs = _latent_post(q, k, v, obc, xs, mods, (kc, vc), wl, l, seq_len=dec_seq, tq=512)
        big = side[:len(BIG_WEIGHTS)]
        mods = side[-1] if side_mods else (pre_side[0] if pre_side else None)

    return (xp.reshape(batch, seq, D_MODEL), xs.reshape(dec_batch, dec_seq, D_MODEL),
            new_ckv, jnp.swapaxes(new_kr, 2, 3))
```

```python
import functools
import math

import jax
import jax.numpy as jnp
import numpy as np
from jax import lax
from jax.experimental import pallas as pl
from jax.experimental.pallas import tpu as pltpu

F32 = jnp.float32
BF16 = jnp.bfloat16

D_MODEL = 1024
DEPTH = 4
GRID_W = 64
N_HEADS = 4
QK_NOPE = 128
QK_ROPE = 64
V_HEAD = 128
Q_LORA = 384
KV_LORA = 256
WIDTH_A = N_HEADS * V_HEAD
ROPE_THETA = 10000.0
WIDTH_B = 256
N_HEADS_B = 4
HEAD_B = WIDTH_B // N_HEADS_B
CHUNK = 128
WIDTH_C = 256
D_FF = 4 * D_MODEL
N_MOD = 6
EPS = 1e-6

LANES = 128
QK_PAD = 2 * LANES
QK_WIDTH = N_HEADS * QK_PAD
KV_COLS = N_HEADS * (QK_NOPE + V_HEAD)
BC_WIDTH = WIDTH_B + WIDTH_C
OFF_Q = 0
OFF_CKV = OFF_Q + Q_LORA
OFF_U = OFF_CKV + KV_LORA
OFF_V = OFF_U + WIDTH_B
OFF_BG = OFF_V + WIDTH_B
OFF_CG = OFF_BG + WIDTH_C
OFF_HH = OFF_CG + WIDTH_C
OFF_KR = OFF_HH + WIDTH_C
IN_COLS_PAD = OFF_KR + LANES
COND_ROWS = 8
VMEM_LIMIT = 56 * 1024 * 1024
SM_SCALE_LOG2 = math.log2(math.e) / math.sqrt(QK_NOPE + QK_ROPE)
FF_CHUNK = 1024
OUT_SPLIT = 2

PRE_WEIGHTS = ("g_pre_mix", "w_in", "g_q", "w_q", "g_kv", "w_kv", "g_v", "w_s", "b_s", "w_conv")
POST_WEIGHTS = ("g_post_mix", "w_out", "g_pre_ffn", "w_ff1", "w_ff2", "g_post_ffn")
BIG_WEIGHTS = ("w_in", "w_out", "w_ff1", "w_ff2")
BIG_SHAPES = ((D_MODEL, IN_COLS_PAD), (D_MODEL, D_MODEL), (D_MODEL, D_FF), (D_FF, D_MODEL))
CAST_STEPS = 16


def _rms(x, g):
    return x * lax.rsqrt(jnp.mean(x * x, axis=-1, keepdims=True) + EPS) * g


def _dot(a, b):
    return jnp.dot(a, b, preferred_element_type=F32)


def _dot_nt(a, b):
    return lax.dot_general(a, b, (((1,), (1,)), ((), ())), preferred_element_type=F32)


def _params(n_axes):
    return pltpu.CompilerParams(
        dimension_semantics=("parallel",) * n_axes, vmem_limit_bytes=VMEM_LIMIT)


def _layer_spec(shape, l):
    if len(shape) == 2:
        return pl.BlockSpec(tuple(shape), lambda *_: (0, 0), pipeline_mode=pl.Buffered(1))
    return pl.BlockSpec((None,) + tuple(shape[1:]), lambda *_: (l, 0, 0),
                        pipeline_mode=pl.Buffered(1))


class _LayerRow:
    def __init__(self, ref, layer):
        self.ref, self.layer = ref, layer

    def __getitem__(self, idx):
        assert idx is Ellipsis
        return self.ref[self.layer:self.layer + 1, :]


def _weight_refs(names, refs, layer):
    return {n: _LayerRow(r, layer) if n.startswith("g_") else r for n, r in zip(names, refs)}


def _mod_spec():
    return pl.BlockSpec((COND_ROWS, N_MOD * D_MODEL), lambda *_: (0, 0),
                        pipeline_mode=pl.Buffered(1))


MOD_COLS = N_MOD * D_MODEL


def _mod_stage(cond_ref, w_ref, b_ref, o_ref):
    c = cond_ref[...]
    s = c / (1.0 + jnp.exp(-c))
    o_ref[...] = _dot(s.astype(BF16), w_ref[...].astype(BF16)) + b_ref[...]


def _mod_specs(l, n_steps):
    tn = MOD_COLS // n_steps
    in_specs = [
        pl.BlockSpec((COND_ROWS, D_MODEL), lambda i: (0, 0), pipeline_mode=pl.Buffered(1)),
        pl.BlockSpec((None, D_MODEL, tn), lambda i: (l, 0, i)),
        pl.BlockSpec((None, 1, tn), lambda i: (l, 0, i)),
    ]
    return (in_specs, pl.BlockSpec((COND_ROWS, tn), lambda i: (0, i)),
            jax.ShapeDtypeStruct((COND_ROWS, MOD_COLS), F32))


def _cache_kv_stage(ckv_ref, kr_ref, wkv_ref, k_ref, v_ref):
    past = ckv_ref.shape[0]
    kv = _dot(ckv_ref[...].astype(BF16), wkv_ref[...])
    pad = jnp.zeros((QK_PAD - QK_NOPE - QK_ROPE, past), F32)
    krz = jnp.concatenate([kr_ref[...], pad], axis=0).T.astype(BF16)
    for h in range(N_HEADS):
        lo = h * QK_PAD
        k_ref[:, lo:lo + QK_NOPE] = kv[:, h * QK_NOPE:(h + 1) * QK_NOPE].astype(BF16)
        k_ref[:, lo + QK_NOPE:lo + QK_PAD] = krz
    v_ref[...] = kv[:, N_HEADS * QK_NOPE:].astype(BF16)


IN_HALF = IN_COLS_PAD // CAST_STEPS // 2


def _cast_stage(in_refs, out_refs):
    win_a, win_b, wout, w1, w2 = in_refs
    win_o, wout_o, w1_o, w2_o = out_refs
    last = pl.program_id(0) == CAST_STEPS - 1
    b = jnp.where(last, 0.0, win_b[...])
    win_o[...] = jnp.concatenate([win_a[...], b], axis=0).T.astype(BF16)
    wout_o[...] = wout[...].astype(BF16)
    w1_o[...] = w1[...].astype(BF16)
    w2_o[...] = w2[...].astype(BF16)


def _w_in_block(i, half):
    kr_block = (Q_LORA + KV_LORA) // IN_HALF
    n_front = kr_block // 2
    shifted = jnp.where(i < CAST_STEPS - 1, 2 * i + 1 + half, kr_block)
    return jnp.where(i < n_front, 2 * i + half, shifted)


def _cast_specs(raw, l):
    w_in_t = raw[0]
    in_specs = [pl.BlockSpec((None, IN_HALF, D_MODEL), lambda i, h=h: (l, _w_in_block(i, h), 0))
                for h in range(2)]
    out_specs = [pl.BlockSpec((D_MODEL, IN_COLS_PAD // CAST_STEPS), lambda i: (0, i))]
    out_shape = [jax.ShapeDtypeStruct(BIG_SHAPES[0], BF16)]
    for a, (rows, cols) in zip(raw[1:], BIG_SHAPES[1:]):
        chunk = rows // CAST_STEPS
        in_specs.append(pl.BlockSpec((None, chunk, cols), lambda i: (l, i, 0)))
        out_specs.append(pl.BlockSpec((chunk, cols), lambda i: (i, 0)))
        out_shape.append(jax.ShapeDtypeStruct((rows, cols), BF16))
    return in_specs, out_specs, out_shape, [w_in_t, w_in_t] + list(raw[1:])


def _first_layer_kernel(*refs):
    _cast_stage(refs[:5], refs[9:13])
    _mod_stage(*refs[5:8], refs[13])
    wuq_ref, wq_ref = refs[8], refs[14]
    pad = jnp.zeros((QK_PAD - wuq_ref.shape[0], Q_LORA), F32)
    wq_ref[...] = jnp.concatenate([wuq_ref[...], pad], axis=0).T.astype(BF16)


def _first_layer_setup(raw, mod_args, w_uq):
    assert DEPTH * N_HEADS == CAST_STEPS
    in_specs, out_specs, out_shape, args = _cast_specs(raw, 0)
    m_in, m_out, m_shape = _mod_specs(0, CAST_STEPS)
    w_uq_t = jnp.transpose(w_uq, (0, 2, 3, 1))
    q_in = pl.BlockSpec((None, None, w_uq.shape[-1], Q_LORA),
                        lambda i: (i // N_HEADS, i % N_HEADS, 0, 0))
    q_out = pl.BlockSpec((None, Q_LORA, QK_PAD), lambda i: (i // N_HEADS, 0, i % N_HEADS))
    return pl.pallas_call(
        _first_layer_kernel,
        grid=(CAST_STEPS,),
        in_specs=in_specs + m_in + [q_in],
        out_specs=out_specs + [m_out, q_out],
        out_shape=out_shape + [m_shape, jax.ShapeDtypeStruct((DEPTH, Q_LORA, QK_WIDTH), BF16)],
        compiler_params=_params(1),
        name="first_layer_setup",
    )(*args, *mod_args, w_uq_t)


def _swap_halves(x):
    lane = lax.broadcasted_iota(jnp.int32, x.shape, 1)
    quarter = QK_ROPE // 4
    first_half = (lane & (2 * quarter - 1)) < quarter
    return jnp.where(first_half, pltpu.roll(x, LANES - quarter, 1), pltpu.roll(x, quarter, 1))


def _pre_stage(x, mod, w, rope, seq_len, q_ref, k_ref, v_ref, ob_ref, oc_ref, cache_refs):
    tm = x.shape[0]
    sh1 = mod[:, 0:D_MODEL]
    sc1 = mod[:, D_MODEL:2 * D_MODEL]
    h = _rms(x, w["g_pre_mix"][...] * (1.0 + sc1)) + sh1
    z = _dot(h.astype(BF16), w["w_in"][...])

    ckv = _rms(z[:, OFF_CKV:OFF_CKV + KV_LORA], w["g_kv"][...])
    krz = z[:, OFF_KR:OFF_KR + LANES]
    if cache_refs is not None:
        ckv_ref, kr_ref = cache_refs
        if len(ckv_ref.shape) == 4:
            ckv_ref[:, 1:] = jnp.zeros((ckv_ref.shape[0], DEPTH - 1) + ckv_ref.shape[2:], F32)
            kr_ref[:, 1:] = jnp.zeros((kr_ref.shape[0], DEPTH - 1) + kr_ref.shape[2:], F32)
            ckv_ref, kr_ref = ckv_ref.at[:, 0], kr_ref.at[:, 0]
        ckv_ref[...] = ckv.reshape(ckv_ref.shape)
        for s in range(tm // seq_len):
            kr_ref[s] = krz[s * seq_len:(s + 1) * seq_len, :].T[0:QK_ROPE, :]
    qn = _rms(z[:, OFF_Q:OFF_Q + Q_LORA], w["g_q"][...])
    q = _dot(qn.astype(BF16), w["w_q"][...]) * SM_SCALE_LOG2
    kv = _dot(ckv.astype(BF16), w["w_kv"][...])
    if rope is not None:
        cos, sin = rope
        krz = krz * cos + _swap_halves(krz) * sin
    krz = krz.astype(BF16)
    for hd in range(N_HEADS):
        lo = hd * QK_PAD
        q_ref[:, lo:lo + QK_NOPE] = q[:, lo:lo + QK_NOPE].astype(BF16)
        qr = q[:, lo + QK_NOPE:lo + QK_PAD]
        if rope is not None:
            qr = qr * cos + _swap_halves(qr) * sin
        q_ref[:, lo + QK_NOPE:lo + QK_PAD] = qr.astype(BF16)
        k_ref[:, lo:lo + QK_NOPE] = kv[:, hd * QK_NOPE:(hd + 1) * QK_NOPE].astype(BF16)
        k_ref[:, lo + QK_NOPE:lo + QK_PAD] = krz
    v_ref[...] = kv[:, N_HEADS * QK_NOPE:].astype(BF16)

    u = jax.nn.gelu(z[:, OFF_U:OFF_U + WIDTH_B])
    vn = _rms(jax.nn.gelu(z[:, OFF_V:OFF_V + WIDTH_B]), w["g_v"][...]).astype(BF16)
    lane = lax.broadcasted_iota(jnp.int32, (CHUNK, WIDTH_B), 1)
    ws = w["w_s"][...]
    bs = w["b_s"][...]
    for c in range(tm // CHUNK):
        rows = slice(c * CHUNK, (c + 1) * CHUNK)
        r = _dot(ws, vn[rows, :])
        mixed = r[(N_HEADS_B - 1) * CHUNK:, :]
        for hb in range(N_HEADS_B - 2, -1, -1):
            mixed = jnp.where(lane < (hb + 1) * HEAD_B, r[hb * CHUNK:(hb + 1) * CHUNK, :], mixed)
        ob_ref[rows, :] = (u[rows, :] * (mixed + bs)).astype(BF16)

    zc = z[:, OFF_CG:OFF_CG + WIDTH_C] * z[:, OFF_HH:OFF_HH + WIDTH_C]
    pos = lax.broadcasted_iota(jnp.int32, (tm, WIDTH_C), 0) & (seq_len - 1)
    z_prev = jnp.where(pos == 0, 0.0, pltpu.roll(zc, 1, 0))
    z_next = jnp.where(pos == seq_len - 1, 0.0, pltpu.roll(zc, tm - 1, 0))
    wc = w["w_conv"][...]
    y = z_prev * wc[0:1, :] + zc * wc[1:2, :] + z_next * wc[2:3, :]
    oc_ref[...] = (z[:, OFF_BG:OFF_BG + WIDTH_C] * y).astype(BF16)


def _attn_head(q_ref, k_ref, v_ref, cache, qrows, krows, hd, oa_ref):
    qk_cols = slice(hd * QK_PAD, (hd + 1) * QK_PAD)
    v_cols = slice(hd * V_HEAD, (hd + 1) * V_HEAD)
    qh = q_ref[qrows, qk_cols]
    s_lat = _dot_nt(qh, k_ref[krows, qk_cols])
    mx = jnp.max(s_lat, axis=-1, keepdims=True)
    if cache is not None:
        kc_ref, vc_ref = cache
        s_ctx = _dot_nt(qh, kc_ref[:, qk_cols])
        mx = jnp.maximum(mx, jnp.max(s_ctx, axis=-1, keepdims=True))
    p_lat = jnp.exp2(s_lat - mx)
    den = jnp.sum(p_lat, axis=-1, keepdims=True)
    o = _dot(p_lat.astype(BF16), v_ref[krows, v_cols])
    if cache is not None:
        p_ctx = jnp.exp2(s_ctx - mx)
        den = den + jnp.sum(p_ctx, axis=-1, keepdims=True)
        o = o + _dot(p_ctx.astype(BF16), vc_ref[:, v_cols])
    oa_ref[qrows, v_cols] = (o * (1.0 / den)).astype(BF16)


def _post_stage(x_ref, mod, mix_ref, w, o_ref):
    ga1 = mod[:, 2 * D_MODEL:3 * D_MODEL]
    sh2 = mod[:, 3 * D_MODEL:4 * D_MODEL]
    sc2 = mod[:, 4 * D_MODEL:5 * D_MODEL]
    ga2 = mod[:, 5 * D_MODEL:6 * D_MODEL]
    post_mix_row = ga1 * w["g_post_mix"][...]
    pre_ffn_row = w["g_pre_ffn"][...] * (1.0 + sc2)
    post_ffn_row = ga2 * w["g_post_ffn"][...]
    group = x_ref.shape[0] // OUT_SPLIT
    for r in range(OUT_SPLIT):
        rows = slice(r * group, (r + 1) * group)
        mo = _dot(mix_ref[rows, :], w["w_out"][...])
        x1 = x_ref[rows, :] + _rms(mo, post_mix_row)
        o_ref[rows, :] = x1
        mix_ref[rows, :] = (_rms(x1, pre_ffn_row) + sh2).astype(BF16)
    h2 = mix_ref[...]
    f = None
    for j in range(D_FF // FF_CHUNK):
        cols = slice(j * FF_CHUNK, (j + 1) * FF_CHUNK)
        a = jnp.square(jnp.maximum(_dot(h2, w["w_ff1"][:, cols]), 0.0)).astype(BF16)
        part = _dot(a, w["w_ff2"][cols, :])
        f = part if f is None else f + part
    o_ref[...] = o_ref[...] + _rms(f, post_ffn_row)


def _context_kernel(*refs, layer, seq_len, aliased, cast_next, mod_next):
    n_w = len(PRE_WEIGHTS) + len(POST_WEIGHTS)
    n_cast_out = len(BIG_WEIGHTS) if cast_next else 0
    n_cast_in = n_cast_out + 1 if cast_next else 0
    n_side_out = n_cast_out + (1 if mod_next else 0)
    n_side_in = n_cast_in + (3 if mod_next else 0)
    x_ref, mod_ref = refs[:2]
    w = _weight_refs(PRE_WEIGHTS + POST_WEIGHTS, refs[2:2 + n_w], layer)
    side_in = refs[2 + n_w:2 + n_w + n_side_in]
    n_in = 2 + n_w + n_side_in + (2 if aliased else 0)
    o_ref, ckv_ref, kr_ref = refs[n_in:n_in + 3]
    side_out = refs[n_in + 3:n_in + 3 + n_side_out]
    q_ref, k_ref, v_ref, mix_ref = refs[n_in + 3 + n_side_out:]
    if cast_next:
        _cast_stage(side_in[:n_cast_in], side_out[:n_cast_out])
    if mod_next:
        _mod_stage(*side_in[n_cast_in:], side_out[n_cast_out])
    x = x_ref[...]
    mod = mod_ref[0:1, :]
    _pre_stage(x, mod, w, None, seq_len, q_ref, k_ref, v_ref,
               mix_ref.at[:, WIDTH_A:WIDTH_A + WIDTH_B], mix_ref.at[:, WIDTH_A + WIDTH_B:],
               (ckv_ref, kr_ref))
    for s in range(x.shape[0] // seq_len):
        rows = slice(s * seq_len, (s + 1) * seq_len)
        for hd in range(N_HEADS):
            _attn_head(q_ref, k_ref, v_ref, None, rows, rows, hd, mix_ref)
    _post_stage(x_ref, mod, mix_ref, w, o_ref)


def _context_layer(x, mods, wts, l, new_ckv, new_kr, raw_big, mod_args, *, seq_len, seqs_per_tile):
    t = x.shape[0]
    tm = seq_len * seqs_per_tile
    n_seq_total = t // seq_len
    aliased = new_ckv is not None
    assert aliased or l == 0
    cast_next = raw_big is not None
    mod_next = mod_args is not None
    n_tiles = t // tm
    row_spec = pl.BlockSpec((tm, D_MODEL), lambda i: (i, 0))
    layer_dim, layer_idx = (None, l) if aliased else (DEPTH, 0)
    ckv_spec = pl.BlockSpec((seqs_per_tile, layer_dim, seq_len, KV_LORA),
                            lambda i: (i, layer_idx, 0, 0))
    kr_spec = pl.BlockSpec((seqs_per_tile, layer_dim, QK_ROPE, seq_len),
                           lambda i: (i, layer_idx, 0, 0))
    in_specs = [row_spec, _mod_spec()]
    args = [x, mods]
    for n in PRE_WEIGHTS + POST_WEIGHTS:
        in_specs.append(_layer_spec(wts[n].shape, l))
        args.append(wts[n])
    out_specs = [row_spec, ckv_spec, kr_spec]
    out_shape = [
        jax.ShapeDtypeStruct((t, D_MODEL), F32),
        jax.ShapeDtypeStruct((n_seq_total, DEPTH, seq_len, KV_LORA), F32),
        jax.ShapeDtypeStruct((n_seq_total, DEPTH, QK_ROPE, seq_len), F32),
    ]
    if cast_next:
        assert n_tiles == CAST_STEPS
        c_in, c_out, c_shape, c_args = _cast_specs(raw_big, l + 1)
        in_specs += c_in
        args += c_args
        out_specs += c_out
        out_shape += c_shape
    if mod_next:
        m_in, m_out, m_shape = _mod_specs(l + 1, n_tiles)
        in_specs += m_in
        args += list(mod_args)
        out_specs.append(m_out)
        out_shape.append(m_shape)
    aliases = {}
    if aliased:
        in_specs += [pl.BlockSpec(memory_space=pl.ANY)] * 2
        aliases = {len(args): 1, len(args) + 1: 2}
        args += [new_ckv, new_kr]
    return pl.pallas_call(
        functools.partial(_context_kernel, layer=l, seq_len=seq_len, aliased=aliased,
                          cast_next=cast_next, mod_next=mod_next),
        grid=(n_tiles,),
        in_specs=in_specs,
        out_specs=out_specs,
        out_shape=out_shape,
        scratch_shapes=[
            pltpu.VMEM((tm, QK_WIDTH), BF16), pltpu.VMEM((tm, QK_WIDTH), BF16),
            pltpu.VMEM((tm, WIDTH_A), BF16), pltpu.VMEM((tm, D_MODEL), BF16),
        ],
        input_output_aliases=aliases,
        compiler_params=_params(1),
        name="context_layer",
    )(*args)


def _latent_pre_kernel(*refs, layer, seq_len, mod_next):
    x_ref, mod_ref = refs[:2]
    w = _weight_refs(PRE_WEIGHTS, refs[2:2 + len(PRE_WEIGHTS)], layer)
    refs = refs[2 + len(PRE_WEIGHTS):]
    cos_ref, sin_ref, cache_ckv_ref, cache_kr_ref = refs[:4]
    if mod_next:
        _mod_stage(*refs[4:7], refs[-1])
        refs = refs[:4] + refs[7:-1]
    q_ref, k_ref, v_ref, obc_ref, kc_ref, vc_ref = refs[4:]
    _cache_kv_stage(cache_ckv_ref, cache_kr_ref, w["w_kv"], kc_ref, vc_ref)
    mod = mod_ref[pl.ds(1 + pl.program_id(0), 1), :]
    _pre_stage(x_ref[...], mod, w, (cos_ref[...], sin_ref[...]), seq_len,
               q_ref, k_ref, v_ref, obc_ref.at[:, 0:WIDTH_B], obc_ref.at[:, WIDTH_B:], None)


def _latent_pre(x, mods, wts, l, rope_tabs, cache_ckv, cache_krope_t, mod_args, *, seq_len):
    t = x.shape[0]
    tm = seq_len
    nb, _, past, _ = cache_ckv.shape
    assert nb == t // tm
    row_spec = lambda w: pl.BlockSpec((tm, w), lambda i: (i, 0))
    in_specs = [row_spec(D_MODEL), _mod_spec()]
    args = [x, mods]
    for n in PRE_WEIGHTS:
        in_specs.append(_layer_spec(wts[n].shape, l))
        args.append(wts[n])
    in_specs += [pl.BlockSpec((tm, LANES), lambda i: (0, 0), pipeline_mode=pl.Buffered(1))] * 2
    args += list(rope_tabs)
    in_specs += [pl.BlockSpec((None, None, past, KV_LORA), lambda i: (i, l, 0, 0)),
                 pl.BlockSpec((None, None, QK_ROPE, past), lambda i: (i, l, 0, 0))]
    args += [cache_ckv, cache_krope_t]
    out_specs = [row_spec(QK_WIDTH), row_spec(QK_WIDTH), row_spec(WIDTH_A), row_spec(BC_WIDTH),
                 pl.BlockSpec((None, past, QK_WIDTH), lambda i: (i, 0, 0)),
                 pl.BlockSpec((None, past, WIDTH_A), lambda i: (i, 0, 0))]
    out_shape = [
        jax.ShapeDtypeStruct((t, QK_WIDTH), BF16),
        jax.ShapeDtypeStruct((t, QK_WIDTH), BF16),
        jax.ShapeDtypeStruct((t, WIDTH_A), BF16),
        jax.ShapeDtypeStruct((t, BC_WIDTH), BF16),
        jax.ShapeDtypeStruct((nb, past, QK_WIDTH), BF16),
        jax.ShapeDtypeStruct((nb, past, WIDTH_A), BF16),
    ]
    if mod_args is not None:
        m_in, m_out, m_shape = _mod_specs(l + 1, t // tm)
        in_specs += m_in
        args += list(mod_args)
        out_specs.append(m_out)
        out_shape.append(m_shape)
    return pl.pallas_call(
        functools.partial(_latent_pre_kernel, layer=l, seq_len=seq_len,
                          mod_next=mod_args is not None),
        grid=(t // tm,),
        in_specs=in_specs,
        out_specs=out_specs,
        out_shape=out_shape,
        compiler_params=_params(1),
        name="latent_pre",
    )(*args)


def _latent_post_kernel(*refs, layer, tiles_per_seq):
    q_ref, k_ref, v_ref, kc_ref, vc_ref, obc_ref, x_ref, mod_ref = refs[:8]
    w = _weight_refs(POST_WEIGHTS, refs[8:8 + len(POST_WEIGHTS)], layer)
    o_ref, mix_ref = refs[8 + len(POST_WEIGHTS):]
    qrows = slice(0, q_ref.shape[0])
    krows = slice(0, k_ref.shape[0])
    for hd in range(N_HEADS):
        _attn_head(q_ref, k_ref, v_ref, (kc_ref, vc_ref), qrows, krows, hd, mix_ref)
    mix_ref[:, WIDTH_A:] = obc_ref[...]
    mod = mod_ref[pl.ds(1 + pl.program_id(0) // tiles_per_seq, 1), :]
    _post_stage(x_ref, mod, mix_ref, w, o_ref)


def _latent_post(q, k, v, obc, x, mods, cache, wts, l, *, seq_len, tq):
    t = x.shape[0]
    tiles_per_seq = seq_len // tq
    kc, vc = cache
    past = kc.shape[1]
    seq_of = lambda i: i // tiles_per_seq
    row_spec = lambda w: pl.BlockSpec((tq, w), lambda i: (i, 0))
    in_specs = [
        row_spec(QK_WIDTH),
        pl.BlockSpec((seq_len, QK_WIDTH), lambda i: (seq_of(i), 0)),
        pl.BlockSpec((seq_len, WIDTH_A), lambda i: (seq_of(i), 0)),
        pl.BlockSpec((None, past, QK_WIDTH), lambda i: (seq_of(i), 0, 0)),
        pl.BlockSpec((None, past, WIDTH_A), lambda i: (seq_of(i), 0, 0)),
        row_spec(BC_WIDTH),
        row_spec(D_MODEL),
        _mod_spec(),
    ]
    args = [q, k, v, kc, vc, obc, x, mods]
    for n in POST_WEIGHTS:
        in_specs.append(_layer_spec(wts[n].shape, l))
        args.append(wts[n])
    return pl.pallas_call(
        functools.partial(_latent_post_kernel, layer=l, tiles_per_seq=tiles_per_seq),
        grid=(t // tq,),
        in_specs=in_specs,
        out_specs=row_spec(D_MODEL),
        out_shape=jax.ShapeDtypeStruct((t, D_MODEL), F32),
        scratch_shapes=[pltpu.VMEM((tq, D_MODEL), BF16)],
        compiler_params=_params(1),
        name="latent_post",
    )(*args)


def _rope_tables(n_tokens):
    rows = n_tokens // GRID_W
    row = np.repeat(np.arange(rows, dtype=np.float64), GRID_W)
    col = np.tile(np.arange(GRID_W, dtype=np.float64), rows)
    nf = QK_ROPE // 4
    inv = ROPE_THETA ** (-np.arange(nf, dtype=np.float64) / nf)
    ang_r = row[:, None] * inv
    ang_c = col[:, None] * inv
    zeros = np.zeros((n_tokens, LANES - QK_ROPE))
    cos = np.concatenate([np.cos(ang_r), np.cos(ang_r), np.cos(ang_c), np.cos(ang_c), zeros], axis=1)
    sin = np.concatenate([-np.sin(ang_r), np.sin(ang_r), -np.sin(ang_c), np.sin(ang_c), zeros], axis=1)
    return jnp.asarray(cos, F32), jnp.asarray(sin, F32)


def _prepare_weights(w_ukv, w_s, b_s, w_conv, gains):
    w_kv = jnp.concatenate(
        [w_ukv[..., :QK_NOPE].reshape(DEPTH, KV_LORA, N_HEADS * QK_NOPE),
         w_ukv[..., QK_NOPE:].reshape(DEPTH, KV_LORA, N_HEADS * V_HEAD)], axis=-1).astype(BF16)
    wts = {
        "w_kv": w_kv,
        "w_s": w_s.reshape(DEPTH, N_HEADS_B * CHUNK, CHUNK).astype(BF16),
        "b_s": jnp.repeat(jnp.swapaxes(b_s, 1, 2), HEAD_B, axis=-1),
        "w_conv": w_conv,
    }
    for name, g in gains.items():
        wts[name] = g
    return wts


def kernel(x_prompt, x_sample, cache_ckv, cache_krope, c, c_ctx, w_ada, b_ada, g_pre_mix, w_in, g_q, w_uq, g_kv, w_ukv, g_v, w_s, b_s, w_conv, w_out, g_post_mix, g_pre_ffn, w_ff1, w_ff2, g_post_ffn):
    batch, seq, _ = x_prompt.shape
    dec_batch, dec_seq, _ = x_sample.shape

    wts = _prepare_weights(
        w_ukv, w_s, b_s, w_conv,
        {"g_pre_mix": g_pre_mix, "g_q": g_q, "g_kv": g_kv, "g_v": g_v,
         "g_post_mix": g_post_mix, "g_pre_ffn": g_pre_ffn, "g_post_ffn": g_post_ffn})
    raw_big = (jnp.swapaxes(w_in, 1, 2), w_out, w_ff1, w_ff2)

    cond = jnp.concatenate(
        [c_ctx[None, :], c, jnp.zeros((COND_ROWS - 1 - dec_batch, D_MODEL), F32)], axis=0)
    mod_args = (cond, w_ada, b_ada.reshape(DEPTH, 1, MOD_COLS))
    *big, mods, wts["w_q"] = _first_layer_setup(raw_big, mod_args, w_uq)

    cache_krope_t = jnp.swapaxes(cache_krope, 2, 3)
    rope_tabs = _rope_tables(dec_seq)

    xp = x_prompt.reshape(batch * seq, D_MODEL)
    xs = x_sample.reshape(dec_batch * dec_seq, D_MODEL)
    new_ckv = new_kr = None
    for l in range(DEPTH):
        wl = dict(wts, **dict(zip(BIG_WEIGHTS, big)))
        side_mods = 1 <= l < DEPTH - 1
        xp, new_ckv, new_kr, *side = _context_layer(
            xp, mods, wl, l, new_ckv, new_kr, raw_big if l + 1 < DEPTH else None,
            mod_args if side_mods else None, seq_len=seq, seqs_per_tile=2)
        q, k, v, obc, kc, vc, *pre_side = _latent_pre(
            xs, mods, wl, l, rope_tabs, cache_ckv, cache_krope_t, mod_args if l == 0 else None,
            seq_len=dec_seq)
        xs = _latent_post(q, k, v, obc, xs, mods, (kc, vc), wl, l, seq_len=dec_seq, tq=512)
        big = side[:len(BIG_WEIGHTS)]
        mods = side[-1] if side_mods else (pre_side[0] if pre_side else None)

    return (xp.reshape(batch, seq, D_MODEL), xs.reshape(dec_batch, dec_seq, D_MODEL),
            new_ckv, jnp.swapaxes(new_kr, 2, 3))
```

```python
import functools
import math

import jax
import jax.numpy as jnp
import numpy as np
from jax import lax
from jax.experimental import pallas as pl
from jax.experimental.pallas import tpu as pltpu

F32 = jnp.float32
BF16 = jnp.bfloat16

D_MODEL = 1024
DEPTH = 4
GRID_W = 64
N_HEADS = 4
QK_NOPE = 128
QK_ROPE = 64
V_HEAD = 128
Q_LORA = 384
KV_LORA = 256
WIDTH_A = N_HEADS * V_HEAD
ROPE_THETA = 10000.0
WIDTH_B = 256
N_HEADS_B = 4
HEAD_B = WIDTH_B // N_HEADS_B
CHUNK = 128
WIDTH_C = 256
D_FF = 4 * D_MODEL
N_MOD = 6
EPS = 1e-6

LANES = 128
QK_PAD = 2 * LANES
QK_WIDTH = N_HEADS * QK_PAD
KV_COLS = N_HEADS * (QK_NOPE + V_HEAD)
BC_WIDTH = WIDTH_B + WIDTH_C
OFF_Q = 0
OFF_CKV = OFF_Q + Q_LORA
OFF_U = OFF_CKV + KV_LORA
OFF_V = OFF_U + WIDTH_B
OFF_BG = OFF_V + WIDTH_B
OFF_CG = OFF_BG + WIDTH_C
OFF_HH = OFF_CG + WIDTH_C
OFF_KR = OFF_HH + WIDTH_C
IN_COLS_PAD = OFF_KR + LANES
COND_ROWS = 8
VMEM_LIMIT = 56 * 1024 * 1024
SM_SCALE_LOG2 = math.log2(math.e) / math.sqrt(QK_NOPE + QK_ROPE)
FF_CHUNK = 1024
OUT_SPLIT = 2

PRE_WEIGHTS = ("g_pre_mix", "w_in", "g_q", "w_q", "g_kv", "w_kv", "g_v", "w_s", "b_s", "w_conv")
POST_WEIGHTS = ("g_post_mix", "w_out", "g_pre_ffn", "w_ff1", "w_ff2", "g_post_ffn")
BIG_WEIGHTS = ("w_in", "w_out", "w_ff1", "w_ff2")
BIG_SHAPES = ((D_MODEL, IN_COLS_PAD), (D_MODEL, D_MODEL), (D_MODEL, D_FF), (D_FF, D_MODEL))
CAST_STEPS = 16


def _rms(x, g):
    return x * lax.rsqrt(jnp.mean(x * x, axis=-1, keepdims=True) + EPS) * g


def _dot(a, b):
    return jnp.dot(a, b, preferred_element_type=F32)


def _dot_nt(a, b):
    return lax.dot_general(a, b, (((1,), (1,)), ((), ())), preferred_element_type=F32)


def _params(n_axes):
    return pltpu.CompilerParams(
        dimension_semantics=("parallel",) * n_axes, vmem_limit_bytes=VMEM_LIMIT)


def _layer_spec(shape, l):
    if len(shape) == 2:
        return pl.BlockSpec(tuple(shape), lambda *_: (0, 0), pipeline_mode=pl.Buffered(1))
    return pl.BlockSpec((None,) + tuple(shape[1:]), lambda *_: (l, 0, 0),
                        pipeline_mode=pl.Buffered(1))


class _LayerRow:
    def __init__(self, ref, layer):
        self.ref, self.layer = ref, layer

    def __getitem__(self, idx):
        assert idx is Ellipsis
        return self.ref[self.layer:self.layer + 1, :]


def _weight_refs(names, refs, layer):
    return {n: _LayerRow(r, layer) if n.startswith("g_") else r for n, r in zip(names, refs)}


def _mod_spec():
    return pl.BlockSpec((COND_ROWS, N_MOD * D_MODEL), lambda *_: (0, 0),
                        pipeline_mode=pl.Buffered(1))


MOD_COLS = N_MOD * D_MODEL


def _mod_stage(cond_ref, w_ref, b_ref, o_ref):
    c = cond_ref[...]
    s = c / (1.0 + jnp.exp(-c))
    o_ref[...] = _dot(s.astype(BF16), w_ref[...].astype(BF16)) + b_ref[...]


def _mod_specs(l, n_steps):
    tn = MOD_COLS // n_steps
    in_specs = [
        pl.BlockSpec((COND_ROWS, D_MODEL), lambda i: (0, 0), pipeline_mode=pl.Buffered(1)),
        pl.BlockSpec((None, D_MODEL, tn), lambda i: (l, 0, i)),
        pl.BlockSpec((None, 1, tn), lambda i: (l, 0, i)),
    ]
    return (in_specs, pl.BlockSpec((COND_ROWS, tn), lambda i: (0, i)),
            jax.ShapeDtypeStruct((COND_ROWS, MOD_COLS), F32))


def _cache_kv_stage(ckv_ref, kr_ref, wkv_ref, k_ref, v_ref):
    past = ckv_ref.shape[0]
    kv = _dot(ckv_ref[...].astype(BF16), wkv_ref[...])
    pad = jnp.zeros((QK_PAD - QK_NOPE - QK_ROPE, past), F32)
    krz = jnp.concatenate([kr_ref[...], pad], axis=0).T.astype(BF16)
    for h in range(N_HEADS):
        lo = h * QK_PAD
        src = h * (QK_NOPE + V_HEAD)
        k_ref[:, lo:lo + QK_NOPE] = kv[:, src:src + QK_NOPE].astype(BF16)
        k_ref[:, lo + QK_NOPE:lo + QK_PAD] = krz
        v_ref[:, h * V_HEAD:(h + 1) * V_HEAD] = kv[:, src + QK_NOPE:src + QK_NOPE + V_HEAD].astype(BF16)


IN_HALF = IN_COLS_PAD // CAST_STEPS // 2


def _cast_stage(in_refs, out_refs):
    win_a, win_b, wout, w1, w2 = in_refs
    win_o, wout_o, w1_o, w2_o = out_refs
    last = pl.program_id(0) == CAST_STEPS - 1
    b = jnp.where(last, 0.0, win_b[...])
    win_o[...] = jnp.concatenate([win_a[...], b], axis=0).T.astype(BF16)
    wout_o[...] = wout[...].astype(BF16)
    w1_o[...] = w1[...].astype(BF16)
    w2_o[...] = w2[...].astype(BF16)


def _w_in_block(i, half):
    kr_block = (Q_LORA + KV_LORA) // IN_HALF
    n_front = kr_block // 2
    shifted = jnp.where(i < CAST_STEPS - 1, 2 * i + 1 + half, kr_block)
    return jnp.where(i < n_front, 2 * i + half, shifted)


def _cast_specs(raw, l):
    w_in_t = raw[0]
    in_specs = [pl.BlockSpec((None, IN_HALF, D_MODEL), lambda i, h=h: (l, _w_in_block(i, h), 0))
                for h in range(2)]
    out_specs = [pl.BlockSpec((D_MODEL, IN_COLS_PAD // CAST_STEPS), lambda i: (0, i))]
    out_shape = [jax.ShapeDtypeStruct(BIG_SHAPES[0], BF16)]
    for a, (rows, cols) in zip(raw[1:], BIG_SHAPES[1:]):
        chunk = rows // CAST_STEPS
        in_specs.append(pl.BlockSpec((None, chunk, cols), lambda i: (l, i, 0)))
        out_specs.append(pl.BlockSpec((chunk, cols), lambda i: (i, 0)))
        out_shape.append(jax.ShapeDtypeStruct((rows, cols), BF16))
    return in_specs, out_specs, out_shape, [w_in_t, w_in_t] + list(raw[1:])


def _first_layer_kernel(*refs):
    _cast_stage(refs[:5], refs[9:13])
    _mod_stage(*refs[5:8], refs[13])
    wuq_ref, wq_ref = refs[8], refs[14]
    pad = jnp.zeros((QK_PAD - wuq_ref.shape[0], Q_LORA), F32)
    wq_ref[...] = jnp.concatenate([wuq_ref[...], pad], axis=0).T.astype(BF16)


def _first_layer_setup(raw, mod_args, w_uq):
    assert DEPTH * N_HEADS == CAST_STEPS
    in_specs, out_specs, out_shape, args = _cast_specs(raw, 0)
    m_in, m_out, m_shape = _mod_specs(0, CAST_STEPS)
    w_uq_t = jnp.transpose(w_uq, (0, 2, 3, 1))
    q_in = pl.BlockSpec((None, None, w_uq.shape[-1], Q_LORA),
                        lambda i: (i // N_HEADS, i % N_HEADS, 0, 0))
    q_out = pl.BlockSpec((None, Q_LORA, QK_PAD), lambda i: (i // N_HEADS, 0, i % N_HEADS))
    return pl.pallas_call(
        _first_layer_kernel,
        grid=(CAST_STEPS,),
        in_specs=in_specs + m_in + [q_in],
        out_specs=out_specs + [m_out, q_out],
        out_shape=out_shape + [m_shape, jax.ShapeDtypeStruct((DEPTH, Q_LORA, QK_WIDTH), BF16)],
        compiler_params=_params(1),
        name="first_layer_setup",
    )(*args, *mod_args, w_uq_t)


def _swap_halves(x):
    lane = lax.broadcasted_iota(jnp.int32, x.shape, 1)
    quarter = QK_ROPE // 4
    first_half = (lane & (2 * quarter - 1)) < quarter
    return jnp.where(first_half, pltpu.roll(x, LANES - quarter, 1), pltpu.roll(x, quarter, 1))


def _pre_stage(x, mod, w, rope, seq_len, q_ref, k_ref, v_ref, ob_ref, oc_ref, cache_refs):
    tm = x.shape[0]
    sh1 = mod[:, 0:D_MODEL]
    sc1 = mod[:, D_MODEL:2 * D_MODEL]
    h = _rms(x, w["g_pre_mix"][...] * (1.0 + sc1)) + sh1
    z = _dot(h.astype(BF16), w["w_in"][...])

    ckv = _rms(z[:, OFF_CKV:OFF_CKV + KV_LORA], w["g_kv"][...])
    krz = z[:, OFF_KR:OFF_KR + LANES]
    if cache_refs is not None:
        ckv_ref, kr_ref = cache_refs
        if len(ckv_ref.shape) == 4:
            ckv_ref[:, 1:] = jnp.zeros((ckv_ref.shape[0], DEPTH - 1) + ckv_ref.shape[2:], F32)
            kr_ref[:, 1:] = jnp.zeros((kr_ref.shape[0], DEPTH - 1) + kr_ref.shape[2:], F32)
            ckv_ref, kr_ref = ckv_ref.at[:, 0], kr_ref.at[:, 0]
        ckv_ref[...] = ckv.reshape(ckv_ref.shape)
        for s in range(tm // seq_len):
            kr_ref[s] = krz[s * seq_len:(s + 1) * seq_len, :].T[0:QK_ROPE, :]
    qn = _rms(z[:, OFF_Q:OFF_Q + Q_LORA], w["g_q"][...])
    q = _dot(qn.astype(BF16), w["w_q"][...]) * SM_SCALE_LOG2
    kv = _dot(ckv.astype(BF16), w["w_kv"][...])
    if rope is not None:
        cos, sin = rope
        krz = krz * cos + _swap_halves(krz) * sin
    krz = krz.astype(BF16)
    for hd in range(N_HEADS):
        lo = hd * QK_PAD
        q_ref[:, lo:lo + QK_NOPE] = q[:, lo:lo + QK_NOPE].astype(BF16)
        qr = q[:, lo + QK_NOPE:lo + QK_PAD]
        if rope is not None:
            qr = qr * cos + _swap_halves(qr) * sin
        q_ref[:, lo + QK_NOPE:lo + QK_PAD] = qr.astype(BF16)
        src = hd * (QK_NOPE + V_HEAD)
        k_ref[:, lo:lo + QK_NOPE] = kv[:, src:src + QK_NOPE].astype(BF16)
        k_ref[:, lo + QK_NOPE:lo + QK_PAD] = krz
        v_ref[:, hd * V_HEAD:(hd + 1) * V_HEAD] = (
            kv[:, src + QK_NOPE:src + QK_NOPE + V_HEAD].astype(BF16))

    u = jax.nn.gelu(z[:, OFF_U:OFF_U + WIDTH_B])
    vn = _rms(jax.nn.gelu(z[:, OFF_V:OFF_V + WIDTH_B]), w["g_v"][...]).astype(BF16)
    lane = lax.broadcasted_iota(jnp.int32, (CHUNK, WIDTH_B), 1)
    ws = w["w_s"][...]
    bs = w["b_s"][...]
    for c in range(tm // CHUNK):
        rows = slice(c * CHUNK, (c + 1) * CHUNK)
        r = _dot(ws, vn[rows, :])
        mixed = r[(N_HEADS_B - 1) * CHUNK:, :]
        for hb in range(N_HEADS_B - 2, -1, -1):
            mixed = jnp.where(lane < (hb + 1) * HEAD_B, r[hb * CHUNK:(hb + 1) * CHUNK, :], mixed)
        ob_ref[rows, :] = (u[rows, :] * (mixed + bs)).astype(BF16)

    zc = z[:, OFF_CG:OFF_CG + WIDTH_C] * z[:, OFF_HH:OFF_HH + WIDTH_C]
    pos = lax.broadcasted_iota(jnp.int32, (tm, WIDTH_C), 0) & (seq_len - 1)
    z_prev = jnp.where(pos == 0, 0.0, pltpu.roll(zc, 1, 0))
    z_next = jnp.where(pos == seq_len - 1, 0.0, pltpu.roll(zc, tm - 1, 0))
    wc = w["w_conv"][...]
    y = z_prev * wc[0:1, :] + zc * wc[1:2, :] + z_next * wc[2:3, :]
    oc_ref[...] = (z[:, OFF_BG:OFF_BG + WIDTH_C] * y).astype(BF16)


def _attn_head(q_ref, k_ref, v_ref, cache, qrows, krows, hd, oa_ref):
    qk_cols = slice(hd * QK_PAD, (hd + 1) * QK_PAD)
    v_cols = slice(hd * V_HEAD, (hd + 1) * V_HEAD)
    qh = q_ref[qrows, qk_cols]
    s_lat = _dot_nt(qh, k_ref[krows, qk_cols])
    mx = jnp.max(s_lat, axis=-1, keepdims=True)
    if cache is not None:
        kc_ref, vc_ref = cache
        s_ctx = _dot_nt(qh, kc_ref[:, qk_cols])
        mx = jnp.maximum(mx, jnp.max(s_ctx, axis=-1, keepdims=True))
    p_lat = jnp.exp2(s_lat - mx)
    den = jnp.sum(p_lat, axis=-1, keepdims=True)
    o = _dot(p_lat.astype(BF16), v_ref[krows, v_cols])
    if cache is not None:
        p_ctx = jnp.exp2(s_ctx - mx)
        den = den + jnp.sum(p_ctx, axis=-1, keepdims=True)
        o = o + _dot(p_ctx.astype(BF16), vc_ref[:, v_cols])
    oa_ref[qrows, v_cols] = (o * (1.0 / den)).astype(BF16)


def _post_stage(x_ref, mod, mix_ref, w, o_ref):
    ga1 = mod[:, 2 * D_MODEL:3 * D_MODEL]
    sh2 = mod[:, 3 * D_MODEL:4 * D_MODEL]
    sc2 = mod[:, 4 * D_MODEL:5 * D_MODEL]
    ga2 = mod[:, 5 * D_MODEL:6 * D_MODEL]
    post_mix_row = ga1 * w["g_post_mix"][...]
    pre_ffn_row = w["g_pre_ffn"][...] * (1.0 + sc2)
    post_ffn_row = ga2 * w["g_post_ffn"][...]
    group = x_ref.shape[0] // OUT_SPLIT
    for r in range(OUT_SPLIT):
        rows = slice(r * group, (r + 1) * group)
        mo = _dot(mix_ref[rows, :], w["w_out"][...])
        x1 = x_ref[rows, :] + _rms(mo, post_mix_row)
        o_ref[rows, :] = x1
        mix_ref[rows, :] = (_rms(x1, pre_ffn_row) + sh2).astype(BF16)
    h2 = mix_ref[...]
    f = None
    for j in range(D_FF // FF_CHUNK):
        cols = slice(j * FF_CHUNK, (j + 1) * FF_CHUNK)
        a = jnp.square(jnp.maximum(_dot(h2, w["w_ff1"][:, cols]), 0.0)).astype(BF16)
        part = _dot(a, w["w_ff2"][cols, :])
        f = part if f is None else f + part
    o_ref[...] = o_ref[...] + _rms(f, post_ffn_row)


def _context_kernel(*refs, layer, seq_len, aliased, cast_next, mod_next):
    n_w = len(PRE_WEIGHTS) + len(POST_WEIGHTS)
    n_cast_out = len(BIG_WEIGHTS) if cast_next else 0
    n_cast_in = n_cast_out + 1 if cast_next else 0
    n_side_out = n_cast_out + (1 if mod_next else 0)
    n_side_in = n_cast_in + (3 if mod_next else 0)
    x_ref, mod_ref = refs[:2]
    w = _weight_refs(PRE_WEIGHTS + POST_WEIGHTS, refs[2:2 + n_w], layer)
    side_in = refs[2 + n_w:2 + n_w + n_side_in]
    n_in = 2 + n_w + n_side_in + (2 if aliased else 0)
    o_ref, ckv_ref, kr_ref = refs[n_in:n_in + 3]
    side_out = refs[n_in + 3:n_in + 3 + n_side_out]
    q_ref, k_ref, v_ref, mix_ref = refs[n_in + 3 + n_side_out:]
    if cast_next:
        _cast_stage(side_in[:n_cast_in], side_out[:n_cast_out])
    if mod_next:
        _mod_stage(*side_in[n_cast_in:], side_out[n_cast_out])
    x = x_ref[...]
    mod = mod_ref[0:1, :]
    _pre_stage(x, mod, w, None, seq_len, q_ref, k_ref, v_ref,
               mix_ref.at[:, WIDTH_A:WIDTH_A + WIDTH_B], mix_ref.at[:, WIDTH_A + WIDTH_B:],
               (ckv_ref, kr_ref))
    for s in range(x.shape[0] // seq_len):
        rows = slice(s * seq_len, (s + 1) * seq_len)
        for hd in range(N_HEADS):
            _attn_head(q_ref, k_ref, v_ref, None, rows, rows, hd, mix_ref)
    _post_stage(x_ref, mod, mix_ref, w, o_ref)


def _context_layer(x, mods, wts, l, new_ckv, new_kr, raw_big, mod_args, *, seq_len, seqs_per_tile):
    t = x.shape[0]
    tm = seq_len * seqs_per_tile
    n_seq_total = t // seq_len
    aliased = new_ckv is not None
    assert aliased or l == 0
    cast_next = raw_big is not None
    mod_next = mod_args is not None
    n_tiles = t // tm
    row_spec = pl.BlockSpec((tm, D_MODEL), lambda i: (i, 0))
    layer_dim, layer_idx = (None, l) if aliased else (DEPTH, 0)
    ckv_spec = pl.BlockSpec((seqs_per_tile, layer_dim, seq_len, KV_LORA),
                            lambda i: (i, layer_idx, 0, 0))
    kr_spec = pl.BlockSpec((seqs_per_tile, layer_dim, QK_ROPE, seq_len),
                           lambda i: (i, layer_idx, 0, 0))
    in_specs = [row_spec, _mod_spec()]
    args = [x, mods]
    for n in PRE_WEIGHTS + POST_WEIGHTS:
        in_specs.append(_layer_spec(wts[n].shape, l))
        args.append(wts[n])
    out_specs = [row_spec, ckv_spec, kr_spec]
    out_shape = [
        jax.ShapeDtypeStruct((t, D_MODEL), F32),
        jax.ShapeDtypeStruct((n_seq_total, DEPTH, seq_len, KV_LORA), F32),
        jax.ShapeDtypeStruct((n_seq_total, DEPTH, QK_ROPE, seq_len), F32),
    ]
    if cast_next:
        assert n_tiles == CAST_STEPS
        c_in, c_out, c_shape, c_args = _cast_specs(raw_big, l + 1)
        in_specs += c_in
        args += c_args
        out_specs += c_out
        out_shape += c_shape
    if mod_next:
        m_in, m_out, m_shape = _mod_specs(l + 1, n_tiles)
        in_specs += m_in
        args += list(mod_args)
        out_specs.append(m_out)
        out_shape.append(m_shape)
    aliases = {}
    if aliased:
        in_specs += [pl.BlockSpec(memory_space=pl.ANY)] * 2
        aliases = {len(args): 1, len(args) + 1: 2}
        args += [new_ckv, new_kr]
    return pl.pallas_call(
        functools.partial(_context_kernel, layer=l, seq_len=seq_len, aliased=aliased,
                          cast_next=cast_next, mod_next=mod_next),
        grid=(n_tiles,),
        in_specs=in_specs,
        out_specs=out_specs,
        out_shape=out_shape,
        scratch_shapes=[
            pltpu.VMEM((tm, QK_WIDTH), BF16), pltpu.VMEM((tm, QK_WIDTH), BF16),
            pltpu.VMEM((tm, WIDTH_A), BF16), pltpu.VMEM((tm, D_MODEL), BF16),
        ],
        input_output_aliases=aliases,
        compiler_params=_params(1),
        name="context_layer",
    )(*args)


def _latent_pre_kernel(*refs, layer, seq_len, mod_next):
    x_ref, mod_ref = refs[:2]
    w = _weight_refs(PRE_WEIGHTS, refs[2:2 + len(PRE_WEIGHTS)], layer)
    refs = refs[2 + len(PRE_WEIGHTS):]
    cos_ref, sin_ref, cache_ckv_ref, cache_kr_ref = refs[:4]
    if mod_next:
        _mod_stage(*refs[4:7], refs[-1])
        refs = refs[:4] + refs[7:-1]
    q_ref, k_ref, v_ref, obc_ref, kc_ref, vc_ref = refs[4:]
    _cache_kv_stage(cache_ckv_ref, cache_kr_ref, w["w_kv"], kc_ref, vc_ref)
    mod = mod_ref[pl.ds(1 + pl.program_id(0), 1), :]
    _pre_stage(x_ref[...], mod, w, (cos_ref[...], sin_ref[...]), seq_len,
               q_ref, k_ref, v_ref, obc_ref.at[:, 0:WIDTH_B], obc_ref.at[:, WIDTH_B:], None)


def _latent_pre(x, mods, wts, l, rope_tabs, cache_ckv, cache_krope_t, mod_args, *, seq_len):
    t = x.shape[0]
    tm = seq_len
    nb, _, past, _ = cache_ckv.shape
    assert nb == t // tm
    row_spec = lambda w: pl.BlockSpec((tm, w), lambda i: (i, 0))
    in_specs = [row_spec(D_MODEL), _mod_spec()]
    args = [x, mods]
    for n in PRE_WEIGHTS:
        in_specs.append(_layer_spec(wts[n].shape, l))
        args.append(wts[n])
    in_specs += [pl.BlockSpec((tm, LANES), lambda i: (0, 0), pipeline_mode=pl.Buffered(1))] * 2
    args += list(rope_tabs)
    in_specs += [pl.BlockSpec((None, None, past, KV_LORA), lambda i: (i, l, 0, 0)),
                 pl.BlockSpec((None, None, QK_ROPE, past), lambda i: (i, l, 0, 0))]
    args += [cache_ckv, cache_krope_t]
    out_specs = [row_spec(QK_WIDTH), row_spec(QK_WIDTH), row_spec(WIDTH_A), row_spec(BC_WIDTH),
                 pl.BlockSpec((None, past, QK_WIDTH), lambda i: (i, 0, 0)),
                 pl.BlockSpec((None, past, WIDTH_A), lambda i: (i, 0, 0))]
    out_shape = [
        jax.ShapeDtypeStruct((t, QK_WIDTH), BF16),
        jax.ShapeDtypeStruct((t, QK_WIDTH), BF16),
        jax.ShapeDtypeStruct((t, WIDTH_A), BF16),
        jax.ShapeDtypeStruct((t, BC_WIDTH), BF16),
        jax.ShapeDtypeStruct((nb, past, QK_WIDTH), BF16),
        jax.ShapeDtypeStruct((nb, past, WIDTH_A), BF16),
    ]
    if mod_args is not None:
        m_in, m_out, m_shape = _mod_specs(l + 1, t // tm)
        in_specs += m_in
        args += list(mod_args)
        out_specs.append(m_out)
        out_shape.append(m_shape)
    return pl.pallas_call(
        functools.partial(_latent_pre_kernel, layer=l, seq_len=seq_len,
                          mod_next=mod_args is not None),
        grid=(t // tm,),
        in_specs=in_specs,
        out_specs=out_specs,
        out_shape=out_shape,
        compiler_params=_params(1),
        name="latent_pre",
    )(*args)


def _latent_post_kernel(*refs, layer, tiles_per_seq):
    q_ref, k_ref, v_ref, kc_ref, vc_ref, obc_ref, x_ref, mod_ref = refs[:8]
    w = _weight_refs(POST_WEIGHTS, refs[8:8 + len(POST_WEIGHTS)], layer)
    o_ref, mix_ref = refs[8 + len(POST_WEIGHTS):]
    qrows = slice(0, q_ref.shape[0])
    krows = slice(0, k_ref.shape[0])
    for hd in range(N_HEADS):
        _attn_head(q_ref, k_ref, v_ref, (kc_ref, vc_ref), qrows, krows, hd, mix_ref)
    mix_ref[:, WIDTH_A:] = obc_ref[...]
    mod = mod_ref[pl.ds(1 + pl.program_id(0) // tiles_per_seq, 1), :]
    _post_stage(x_ref, mod, mix_ref, w, o_ref)


def _latent_post(q, k, v, obc, x, mods, cache, wts, l, *, seq_len, tq):
    t = x.shape[0]
    tiles_per_seq = seq_len // tq
    kc, vc = cache
    past = kc.shape[1]
    seq_of = lambda i: i // tiles_per_seq
    row_spec = lambda w: pl.BlockSpec((tq, w), lambda i: (i, 0))
    in_specs = [
        row_spec(QK_WIDTH),
        pl.BlockSpec((seq_len, QK_WIDTH), lambda i: (seq_of(i), 0)),
        pl.BlockSpec((seq_len, WIDTH_A), lambda i: (seq_of(i), 0)),
        pl.BlockSpec((None, past, QK_WIDTH), lambda i: (seq_of(i), 0, 0)),
        pl.BlockSpec((None, past, WIDTH_A), lambda i: (seq_of(i), 0, 0)),
        row_spec(BC_WIDTH),
        row_spec(D_MODEL),
        _mod_spec(),
    ]
    args = [q, k, v, kc, vc, obc, x, mods]
    for n in POST_WEIGHTS:
        in_specs.append(_layer_spec(wts[n].shape, l))
        args.append(wts[n])
    return pl.pallas_call(
        functools.partial(_latent_post_kernel, layer=l, tiles_per_seq=tiles_per_seq),
        grid=(t // tq,),
        in_specs=in_specs,
        out_specs=row_spec(D_MODEL),
        out_shape=jax.ShapeDtypeStruct((t, D_MODEL), F32),
        scratch_shapes=[pltpu.VMEM((tq, D_MODEL), BF16)],
        compiler_params=_params(1),
        name="latent_post",
    )(*args)


def _rope_tables(n_tokens):
    rows = n_tokens // GRID_W
    row = np.repeat(np.arange(rows, dtype=np.float64), GRID_W)
    col = np.tile(np.arange(GRID_W, dtype=np.float64), rows)
    nf = QK_ROPE // 4
    inv = ROPE_THETA ** (-np.arange(nf, dtype=np.float64) / nf)
    ang_r = row[:, None] * inv
    ang_c = col[:, None] * inv
    zeros = np.zeros((n_tokens, LANES - QK_ROPE))
    cos = np.concatenate([np.cos(ang_r), np.cos(ang_r), np.cos(ang_c), np.cos(ang_c), zeros], axis=1)
    sin = np.concatenate([-np.sin(ang_r), np.sin(ang_r), -np.sin(ang_c), np.sin(ang_c), zeros], axis=1)
    return jnp.asarray(cos, F32), jnp.asarray(sin, F32)


def _prepare_weights(w_ukv, w_s, b_s, w_conv, gains):
    w_kv = w_ukv.reshape(DEPTH, KV_LORA, KV_COLS).astype(BF16)
    wts = {
        "w_kv": w_kv,
        "w_s": w_s.reshape(DEPTH, N_HEADS_B * CHUNK, CHUNK).astype(BF16),
        "b_s": jnp.repeat(jnp.swapaxes(b_s, 1, 2), HEAD_B, axis=-1),
        "w_conv": w_conv,
    }
    for name, g in gains.items():
        wts[name] = g
    return wts


def kernel(x_prompt, x_sample, cache_ckv, cache_krope, c, c_ctx, w_ada, b_ada, g_pre_mix, w_in, g_q, w_uq, g_kv, w_ukv, g_v, w_s, b_s, w_conv, w_out, g_post_mix, g_pre_ffn, w_ff1, w_ff2, g_post_ffn):
    batch, seq, _ = x_prompt.shape
    dec_batch, dec_seq, _ = x_sample.shape

    wts = _prepare_weights(
        w_ukv, w_s, b_s, w_conv,
        {"g_pre_mix": g_pre_mix, "g_q": g_q, "g_kv": g_kv, "g_v": g_v,
         "g_post_mix": g_post_mix, "g_pre_ffn": g_pre_ffn, "g_post_ffn": g_post_ffn})
    raw_big = (jnp.swapaxes(w_in, 1, 2), w_out, w_ff1, w_ff2)

    cond = jnp.concatenate(
        [c_ctx[None, :], c, jnp.zeros((COND_ROWS - 1 - dec_batch, D_MODEL), F32)], axis=0)
    mod_args = (cond, w_ada, b_ada.reshape(DEPTH, 1, MOD_COLS))
    *big, mods, wts["w_q"] = _first_layer_setup(raw_big, mod_args, w_uq)

    cache_krope_t = jnp.swapaxes(cache_krope, 2, 3)
    rope_tabs = _rope_tables(dec_seq)

    xp = x_prompt.reshape(batch * seq, D_MODEL)
    xs = x_sample.reshape(dec_batch * dec_seq, D_MODEL)
    new_ckv = new_kr = None
    for l in range(DEPTH):
        wl = dict(wts, **dict(zip(BIG_WEIGHTS, big)))
        side_mods = 1 <= l < DEPTH - 1
        xp, new_ckv, new_kr, *side = _context_layer(
            xp, mods, wl, l, new_ckv, new_kr, raw_big if l + 1 < DEPTH else None,
            mod_args if side_mods else None, seq_len=seq, seqs_per_tile=2)
        q, k, v, obc, kc, vc, *pre_side = _latent_pre(
            xs, mods, wl, l, rope_tabs, cache_ckv, cache_krope_t, mod_args if l == 0 else None,
            seq_len=dec_seq)
        xs = _latent_post(q, k, v, obc, xs, mods, (kc, vc), wl, l, seq_len=dec_seq, tq=512)
        big = side[:len(BIG_WEIGHTS)]
        mods = side[-1] if side_mods else (pre_side[0] if pre_side else None)

    return (xp.reshape(batch, seq, D_MODEL), xs.reshape(dec_batch, dec_seq, D_MODEL),
            new_ckv, jnp.swapaxes(new_kr, 2, 3))
```
